```python
import jax, jax.numpy as jnp
from jax import lax
import numpy as np

D_MODEL = 4096
BATCH = 8
SEQ = 4096
DEPTH = 1

CHUNK = 64
N_LEFT_CHUNKS = 8
BAND = (N_LEFT_CHUNKS + 1) * CHUNK
ATTN_WIDTH = D_MODEL // 2
ATTN_HEAD_DIM = 128
ATTN_HEADS = ATTN_WIDTH // ATTN_HEAD_DIM
MAX_REL = 128
N_REL = 2 * MAX_REL + 1
POOL_WIDTH = D_MODEL // 2
POOL_WINDOWS = (2, 4, 8, 16)
N_POOL_GROUPS = len(POOL_WINDOWS)
POOL_GROUP_DIM = POOL_WIDTH // N_POOL_GROUPS
N_BRANCHES = 2
IN_COLS = 4 * ATTN_WIDTH + 2 * POOL_WIDTH + N_BRANCHES * D_MODEL
EPS = 1e-6

kernel_name = "hybrid_chunk_attn_pool_gated_block"


def _rmsnorm(x, g):
    xf = x.astype(jnp.float32)
    y = xf * lax.rsqrt(jnp.mean(xf * xf, axis=-1, keepdims=True) + EPS)
    return (y * g.astype(jnp.float32)).astype(x.dtype)


def _chunk_band_attention(q, k, v, rel_bias):
    B, S, H, Dh = q.shape
    nc = S // CHUNK
    pad = N_LEFT_CHUNKS * CHUNK
    kp = jnp.pad(k, ((0, 0), (pad, 0), (0, 0), (0, 0)))
    vp = jnp.pad(v, ((0, 0), (pad, 0), (0, 0), (0, 0)))
    qc = q.reshape(B, nc, CHUNK, H, Dh).transpose(1, 0, 2, 3, 4)
    rel = pad + jnp.arange(CHUNK)[:, None] - jnp.arange(BAND)[None, :]
    bias = rel_bias[:, jnp.clip(rel, -MAX_REL, MAX_REL) + MAX_REL].astype(jnp.float32)
    scale = Dh ** -0.5
    band_offsets = jnp.arange(BAND) - pad

    def one_chunk(args):
        c, qb = args
        start = c * CHUNK
        kb = lax.dynamic_slice_in_dim(kp, start, BAND, axis=1)
        vb = lax.dynamic_slice_in_dim(vp, start, BAND, axis=1)
        s = jnp.einsum('bqhd,bkhd->bhqk', qb, kb,
                       preferred_element_type=jnp.float32) * scale + bias
        valid = (start + band_offsets) >= 0
        s = jnp.where(valid[None, None, None, :], s, -jnp.inf)
        p = jax.nn.softmax(s, axis=-1)
        return jnp.einsum('bhqk,bkhd->bqhd', p.astype(vb.dtype), vb)

    out = lax.map(one_chunk, (jnp.arange(nc), qc))
    return out.transpose(1, 0, 2, 3, 4).reshape(B, S, H * Dh)


def _multiscale_pool(u, pool_w, pool_scale):
    B, S, W = u.shape
    uf = u.astype(jnp.float32).reshape(B, S, N_POOL_GROUPS, POOL_GROUP_DIM)
    cs = jnp.pad(jnp.cumsum(uf, axis=1), ((0, 0), (1, 0), (0, 0), (0, 0)))
    t = jnp.arange(S)
    means = []
    for g, w in enumerate(POOL_WINDOWS):
        csg = cs[:, :, g]
        lo = jnp.pad(csg[:, :S + 1 - w], ((0, 0), (w - 1, 0), (0, 0)))
        cnt = jnp.minimum(t + 1, w).astype(jnp.float32)[None, :, None]
        means.append((csg[:, 1:] - lo) / cnt)
    mean = jnp.stack(means, axis=2)
    d = (mean - uf).astype(u.dtype)
    y = jnp.einsum('bsgc,gcd->bsgd', d, pool_w)
    return y.reshape(B, S, W) * pool_scale


def _fwd_setup_inputs(seed: int = 0) -> dict:
    key = jax.random.key(seed)
    ks = jax.random.split(key, 12)
    f = jnp.float32
    x = jax.random.normal(ks[0], (BATCH, SEQ, D_MODEL), f)
    norm_gain = 1.0 + 0.1 * jax.random.normal(ks[1], (D_MODEL,), f)
    w_in = jax.random.normal(ks[2], (D_MODEL, IN_COLS), f) * D_MODEL ** -0.5
    rel_bias = 0.5 * jax.random.normal(ks[3], (ATTN_HEADS, N_REL), f)
    pool_w = jax.random.normal(ks[4], (N_POOL_GROUPS, POOL_GROUP_DIM, POOL_GROUP_DIM), f) * POOL_GROUP_DIM ** -0.5
    pool_scale = 1.0 + 0.1 * jax.random.normal(ks[5], (POOL_WIDTH,), f)
    w_out_attn = jax.random.normal(ks[6], (ATTN_WIDTH, D_MODEL), f) * ATTN_WIDTH ** -0.5
    w_out_pool = jax.random.normal(ks[7], (POOL_WIDTH, D_MODEL), f) * POOL_WIDTH ** -0.5
    gate_bias = 0.1 * jax.random.normal(ks[8], (N_BRANCHES, D_MODEL), f)
    w_out = jax.random.normal(ks[9], (D_MODEL, D_MODEL), f) * D_MODEL ** -0.5
    final_gain = 1.0 + 0.1 * jax.random.normal(ks[10], (D_MODEL,), f)
    return {"x": x, "norm_gain": norm_gain, "w_in": w_in, "rel_bias": rel_bias,
            "pool_w": pool_w, "pool_scale": pool_scale, "w_out_attn": w_out_attn,
            "w_out_pool": w_out_pool, "gate_bias": gate_bias, "w_out": w_out,
            "final_gain": final_gain}


def _fwd_reference(x, norm_gain, w_in, rel_bias, pool_w, pool_scale, w_out_attn,
              w_out_pool, gate_bias, w_out, final_gain):
    B, S, D = x.shape
    A, P = ATTN_WIDTH, POOL_WIDTH
    for _ in range(DEPTH):
        h = _rmsnorm(x, norm_gain)
        proj = jnp.einsum('bsd,dn->bsn', h, w_in)
        o = 0
        q = proj[..., o:o + A]; o += A
        k = proj[..., o:o + A]; o += A
        v = proj[..., o:o + A]; o += A
        z_attn = proj[..., o:o + A]; o += A
        u_pool = proj[..., o:o + P]; o += P
        z_pool = proj[..., o:o + P]; o += P
        g_attn = proj[..., o:o + D]; o += D
        g_pool = proj[..., o:o + D]

        hs = (B, S, ATTN_HEADS, ATTN_HEAD_DIM)
        y_attn = _chunk_band_attention(q.reshape(hs), k.reshape(hs), v.reshape(hs), rel_bias)
        y_attn = y_attn * jax.nn.silu(z_attn)
        y_pool = _multiscale_pool(u_pool, pool_w, pool_scale) * jax.nn.silu(z_pool)

        m = (jax.nn.sigmoid(g_attn + gate_bias[0]) * jnp.einsum('bsa,ad->bsd', y_attn, w_out_attn)
             + jax.nn.sigmoid(g_pool + gate_bias[1]) * jnp.einsum('bsp,pd->bsd', y_pool, w_out_pool))
        x = x + jnp.einsum('bsd,de->bse', m, w_out)
    return _rmsnorm(x, final_gain)


import jax as _jax
import jax.numpy as _jnp

TWIN_FORMAT = 'train_step'
FWD_PARAMS = ['x', 'norm_gain', 'w_in', 'rel_bias', 'pool_w', 'pool_scale', 'w_out_attn', 'w_out_pool', 'gate_bias', 'w_out', 'final_gain']
TWIN_WEIGHTS = ['norm_gain', 'w_in', 'rel_bias', 'pool_w', 'pool_scale', 'w_out_attn', 'w_out_pool', 'gate_bias', 'w_out', 'final_gain']
TWIN_DIFF_INPUT = 'x'
TWIN_INPUTS = ['x', 'norm_gain', 'w_in', 'rel_bias', 'pool_w', 'pool_scale', 'w_out_attn', 'w_out_pool', 'gate_bias', 'w_out', 'final_gain', 'loss_target', 'm_norm_gain', 'm_w_in', 'm_rel_bias', 'm_pool_w', 'm_pool_scale', 'm_w_out_attn', 'm_w_out_pool', 'm_gate_bias', 'm_w_out', 'm_final_gain', 'v_norm_gain', 'v_w_in', 'v_rel_bias', 'v_pool_w', 'v_pool_scale', 'v_w_out_attn', 'v_w_out_pool', 'v_gate_bias', 'v_w_out', 'v_final_gain']
TWIN_OUTPUTS = ['loss', 'grad_x', 'grad_norm_gain', 'grad_w_in', 'grad_rel_bias', 'grad_pool_w', 'grad_pool_scale', 'grad_w_out_attn', 'grad_w_out_pool', 'grad_gate_bias', 'grad_w_out', 'grad_final_gain', 'delta_norm_gain', 'delta_w_in', 'delta_rel_bias', 'delta_pool_w', 'delta_pool_scale', 'delta_w_out_attn', 'delta_w_out_pool', 'delta_gate_bias', 'delta_w_out', 'delta_final_gain', 'new_m_norm_gain', 'new_m_w_in', 'new_m_rel_bias', 'new_m_pool_w', 'new_m_pool_scale', 'new_m_w_out_attn', 'new_m_w_out_pool', 'new_m_gate_bias', 'new_m_w_out', 'new_m_final_gain', 'new_v_norm_gain', 'new_v_w_in', 'new_v_rel_bias', 'new_v_pool_w', 'new_v_pool_scale', 'new_v_w_out_attn', 'new_v_w_out_pool', 'new_v_gate_bias', 'new_v_w_out', 'new_v_final_gain']
TWIN_LEAF_KINDS = {'loss': 'loss', 'grad_x': 'grad_x', 'grad_norm_gain': 'grad_w', 'grad_w_in': 'grad_w', 'grad_rel_bias': 'grad_w', 'grad_pool_w': 'grad_w', 'grad_pool_scale': 'grad_w', 'grad_w_out_attn': 'grad_w', 'grad_w_out_pool': 'grad_w', 'grad_gate_bias': 'grad_w', 'grad_w_out': 'grad_w', 'grad_final_gain': 'grad_w', 'delta_norm_gain': 'delta_w', 'delta_w_in': 'delta_w', 'delta_rel_bias': 'delta_w', 'delta_pool_w': 'delta_w', 'delta_pool_scale': 'delta_w', 'delta_w_out_attn': 'delta_w', 'delta_w_out_pool': 'delta_w', 'delta_gate_bias': 'delta_w', 'delta_w_out': 'delta_w', 'delta_final_gain': 'delta_w', 'new_m_norm_gain': 'new_m', 'new_m_w_in': 'new_m', 'new_m_rel_bias': 'new_m', 'new_m_pool_w': 'new_m', 'new_m_pool_scale': 'new_m', 'new_m_w_out_attn': 'new_m', 'new_m_w_out_pool': 'new_m', 'new_m_gate_bias': 'new_m', 'new_m_w_out': 'new_m', 'new_m_final_gain': 'new_m', 'new_v_norm_gain': 'new_v', 'new_v_w_in': 'new_v', 'new_v_rel_bias': 'new_v', 'new_v_pool_w': 'new_v', 'new_v_pool_scale': 'new_v', 'new_v_w_out_attn': 'new_v', 'new_v_w_out_pool': 'new_v', 'new_v_gate_bias': 'new_v', 'new_v_w_out': 'new_v', 'new_v_final_gain': 'new_v'}


def _forward(args):
    return _fwd_reference(*[args[k] for k in FWD_PARAMS])


def _output_shape():
    out = _jax.eval_shape(lambda: _forward(_fwd_setup_inputs(0)))
    return out.shape, out.dtype

N_MICROBATCH = 1
ADAM_LR = 0.001
ADAM_B1 = 0.9
ADAM_B2 = 0.999
ADAM_EPS = 1e-08
ADAM_WD = 0.01
ADAM_STEP = 10
PER_EXAMPLE_BATCH_AXIS = {'x': 0, 'loss_target': 0}
SHARED_INPUTS = []
_WEIGHT_DTYPES = {'norm_gain': _jnp.float32, 'w_in': _jnp.float32, 'rel_bias': _jnp.float32, 'pool_w': _jnp.float32, 'pool_scale': _jnp.float32, 'w_out_attn': _jnp.float32, 'w_out_pool': _jnp.float32, 'gate_bias': _jnp.float32, 'w_out': _jnp.float32, 'final_gain': _jnp.float32}
MOMENT_SCALE = {'norm_gain': 1.935349e-02, 'w_in': 8.540839e-03, 'rel_bias': 2.127280e-03, 'pool_w': 1.791851e-02, 'pool_scale': 1.872287e-02, 'w_out_attn': 2.313329e-03, 'w_out_pool': 1.265627e-02, 'gate_bias': 3.572739e-03, 'w_out': 1.264218e-02, 'final_gain': 8.015111e+00}


def _to_microbatches(a, axis):
    t = _jnp.moveaxis(a, axis, 0)
    t = t.reshape((N_MICROBATCH, t.shape[0] // N_MICROBATCH) + t.shape[1:])
    return _jnp.moveaxis(t, 1, axis + 1)


def setup_inputs(seed: int = 0) -> dict:
    inp = _fwd_setup_inputs(seed)
    key = _jax.random.fold_in(_jax.random.key(seed), 7919)
    shape, _ = _output_shape()
    out = dict(inp)
    out["loss_target"] = _jax.random.normal(_jax.random.fold_in(key, 0), shape, _jnp.float32)
    for i, name in enumerate(TWIN_WEIGHTS):
        w = inp[name].astype(_jnp.float32)
        if MOMENT_SCALE is None:
            s = _jnp.sqrt(_jnp.mean(_jnp.square(w)) + 1e-30)
        else:
            s = MOMENT_SCALE[name]
        km, kv = _jax.random.split(_jax.random.fold_in(key, i + 1))
        out[name] = w
        out["m_" + name] = s * _jax.random.normal(km, w.shape, _jnp.float32)
        out["v_" + name] = (s * s) * _jax.random.uniform(kv, w.shape, _jnp.float32, 0.5, 1.5)
    if N_MICROBATCH > 1:
        for name, axis in PER_EXAMPLE_BATCH_AXIS.items():
            out[name] = _to_microbatches(out[name], axis)
    return {'x': out['x'], 'norm_gain': out['norm_gain'], 'w_in': out['w_in'], 'rel_bias': out['rel_bias'], 'pool_w': out['pool_w'], 'pool_scale': out['pool_scale'], 'w_out_attn': out['w_out_attn'], 'w_out_pool': out['w_out_pool'], 'gate_bias': out['gate_bias'], 'w_out': out['w_out'], 'final_gain': out['final_gain'], 'loss_target': out['loss_target'], 'm_norm_gain': out['m_norm_gain'], 'm_w_in': out['m_w_in'], 'm_rel_bias': out['m_rel_bias'], 'm_pool_w': out['m_pool_w'], 'm_pool_scale': out['m_pool_scale'], 'm_w_out_attn': out['m_w_out_attn'], 'm_w_out_pool': out['m_w_out_pool'], 'm_gate_bias': out['m_gate_bias'], 'm_w_out': out['m_w_out'], 'm_final_gain': out['m_final_gain'], 'v_norm_gain': out['v_norm_gain'], 'v_w_in': out['v_w_in'], 'v_rel_bias': out['v_rel_bias'], 'v_pool_w': out['v_pool_w'], 'v_pool_scale': out['v_pool_scale'], 'v_w_out_attn': out['v_w_out_attn'], 'v_w_out_pool': out['v_w_out_pool'], 'v_gate_bias': out['v_gate_bias'], 'v_w_out': out['v_w_out'], 'v_final_gain': out['v_final_gain']}


def _loss(weights, diff, rest, loss_target):
    with _jax.named_scope("forward"):
        args = {**rest, TWIN_DIFF_INPUT: diff, **{k: w.astype(_WEIGHT_DTYPES[k]) for k, w in weights.items()}}
        y = _forward(args)
    with _jax.named_scope("loss_head"):
        err = _jnp.square(y.astype(_jnp.float32) - loss_target)
        return 0.5 * _jnp.sum(_jnp.mean(err, axis=-1)) if err.ndim else 0.5 * err


def _adamw(w, g, m, v):
    m = ADAM_B1 * m + (1.0 - ADAM_B1) * g
    v = ADAM_B2 * v + (1.0 - ADAM_B2) * _jnp.square(g)
    m_hat = m / (1.0 - ADAM_B1 ** ADAM_STEP)
    v_hat = v / (1.0 - ADAM_B2 ** ADAM_STEP)
    delta = -ADAM_LR * (m_hat / (_jnp.sqrt(v_hat) + ADAM_EPS) + ADAM_WD * w)
    return delta, m, v


def reference(x, norm_gain, w_in, rel_bias, pool_w, pool_scale, w_out_attn, w_out_pool, gate_bias, w_out, final_gain, loss_target, m_norm_gain, m_w_in, m_rel_bias, m_pool_w, m_pool_scale, m_w_out_attn, m_w_out_pool, m_gate_bias, m_w_out, m_final_gain, v_norm_gain, v_w_in, v_rel_bias, v_pool_w, v_pool_scale, v_w_out_attn, v_w_out_pool, v_gate_bias, v_w_out, v_final_gain):
    given = dict(x=x, norm_gain=norm_gain, w_in=w_in, rel_bias=rel_bias, pool_w=pool_w, pool_scale=pool_scale, w_out_attn=w_out_attn, w_out_pool=w_out_pool, gate_bias=gate_bias, w_out=w_out, final_gain=final_gain, loss_target=loss_target, m_norm_gain=m_norm_gain, m_w_in=m_w_in, m_rel_bias=m_rel_bias, m_pool_w=m_pool_w, m_pool_scale=m_pool_scale, m_w_out_attn=m_w_out_attn, m_w_out_pool=m_w_out_pool, m_gate_bias=m_gate_bias, m_w_out=m_w_out, m_final_gain=m_final_gain, v_norm_gain=v_norm_gain, v_w_in=v_w_in, v_rel_bias=v_rel_bias, v_pool_w=v_pool_w, v_pool_scale=v_pool_scale, v_w_out_attn=v_w_out_attn, v_w_out_pool=v_w_out_pool, v_gate_bias=v_gate_bias, v_w_out=v_w_out, v_final_gain=v_final_gain)
    weights = {n: given[n] for n in TWIN_WEIGHTS}
    shared = {n: given[n] for n in SHARED_INPUTS}
    per_example = {n: given[n] for n in ['x']}
    grad_fn = _jax.value_and_grad(_loss, argnums=(0, 1))

    def one_microbatch(ex, loss_target):
        ex = dict(ex)
        diff = ex.pop(TWIN_DIFF_INPUT)
        return grad_fn(weights, diff, {**shared, **ex}, loss_target)

    if N_MICROBATCH == 1:
        loss, (grad_w, grad_x) = one_microbatch(per_example, given["loss_target"])
    else:
        def body(carry, xs):
            loss_sum, grad_sum = carry
            l_k, (gw_k, gx_k) = one_microbatch(xs[0], xs[1])
            with _jax.named_scope("update"):
                return (loss_sum + l_k, _jax.tree.map(_jnp.add, grad_sum, gw_k)), gx_k

        init = (_jnp.zeros((), _jnp.float32), _jax.tree.map(_jnp.zeros_like, weights))
        (loss, grad_w), grad_x = _jax.lax.scan(body, init, (per_example, given["loss_target"]))
    with _jax.named_scope("update"):
        delta_w, new_m, new_v = {}, {}, {}
        for n in TWIN_WEIGHTS:
            delta_w[n], new_m[n], new_v[n] = _adamw(weights[n], grad_w[n], given["m_" + n], given["v_" + n])
    return (loss, grad_x, *[grad_w[n] for n in TWIN_WEIGHTS], *[delta_w[n] for n in TWIN_WEIGHTS],
            *[new_m[n] for n in TWIN_WEIGHTS], *[new_v[n] for n in TWIN_WEIGHTS])
```

```python
import functools

import jax
import jax.numpy as jnp
from jax import lax
from jax.experimental import pallas as pl
from jax.experimental.pallas import tpu as pltpu

F32 = jnp.float32
BF = jnp.bfloat16
MESH = pl.DeviceIdType.MESH

N_DEV = 8
CHUNK = 64
N_LEFT_CHUNKS = 8
HEAD_DIM = 128
MAX_REL = 128
N_REL = 2 * MAX_REL + 1
N_REL_PAD = 384
POOL_WINDOWS = (2, 4, 8, 16)
HALO = 16
EPS = 1e-6
ADAM_LR = 0.001
ADAM_B1 = 0.9
ADAM_B2 = 0.999
ADAM_EPS = 1e-08
ADAM_WD = 0.01
ADAM_STEP = 10
NEG = -1e30
LANES = 128
TQ = N_LEFT_CHUNKS * CHUNK
TK = 2 * TQ
SKEW = 2 * TK
VMEM_LIMIT = 52 * 1024 * 1024

NN = (((1,), (0,)), ((), ()))
NT = (((1,), (1,)), ((), ()))
TN = (((0,), (0,)), ((), ()))


def _params(n_grid):
    return pltpu.CompilerParams(dimension_semantics=("arbitrary",) * n_grid, vmem_limit_bytes=VMEM_LIMIT)


def _sig(z):
    return 1.0 / (1.0 + jnp.exp(-z))


def _silu_and_grad(z):
    s = _sig(z)
    return z * s, s * (1.0 + z * (1.0 - s))


def _tile(n, pref):
    t = min(n, pref)
    assert n % t == 0, (n, pref)
    return t


def _mm(name, grid, ins, outs, dims, epi, aliases=None):
    n_in, n_out = len(ins), len(outs)

    def kern(*refs):
        acc = lax.dot_general(refs[0][...], refs[1][...], dims, preferred_element_type=F32)
        epi(acc, refs[2:n_in], refs[n_in:n_in + n_out])

    return pl.pallas_call(
        kern, grid=grid, in_specs=[s for _, s in ins], out_specs=[s for _, s in outs],
        out_shape=[o for o, _ in outs], name=name, compiler_params=_params(len(grid)),
        input_output_aliases=aliases or {},
    )(*[a for a, _ in ins])


def _sds(shape, dtype):
    return jax.ShapeDtypeStruct(shape, dtype)


def _rms_fwd(x, g):
    s, d = x.shape
    tr = _tile(s, 256)

    def kern(x_ref, g_ref, o_ref):
        xv = x_ref[...]
        r = lax.rsqrt(jnp.mean(xv * xv, axis=-1, keepdims=True) + EPS)
        o_ref[...] = (xv * r * g_ref[...]).astype(BF)

    return pl.pallas_call(
        kern, grid=(s // tr,),
        in_specs=[pl.BlockSpec((tr, d), lambda i: (i, 0)), pl.BlockSpec((1, d), lambda i: (0, 0))],
        out_specs=pl.BlockSpec((tr, d), lambda i: (i, 0)), out_shape=_sds((s, d), BF),
        name="rms_fwd", compiler_params=_params(1))(x, g)


def _final_norm(x2, target, g):
    s, d = x2.shape
    tr = _tile(s, 128)

    def kern(x_ref, t_ref, g_ref, dx_ref, dxb_ref, dg_ref, loss_ref):
        i = pl.program_id(0)
        xv = x_ref[...]
        gv = g_ref[...]
        r = lax.rsqrt(jnp.mean(xv * xv, axis=-1, keepdims=True) + EPS)
        xhat = xv * r
        err = xhat * gv - t_ref[...]
        dy = err * (1.0 / d)
        gy = dy * gv
        dx = r * (gy - xhat * jnp.mean(gy * xhat, axis=-1, keepdims=True))
        dx_ref[...] = dx
        dxb_ref[...] = dx.astype(BF)
        dg = jnp.sum(dy * xhat, axis=0, keepdims=True)
        ls = jnp.broadcast_to(0.5 * jnp.sum(jnp.mean(err * err, axis=-1, keepdims=True)), (1, LANES))

        @pl.when(i == 0)
        def _():
            dg_ref[...] = dg
            loss_ref[...] = ls

        @pl.when(i > 0)
        def _():
            dg_ref[...] += dg
            loss_ref[...] += ls

    row = pl.BlockSpec((tr, d), lambda i: (i, 0))
    vec = pl.BlockSpec((1, d), lambda i: (0, 0))
    return pl.pallas_call(
        kern, grid=(s // tr,), in_specs=[row, row, vec],
        out_specs=[row, row, vec, pl.BlockSpec((1, LANES), lambda i: (0, 0))],
        out_shape=[_sds((s, d), F32), _sds((s, d), BF), _sds((1, d), F32), _sds((1, LANES), F32)],
        name="final_norm", compiler_params=_params(1))(x2, target, g)


def _rms_bwd(x, dh, dx2, g):
    s, d = x.shape
    tr = _tile(s, 128)

    def kern(x_ref, dh_ref, dx2_ref, g_ref, dx_ref, dg_ref):
        i = pl.program_id(0)
        xv = x_ref[...]
        r = lax.rsqrt(jnp.mean(xv * xv, axis=-1, keepdims=True) + EPS)
        xhat = xv * r
        dhv = dh_ref[...]
        gh = dhv * g_ref[...]
        dx_ref[...] = dx2_ref[...] + r * (gh - xhat * jnp.mean(gh * xhat, axis=-1, keepdims=True))
        dg = jnp.sum(dhv * xhat, axis=0, keepdims=True)

        @pl.when(i == 0)
        def _():
            dg_ref[...] = dg

        @pl.when(i > 0)
        def _():
            dg_ref[...] += dg

    row = pl.BlockSpec((tr, d), lambda i: (i, 0))
    vec = pl.BlockSpec((1, d), lambda i: (0, 0))
    return pl.pallas_call(
        kern, grid=(s // tr,), in_specs=[row, row, row, vec], out_specs=[row, vec],
        out_shape=[_sds((s, d), F32), _sds((1, d), F32)],
        name="rms_bwd", compiler_params=_params(1))(x, dh, dx2, g)


def _rel_index(j, backward):
    if backward:
        rel = 2 * TQ - 1 - j
    else:
        rel = TQ - jnp.where(j < TK, j, j - SKEW)
    return jnp.clip(rel, -MAX_REL, MAX_REL) + MAX_REL


def _bias_rows(rel_bias_pad):
    h = rel_bias_pad.shape[0]

    def kern(rb_ref, o_ref):
        j = lax.broadcasted_iota(jnp.int32, (N_REL_PAD, SKEW), 1)
        k = lax.broadcasted_iota(jnp.int32, (N_REL_PAD, SKEW), 0)
        onehot = (_rel_index(j, False) == k).astype(F32)
        o_ref[...] = jnp.dot(rb_ref[...], onehot, preferred_element_type=F32, precision=lax.Precision.HIGHEST)

    return pl.pallas_call(kern, out_shape=_sds((h, SKEW), F32), name="bias_rows")(rel_bias_pad)


def _bias_grad(ddiag):
    h = ddiag.shape[0]

    def kern(d_ref, o_ref):
        j = lax.broadcasted_iota(jnp.int32, (N_REL_PAD, SKEW), 1)
        k = lax.broadcasted_iota(jnp.int32, (N_REL_PAD, SKEW), 0)
        onehot = ((_rel_index(j, True) == k) & (j < TQ + TK - 1)).astype(F32)
        o_ref[...] = lax.dot_general(d_ref[...], onehot, NT, preferred_element_type=F32,
                                     precision=lax.Precision.HIGHEST)

    return pl.pallas_call(kern, out_shape=_sds((h, N_REL_PAD), F32), name="bias_grad")(ddiag)


def _bias_tile(row):
    t = pltpu.roll(jnp.broadcast_to(row, (TQ, SKEW)), 0, 1, stride=1, stride_axis=0)[:, :TK]
    r = lax.broadcasted_iota(jnp.int32, (TQ, TK), 0) // CHUNK
    c = lax.broadcasted_iota(jnp.int32, (TQ, TK), 1) // CHUNK
    dist = N_LEFT_CHUNKS + r - c
    return jnp.where((dist >= 0) & (dist <= N_LEFT_CHUNKS), t, NEG)


def _scores(q, kcat, tile, first):
    s = lax.dot_general(q, kcat, NT, preferred_element_type=F32) * (HEAD_DIM ** -0.5) + tile
    col = lax.broadcasted_iota(jnp.int32, (TQ, TK), 1)
    s = jnp.where(first & (col < TQ), NEG, s)
    m = jnp.max(s, axis=1, keepdims=True)
    p = jnp.exp(s - m)
    return p, jnp.sum(p, axis=1, keepdims=True)


def _attn_fwd(proj, base, a_width):
    s_len = proj.shape[0]
    heads = a_width // HEAD_DIM
    nq = s_len // TQ
    kb, vb, zb = heads, 2 * heads, 3 * heads

    def kern(q_ref, kp_ref, kc_ref, vp_ref, vc_ref, z_ref, base_ref, att_ref, ya_ref, tile_ref):
        i = pl.program_id(1)

        @pl.when(i == 0)
        def _():
            tile_ref[...] = _bias_tile(base_ref[...])

        kcat = jnp.concatenate([kp_ref[...], kc_ref[...]], axis=0)
        vcat = jnp.concatenate([vp_ref[...], vc_ref[...]], axis=0)
        p, l = _scores(q_ref[...], kcat, tile_ref[...], i == 0)
        o = jnp.dot(p.astype(BF), vcat, preferred_element_type=F32) / l
        att_ref[...] = o.astype(BF)
        z = z_ref[...].astype(F32)
        ya_ref[...] = (o * (z * _sig(z))).astype(BF)

    blk = lambda off: pl.BlockSpec((TQ, HEAD_DIM), lambda h, i: (i, off + h))
    prev = lambda off: pl.BlockSpec((TQ, HEAD_DIM), lambda h, i: (jnp.maximum(i - 1, 0), off + h))
    out = pl.BlockSpec((TQ, HEAD_DIM), lambda h, i: (i, h))
    return pl.pallas_call(
        kern, grid=(heads, nq),
        in_specs=[blk(0), prev(kb), blk(kb), prev(vb), blk(vb), blk(zb),
                  pl.BlockSpec((None, 1, SKEW), lambda h, i: (h, 0, 0))],
        out_specs=[out, out], out_shape=[_sds((s_len, a_width), BF)] * 2,
        scratch_shapes=[pltpu.VMEM((TQ, TK), F32)],
        name="attn_fwd", compiler_params=_params(2))(proj, proj, proj, proj, proj, proj, base)


def _attn_bwd(proj, datt, base, a_width):
    s_len = proj.shape[0]
    heads = a_width // HEAD_DIM
    nq = s_len // TQ
    kb, vb = heads, 2 * heads
    scale = HEAD_DIM ** -0.5

    def kern(q_ref, kp_ref, kc_ref, vp_ref, vc_ref, do_ref, base_ref,
             dq_ref, dk_ref, dv_ref, dd_ref, tile_ref, dsacc_ref, ak_ref, av_ref):
        i = pl.program_id(1)

        @pl.when(i == 0)
        def _():
            tile_ref[...] = _bias_tile(base_ref[...])
            dsacc_ref[...] = jnp.zeros_like(dsacc_ref)
            ak_ref[...] = jnp.zeros_like(ak_ref)
            av_ref[...] = jnp.zeros_like(av_ref)

        @pl.when(i < nq)
        def _():
            q = q_ref[...]
            do = do_ref[...]
            kcat = jnp.concatenate([kp_ref[...], kc_ref[...]], axis=0)
            vcat = jnp.concatenate([vp_ref[...], vc_ref[...]], axis=0)
            p, l = _scores(q, kcat, tile_ref[...], i == 0)
            p = p / l
            dp = lax.dot_general(do, vcat, NT, preferred_element_type=F32)
            ds = p * (dp - jnp.sum(p * dp, axis=1, keepdims=True))
            dsacc_ref[...] += ds
            dsb = ds.astype(BF)
            dq_ref[...] = (jnp.dot(dsb, kcat, preferred_element_type=F32) * scale).astype(BF)
            dkc = lax.dot_general(dsb, q, TN, preferred_element_type=F32) * scale
            dvc = lax.dot_general(p.astype(BF), do, TN, preferred_element_type=F32)
            dk_ref[...] = (ak_ref[...] + dkc[:TQ]).astype(BF)
            dv_ref[...] = (av_ref[...] + dvc[:TQ]).astype(BF)
            ak_ref[...] = dkc[TQ:]
            av_ref[...] = dvc[TQ:]

        @pl.when(i == nq)
        def _():
            dk_ref[...] = ak_ref[...].astype(BF)
            dv_ref[...] = av_ref[...].astype(BF)
            acc = dsacc_ref[...]
            rr = lax.broadcasted_iota(jnp.int32, (TQ, TQ), 0)
            cc = lax.broadcasted_iota(jnp.int32, (TQ, TQ), 1)
            flip = (rr + cc == TQ - 1).astype(BF)
            hi = acc.astype(BF)
            lo = (acc - hi.astype(F32)).astype(BF)
            rev = jnp.dot(flip, hi, preferred_element_type=F32) + jnp.dot(flip, lo, preferred_element_type=F32)
            wide = jnp.concatenate([rev, jnp.zeros((TQ, SKEW - TK), F32)], axis=1)
            dd_ref[...] = jnp.sum(pltpu.roll(wide, 0, 1, stride=1, stride_axis=0), axis=0, keepdims=True)

    last = nq - 1
    cur = lambda off: pl.BlockSpec((TQ, HEAD_DIM), lambda h, i: (jnp.minimum(i, last), off + h))
    prev = lambda off: pl.BlockSpec((TQ, HEAD_DIM), lambda h, i: (jnp.maximum(jnp.minimum(i, last) - 1, 0), off + h))
    done = pl.BlockSpec((TQ, HEAD_DIM), lambda h, i: (jnp.maximum(i - 1, 0), h))
    return pl.pallas_call(
        kern, grid=(heads, nq + 1),
        in_specs=[cur(0), prev(kb), cur(kb), prev(vb), cur(vb), cur(0),
                  pl.BlockSpec((None, 1, SKEW), lambda h, i: (h, 0, 0))],
        out_specs=[cur(0), done, done, pl.BlockSpec((None, 1, SKEW), lambda h, i: (h, 0, 0))],
        out_shape=[_sds((s_len, a_width), BF)] * 3 + [_sds((heads, 1, SKEW), F32)],
        scratch_shapes=[pltpu.VMEM((TQ, TK), F32), pltpu.VMEM((TQ, TK), F32),
                        pltpu.VMEM((TQ, HEAD_DIM), F32), pltpu.VMEM((TQ, HEAD_DIM), F32)],
        name="attn_bwd", compiler_params=_params(2))(proj, proj, proj, proj, proj, datt, base)


def _pool_fwd(proj, pool_w, pool_scale, p_width, u_blk, z_blk):
    s_len = proj.shape[0]
    cg = p_width // len(POOL_WINDOWS)
    tt = _tile(s_len, 512)

    def kern(up_ref, uc_ref, z_ref, pw_ref, sc_ref, d_ref, y_ref, yp_ref):
        t = pl.program_id(0)
        row = lax.broadcasted_iota(jnp.int32, (tt, 1), 0) + t * tt
        for g, w in enumerate(POOL_WINDOWS):
            cs = slice(g * cg, (g + 1) * cg)
            prev = jnp.where(t == 0, 0.0, up_ref[:, cs].astype(F32))
            cur = uc_ref[:, cs].astype(F32)
            ws = jnp.concatenate([prev, cur], axis=0)
            sh = 1
            while sh < w:
                ws = ws + pltpu.roll(ws, sh, 0)
                sh *= 2
            cnt = jnp.minimum(row + 1, w).astype(F32)
            db = (ws[HALO:, :] / cnt - cur).astype(BF)
            y = jnp.dot(db, pw_ref[g], preferred_element_type=F32)
            d_ref[:, cs] = db
            y_ref[:, cs] = y.astype(BF)
            z = z_ref[:, cs].astype(F32)
            yp_ref[:, cs] = (y * sc_ref[:, cs] * (z * _sig(z))).astype(BF)

    full = pl.BlockSpec((tt, p_width), lambda t: (t, 0))
    return pl.pallas_call(
        kern, grid=(s_len // tt,),
        in_specs=[pl.BlockSpec((HALO, p_width), lambda t: (jnp.maximum(t * (tt // HALO) - 1, 0), u_blk)),
                  pl.BlockSpec((tt, p_width), lambda t: (t, u_blk)),
                  pl.BlockSpec((tt, p_width), lambda t: (t, z_blk)),
                  pl.BlockSpec((len(POOL_WINDOWS), cg, cg), lambda t: (0, 0, 0)),
                  pl.BlockSpec((1, p_width), lambda t: (0, 0))],
        out_specs=[full, full, full], out_shape=[_sds((s_len, p_width), BF)] * 3,
        name="pool_fwd", compiler_params=_params(1))(proj, proj, proj, pool_w, pool_scale)


def _pool_bwd(dy, dmean, pool_w):
    s_len, p_width = dy.shape
    ng = len(POOL_WINDOWS)
    cg = p_width // ng
    tt = _tile(s_len, 512)
    nt = s_len // tt

    def kern(dyc_ref, dyn_ref, d_ref, pw_ref, du_ref, dpw_ref):
        t = pl.program_id(0)

        @pl.when(t == 0)
        def _():
            dpw_ref[...] = jnp.zeros_like(dpw_ref)

        row = lax.broadcasted_iota(jnp.int32, (tt + HALO, 1), 0) + t * tt
        for g, w in enumerate(POOL_WINDOWS):
            cs = slice(g * cg, (g + 1) * cg)
            dyc = dyc_ref[:, cs]
            ddc = lax.dot_general(dyc, pw_ref[g], NT, preferred_element_type=F32)
            ddn = lax.dot_general(dyn_ref[:, cs], pw_ref[g], NT, preferred_element_type=F32)
            ddn = jnp.where(t == nt - 1, 0.0, ddn)
            cnt = jnp.minimum(row + 1, w).astype(F32)
            ws = jnp.concatenate([ddc, ddn], axis=0) / cnt
            sh = 1
            while sh < w:
                ws = ws + pltpu.roll(ws, tt + HALO - sh, 0)
                sh *= 2
            du_ref[:, cs] = (ws[:tt, :] - ddc).astype(BF)
            dpw_ref[g] += lax.dot_general(d_ref[:, cs], dyc, TN, preferred_element_type=F32)

    full = pl.BlockSpec((tt, p_width), lambda t: (t, 0))
    pw_spec = pl.BlockSpec((ng, cg, cg), lambda t: (0, 0, 0))
    return pl.pallas_call(
        kern, grid=(nt,),
        in_specs=[full,
                  pl.BlockSpec((HALO, p_width), lambda t: (jnp.minimum((t + 1) * (tt // HALO), s_len // HALO - 1), 0)),
                  full, pw_spec],
        out_specs=[full, pw_spec], out_shape=[_sds((s_len, p_width), BF), _sds((ng, cg, cg), F32)],
        name="pool_bwd", compiler_params=_params(1))(dy, dy, dmean, pool_w)


def _adam(g, w_ref, m_ref, v_ref, g_out, d_out, m_out, v_out):
    m = ADAM_B1 * m_ref[...] + (1.0 - ADAM_B1) * g
    v = ADAM_B2 * v_ref[...] + (1.0 - ADAM_B2) * (g * g)
    m_hat = m / (1.0 - ADAM_B1 ** ADAM_STEP)
    v_hat = v / (1.0 - ADAM_B2 ** ADAM_STEP)
    g_out[...] = g
    d_out[...] = -ADAM_LR * (m_hat / (jnp.sqrt(v_hat) + ADAM_EPS) + ADAM_WD * w_ref[...])
    m_out[...] = m
    v_out[...] = v


def _adamw_shard(name, parts, w, m, v, row_off):
    rw, cw = w.shape
    sw = parts.shape[2]
    tr = _tile(rw, 512)
    assert row_off % tr == 0 and cw % sw == 0

    def kern(b_ref, w_ref, m_ref, v_ref, g_out, d_out, m_out, v_out):
        b = b_ref[...].astype(F32)
        _adam(((b[0] + b[1]) + b[2]) + b[3], w_ref, m_ref, v_ref, g_out, d_out, m_out, v_out)

    blk = pl.BlockSpec((tr, sw), lambda ct, i: (i, ct))
    return pl.pallas_call(
        kern, grid=(cw // sw, rw // tr),
        in_specs=[pl.BlockSpec((4, tr, sw), lambda ct, i: (0, (row_off + ct * rw) // tr + i, 0)), blk, blk, blk],
        out_specs=[blk] * 4, out_shape=[_sds((rw, cw), F32)] * 4,
        name=name, compiler_params=_params(2))(parts, w, m, v)


def _position():
    return lax.axis_index("x"), lax.axis_index("y"), lax.axis_index("c")


def _all_gather(shards):
    n = len(shards)

    def kern(*refs):
        xs, outs = refs[:n], refs[n:2 * n]
        send_sems, recv_sems, local_sems = refs[2 * n:]
        x, y, c = _position()
        me, sibling = (x, y, c), (x, y, 1 - c)
        chips = [(1 - x, y), (x, 1 - y), (1 - x, 1 - y)]

        def copy(a, k, block, to, src=None):
            dst = outs[a].at[4 * block[0] + 2 * block[1] + block[2]]
            return pltpu.make_async_remote_copy(
                src_ref=dst if src is None else src, dst_ref=dst,
                send_sem=send_sems.at[7 * a + k], recv_sem=recv_sems.at[7 * a + k],
                device_id=to, device_id_type=MESH)

        mine = [pltpu.make_async_copy(xs[a], outs[a].at[4 * x + 2 * y + c], local_sems.at[a]) for a in range(n)]
        for cp in mine:
            cp.start()
        first = [copy(a, 0, me, sibling, src=xs[a]) for a in range(n)]
        first += [copy(a, 1 + j, me, (*chip, c), src=xs[a]) for j, chip in enumerate(chips) for a in range(n)]
        for cp in first:
            cp.start()
        passed = []
        for j, chip in enumerate(chips):
            for a in range(n):
                copy(a, 1 + j, (*chip, c), me).wait_recv()
                cp = copy(a, 4 + j, (*chip, c), sibling)
                cp.start()
                passed.append(cp)
        for a in range(n):
            copy(a, 0, sibling, me).wait_recv()
            for j, chip in enumerate(chips):
                copy(a, 4 + j, (*chip, 1 - c), me).wait_recv()
        for cp in first + passed:
            cp.wait_send()
        for cp in mine:
            cp.wait()

    hbm = pl.BlockSpec(memory_space=pl.ANY)
    return pl.pallas_call(
        kern, in_specs=[hbm] * n, out_specs=[hbm] * n,
        out_shape=[_sds((N_DEV,) + s.shape, s.dtype) for s in shards],
        scratch_shapes=[pltpu.SemaphoreType.DMA((7 * n,)), pltpu.SemaphoreType.DMA((7 * n,)),
                        pltpu.SemaphoreType.DMA((n,))],
        name="all_gather_weights")(*shards)


def _exchange_cores(slab):
    _, r, sw = slab.shape

    def kern(slab_ref, buf_ref, send_sems, recv_sems):
        x, y, c = _position()
        copies = [pltpu.make_async_remote_copy(
            src_ref=slab_ref.at[2 * k + (1 - c)], dst_ref=buf_ref.at[k],
            send_sem=send_sems.at[k], recv_sem=recv_sems.at[k],
            device_id=(x, y, 1 - c), device_id_type=MESH) for k in range(4)]
        for cp in copies:
            cp.start()
        for cp in copies:
            cp.wait()

    hbm = pl.BlockSpec(memory_space=pl.ANY)
    return pl.pallas_call(
        kern, in_specs=[hbm], out_specs=hbm, out_shape=_sds((4, r, sw), slab.dtype),
        scratch_shapes=[pltpu.SemaphoreType.DMA((4,)), pltpu.SemaphoreType.DMA((4,))],
        name="reduce_scatter_cores")(slab)


def _add_core_partials(slab, recv, core):
    _, r, sw = slab.shape
    tr = _tile(r, 3 * sw)

    def kern(c_ref, a_ref, b_ref, o_ref):
        o_ref[...] = (a_ref[...].astype(F32) + b_ref[...].astype(F32)).astype(BF)

    return pl.pallas_call(
        kern,
        grid_spec=pltpu.PrefetchScalarGridSpec(
            num_scalar_prefetch=1, grid=(4, r // tr),
            in_specs=[pl.BlockSpec((None, tr, sw), lambda k, i, c_ref: (2 * k + c_ref[0], i, 0)),
                      pl.BlockSpec((None, tr, sw), lambda k, i, c_ref: (k, i, 0))],
            out_specs=pl.BlockSpec((None, tr, sw), lambda k, i, c_ref: (k, i, 0))),
        out_shape=_sds((4, r, sw), BF), name="add_core_partials", compiler_params=_params(2))(core, slab, recv)


def _exchange_chips(part):
    _, r, sw = part.shape

    def kern(part_ref, buf_ref, send_sems, recv_sems, local_sem):
        x, y, c = _position()
        mine = 2 * x + y
        local = pltpu.make_async_copy(part_ref.at[mine], buf_ref.at[mine], local_sem)
        local.start()
        chips = [(1 - x, y), (x, 1 - y), (1 - x, 1 - y)]
        copies = [pltpu.make_async_remote_copy(
            src_ref=part_ref.at[2 * px + py], dst_ref=buf_ref.at[mine],
            send_sem=send_sems.at[j], recv_sem=recv_sems.at[j],
            device_id=(px, py, c), device_id_type=MESH) for j, (px, py) in enumerate(chips)]
        for cp in copies:
            cp.start()
        for cp in copies:
            cp.wait()
        local.wait()

    hbm = pl.BlockSpec(memory_space=pl.ANY)
    return pl.pallas_call(
        kern, in_specs=[hbm], out_specs=hbm, out_shape=_sds((4, r, sw), part.dtype),
        scratch_shapes=[pltpu.SemaphoreType.DMA((3,)), pltpu.SemaphoreType.DMA((3,)), pltpu.SemaphoreType.DMA],
        name="reduce_scatter_chips")(part)


def _small_allreduce_adamw(partial, w, m, v):
    nr = partial.shape[0]

    def kern(p_ref, w_ref, m_ref, v_ref, g_out, d_out, m_out, v_out, gath_ref, send_sems, recv_sems):
        x, y, c = _position()
        me = 4 * x + 2 * y + c
        gath_ref[me] = p_ref[...]
        copies = []
        for mask in range(1, N_DEV):
            peer = (x ^ (mask >> 2), y ^ ((mask >> 1) & 1), c ^ (mask & 1))
            copies.append(pltpu.make_async_remote_copy(
                src_ref=p_ref, dst_ref=gath_ref.at[me],
                send_sem=send_sems.at[mask - 1], recv_sem=recv_sems.at[mask - 1],
                device_id=peer, device_id_type=MESH))
        for cp in copies:
            cp.start()
        for cp in copies:
            cp.wait()
        tot = gath_ref[0]
        for k in range(1, N_DEV):
            tot = tot + gath_ref[k]
        _adam(tot, w_ref, m_ref, v_ref, g_out, d_out, m_out, v_out)

    vmem = pl.BlockSpec(memory_space=pltpu.VMEM)
    return pl.pallas_call(
        kern, in_specs=[vmem] * 4, out_specs=[vmem] * 4, out_shape=[_sds((nr, LANES), F32)] * 4,
        scratch_shapes=[pltpu.VMEM((N_DEV, nr, LANES), F32),
                        pltpu.SemaphoreType.DMA((N_DEV - 1,)), pltpu.SemaphoreType.DMA((N_DEV - 1,))],
        name="small_allreduce_adamw")(partial, w, m, v)


def kernel(x, norm_gain, w_in, rel_bias, pool_w, pool_scale, w_out_attn, w_out_pool, gate_bias, w_out, final_gain, loss_target, m_norm_gain, m_w_in, m_rel_bias, m_pool_w, m_pool_scale, m_w_out_attn, m_w_out_pool, m_gate_bias, m_w_out, m_final_gain, v_norm_gain, v_w_in, v_rel_bias, v_pool_w, v_pool_scale, v_w_out_attn, v_w_out_pool, v_gate_bias, v_w_out, v_final_gain):
    _, s_len, d = x.shape
    a = w_out_attn.shape[0]
    p = w_out_pool.shape[0]
    heads = a // HEAD_DIM
    ng = len(POOL_WINDOWS)
    cg = p // ng
    sw = d // N_DEV
    n_in = w_in.shape[1] * N_DEV
    assert a == p and a + p == d and cg == sw and n_in == 5 * d and w_in.shape[1] == 5 * sw
    assert s_len % TQ == 0 and rel_bias.shape == (heads, N_REL)
    ct = _tile(a, 512)
    tm = _tile(s_len, 1024)
    x2d = x.reshape(s_len, d)
    tgt = loss_target.reshape(s_len, d)

    gathered = _all_gather([w_in.astype(BF), w_out_attn.astype(BF), w_out_pool.astype(BF),
                            w_out.astype(BF), pool_w.astype(BF), gate_bias])
    win_g = gathered[0]
    woa = gathered[1].transpose(1, 0, 2).reshape(a, d)
    wop = gathered[2].transpose(1, 0, 2).reshape(p, d)
    wo = gathered[3].reshape(d, d)
    pw = gathered[4].transpose(1, 0, 2, 3).reshape(ng, cg, cg)
    gb = gathered[5].transpose(1, 0, 2).reshape(2, d)
    g1 = norm_gain.reshape(1, d)
    g2 = final_gain.reshape(1, d)
    scale_row = pool_scale.reshape(1, p)

    hb = _rms_fwd(x2d, g1)
    tn = sw
    per = w_in.shape[1] // tn

    def store_bf16(acc, _, outs):
        outs[0][...] = acc.astype(BF)

    proj = _mm("proj", (s_len // tm, n_in // tn),
               [(hb, pl.BlockSpec((tm, d), lambda i, j: (i, 0))),
                (win_g, pl.BlockSpec((None, d, tn), lambda i, j: (j // per, 0, j % per)))],
               [(_sds((s_len, n_in), BF), pl.BlockSpec((tm, tn), lambda i, j: (i, j)))], NN, store_bf16)[0]

    rb_pad = jnp.pad(rel_bias, ((0, 0), (0, N_REL_PAD - N_REL)))
    base = _bias_rows(rb_pad).reshape(heads, 1, SKEW)
    att, ya = _attn_fwd(proj, base, a)
    u_blk, z_blk = 4 * a // p, 4 * a // p + 1
    dmean, ypre, yp = _pool_fwd(proj, pw, scale_row, p, u_blk, z_blk)

    ga_t, gp_t = (4 * a + 2 * p) // tn, (4 * a + 2 * p + d) // tn

    def gate_kernel(ya_ref, woa_ref, yp_ref, wop_ref, ga_ref, gp_ref, gb_ref, m_ref, a_ref, p_ref):
        am = jnp.dot(ya_ref[...], woa_ref[...], preferred_element_type=F32)
        pm = jnp.dot(yp_ref[...], wop_ref[...], preferred_element_type=F32)
        sa = _sig(ga_ref[...].astype(F32) + gb_ref[0:1, :])
        sp = _sig(gp_ref[...].astype(F32) + gb_ref[1:2, :])
        m_ref[...] = (sa * am + sp * pm).astype(BF)
        a_ref[...] = am.astype(BF)
        p_ref[...] = pm.astype(BF)

    tile_ij = pl.BlockSpec((tm, tn), lambda i, j: (i, j))
    merged, am, pm = pl.pallas_call(
        gate_kernel, grid=(s_len // tm, d // tn),
        in_specs=[pl.BlockSpec((tm, a), lambda i, j: (i, 0)), pl.BlockSpec((a, tn), lambda i, j: (0, j)),
                  pl.BlockSpec((tm, p), lambda i, j: (i, 0)), pl.BlockSpec((p, tn), lambda i, j: (0, j)),
                  pl.BlockSpec((tm, tn), lambda i, j: (i, ga_t + j)), pl.BlockSpec((tm, tn), lambda i, j: (i, gp_t + j)),
                  pl.BlockSpec((2, tn), lambda i, j: (0, j))],
        out_specs=[tile_ij] * 3, out_shape=[_sds((s_len, d), BF)] * 3,
        name="gate_merge", compiler_params=_params(2))(ya, woa, yp, wop, proj, proj, gb)

    def add_residual(acc, ex, outs):
        outs[0][...] = ex[0][...] + acc

    x2 = _mm("out_proj", (s_len // tm, d // tn),
             [(merged, pl.BlockSpec((tm, d), lambda i, j: (i, 0))), (wo, pl.BlockSpec((d, tn), lambda i, j: (0, j))),
              (x2d, tile_ij)],
             [(_sds((s_len, d), F32), tile_ij)], NN, add_residual)[0]

    dx2, dx2b, dg2, loss_part = _final_norm(x2, tgt, g2)

    tmb = _tile(s_len, 512)
    tile_ji = pl.BlockSpec((tmb, tn), lambda j, i: (i, j))

    def gate_bwd(dm, ex, outs):
        a_ref, p_ref, ga_ref, gp_ref, gb_ref = ex
        da_ref, dp_ref, dga_ref, dgp_ref, dgb_ref = outs
        i = pl.program_id(1)
        sa = _sig(ga_ref[...].astype(F32) + gb_ref[0:1, :])
        sp = _sig(gp_ref[...].astype(F32) + gb_ref[1:2, :])
        dga = dm * a_ref[...].astype(F32) * sa * (1.0 - sa)
        dgp = dm * p_ref[...].astype(F32) * sp * (1.0 - sp)
        da_ref[...] = (dm * sa).astype(BF)
        dp_ref[...] = (dm * sp).astype(BF)
        dga_ref[...] = dga.astype(BF)
        dgp_ref[...] = dgp.astype(BF)
        r = lax.broadcasted_iota(jnp.int32, (8, tn), 0)
        sums = jnp.where(r == 0, jnp.sum(dga, axis=0, keepdims=True),
                         jnp.where(r == 1, jnp.sum(dgp, axis=0, keepdims=True), 0.0))

        @pl.when(i == 0)
        def _():
            dgb_ref[...] = sums

        @pl.when(i > 0)
        def _():
            dgb_ref[...] += sums

    d_am, d_pm, dga, dgp, dgb8 = _mm(
        "gate_bwd", (d // tn, s_len // tmb),
        [(dx2b, pl.BlockSpec((tmb, d), lambda j, i: (i, 0))), (wo, pl.BlockSpec((tn, d), lambda j, i: (j, 0))),
         (am, tile_ji), (pm, tile_ji),
         (proj, pl.BlockSpec((tmb, tn), lambda j, i: (i, ga_t + j))),
         (proj, pl.BlockSpec((tmb, tn), lambda j, i: (i, gp_t + j))),
         (gb, pl.BlockSpec((2, tn), lambda j, i: (0, j)))],
        [(_sds((s_len, d), BF), tile_ji)] * 4 + [(_sds((8, d), F32), pl.BlockSpec((8, tn), lambda j, i: (0, j)))],
        NT, gate_bwd)

    za_t = 3 * a // tn

    def attn_gate_bwd(dya, ex, outs):
        silu, dsilu = _silu_and_grad(ex[0][...].astype(F32))
        outs[0][...] = (dya * silu).astype(BF)
        outs[1][...] = (dya * ex[1][...].astype(F32) * dsilu).astype(BF)

    datt, dza = _mm(
        "attn_gate_bwd", (s_len // tm, a // tn),
        [(d_am, pl.BlockSpec((tm, d), lambda i, j: (i, 0))), (woa, pl.BlockSpec((tn, d), lambda i, j: (j, 0))),
         (proj, pl.BlockSpec((tm, tn), lambda i, j: (i, za_t + j))), (att, tile_ij)],
        [(_sds((s_len, a), BF), tile_ij)] * 2, NT, attn_gate_bwd)

    zp_t = (4 * a + p) // tn

    def pool_gate_bwd(dyp, ex, outs):
        z_ref, y_ref, sc_ref = ex
        dzp_ref, dy_ref, dps_ref = outs
        i = pl.program_id(1)
        silu, dsilu = _silu_and_grad(z_ref[...].astype(F32))
        y = y_ref[...].astype(F32)
        sc = sc_ref[...]
        dyp0 = dyp * silu
        dzp_ref[...] = (dyp * (y * sc) * dsilu).astype(BF)
        dy_ref[...] = (dyp0 * sc).astype(BF)
        dps = jnp.sum(dyp0 * y, axis=0, keepdims=True)

        @pl.when(i == 0)
        def _():
            dps_ref[...] = dps

        @pl.when(i > 0)
        def _():
            dps_ref[...] += dps

    dzp, dy_pool, dps = _mm(
        "pool_gate_bwd", (p // tn, s_len // tmb),
        [(d_pm, pl.BlockSpec((tmb, d), lambda j, i: (i, 0))), (wop, pl.BlockSpec((tn, d), lambda j, i: (j, 0))),
         (proj, pl.BlockSpec((tmb, tn), lambda j, i: (i, zp_t + j))), (ypre, tile_ji),
         (scale_row, pl.BlockSpec((1, tn), lambda j, i: (0, j)))],
        [(_sds((s_len, p), BF), tile_ji)] * 2 + [(_sds((1, p), F32), pl.BlockSpec((1, tn), lambda j, i: (0, j)))],
        NT, pool_gate_bwd)

    dq, dk, dv, ddiag = _attn_bwd(proj, datt, base, a)
    drb = _bias_grad(ddiag.reshape(heads, SKEW))
    du, dpw = _pool_bwd(dy_pool, dmean, pw)

    dproj = jnp.concatenate([dq, dk, dv, dza, du, dzp, dga, dgp], axis=1)

    tk = w_in.shape[1]
    tnh = _tile(d, 1024)

    def dh_kernel(a_ref, b_ref, o_ref, acc_ref):
        k = pl.program_id(2)
        part = lax.dot_general(a_ref[...], b_ref[...], NT, preferred_element_type=F32)

        @pl.when(k == 0)
        def _():
            acc_ref[...] = part

        @pl.when(k > 0)
        def _():
            acc_ref[...] += part

        @pl.when(k == N_DEV - 1)
        def _():
            o_ref[...] = acc_ref[...]

    dh = pl.pallas_call(
        dh_kernel, grid=(s_len // tm, d // tnh, N_DEV),
        in_specs=[pl.BlockSpec((tm, tk), lambda i, j, k: (i, k)),
                  pl.BlockSpec((None, tnh, tk), lambda i, j, k: (k, j, 0))],
        out_specs=pl.BlockSpec((tm, tnh), lambda i, j, k: (i, j)), out_shape=_sds((s_len, d), F32),
        scratch_shapes=[pltpu.VMEM((tm, tnh), F32)],
        name="dh", compiler_params=_params(3))(dproj, win_g)

    dx, dg1 = _rms_bwd(x2d, dh, dx2, g1)

    o_woa, o_wop, o_wo, o_pool = 5 * d, 5 * d + a, 6 * d, 7 * d
    r_slab = 7 * d + cg
    slab_shape = _sds((N_DEV, r_slab, sw), BF)
    hbm = pl.BlockSpec(memory_space=pl.ANY)
    tmw = _tile(a, 1024)

    def pack_small(dpw_ref, dgb_ref, o_ref):
        rows = cg // N_DEV
        for j in range(N_DEV):
            for g in range(ng):
                o_ref[j, g * rows:(g + 1) * rows, :] = dpw_ref[g, j * rows:(j + 1) * rows, :].astype(BF)
            o_ref[j, ng * rows:, :] = jnp.concatenate(
                [dgb_ref[:, j * sw:(j + 1) * sw], jnp.zeros((cg - ng * rows - 8, sw), F32)], axis=0).astype(BF)

    slab = pl.pallas_call(
        pack_small, grid=(1,),
        in_specs=[pl.BlockSpec((ng, cg, cg), lambda i: (0, 0, 0)), pl.BlockSpec((8, d), lambda i: (0, 0))],
        out_specs=pl.BlockSpec((N_DEV, cg, sw), lambda i: (0, o_pool // cg, 0)), out_shape=slab_shape,
        name="dw_small", compiler_params=_params(1))(dpw, dgb8)

    def into_slab(acc, _, outs):
        outs[0][...] = acc.astype(BF)

    def weight_grad(name, slab, lhs, rhs, grid, lhs_spec, rhs_spec, out_spec):
        return _mm(name, grid, [(lhs, lhs_spec), (rhs, rhs_spec), (slab, hbm)], [(slab_shape, out_spec)],
                   TN, into_slab, aliases={2: 0})[0]

    slab = weight_grad("dw_out", slab, merged, dx2b, (N_DEV, d // sw),
                       pl.BlockSpec((s_len, sw), lambda j, t: (0, j)), pl.BlockSpec((s_len, sw), lambda j, t: (0, t)),
                       pl.BlockSpec((None, sw, sw), lambda j, t: (j, o_wo // sw + t, 0)))
    slab = weight_grad("dw_out_attn", slab, ya, d_am, (a // tmw, N_DEV),
                       pl.BlockSpec((s_len, tmw), lambda i, j: (0, i)), pl.BlockSpec((s_len, sw), lambda i, j: (0, j)),
                       pl.BlockSpec((None, tmw, sw), lambda i, j: (j, o_woa // tmw + i, 0)))
    slab = weight_grad("dw_out_pool", slab, yp, d_pm, (p // tmw, N_DEV),
                       pl.BlockSpec((s_len, tmw), lambda i, j: (0, i)), pl.BlockSpec((s_len, sw), lambda i, j: (0, j)),
                       pl.BlockSpec((None, tmw, sw), lambda i, j: (j, o_wop // tmw + i, 0)))
    tmd = _tile(d, 1024)
    slab = weight_grad("dw_in", slab, hb, dproj, (d // tmd, n_in // sw),
                       pl.BlockSpec((s_len, tmd), lambda i, t: (0, i)), pl.BlockSpec((s_len, sw), lambda i, t: (0, t)),
                       pl.BlockSpec((None, tmd, sw), lambda i, t: (t // per, (t % per) * (d // tmd) + i, 0)))

    core = lax.axis_index("c").astype(jnp.int32).reshape(1)
    from_sibling = _exchange_cores(slab)
    chip_part = _add_core_partials(slab, from_sibling, core)
    parts = _exchange_chips(chip_part)

    g_win, d_win, m_win, v_win = _adamw_shard("adamw_w_in", parts, w_in, m_w_in, v_w_in, 0)
    g_woa, d_woa, m_woa, v_woa = _adamw_shard("adamw_w_out_attn", parts, w_out_attn, m_w_out_attn, v_w_out_attn, o_woa)
    g_wop, d_wop, m_wop, v_wop = _adamw_shard("adamw_w_out_pool", parts, w_out_pool, m_w_out_pool, v_w_out_pool, o_wop)
    g_wo, d_wo, m_wo, v_wo = _adamw_shard("adamw_w_out", parts, w_out, m_w_out, v_w_out, o_wo)
    flat = lambda t: t.reshape(cg // 2, sw)
    pool_out = _adamw_shard("adamw_pool_w", parts, flat(pool_w), flat(m_pool_w), flat(v_pool_w), o_pool)
    g_pw, d_pw, m_pw, v_pw = [t.reshape(pool_w.shape) for t in pool_out]
    pad16 = lambda t: jnp.pad(t, ((0, 14), (0, 0)))
    gb_out = _adamw_shard("adamw_gate_bias", parts, pad16(gate_bias), pad16(m_gate_bias), pad16(v_gate_bias),
                          o_pool + cg // 2)
    g_gb, d_gb, m_gb, v_gb = [t[:2] for t in gb_out]

    def pack(n_gain, f_gain, scale, rb, last):
        rows = [n_gain.reshape(-1, LANES), f_gain.reshape(-1, LANES), scale.reshape(-1, LANES),
                rb.reshape(-1, LANES), last]
        return jnp.concatenate(rows, axis=0)

    pad_rb = lambda t: jnp.pad(t, ((0, 0), (0, N_REL_PAD - N_REL)))
    zeros8 = jnp.zeros((8, LANES), F32)
    loss_rows = jnp.pad(loss_part, ((0, 7), (0, 0)))
    small = _small_allreduce_adamw(
        pack(dg1, dg2, dps, drb, loss_rows),
        pack(norm_gain, final_gain, pool_scale, pad_rb(rel_bias), zeros8),
        pack(m_norm_gain, m_final_gain, m_pool_scale, pad_rb(m_rel_bias), zeros8),
        pack(v_norm_gain, v_final_gain, v_pool_scale, pad_rb(v_rel_bias), zeros8))

    n1, n2, n3 = d // LANES, 2 * d // LANES, (2 * d + p) // LANES
    n4 = n3 + heads * N_REL_PAD // LANES

    def unpack(t):
        return (t[:n1].reshape(d), t[n1:n2].reshape(d), t[n2:n3].reshape(p),
                t[n3:n4].reshape(heads, N_REL_PAD)[:, :N_REL])

    (g_ng, g_fg, g_ps, g_rb), (d_ng, d_fg, d_ps, d_rb), (m_ng, m_fg, m_ps, m_rb), (v_ng, v_fg, v_ps, v_rb) = [
        unpack(t) for t in small]
    loss = small[0][n4, 0]

    return (loss, dx.reshape(x.shape),
            g_ng, g_win, g_rb, g_pw, g_ps, g_woa, g_wop, g_gb, g_wo, g_fg,
            d_ng, d_win, d_rb, d_pw, d_ps, d_woa, d_wop, d_gb, d_wo, d_fg,
            m_ng, m_win, m_rb, m_pw, m_ps, m_woa, m_wop, m_gb, m_wo, m_fg,
            v_ng, v_win, v_rb, v_pw, v_ps, v_woa, v_wop, v_gb, v_wo, v_fg)
```

```python
import jax
import jax.numpy as jnp
from jax import lax
from jax.experimental import pallas as pl
from jax.experimental.pallas import tpu as pltpu

F32 = jnp.float32
BF = jnp.bfloat16
MESH = pl.DeviceIdType.MESH

N_DEV = 8
CHUNK = 64
N_LEFT_CHUNKS = 8
HEAD_DIM = 128
MAX_REL = 128
N_REL = 2 * MAX_REL + 1
N_REL_PAD = 384
POOL_WINDOWS = (2, 4, 8, 16)
HALO = 16
EPS = 1e-6
ADAM_LR = 0.001
ADAM_B1 = 0.9
ADAM_B2 = 0.999
ADAM_EPS = 1e-08
ADAM_WD = 0.01
ADAM_STEP = 10
NEG = -1e30
LANES = 128
TQ = N_LEFT_CHUNKS * CHUNK
TK = 2 * TQ
SKEW = 2 * TK
VMEM_LIMIT = 52 * 1024 * 1024

NN = (((1,), (0,)), ((), ()))
NT = (((1,), (1,)), ((), ()))
TN = (((0,), (0,)), ((), ()))


def _params(n_grid):
    return pltpu.CompilerParams(dimension_semantics=("arbitrary",) * n_grid, vmem_limit_bytes=VMEM_LIMIT)


def _sig(z):
    return 1.0 / (1.0 + jnp.exp(-z))


def _silu_and_grad(z):
    s = _sig(z)
    return z * s, s * (1.0 + z * (1.0 - s))


def _tile(n, pref):
    t = min(n, pref)
    assert n % t == 0, (n, pref)
    return t


def _sds(shape, dtype):
    return jax.ShapeDtypeStruct(shape, dtype)


class _Comm:
    def __init__(self, ins, outs, scratch, start, wait, mid=None):
        self.ins, self.outs, self.scratch = list(ins), list(outs), list(scratch)
        self.start, self.wait, self.mid = start, wait, mid


def _call(name, body, grid, ins, outs, scratch=(), aliases=None, comm=None):
    n_in, n_out, n_scr = len(ins), len(outs), len(scratch)
    c_in = len(comm.ins) if comm else 0
    c_out = len(comm.outs) if comm else 0
    n_steps = 1
    for g in grid:
        n_steps *= g

    def kern(*refs):
        o0 = n_in + c_in
        s0 = o0 + n_out + c_out
        if comm:
            c_refs = (refs[n_in:o0], refs[o0 + n_out:s0], refs[s0 + n_scr:])
            step = pl.program_id(0)
            for ax in range(1, len(grid)):
                step = step * grid[ax] + pl.program_id(ax)

            @pl.when(step == 0)
            def _():
                comm.start(*c_refs)

            if comm.mid is not None:
                @pl.when(step == (3 * n_steps) // 4)
                def _():
                    comm.mid(*c_refs)

        body(refs[:n_in], refs[o0:o0 + n_out], refs[s0:s0 + n_scr])

        if comm:
            @pl.when(step == n_steps - 1)
            def _():
                comm.wait(*c_refs)

    hbm = pl.BlockSpec(memory_space=pl.ANY)
    return pl.pallas_call(
        kern, grid=grid,
        in_specs=[s for _, s in ins] + [hbm] * c_in, out_specs=[s for _, s in outs] + [hbm] * c_out,
        out_shape=[o for o, _ in outs] + (comm.outs if comm else []),
        scratch_shapes=list(scratch) + (comm.scratch if comm else []),
        name=name, compiler_params=_params(len(grid)), input_output_aliases=aliases or {},
    )(*([a for a, _ in ins] + (comm.ins if comm else [])))


def _mm(name, grid, ins, outs, dims, epi, aliases=None, comm=None):
    def body(in_refs, out_refs, _):
        acc = lax.dot_general(in_refs[0][...], in_refs[1][...], dims, preferred_element_type=F32)
        epi(acc, in_refs[2:], out_refs)

    return _call(name, body, grid, ins, outs, aliases=aliases, comm=comm)


def _rms_fwd(x, g):
    s, d = x.shape
    tr = _tile(s, 256)

    def kern(x_ref, g_ref, o_ref):
        xv = x_ref[...]
        r = lax.rsqrt(jnp.mean(xv * xv, axis=-1, keepdims=True) + EPS)
        o_ref[...] = (xv * r * g_ref[...]).astype(BF)

    return pl.pallas_call(
        kern, grid=(s // tr,),
        in_specs=[pl.BlockSpec((tr, d), lambda i: (i, 0)), pl.BlockSpec((1, d), lambda i: (0, 0))],
        out_specs=pl.BlockSpec((tr, d), lambda i: (i, 0)), out_shape=_sds((s, d), BF),
        name="rms_fwd", compiler_params=_params(1))(x, g)


def _final_norm(x2, target, g):
    s, d = x2.shape
    tr = _tile(s, 128)

    def kern(x_ref, t_ref, g_ref, dx_ref, dxb_ref, dg_ref, loss_ref):
        i = pl.program_id(0)
        xv = x_ref[...]
        gv = g_ref[...]
        r = lax.rsqrt(jnp.mean(xv * xv, axis=-1, keepdims=True) + EPS)
        xhat = xv * r
        err = xhat * gv - t_ref[...]
        dy = err * (1.0 / d)
        gy = dy * gv
        dx = r * (gy - xhat * jnp.mean(gy * xhat, axis=-1, keepdims=True))
        dx_ref[...] = dx
        dxb_ref[...] = dx.astype(BF)
        dg = jnp.sum(dy * xhat, axis=0, keepdims=True)
        ls = jnp.broadcast_to(0.5 * jnp.sum(jnp.mean(err * err, axis=-1, keepdims=True)), (1, LANES))

        @pl.when(i == 0)
        def _():
            dg_ref[...] = dg
            loss_ref[...] = ls

        @pl.when(i > 0)
        def _():
            dg_ref[...] += dg
            loss_ref[...] += ls

    row = pl.BlockSpec((tr, d), lambda i: (i, 0))
    vec = pl.BlockSpec((1, d), lambda i: (0, 0))
    return pl.pallas_call(
        kern, grid=(s // tr,), in_specs=[row, row, vec],
        out_specs=[row, row, vec, pl.BlockSpec((1, LANES), lambda i: (0, 0))],
        out_shape=[_sds((s, d), F32), _sds((s, d), BF), _sds((1, d), F32), _sds((1, LANES), F32)],
        name="final_norm", compiler_params=_params(1))(x2, target, g)


def _rms_bwd(x, dh, dx2, g):
    s, d = x.shape
    tr = _tile(s, 128)

    def kern(x_ref, dh_ref, dx2_ref, g_ref, dx_ref, dg_ref):
        i = pl.program_id(0)
        xv = x_ref[...]
        r = lax.rsqrt(jnp.mean(xv * xv, axis=-1, keepdims=True) + EPS)
        xhat = xv * r
        dhv = dh_ref[...]
        gh = dhv * g_ref[...]
        dx_ref[...] = dx2_ref[...] + r * (gh - xhat * jnp.mean(gh * xhat, axis=-1, keepdims=True))
        dg = jnp.sum(dhv * xhat, axis=0, keepdims=True)

        @pl.when(i == 0)
        def _():
            dg_ref[...] = dg

        @pl.when(i > 0)
        def _():
            dg_ref[...] += dg

    row = pl.BlockSpec((tr, d), lambda i: (i, 0))
    vec = pl.BlockSpec((1, d), lambda i: (0, 0))
    return pl.pallas_call(
        kern, grid=(s // tr,), in_specs=[row, row, row, vec], out_specs=[row, vec],
        out_shape=[_sds((s, d), F32), _sds((1, d), F32)],
        name="rms_bwd", compiler_params=_params(1))(x, dh, dx2, g)


def _rel_index(j, backward):
    if backward:
        rel = 2 * TQ - 1 - j
    else:
        rel = TQ - jnp.where(j < TK, j, j - SKEW)
    return jnp.clip(rel, -MAX_REL, MAX_REL) + MAX_REL


def _bias_rows(rel_bias_pad):
    h = rel_bias_pad.shape[0]

    def kern(rb_ref, o_ref):
        j = lax.broadcasted_iota(jnp.int32, (N_REL_PAD, SKEW), 1)
        k = lax.broadcasted_iota(jnp.int32, (N_REL_PAD, SKEW), 0)
        onehot = (_rel_index(j, False) == k).astype(F32)
        o_ref[...] = jnp.dot(rb_ref[...], onehot, preferred_element_type=F32, precision=lax.Precision.HIGHEST)

    return pl.pallas_call(kern, out_shape=_sds((h, SKEW), F32), name="bias_rows")(rel_bias_pad)


def _bias_grad(ddiag):
    h = ddiag.shape[0]

    def kern(d_ref, o_ref):
        j = lax.broadcasted_iota(jnp.int32, (N_REL_PAD, SKEW), 1)
        k = lax.broadcasted_iota(jnp.int32, (N_REL_PAD, SKEW), 0)
        onehot = ((_rel_index(j, True) == k) & (j < TQ + TK - 1)).astype(F32)
        o_ref[...] = lax.dot_general(d_ref[...], onehot, NT, preferred_element_type=F32,
                                     precision=lax.Precision.HIGHEST)

    return pl.pallas_call(kern, out_shape=_sds((h, N_REL_PAD), F32), name="bias_grad")(ddiag)


def _bias_tile(row):
    t = pltpu.roll(jnp.broadcast_to(row, (TQ, SKEW)), 0, 1, stride=1, stride_axis=0)[:, :TK]
    r = lax.broadcasted_iota(jnp.int32, (TQ, TK), 0) // CHUNK
    c = lax.broadcasted_iota(jnp.int32, (TQ, TK), 1) // CHUNK
    dist = N_LEFT_CHUNKS + r - c
    return jnp.where((dist >= 0) & (dist <= N_LEFT_CHUNKS), t, NEG)


def _scores(q, kcat, tile, first):
    s = lax.dot_general(q, kcat, NT, preferred_element_type=F32) * (HEAD_DIM ** -0.5) + tile
    col = lax.broadcasted_iota(jnp.int32, (TQ, TK), 1)
    s = jnp.where(first & (col < TQ), NEG, s)
    m = jnp.max(s, axis=1, keepdims=True)
    p = jnp.exp(s - m)
    return p, jnp.sum(p, axis=1, keepdims=True)


def _attn_fwd(proj, base, a_width):
    s_len = proj.shape[0]
    heads = a_width // HEAD_DIM
    nq = s_len // TQ
    kb, vb, zb = heads, 2 * heads, 3 * heads

    def kern(q_ref, kp_ref, kc_ref, vp_ref, vc_ref, z_ref, base_ref, att_ref, ya_ref, tile_ref):
        i = pl.program_id(1)

        @pl.when(i == 0)
        def _():
            tile_ref[...] = _bias_tile(base_ref[...])

        kcat = jnp.concatenate([kp_ref[...], kc_ref[...]], axis=0)
        vcat = jnp.concatenate([vp_ref[...], vc_ref[...]], axis=0)
        p, l = _scores(q_ref[...], kcat, tile_ref[...], i == 0)
        o = jnp.dot(p.astype(BF), vcat, preferred_element_type=F32) / l
        att_ref[...] = o.astype(BF)
        z = z_ref[...].astype(F32)
        ya_ref[...] = (o * (z * _sig(z))).astype(BF)

    blk = lambda off: pl.BlockSpec((TQ, HEAD_DIM), lambda h, i: (i, off + h))
    prev = lambda off: pl.BlockSpec((TQ, HEAD_DIM), lambda h, i: (jnp.maximum(i - 1, 0), off + h))
    out = pl.BlockSpec((TQ, HEAD_DIM), lambda h, i: (i, h))
    return pl.pallas_call(
        kern, grid=(heads, nq),
        in_specs=[blk(0), prev(kb), blk(kb), prev(vb), blk(vb), blk(zb),
                  pl.BlockSpec((None, 1, SKEW), lambda h, i: (h, 0, 0))],
        out_specs=[out, out], out_shape=[_sds((s_len, a_width), BF)] * 2,
        scratch_shapes=[pltpu.VMEM((TQ, TK), F32)],
        name="attn_fwd", compiler_params=_params(2))(proj, proj, proj, proj, proj, proj, base)


def _attn_bwd(proj, datt, base, a_width, comm=None):
    s_len = proj.shape[0]
    heads = a_width // HEAD_DIM
    nq = s_len // TQ
    kb, vb = heads, 2 * heads
    scale = HEAD_DIM ** -0.5

    def body(in_refs, out_refs, scratch_refs):
        q_ref, kp_ref, kc_ref, vp_ref, vc_ref, do_ref, base_ref = in_refs
        dq_ref, dk_ref, dv_ref, dd_ref = out_refs
        tile_ref, dsacc_ref, ak_ref, av_ref = scratch_refs
        i = pl.program_id(1)

        @pl.when(i == 0)
        def _():
            tile_ref[...] = _bias_tile(base_ref[...])
            dsacc_ref[...] = jnp.zeros_like(dsacc_ref)
            ak_ref[...] = jnp.zeros_like(ak_ref)
            av_ref[...] = jnp.zeros_like(av_ref)

        @pl.when(i < nq)
        def _():
            q = q_ref[...]
            do = do_ref[...]
            kcat = jnp.concatenate([kp_ref[...], kc_ref[...]], axis=0)
            vcat = jnp.concatenate([vp_ref[...], vc_ref[...]], axis=0)
            p, l = _scores(q, kcat, tile_ref[...], i == 0)
            p = p / l
            dp = lax.dot_general(do, vcat, NT, preferred_element_type=F32)
            ds = p * (dp - jnp.sum(p * dp, axis=1, keepdims=True))
            dsacc_ref[...] += ds
            dsb = ds.astype(BF)
            dq_ref[...] = (jnp.dot(dsb, kcat, preferred_element_type=F32) * scale).astype(BF)
            dkc = lax.dot_general(dsb, q, TN, preferred_element_type=F32) * scale
            dvc = lax.dot_general(p.astype(BF), do, TN, preferred_element_type=F32)
            dk_ref[...] = (ak_ref[...] + dkc[:TQ]).astype(BF)
            dv_ref[...] = (av_ref[...] + dvc[:TQ]).astype(BF)
            ak_ref[...] = dkc[TQ:]
            av_ref[...] = dvc[TQ:]

        @pl.when(i == nq)
        def _():
            dk_ref[...] = ak_ref[...].astype(BF)
            dv_ref[...] = av_ref[...].astype(BF)
            acc = dsacc_ref[...]
            rr = lax.broadcasted_iota(jnp.int32, (TQ, TQ), 0)
            cc = lax.broadcasted_iota(jnp.int32, (TQ, TQ), 1)
            flip = (rr + cc == TQ - 1).astype(BF)
            hi = acc.astype(BF)
            lo = (acc - hi.astype(F32)).astype(BF)
            rev = jnp.dot(flip, hi, preferred_element_type=F32) + jnp.dot(flip, lo, preferred_element_type=F32)
            wide = jnp.concatenate([rev, jnp.zeros((TQ, SKEW - TK), F32)], axis=1)
            dd_ref[...] = jnp.sum(pltpu.roll(wide, 0, 1, stride=1, stride_axis=0), axis=0, keepdims=True)

    last = nq - 1
    cur = lambda off: pl.BlockSpec((TQ, HEAD_DIM), lambda h, i: (jnp.minimum(i, last), off + h))
    prev = lambda off: pl.BlockSpec((TQ, HEAD_DIM), lambda h, i: (jnp.maximum(jnp.minimum(i, last) - 1, 0), off + h))
    done = pl.BlockSpec((TQ, HEAD_DIM), lambda h, i: (jnp.maximum(i - 1, 0), h))
    row = pl.BlockSpec((None, 1, SKEW), lambda h, i: (h, 0, 0))
    act = _sds((s_len, a_width), BF)
    res = _call(
        "attn_bwd", body, (heads, nq + 1),
        [(proj, cur(0)), (proj, prev(kb)), (proj, cur(kb)), (proj, prev(vb)), (proj, cur(vb)), (datt, cur(0)),
         (base, row)],
        [(act, cur(0)), (act, done), (act, done), (_sds((heads, 1, SKEW), F32), row)],
        scratch=[pltpu.VMEM((TQ, TK), F32), pltpu.VMEM((TQ, TK), F32),
                 pltpu.VMEM((TQ, HEAD_DIM), F32), pltpu.VMEM((TQ, HEAD_DIM), F32)],
        comm=comm)
    return res[:4], res[4:]


def _pool_fwd(proj, pool_w, pool_scale, p_width, u_blk, z_blk):
    s_len = proj.shape[0]
    cg = p_width // len(POOL_WINDOWS)
    tt = _tile(s_len, 512)

    def kern(up_ref, uc_ref, z_ref, pw_ref, sc_ref, d_ref, y_ref, yp_ref):
        t = pl.program_id(0)
        row = lax.broadcasted_iota(jnp.int32, (tt, 1), 0) + t * tt
        for g, w in enumerate(POOL_WINDOWS):
            cs = slice(g * cg, (g + 1) * cg)
            prev = jnp.where(t == 0, 0.0, up_ref[:, cs].astype(F32))
            cur = uc_ref[:, cs].astype(F32)
            ws = jnp.concatenate([prev, cur], axis=0)
            sh = 1
            while sh < w:
                ws = ws + pltpu.roll(ws, sh, 0)
                sh *= 2
            cnt = jnp.minimum(row + 1, w).astype(F32)
            db = (ws[HALO:, :] / cnt - cur).astype(BF)
            y = jnp.dot(db, pw_ref[g], preferred_element_type=F32)
            d_ref[:, cs] = db
            y_ref[:, cs] = y.astype(BF)
            z = z_ref[:, cs].astype(F32)
            yp_ref[:, cs] = (y * sc_ref[:, cs] * (z * _sig(z))).astype(BF)

    full = pl.BlockSpec((tt, p_width), lambda t: (t, 0))
    return pl.pallas_call(
        kern, grid=(s_len // tt,),
        in_specs=[pl.BlockSpec((HALO, p_width), lambda t: (jnp.maximum(t * (tt // HALO) - 1, 0), u_blk)),
                  pl.BlockSpec((tt, p_width), lambda t: (t, u_blk)),
                  pl.BlockSpec((tt, p_width), lambda t: (t, z_blk)),
                  pl.BlockSpec((len(POOL_WINDOWS), cg, cg), lambda t: (0, 0, 0)),
                  pl.BlockSpec((1, p_width), lambda t: (0, 0))],
        out_specs=[full, full, full], out_shape=[_sds((s_len, p_width), BF)] * 3,
        name="pool_fwd", compiler_params=_params(1))(proj, proj, proj, pool_w, pool_scale)


def _pool_bwd(dy, dmean, pool_w):
    s_len, p_width = dy.shape
    ng = len(POOL_WINDOWS)
    cg = p_width // ng
    tt = _tile(s_len, 512)
    nt = s_len // tt

    def kern(dyc_ref, dyn_ref, d_ref, pw_ref, du_ref, dpw_ref):
        t = pl.program_id(0)

        @pl.when(t == 0)
        def _():
            dpw_ref[...] = jnp.zeros_like(dpw_ref)

        row = lax.broadcasted_iota(jnp.int32, (tt + HALO, 1), 0) + t * tt
        for g, w in enumerate(POOL_WINDOWS):
            cs = slice(g * cg, (g + 1) * cg)
            dyc = dyc_ref[:, cs]
            ddc = lax.dot_general(dyc, pw_ref[g], NT, preferred_element_type=F32)
            ddn = lax.dot_general(dyn_ref[:, cs], pw_ref[g], NT, preferred_element_type=F32)
            ddn = jnp.where(t == nt - 1, 0.0, ddn)
            cnt = jnp.minimum(row + 1, w).astype(F32)
            ws = jnp.concatenate([ddc, ddn], axis=0) / cnt
            sh = 1
            while sh < w:
                ws = ws + pltpu.roll(ws, tt + HALO - sh, 0)
                sh *= 2
            du_ref[:, cs] = (ws[:tt, :] - ddc).astype(BF)
            dpw_ref[g] += lax.dot_general(d_ref[:, cs], dyc, TN, preferred_element_type=F32)

    full = pl.BlockSpec((tt, p_width), lambda t: (t, 0))
    pw_spec = pl.BlockSpec((ng, cg, cg), lambda t: (0, 0, 0))
    return pl.pallas_call(
        kern, grid=(nt,),
        in_specs=[full,
                  pl.BlockSpec((HALO, p_width), lambda t: (jnp.minimum((t + 1) * (tt // HALO), s_len // HALO - 1), 0)),
                  full, pw_spec],
        out_specs=[full, pw_spec], out_shape=[_sds((s_len, p_width), BF), _sds((ng, cg, cg), F32)],
        name="pool_bwd", compiler_params=_params(1))(dy, dy, dmean, pool_w)


def _adam(g, w_ref, m_ref, v_ref, g_out, d_out, m_out, v_out):
    m = ADAM_B1 * m_ref[...] + (1.0 - ADAM_B1) * g
    v = ADAM_B2 * v_ref[...] + (1.0 - ADAM_B2) * (g * g)
    m_hat = m / (1.0 - ADAM_B1 ** ADAM_STEP)
    v_hat = v / (1.0 - ADAM_B2 ** ADAM_STEP)
    g_out[...] = g
    d_out[...] = -ADAM_LR * (m_hat / (jnp.sqrt(v_hat) + ADAM_EPS) + ADAM_WD * w_ref[...])
    m_out[...] = m
    v_out[...] = v


def _adamw_shard(name, parts, w, m, v, row_off):
    rw, cw = w.shape
    sw = parts.shape[2]
    tr = _tile(rw, 512)
    assert row_off % tr == 0 and cw % sw == 0

    def kern(b_ref, w_ref, m_ref, v_ref, g_out, d_out, m_out, v_out):
        b = b_ref[...].astype(F32)
        _adam(((b[0] + b[1]) + b[2]) + b[3], w_ref, m_ref, v_ref, g_out, d_out, m_out, v_out)

    blk = pl.BlockSpec((tr, sw), lambda ct, i: (i, ct))
    return pl.pallas_call(
        kern, grid=(cw // sw, rw // tr),
        in_specs=[pl.BlockSpec((4, tr, sw), lambda ct, i: (0, (row_off + ct * rw) // tr + i, 0)), blk, blk, blk],
        out_specs=[blk] * 4, out_shape=[_sds((rw, cw), F32)] * 4,
        name=name, compiler_params=_params(2))(parts, w, m, v)


def _position():
    return lax.axis_index("x"), lax.axis_index("y"), lax.axis_index("c")


def _gather_comm(shards):
    n = len(shards)

    def plan(xs, outs, sems):
        send_sems, recv_sems, local_sems = sems
        x, y, c = _position()
        me, sibling = (x, y, c), (x, y, 1 - c)
        chips = [(1 - x, y), (x, 1 - y), (1 - x, 1 - y)]

        def copy(a, k, block, to, src=None):
            dst = outs[a].at[4 * block[0] + 2 * block[1] + block[2]]
            return pltpu.make_async_remote_copy(
                src_ref=dst if src is None else src, dst_ref=dst,
                send_sem=send_sems.at[7 * a + k], recv_sem=recv_sems.at[7 * a + k],
                device_id=to, device_id_type=MESH)

        by_chip = [(j, chip, a) for j, chip in enumerate(chips) for a in range(n)]
        return dict(
            mine=lambda: [pltpu.make_async_copy(xs[a], outs[a].at[4 * x + 2 * y + c], local_sems.at[a])
                          for a in range(n)],
            first=lambda: ([copy(a, 0, me, sibling, src=xs[a]) for a in range(n)]
                           + [copy(a, 1 + j, me, (*chip, c), src=xs[a]) for j, chip, a in by_chip]),
            landed=lambda: [copy(a, 1 + j, (*chip, c), me) for j, chip, a in by_chip],
            passed=lambda: [copy(a, 4 + j, (*chip, c), sibling) for j, chip, a in by_chip],
            rest=lambda: ([copy(a, 0, sibling, me) for a in range(n)]
                          + [copy(a, 4 + j, (*chip, 1 - c), me) for j, chip, a in by_chip]))

    def start(*refs):
        p = plan(*refs)
        for cp in p["mine"]() + p["first"]():
            cp.start()

    def mid(*refs):
        p = plan(*refs)
        for arrived, onward in zip(p["landed"](), p["passed"]()):
            arrived.wait_recv()
            onward.start()

    def wait(*refs):
        p = plan(*refs)
        for cp in p["rest"]():
            cp.wait_recv()
        for cp in p["first"]() + p["passed"]():
            cp.wait_send()
        for cp in p["mine"]():
            cp.wait()

    return _Comm(shards, [_sds((N_DEV,) + s.shape, s.dtype) for s in shards],
                 [pltpu.SemaphoreType.DMA((7 * n,)), pltpu.SemaphoreType.DMA((7 * n,)),
                  pltpu.SemaphoreType.DMA((n,))], start, wait, mid)


def _all_gather(shards):
    comm = _gather_comm(shards)
    n = len(shards)

    def kern(*refs):
        c_refs = (refs[:n], refs[n:2 * n], refs[2 * n:])
        comm.start(*c_refs)
        comm.mid(*c_refs)
        comm.wait(*c_refs)

    hbm = pl.BlockSpec(memory_space=pl.ANY)
    return pl.pallas_call(
        kern, in_specs=[hbm] * n, out_specs=[hbm] * n, out_shape=comm.outs,
        scratch_shapes=comm.scratch, name="all_gather_w_in")(*shards)


def _cores_comm(slab):
    _, _, r, sw = slab.shape

    def copies(ins, outs, sems):
        x, y, c = _position()
        return [pltpu.make_async_remote_copy(
            src_ref=ins[0].at[1 - c], dst_ref=outs[0], send_sem=sems[0], recv_sem=sems[1],
            device_id=(x, y, 1 - c), device_id_type=MESH)]

    def start(*refs):
        for cp in copies(*refs):
            cp.start()

    def wait(*refs):
        for cp in copies(*refs):
            cp.wait()

    return _Comm([slab], [_sds((4, r, sw), slab.dtype)],
                 [pltpu.SemaphoreType.DMA, pltpu.SemaphoreType.DMA], start, wait)


def _add_core_partials(name, slab, recv, core, tr):
    _, _, r, sw = slab.shape

    def kern(c_ref, a_ref, b_ref, o_ref):
        o_ref[...] = (a_ref[...].astype(F32) + b_ref[...].astype(F32)).astype(BF)

    return pl.pallas_call(
        kern,
        grid_spec=pltpu.PrefetchScalarGridSpec(
            num_scalar_prefetch=1, grid=(4, r // tr),
            in_specs=[pl.BlockSpec((None, None, tr, sw), lambda k, i, c_ref: (c_ref[0], k, i, 0)),
                      pl.BlockSpec((None, tr, sw), lambda k, i, c_ref: (k, i, 0))],
            out_specs=pl.BlockSpec((None, tr, sw), lambda k, i, c_ref: (k, i, 0))),
        out_shape=_sds((4, r, sw), BF), name=name, compiler_params=_params(2))(core, slab, recv)


def _chips_comm(part):
    _, r, sw = part.shape

    def copies(ins, outs, sems):
        send_sems, recv_sems, local_sem = sems
        x, y, c = _position()
        mine = 2 * x + y
        local = pltpu.make_async_copy(ins[0].at[mine], outs[0].at[mine], local_sem)
        chips = [(1 - x, y), (x, 1 - y), (1 - x, 1 - y)]
        remote = [pltpu.make_async_remote_copy(
            src_ref=ins[0].at[2 * px + py], dst_ref=outs[0].at[mine],
            send_sem=send_sems.at[j], recv_sem=recv_sems.at[j],
            device_id=(px, py, c), device_id_type=MESH) for j, (px, py) in enumerate(chips)]
        return [local] + remote

    def start(*refs):
        for cp in copies(*refs):
            cp.start()

    def wait(*refs):
        for cp in copies(*refs):
            cp.wait()

    return _Comm([part], [_sds((4, r, sw), part.dtype)],
                 [pltpu.SemaphoreType.DMA((3,)), pltpu.SemaphoreType.DMA((3,)), pltpu.SemaphoreType.DMA],
                 start, wait)


def _small_allreduce_adamw(partial, w, m, v):
    nr = partial.shape[0]

    def kern(p_ref, w_ref, m_ref, v_ref, g_out, d_out, m_out, v_out, gath_ref, send_sems, recv_sems):
        x, y, c = _position()
        me = 4 * x + 2 * y + c
        gath_ref[me] = p_ref[...]
        copies = []
        for mask in range(1, N_DEV):
            peer = (x ^ (mask >> 2), y ^ ((mask >> 1) & 1), c ^ (mask & 1))
            copies.append(pltpu.make_async_remote_copy(
                src_ref=p_ref, dst_ref=gath_ref.at[me],
                send_sem=send_sems.at[mask - 1], recv_sem=recv_sems.at[mask - 1],
                device_id=peer, device_id_type=MESH))
        for cp in copies:
            cp.start()
        for cp in copies:
            cp.wait()
        tot = gath_ref[0]
        for k in range(1, N_DEV):
            tot = tot + gath_ref[k]
        _adam(tot, w_ref, m_ref, v_ref, g_out, d_out, m_out, v_out)

    vmem = pl.BlockSpec(memory_space=pltpu.VMEM)
    return pl.pallas_call(
        kern, in_specs=[vmem] * 4, out_specs=[vmem] * 4, out_shape=[_sds((nr, LANES), F32)] * 4,
        scratch_shapes=[pltpu.VMEM((N_DEV, nr, LANES), F32),
                        pltpu.SemaphoreType.DMA((N_DEV - 1,)), pltpu.SemaphoreType.DMA((N_DEV - 1,))],
        name="small_allreduce_adamw")(partial, w, m, v)


def kernel(x, norm_gain, w_in, rel_bias, pool_w, pool_scale, w_out_attn, w_out_pool, gate_bias, w_out, final_gain, loss_target, m_norm_gain, m_w_in, m_rel_bias, m_pool_w, m_pool_scale, m_w_out_attn, m_w_out_pool, m_gate_bias, m_w_out, m_final_gain, v_norm_gain, v_w_in, v_rel_bias, v_pool_w, v_pool_scale, v_w_out_attn, v_w_out_pool, v_gate_bias, v_w_out, v_final_gain):
    _, s_len, d = x.shape
    a = w_out_attn.shape[0]
    p = w_out_pool.shape[0]
    heads = a // HEAD_DIM
    ng = len(POOL_WINDOWS)
    cg = p // ng
    sw = d // N_DEV
    n_in = w_in.shape[1] * N_DEV
    assert a == p and a + p == d and cg == sw and n_in == 5 * d and w_in.shape[1] == 5 * sw
    assert s_len % TQ == 0 and rel_bias.shape == (heads, N_REL)
    tm = _tile(s_len, 1024)
    x2d = x.reshape(s_len, d)
    tgt = loss_target.reshape(s_len, d)

    g1 = norm_gain.reshape(1, d)
    g2 = final_gain.reshape(1, d)
    scale_row = pool_scale.reshape(1, p)
    hb = _rms_fwd(x2d, g1)
    win_g = _all_gather([w_in.astype(BF)])[0]
    tn = sw
    per = w_in.shape[1] // tn

    def store_bf16(acc, _, outs):
        outs[0][...] = acc.astype(BF)

    proj, *gathered = _mm(
        "proj", (s_len // tm, n_in // tn),
        [(hb, pl.BlockSpec((tm, d), lambda i, j: (i, 0))),
         (win_g, pl.BlockSpec((None, d, tn), lambda i, j: (j // per, 0, j % per)))],
        [(_sds((s_len, n_in), BF), pl.BlockSpec((tm, tn), lambda i, j: (i, j)))], NN, store_bf16,
        comm=_gather_comm([w_out_attn.astype(BF), w_out_pool.astype(BF), w_out.astype(BF),
                           pool_w.astype(BF), gate_bias]))
    woa = gathered[0].transpose(1, 0, 2).reshape(a, d)
    wop = gathered[1].transpose(1, 0, 2).reshape(p, d)
    wo = gathered[2].reshape(d, d)
    pw = gathered[3].transpose(1, 0, 2, 3).reshape(ng, cg, cg)
    gb = gathered[4].transpose(1, 0, 2).reshape(2, d)

    rb_pad = jnp.pad(rel_bias, ((0, 0), (0, N_REL_PAD - N_REL)))
    base = _bias_rows(rb_pad).reshape(heads, 1, SKEW)
    att, ya = _attn_fwd(proj, base, a)
    u_blk, z_blk = 4 * a // p, 4 * a // p + 1
    dmean, ypre, yp = _pool_fwd(proj, pw, scale_row, p, u_blk, z_blk)

    ga_t, gp_t = (4 * a + 2 * p) // tn, (4 * a + 2 * p + d) // tn

    def gate_kernel(ya_ref, woa_ref, yp_ref, wop_ref, ga_ref, gp_ref, gb_ref, m_ref, a_ref, p_ref):
        am = jnp.dot(ya_ref[...], woa_ref[...], preferred_element_type=F32)
        pm = jnp.dot(yp_ref[...], wop_ref[...], preferred_element_type=F32)
        sa = _sig(ga_ref[...].astype(F32) + gb_ref[0:1, :])
        sp = _sig(gp_ref[...].astype(F32) + gb_ref[1:2, :])
        m_ref[...] = (sa * am + sp * pm).astype(BF)
        a_ref[...] = am.astype(BF)
        p_ref[...] = pm.astype(BF)

    tile_ij = pl.BlockSpec((tm, tn), lambda i, j: (i, j))
    merged, am, pm = pl.pallas_call(
        gate_kernel, grid=(s_len // tm, d // tn),
        in_specs=[pl.BlockSpec((tm, a), lambda i, j: (i, 0)), pl.BlockSpec((a, tn), lambda i, j: (0, j)),
                  pl.BlockSpec((tm, p), lambda i, j: (i, 0)), pl.BlockSpec((p, tn), lambda i, j: (0, j)),
                  pl.BlockSpec((tm, tn), lambda i, j: (i, ga_t + j)), pl.BlockSpec((tm, tn), lambda i, j: (i, gp_t + j)),
                  pl.BlockSpec((2, tn), lambda i, j: (0, j))],
        out_specs=[tile_ij] * 3, out_shape=[_sds((s_len, d), BF)] * 3,
        name="gate_merge", compiler_params=_params(2))(ya, woa, yp, wop, proj, proj, gb)

    def add_residual(acc, ex, outs):
        outs[0][...] = ex[0][...] + acc

    x2 = _mm("out_proj", (s_len // tm, d // tn),
             [(merged, pl.BlockSpec((tm, d), lambda i, j: (i, 0))), (wo, pl.BlockSpec((d, tn), lambda i, j: (0, j))),
              (x2d, tile_ij)],
             [(_sds((s_len, d), F32), tile_ij)], NN, add_residual)[0]

    dx2, dx2b, dg2, loss_part = _final_norm(x2, tgt, g2)

    tmb = _tile(s_len, 512)
    tile_ji = pl.BlockSpec((tmb, tn), lambda j, i: (i, j))

    def gate_bwd(dm, ex, outs):
        a_ref, p_ref, ga_ref, gp_ref, gb_ref = ex
        da_ref, dp_ref, dga_ref, dgp_ref, dgb_ref = outs
        i = pl.program_id(1)
        sa = _sig(ga_ref[...].astype(F32) + gb_ref[0:1, :])
        sp = _sig(gp_ref[...].astype(F32) + gb_ref[1:2, :])
        dga = dm * a_ref[...].astype(F32) * sa * (1.0 - sa)
        dgp = dm * p_ref[...].astype(F32) * sp * (1.0 - sp)
        da_ref[...] = (dm * sa).astype(BF)
        dp_ref[...] = (dm * sp).astype(BF)
        dga_ref[...] = dga.astype(BF)
        dgp_ref[...] = dgp.astype(BF)
        r = lax.broadcasted_iota(jnp.int32, (8, tn), 0)
        sums = jnp.where(r == 0, jnp.sum(dga, axis=0, keepdims=True),
                         jnp.where(r == 1, jnp.sum(dgp, axis=0, keepdims=True), 0.0))

        @pl.when(i == 0)
        def _():
            dgb_ref[...] = sums

        @pl.when(i > 0)
        def _():
            dgb_ref[...] += sums

    d_am, d_pm, dga, dgp, dgb8 = _mm(
        "gate_bwd", (d // tn, s_len // tmb),
        [(dx2b, pl.BlockSpec((tmb, d), lambda j, i: (i, 0))), (wo, pl.BlockSpec((tn, d), lambda j, i: (j, 0))),
         (am, tile_ji), (pm, tile_ji),
         (proj, pl.BlockSpec((tmb, tn), lambda j, i: (i, ga_t + j))),
         (proj, pl.BlockSpec((tmb, tn), lambda j, i: (i, gp_t + j))),
         (gb, pl.BlockSpec((2, tn), lambda j, i: (0, j)))],
        [(_sds((s_len, d), BF), tile_ji)] * 4 + [(_sds((8, d), F32), pl.BlockSpec((8, tn), lambda j, i: (0, j)))],
        NT, gate_bwd)

    za_t = 3 * a // tn

    def attn_gate_bwd(dya, ex, outs):
        silu, dsilu = _silu_and_grad(ex[0][...].astype(F32))
        outs[0][...] = (dya * silu).astype(BF)
        outs[1][...] = (dya * ex[1][...].astype(F32) * dsilu).astype(BF)

    datt, dza = _mm(
        "attn_gate_bwd", (s_len // tm, a // tn),
        [(d_am, pl.BlockSpec((tm, d), lambda i, j: (i, 0))), (woa, pl.BlockSpec((tn, d), lambda i, j: (j, 0))),
         (proj, pl.BlockSpec((tm, tn), lambda i, j: (i, za_t + j))), (att, tile_ij)],
        [(_sds((s_len, a), BF), tile_ij)] * 2, NT, attn_gate_bwd)

    zp_t = (4 * a + p) // tn

    def pool_gate_bwd(dyp, ex, outs):
        z_ref, y_ref, sc_ref = ex
        dzp_ref, dy_ref, dps_ref = outs
        i = pl.program_id(1)
        silu, dsilu = _silu_and_grad(z_ref[...].astype(F32))
        y = y_ref[...].astype(F32)
        sc = sc_ref[...]
        dyp0 = dyp * silu
        dzp_ref[...] = (dyp * (y * sc) * dsilu).astype(BF)
        dy_ref[...] = (dyp0 * sc).astype(BF)
        dps = jnp.sum(dyp0 * y, axis=0, keepdims=True)

        @pl.when(i == 0)
        def _():
            dps_ref[...] = dps

        @pl.when(i > 0)
        def _():
            dps_ref[...] += dps

    dzp, dy_pool, dps = _mm(
        "pool_gate_bwd", (p // tn, s_len // tmb),
        [(d_pm, pl.BlockSpec((tmb, d), lambda j, i: (i, 0))), (wop, pl.BlockSpec((tn, d), lambda j, i: (j, 0))),
         (proj, pl.BlockSpec((tmb, tn), lambda j, i: (i, zp_t + j))), (ypre, tile_ji),
         (scale_row, pl.BlockSpec((1, tn), lambda j, i: (0, j)))],
        [(_sds((s_len, p), BF), tile_ji)] * 2 + [(_sds((1, p), F32), pl.BlockSpec((1, tn), lambda j, i: (0, j)))],
        NT, pool_gate_bwd)

    du, dpw = _pool_bwd(dy_pool, dmean, pw)

    o_wop, o_wo, o_pool = a, d, 2 * d
    slab_a = _sds((2, 4, 2 * d + cg, sw), BF)
    slab_b = _sds((2, 4, 5 * d, sw), BF)
    hbm = pl.BlockSpec(memory_space=pl.ANY)
    tmw = _tile(a, 1024)
    core = lax.axis_index("c").astype(jnp.int32).reshape(1)

    def pack_small(dpw_ref, dgb_ref, o_ref):
        rows = cg // N_DEV
        for j in range(N_DEV):
            for g in range(ng):
                o_ref[j % 2, j // 2, g * rows:(g + 1) * rows, :] = dpw_ref[g, j * rows:(j + 1) * rows, :].astype(BF)
            o_ref[j % 2, j // 2, ng * rows:, :] = jnp.concatenate(
                [dgb_ref[:, j * sw:(j + 1) * sw], jnp.zeros((cg - ng * rows - 8, sw), F32)], axis=0).astype(BF)

    slab = pl.pallas_call(
        pack_small, grid=(1,),
        in_specs=[pl.BlockSpec((ng, cg, cg), lambda i: (0, 0, 0)), pl.BlockSpec((8, d), lambda i: (0, 0))],
        out_specs=pl.BlockSpec((2, 4, cg, sw), lambda i: (0, 0, o_pool // cg, 0)), out_shape=slab_a,
        name="dw_small", compiler_params=_params(1))(dpw, dgb8)

    def into_slab(acc, _, outs):
        outs[0][...] = acc.astype(BF)

    def weight_grad(name, slab, lhs, rhs, grid, lhs_spec, rhs_spec, out_spec, comm=None):
        return _mm(name, grid, [(lhs, lhs_spec), (rhs, rhs_spec), (slab, hbm)], [(slab_a, out_spec)],
                   TN, into_slab, aliases={2: 0}, comm=comm)

    slab = weight_grad("dw_out", slab, merged, dx2b, (N_DEV, d // sw),
                       pl.BlockSpec((s_len, sw), lambda j, t: (0, j)), pl.BlockSpec((s_len, sw), lambda j, t: (0, t)),
                       pl.BlockSpec((None, None, sw, sw), lambda j, t: (j % 2, j // 2, o_wo // sw + t, 0)))[0]
    slab = weight_grad("dw_out_attn", slab, ya, d_am, (a // tmw, N_DEV),
                       pl.BlockSpec((s_len, tmw), lambda i, j: (0, i)), pl.BlockSpec((s_len, sw), lambda i, j: (0, j)),
                       pl.BlockSpec((None, None, tmw, sw), lambda i, j: (j % 2, j // 2, i, 0)))[0]
    slab = weight_grad("dw_out_pool", slab, yp, d_pm, (p // tmw, N_DEV),
                       pl.BlockSpec((s_len, tmw), lambda i, j: (0, i)), pl.BlockSpec((s_len, sw), lambda i, j: (0, j)),
                       pl.BlockSpec((None, None, tmw, sw), lambda i, j: (j % 2, j // 2, o_wop // tmw + i, 0)))[0]

    (dq, dk, dv, ddiag), (from_sibling_a,) = _attn_bwd(proj, datt, base, a, comm=_cores_comm(slab))
    chip_part_a = _add_core_partials("add_core_partials_a", slab, from_sibling_a, core, sw)
    drb = _bias_grad(ddiag.reshape(heads, SKEW))
    dproj = jnp.concatenate([dq, dk, dv, dza, du, dzp, dga, dgp], axis=1)

    tmd = _tile(d, 1024)
    slab_w, parts_a = _mm(
        "dw_in", (d // tmd, n_in // sw),
        [(hb, pl.BlockSpec((s_len, tmd), lambda i, t: (0, i))), (dproj, pl.BlockSpec((s_len, sw), lambda i, t: (0, t)))],
        [(slab_b, pl.BlockSpec((None, None, tmd, sw),
                               lambda i, t: ((t // per) % 2, (t // per) // 2, (t % per) * (d // tmd) + i, 0)))],
        TN, into_slab, comm=_chips_comm(chip_part_a))

    tk = w_in.shape[1]
    tnh = _tile(d, 1024)
    tmh = tm if s_len > tm else s_len // 2
    n_row = s_len // tmh

    def dh_rows(name, lo, hi, prev, comm):
        def body(in_refs, out_refs, scratch_refs):
            acc_ref = scratch_refs[0]
            k = pl.program_id(2)
            part = lax.dot_general(in_refs[0][...], in_refs[1][...], NT, preferred_element_type=F32)

            @pl.when(k == 0)
            def _():
                acc_ref[...] = part

            @pl.when(k > 0)
            def _():
                acc_ref[...] += part

            @pl.when(k == N_DEV - 1)
            def _():
                out_refs[0][...] = acc_ref[...]

        ins = [(dproj, pl.BlockSpec((tmh, tk), lambda i, j, k: (lo + i, k))),
               (win_g, pl.BlockSpec((None, tnh, tk), lambda i, j, k: (k, j, 0)))]
        if prev is not None:
            ins.append((prev, hbm))
        return _call(name, body, (hi - lo, d // tnh, N_DEV), ins,
                     [(_sds((s_len, d), F32), pl.BlockSpec((tmh, tnh), lambda i, j, k: (lo + i, j)))],
                     scratch=[pltpu.VMEM((tmh, tnh), F32)], aliases={2: 0} if prev is not None else None, comm=comm)

    dh, from_sibling_b = dh_rows("dh_head", 0, 1, None, _cores_comm(slab_w))
    chip_part_b = _add_core_partials("add_core_partials_b", slab_w, from_sibling_b, core, 4 * sw)
    dh, parts_b = dh_rows("dh_rest", 1, n_row, dh, _chips_comm(chip_part_b))

    dx, dg1 = _rms_bwd(x2d, dh, dx2, g1)

    g_win, d_win, m_win, v_win = _adamw_shard("adamw_w_in", parts_b, w_in, m_w_in, v_w_in, 0)
    g_woa, d_woa, m_woa, v_woa = _adamw_shard("adamw_w_out_attn", parts_a, w_out_attn, m_w_out_attn, v_w_out_attn, 0)
    g_wop, d_wop, m_wop, v_wop = _adamw_shard("adamw_w_out_pool", parts_a, w_out_pool, m_w_out_pool, v_w_out_pool, o_wop)
    g_wo, d_wo, m_wo, v_wo = _adamw_shard("adamw_w_out", parts_a, w_out, m_w_out, v_w_out, o_wo)
    flat = lambda t: t.reshape(cg // 2, sw)
    pool_out = _adamw_shard("adamw_pool_w", parts_a, flat(pool_w), flat(m_pool_w), flat(v_pool_w), o_pool)
    g_pw, d_pw, m_pw, v_pw = [t.reshape(pool_w.shape) for t in pool_out]
    pad16 = lambda t: jnp.pad(t, ((0, 14), (0, 0)))
    gb_out = _adamw_shard("adamw_gate_bias", parts_a, pad16(gate_bias), pad16(m_gate_bias), pad16(v_gate_bias),
                          o_pool + cg // 2)
    g_gb, d_gb, m_gb, v_gb = [t[:2] for t in gb_out]

    def pack(n_gain, f_gain, scale, rb, last):
        rows = [n_gain.reshape(-1, LANES), f_gain.reshape(-1, LANES), scale.reshape(-1, LANES),
                rb.reshape(-1, LANES), last]
        return jnp.concatenate(rows, axis=0)

    pad_rb = lambda t: jnp.pad(t, ((0, 0), (0, N_REL_PAD - N_REL)))
    zeros8 = jnp.zeros((8, LANES), F32)
    loss_rows = jnp.pad(loss_part, ((0, 7), (0, 0)))
    small = _small_allreduce_adamw(
        pack(dg1, dg2, dps, drb, loss_rows),
        pack(norm_gain, final_gain, pool_scale, pad_rb(rel_bias), zeros8),
        pack(m_norm_gain, m_final_gain, m_pool_scale, pad_rb(m_rel_bias), zeros8),
        pack(v_norm_gain, v_final_gain, v_pool_scale, pad_rb(v_rel_bias), zeros8))

    n1, n2, n3 = d // LANES, 2 * d // LANES, (2 * d + p) // LANES
    n4 = n3 + heads * N_REL_PAD // LANES

    def unpack(t):
        return (t[:n1].reshape(d), t[n1:n2].reshape(d), t[n2:n3].reshape(p),
                t[n3:n4].reshape(heads, N_REL_PAD)[:, :N_REL])

    (g_ng, g_fg, g_ps, g_rb), (d_ng, d_fg, d_ps, d_rb), (m_ng, m_fg, m_ps, m_rb), (v_ng, v_fg, v_ps, v_rb) = [
        unpack(t) for t in small]
    loss = small[0][n4, 0]

    return (loss, dx.reshape(x.shape),
            g_ng, g_win, g_rb, g_pw, g_ps, g_woa, g_wop, g_gb, g_wo, g_fg,
            d_ng, d_win, d_rb, d_pw, d_ps, d_woa, d_wop, d_gb, d_wo, d_fg,
            m_ng, m_win, m_rb, m_pw, m_ps, m_woa, m_wop, m_gb, m_wo, m_fg,
            v_ng, v_win, v_rb, v_pw, v_ps, v_woa, v_wop, v_gb, v_wo, v_fg)
```

```python
import jax
import jax.numpy as jnp
from jax import lax
from jax.experimental import pallas as pl
from jax.experimental.pallas import tpu as pltpu

F32 = jnp.float32
BF = jnp.bfloat16
MESH = pl.DeviceIdType.MESH

N_DEV = 8
CHUNK = 64
N_LEFT_CHUNKS = 8
HEAD_DIM = 128
MAX_REL = 128
N_REL = 2 * MAX_REL + 1
N_REL_PAD = 384
POOL_WINDOWS = (2, 4, 8, 16)
HALO = 16
EPS = 1e-6
ADAM_LR = 0.001
ADAM_B1 = 0.9
ADAM_B2 = 0.999
ADAM_EPS = 1e-08
ADAM_WD = 0.01
ADAM_STEP = 10
NEG = -1e30
LANES = 128
TQ = N_LEFT_CHUNKS * CHUNK
TK = 2 * TQ
SKEW = 2 * TK
VMEM_LIMIT = 52 * 1024 * 1024

NN = (((1,), (0,)), ((), ()))
NT = (((1,), (1,)), ((), ()))
TN = (((0,), (0,)), ((), ()))


def _params(n_grid):
    return pltpu.CompilerParams(dimension_semantics=("arbitrary",) * n_grid, vmem_limit_bytes=VMEM_LIMIT)


def _sig(z):
    return 1.0 / (1.0 + jnp.exp(-z))


def _silu_and_grad(z):
    s = _sig(z)
    return z * s, s * (1.0 + z * (1.0 - s))


def _tile(n, pref):
    t = min(n, pref)
    assert n % t == 0, (n, pref)
    return t


def _sds(shape, dtype):
    return jax.ShapeDtypeStruct(shape, dtype)


class _Comm:
    def __init__(self, ins, outs, scratch, start, wait, mid=None):
        self.ins, self.outs, self.scratch = list(ins), list(outs), list(scratch)
        self.start, self.wait, self.mid = start, wait, mid


def _call(name, body, grid, ins, outs, scratch=(), aliases=None, comm=None):
    n_in, n_out, n_scr = len(ins), len(outs), len(scratch)
    c_in = len(comm.ins) if comm else 0
    c_out = len(comm.outs) if comm else 0
    n_steps = 1
    for g in grid:
        n_steps *= g

    def kern(*refs):
        o0 = n_in + c_in
        s0 = o0 + n_out + c_out
        if comm:
            c_refs = (refs[n_in:o0], refs[o0 + n_out:s0], refs[s0 + n_scr:])
            step = pl.program_id(0)
            for ax in range(1, len(grid)):
                step = step * grid[ax] + pl.program_id(ax)

            @pl.when(step == 0)
            def _():
                comm.start(*c_refs)

            if comm.mid is not None:
                @pl.when(step == (3 * n_steps) // 4)
                def _():
                    comm.mid(*c_refs)

        body(refs[:n_in], refs[o0:o0 + n_out], refs[s0:s0 + n_scr])

        if comm:
            @pl.when(step == n_steps - 1)
            def _():
                comm.wait(*c_refs)

    hbm = pl.BlockSpec(memory_space=pl.ANY)
    return pl.pallas_call(
        kern, grid=grid,
        in_specs=[s for _, s in ins] + [hbm] * c_in, out_specs=[s for _, s in outs] + [hbm] * c_out,
        out_shape=[o for o, _ in outs] + (comm.outs if comm else []),
        scratch_shapes=list(scratch) + (comm.scratch if comm else []),
        name=name, compiler_params=_params(len(grid)), input_output_aliases=aliases or {},
    )(*([a for a, _ in ins] + (comm.ins if comm else [])))


def _mm(name, grid, ins, outs, dims, epi, aliases=None, comm=None):
    def body(in_refs, out_refs, _):
        acc = lax.dot_general(in_refs[0][...], in_refs[1][...], dims, preferred_element_type=F32)
        epi(acc, in_refs[2:], out_refs)

    return _call(name, body, grid, ins, outs, aliases=aliases, comm=comm)


def _rms_fwd(x, g, comm=None):
    s, d = x.shape
    tr = _tile(s, 256)

    def body(in_refs, out_refs, _):
        xv = in_refs[0][...]
        r = lax.rsqrt(jnp.mean(xv * xv, axis=-1, keepdims=True) + EPS)
        out_refs[0][...] = (xv * r * in_refs[1][...]).astype(BF)

    row = pl.BlockSpec((tr, d), lambda i: (i, 0))
    return _call("rms_fwd", body, (s // tr,), [(x, row), (g, pl.BlockSpec((1, d), lambda i: (0, 0)))],
                 [(_sds((s, d), BF), row)], comm=comm)


def _final_norm(x2, target, g):
    s, d = x2.shape
    tr = _tile(s, 128)

    def kern(x_ref, t_ref, g_ref, dx_ref, dxb_ref, dg_ref, loss_ref):
        i = pl.program_id(0)
        xv = x_ref[...]
        gv = g_ref[...]
        r = lax.rsqrt(jnp.mean(xv * xv, axis=-1, keepdims=True) + EPS)
        xhat = xv * r
        err = xhat * gv - t_ref[...]
        dy = err * (1.0 / d)
        gy = dy * gv
        dx = r * (gy - xhat * jnp.mean(gy * xhat, axis=-1, keepdims=True))
        dx_ref[...] = dx
        dxb_ref[...] = dx.astype(BF)
        dg = jnp.sum(dy * xhat, axis=0, keepdims=True)
        ls = jnp.broadcast_to(0.5 * jnp.sum(jnp.mean(err * err, axis=-1, keepdims=True)), (1, LANES))

        @pl.when(i == 0)
        def _():
            dg_ref[...] = dg
            loss_ref[...] = ls

        @pl.when(i > 0)
        def _():
            dg_ref[...] += dg
            loss_ref[...] += ls

    row = pl.BlockSpec((tr, d), lambda i: (i, 0))
    vec = pl.BlockSpec((1, d), lambda i: (0, 0))
    return pl.pallas_call(
        kern, grid=(s // tr,), in_specs=[row, row, vec],
        out_specs=[row, row, vec, pl.BlockSpec((1, LANES), lambda i: (0, 0))],
        out_shape=[_sds((s, d), F32), _sds((s, d), BF), _sds((1, d), F32), _sds((1, LANES), F32)],
        name="final_norm", compiler_params=_params(1))(x2, target, g)


def _rms_bwd(x, dh, dx2, g):
    s, d = x.shape
    tr = _tile(s, 128)

    def kern(x_ref, dh_ref, dx2_ref, g_ref, dx_ref, dg_ref):
        i = pl.program_id(0)
        xv = x_ref[...]
        r = lax.rsqrt(jnp.mean(xv * xv, axis=-1, keepdims=True) + EPS)
        xhat = xv * r
        dhv = dh_ref[...]
        gh = dhv * g_ref[...]
        dx_ref[...] = dx2_ref[...] + r * (gh - xhat * jnp.mean(gh * xhat, axis=-1, keepdims=True))
        dg = jnp.sum(dhv * xhat, axis=0, keepdims=True)

        @pl.when(i == 0)
        def _():
            dg_ref[...] = dg

        @pl.when(i > 0)
        def _():
            dg_ref[...] += dg

    row = pl.BlockSpec((tr, d), lambda i: (i, 0))
    vec = pl.BlockSpec((1, d), lambda i: (0, 0))
    return pl.pallas_call(
        kern, grid=(s // tr,), in_specs=[row, row, row, vec], out_specs=[row, vec],
        out_shape=[_sds((s, d), F32), _sds((1, d), F32)],
        name="rms_bwd", compiler_params=_params(1))(x, dh, dx2, g)


def _rel_index(j, backward):
    if backward:
        rel = 2 * TQ - 1 - j
    else:
        rel = TQ - jnp.where(j < TK, j, j - SKEW)
    return jnp.clip(rel, -MAX_REL, MAX_REL) + MAX_REL


def _bias_rows(rel_bias_pad):
    h = rel_bias_pad.shape[0]

    def kern(rb_ref, o_ref):
        j = lax.broadcasted_iota(jnp.int32, (N_REL_PAD, SKEW), 1)
        k = lax.broadcasted_iota(jnp.int32, (N_REL_PAD, SKEW), 0)
        onehot = (_rel_index(j, False) == k).astype(F32)
        o_ref[...] = jnp.dot(rb_ref[...], onehot, preferred_element_type=F32, precision=lax.Precision.HIGHEST)

    return pl.pallas_call(kern, out_shape=_sds((h, SKEW), F32), name="bias_rows")(rel_bias_pad)


def _bias_grad(ddiag):
    h = ddiag.shape[0]

    def kern(d_ref, o_ref):
        j = lax.broadcasted_iota(jnp.int32, (N_REL_PAD, SKEW), 1)
        k = lax.broadcasted_iota(jnp.int32, (N_REL_PAD, SKEW), 0)
        onehot = ((_rel_index(j, True) == k) & (j < TQ + TK - 1)).astype(F32)
        o_ref[...] = lax.dot_general(d_ref[...], onehot, NT, preferred_element_type=F32,
                                     precision=lax.Precision.HIGHEST)

    return pl.pallas_call(kern, out_shape=_sds((h, N_REL_PAD), F32), name="bias_grad")(ddiag)


def _bias_tile(row):
    t = pltpu.roll(jnp.broadcast_to(row, (TQ, SKEW)), 0, 1, stride=1, stride_axis=0)[:, :TK]
    r = lax.broadcasted_iota(jnp.int32, (TQ, TK), 0) // CHUNK
    c = lax.broadcasted_iota(jnp.int32, (TQ, TK), 1) // CHUNK
    dist = N_LEFT_CHUNKS + r - c
    return jnp.where((dist >= 0) & (dist <= N_LEFT_CHUNKS), t, NEG)


def _scores(q, kcat, tile, first):
    s = lax.dot_general(q, kcat, NT, preferred_element_type=F32) * (HEAD_DIM ** -0.5) + tile
    col = lax.broadcasted_iota(jnp.int32, (TQ, TK), 1)
    s = jnp.where(first & (col < TQ), NEG, s)
    m = jnp.max(s, axis=1, keepdims=True)
    p = jnp.exp(s - m)
    return p, jnp.sum(p, axis=1, keepdims=True)


def _attn_fwd(proj, base, a_width, comm=None):
    s_len = proj.shape[0]
    heads = a_width // HEAD_DIM
    nq = s_len // TQ
    kb, vb, zb = heads, 2 * heads, 3 * heads

    def kern(q_ref, kp_ref, kc_ref, vp_ref, vc_ref, z_ref, base_ref, att_ref, ya_ref, tile_ref):
        i = pl.program_id(1)

        @pl.when(i == 0)
        def _():
            tile_ref[...] = _bias_tile(base_ref[...])

        kcat = jnp.concatenate([kp_ref[...], kc_ref[...]], axis=0)
        vcat = jnp.concatenate([vp_ref[...], vc_ref[...]], axis=0)
        p, l = _scores(q_ref[...], kcat, tile_ref[...], i == 0)
        o = jnp.dot(p.astype(BF), vcat, preferred_element_type=F32) / l
        att_ref[...] = o.astype(BF)
        z = z_ref[...].astype(F32)
        ya_ref[...] = (o * (z * _sig(z))).astype(BF)

    blk = lambda off: pl.BlockSpec((TQ, HEAD_DIM), lambda h, i: (i, off + h))
    prev = lambda off: pl.BlockSpec((TQ, HEAD_DIM), lambda h, i: (jnp.maximum(i - 1, 0), off + h))
    out = pl.BlockSpec((TQ, HEAD_DIM), lambda h, i: (i, h))
    def body(in_refs, out_refs, scratch_refs):
        kern(*in_refs, *out_refs, *scratch_refs)

    act = _sds((s_len, a_width), BF)
    res = _call(
        "attn_fwd", body, (heads, nq),
        [(proj, blk(0)), (proj, prev(kb)), (proj, blk(kb)), (proj, prev(vb)), (proj, blk(vb)), (proj, blk(zb)),
         (base, pl.BlockSpec((None, 1, SKEW), lambda h, i: (h, 0, 0)))],
        [(act, out), (act, out)], scratch=[pltpu.VMEM((TQ, TK), F32)], comm=comm)
    return res[:2], res[2:]


def _attn_bwd(proj, datt, base, a_width, comm=None):
    s_len = proj.shape[0]
    heads = a_width // HEAD_DIM
    nq = s_len // TQ
    kb, vb = heads, 2 * heads
    scale = HEAD_DIM ** -0.5

    def body(in_refs, out_refs, scratch_refs):
        q_ref, kp_ref, kc_ref, vp_ref, vc_ref, do_ref, base_ref = in_refs
        dq_ref, dk_ref, dv_ref, dd_ref = out_refs
        tile_ref, dsacc_ref, ak_ref, av_ref = scratch_refs
        i = pl.program_id(1)

        @pl.when(i == 0)
        def _():
            tile_ref[...] = _bias_tile(base_ref[...])
            dsacc_ref[...] = jnp.zeros_like(dsacc_ref)
            ak_ref[...] = jnp.zeros_like(ak_ref)
            av_ref[...] = jnp.zeros_like(av_ref)

        @pl.when(i < nq)
        def _():
            q = q_ref[...]
            do = do_ref[...]
            kcat = jnp.concatenate([kp_ref[...], kc_ref[...]], axis=0)
            vcat = jnp.concatenate([vp_ref[...], vc_ref[...]], axis=0)
            p, l = _scores(q, kcat, tile_ref[...], i == 0)
            p = p / l
            dp = lax.dot_general(do, vcat, NT, preferred_element_type=F32)
            ds = p * (dp - jnp.sum(p * dp, axis=1, keepdims=True))
            dsacc_ref[...] += ds
            dsb = ds.astype(BF)
            dq_ref[...] = (jnp.dot(dsb, kcat, preferred_element_type=F32) * scale).astype(BF)
            dkc = lax.dot_general(dsb, q, TN, preferred_element_type=F32) * scale
            dvc = lax.dot_general(p.astype(BF), do, TN, preferred_element_type=F32)
            dk_ref[...] = (ak_ref[...] + dkc[:TQ]).astype(BF)
            dv_ref[...] = (av_ref[...] + dvc[:TQ]).astype(BF)
            ak_ref[...] = dkc[TQ:]
            av_ref[...] = dvc[TQ:]

        @pl.when(i == nq)
        def _():
            dk_ref[...] = ak_ref[...].astype(BF)
            dv_ref[...] = av_ref[...].astype(BF)
            acc = dsacc_ref[...]
            rr = lax.broadcasted_iota(jnp.int32, (TQ, TQ), 0)
            cc = lax.broadcasted_iota(jnp.int32, (TQ, TQ), 1)
            flip = (rr + cc == TQ - 1).astype(BF)
            hi = acc.astype(BF)
            lo = (acc - hi.astype(F32)).astype(BF)
            rev = jnp.dot(flip, hi, preferred_element_type=F32) + jnp.dot(flip, lo, preferred_element_type=F32)
            wide = jnp.concatenate([rev, jnp.zeros((TQ, SKEW - TK), F32)], axis=1)
            dd_ref[...] = jnp.sum(pltpu.roll(wide, 0, 1, stride=1, stride_axis=0), axis=0, keepdims=True)

    last = nq - 1
    cur = lambda off: pl.BlockSpec((TQ, HEAD_DIM), lambda h, i: (jnp.minimum(i, last), off + h))
    prev = lambda off: pl.BlockSpec((TQ, HEAD_DIM), lambda h, i: (jnp.maximum(jnp.minimum(i, last) - 1, 0), off + h))
    done = pl.BlockSpec((TQ, HEAD_DIM), lambda h, i: (jnp.maximum(i - 1, 0), h))
    row = pl.BlockSpec((None, 1, SKEW), lambda h, i: (h, 0, 0))
    act = _sds((s_len, a_width), BF)
    res = _call(
        "attn_bwd", body, (heads, nq + 1),
        [(proj, cur(0)), (proj, prev(kb)), (proj, cur(kb)), (proj, prev(vb)), (proj, cur(vb)), (datt, cur(0)),
         (base, row)],
        [(act, cur(0)), (act, done), (act, done), (_sds((heads, 1, SKEW), F32), row)],
        scratch=[pltpu.VMEM((TQ, TK), F32), pltpu.VMEM((TQ, TK), F32),
                 pltpu.VMEM((TQ, HEAD_DIM), F32), pltpu.VMEM((TQ, HEAD_DIM), F32)],
        comm=comm)
    return res[:4], res[4:]


def _pool_fwd(proj, pool_w, pool_scale, p_width, u_blk, z_blk):
    s_len = proj.shape[0]
    cg = p_width // len(POOL_WINDOWS)
    tt = _tile(s_len, 512)

    def kern(up_ref, uc_ref, z_ref, pw_ref, sc_ref, d_ref, y_ref, yp_ref):
        t = pl.program_id(0)
        row = lax.broadcasted_iota(jnp.int32, (tt, 1), 0) + t * tt
        for g, w in enumerate(POOL_WINDOWS):
            cs = slice(g * cg, (g + 1) * cg)
            prev = jnp.where(t == 0, 0.0, up_ref[:, cs].astype(F32))
            cur = uc_ref[:, cs].astype(F32)
            ws = jnp.concatenate([prev, cur], axis=0)
            sh = 1
            while sh < w:
                ws = ws + pltpu.roll(ws, sh, 0)
                sh *= 2
            cnt = jnp.minimum(row + 1, w).astype(F32)
            db = (ws[HALO:, :] / cnt - cur).astype(BF)
            y = jnp.dot(db, pw_ref[g], preferred_element_type=F32)
            d_ref[:, cs] = db
            y_ref[:, cs] = y.astype(BF)
            z = z_ref[:, cs].astype(F32)
            yp_ref[:, cs] = (y * sc_ref[:, cs] * (z * _sig(z))).astype(BF)

    full = pl.BlockSpec((tt, p_width), lambda t: (t, 0))
    return pl.pallas_call(
        kern, grid=(s_len // tt,),
        in_specs=[pl.BlockSpec((HALO, p_width), lambda t: (jnp.maximum(t * (tt // HALO) - 1, 0), u_blk)),
                  pl.BlockSpec((tt, p_width), lambda t: (t, u_blk)),
                  pl.BlockSpec((tt, p_width), lambda t: (t, z_blk)),
                  pl.BlockSpec((len(POOL_WINDOWS), cg, cg), lambda t: (0, 0, 0)),
                  pl.BlockSpec((1, p_width), lambda t: (0, 0))],
        out_specs=[full, full, full], out_shape=[_sds((s_len, p_width), BF)] * 3,
        name="pool_fwd", compiler_params=_params(1))(proj, proj, proj, pool_w, pool_scale)


def _pool_bwd(dy, dmean, pool_w):
    s_len, p_width = dy.shape
    ng = len(POOL_WINDOWS)
    cg = p_width // ng
    tt = _tile(s_len, 512)
    nt = s_len // tt

    def kern(dyc_ref, dyn_ref, d_ref, pw_ref, du_ref, dpw_ref):
        t = pl.program_id(0)

        @pl.when(t == 0)
        def _():
            dpw_ref[...] = jnp.zeros_like(dpw_ref)

        row = lax.broadcasted_iota(jnp.int32, (tt + HALO, 1), 0) + t * tt
        for g, w in enumerate(POOL_WINDOWS):
            cs = slice(g * cg, (g + 1) * cg)
            dyc = dyc_ref[:, cs]
            ddc = lax.dot_general(dyc, pw_ref[g], NT, preferred_element_type=F32)
            ddn = lax.dot_general(dyn_ref[:, cs], pw_ref[g], NT, preferred_element_type=F32)
            ddn = jnp.where(t == nt - 1, 0.0, ddn)
            cnt = jnp.minimum(row + 1, w).astype(F32)
            ws = jnp.concatenate([ddc, ddn], axis=0) / cnt
            sh = 1
            while sh < w:
                ws = ws + pltpu.roll(ws, tt + HALO - sh, 0)
                sh *= 2
            du_ref[:, cs] = (ws[:tt, :] - ddc).astype(BF)
            dpw_ref[g] += lax.dot_general(d_ref[:, cs], dyc, TN, preferred_element_type=F32)

    full = pl.BlockSpec((tt, p_width), lambda t: (t, 0))
    pw_spec = pl.BlockSpec((ng, cg, cg), lambda t: (0, 0, 0))
    return pl.pallas_call(
        kern, grid=(nt,),
        in_specs=[full,
                  pl.BlockSpec((HALO, p_width), lambda t: (jnp.minimum((t + 1) * (tt // HALO), s_len // HALO - 1), 0)),
                  full, pw_spec],
        out_specs=[full, pw_spec], out_shape=[_sds((s_len, p_width), BF), _sds((ng, cg, cg), F32)],
        name="pool_bwd", compiler_params=_params(1))(dy, dy, dmean, pool_w)


def _adam(g, w_ref, m_ref, v_ref, g_out, d_out, m_out, v_out):
    m = ADAM_B1 * m_ref[...] + (1.0 - ADAM_B1) * g
    v = ADAM_B2 * v_ref[...] + (1.0 - ADAM_B2) * (g * g)
    m_hat = m / (1.0 - ADAM_B1 ** ADAM_STEP)
    v_hat = v / (1.0 - ADAM_B2 ** ADAM_STEP)
    g_out[...] = g
    d_out[...] = -ADAM_LR * (m_hat / (jnp.sqrt(v_hat) + ADAM_EPS) + ADAM_WD * w_ref[...])
    m_out[...] = m
    v_out[...] = v


def _adamw_shard(name, parts, w, m, v, row_off):
    rw, cw = w.shape
    sw = parts.shape[2]
    tr = _tile(rw, 512)
    assert row_off % tr == 0 and cw % sw == 0

    def kern(b_ref, w_ref, m_ref, v_ref, g_out, d_out, m_out, v_out):
        b = b_ref[...].astype(F32)
        _adam(((b[0] + b[1]) + b[2]) + b[3], w_ref, m_ref, v_ref, g_out, d_out, m_out, v_out)

    blk = pl.BlockSpec((tr, sw), lambda ct, i: (i, ct))
    return pl.pallas_call(
        kern, grid=(cw // sw, rw // tr),
        in_specs=[pl.BlockSpec((4, tr, sw), lambda ct, i: (0, (row_off + ct * rw) // tr + i, 0)), blk, blk, blk],
        out_specs=[blk] * 4, out_shape=[_sds((rw, cw), F32)] * 4,
        name=name, compiler_params=_params(2))(parts, w, m, v)


def _adamw_row_halves(name, parts_lo, parts_hi, w, m, v):
    rw, cw = w.shape
    sw = parts_lo.shape[2]
    half = rw // 2
    tr = _tile(half, 512)
    nh = half // tr

    def kern(lo_ref, hi_ref, w_ref, m_ref, v_ref, g_out, d_out, m_out, v_out):
        i = pl.program_id(1)

        def update(b_ref):
            b = b_ref[...].astype(F32)
            _adam(((b[0] + b[1]) + b[2]) + b[3], w_ref, m_ref, v_ref, g_out, d_out, m_out, v_out)

        @pl.when(i < nh)
        def _():
            update(lo_ref)

        @pl.when(i >= nh)
        def _():
            update(hi_ref)

    blk = pl.BlockSpec((tr, sw), lambda ct, i: (i, ct))
    return pl.pallas_call(
        kern, grid=(cw // sw, rw // tr),
        in_specs=[pl.BlockSpec((4, tr, sw), lambda ct, i: (0, ct * nh + jnp.minimum(i, nh - 1), 0)),
                  pl.BlockSpec((4, tr, sw), lambda ct, i: (0, ct * nh + jnp.maximum(i - nh, 0), 0)), blk, blk, blk],
        out_specs=[blk] * 4, out_shape=[_sds((rw, cw), F32)] * 4,
        name=name, compiler_params=_params(2))(parts_lo, parts_hi, w, m, v)


def _both(c1, c2):
    n_in, n_out, n_sem = len(c1.ins), len(c1.outs), len(c1.scratch)

    def split(ins, outs, sems):
        return (ins[:n_in], outs[:n_out], sems[:n_sem]), (ins[n_in:], outs[n_out:], sems[n_sem:])

    def start(*refs):
        r1, r2 = split(*refs)
        c1.start(*r1)
        c2.start(*r2)

    def wait(*refs):
        r1, r2 = split(*refs)
        c1.wait(*r1)
        c2.wait(*r2)

    return _Comm(c1.ins + c2.ins, c1.outs + c2.outs, c1.scratch + c2.scratch, start, wait)


def _position():
    return lax.axis_index("x"), lax.axis_index("y"), lax.axis_index("c")


def _gather_comm(shards, cols=None):
    n = len(shards)
    shapes = [_sds(s.shape if cols is None else (s.shape[0], cols[1]), s.dtype) for s in shards]

    def plan(xs, outs, sems):
        send_sems, recv_sems, local_sems = sems
        if cols is not None:
            xs = [xr.at[:, pl.ds(cols[0], cols[1])] for xr in xs]
        x, y, c = _position()
        me, sibling = (x, y, c), (x, y, 1 - c)
        chips = [(1 - x, y), (x, 1 - y), (1 - x, 1 - y)]

        def copy(a, k, block, to, src=None):
            dst = outs[a].at[4 * block[0] + 2 * block[1] + block[2]]
            return pltpu.make_async_remote_copy(
                src_ref=dst if src is None else src, dst_ref=dst,
                send_sem=send_sems.at[7 * a + k], recv_sem=recv_sems.at[7 * a + k],
                device_id=to, device_id_type=MESH)

        by_chip = [(j, chip, a) for j, chip in enumerate(chips) for a in range(n)]
        return dict(
            mine=lambda: [pltpu.make_async_copy(xs[a], outs[a].at[4 * x + 2 * y + c], local_sems.at[a])
                          for a in range(n)],
            first=lambda: ([copy(a, 0, me, sibling, src=xs[a]) for a in range(n)]
                           + [copy(a, 1 + j, me, (*chip, c), src=xs[a]) for j, chip, a in by_chip]),
            landed=lambda: [copy(a, 1 + j, (*chip, c), me) for j, chip, a in by_chip],
            passed=lambda: [copy(a, 4 + j, (*chip, c), sibling) for j, chip, a in by_chip],
            rest=lambda: ([copy(a, 0, sibling, me) for a in range(n)]
                          + [copy(a, 4 + j, (*chip, 1 - c), me) for j, chip, a in by_chip]))

    def start(*refs):
        p = plan(*refs)
        for cp in p["mine"]() + p["first"]():
            cp.start()

    def mid(*refs):
        p = plan(*refs)
        for arrived, onward in zip(p["landed"](), p["passed"]()):
            arrived.wait_recv()
            onward.start()

    def wait(*refs):
        p = plan(*refs)
        for cp in p["rest"]():
            cp.wait_recv()
        for cp in p["first"]() + p["passed"]():
            cp.wait_send()
        for cp in p["mine"]():
            cp.wait()

    return _Comm(shards, [_sds((N_DEV,) + s.shape, s.dtype) for s in shapes],
                 [pltpu.SemaphoreType.DMA((7 * n,)), pltpu.SemaphoreType.DMA((7 * n,)),
                  pltpu.SemaphoreType.DMA((n,))], start, wait, mid)


def _cores_comm(slab):
    _, _, r, sw = slab.shape

    def copies(ins, outs, sems):
        x, y, c = _position()
        return [pltpu.make_async_remote_copy(
            src_ref=ins[0].at[1 - c], dst_ref=outs[0], send_sem=sems[0], recv_sem=sems[1],
            device_id=(x, y, 1 - c), device_id_type=MESH)]

    def start(*refs):
        for cp in copies(*refs):
            cp.start()

    def wait(*refs):
        for cp in copies(*refs):
            cp.wait()

    return _Comm([slab], [_sds((4, r, sw), slab.dtype)],
                 [pltpu.SemaphoreType.DMA, pltpu.SemaphoreType.DMA], start, wait)


def _add_core_partials(name, slab, recv, core, tr):
    _, _, r, sw = slab.shape

    def kern(c_ref, a_ref, b_ref, o_ref):
        o_ref[...] = (a_ref[...].astype(F32) + b_ref[...].astype(F32)).astype(BF)

    return pl.pallas_call(
        kern,
        grid_spec=pltpu.PrefetchScalarGridSpec(
            num_scalar_prefetch=1, grid=(4, r // tr),
            in_specs=[pl.BlockSpec((None, None, tr, sw), lambda k, i, c_ref: (c_ref[0], k, i, 0)),
                      pl.BlockSpec((None, tr, sw), lambda k, i, c_ref: (k, i, 0))],
            out_specs=pl.BlockSpec((None, tr, sw), lambda k, i, c_ref: (k, i, 0))),
        out_shape=_sds((4, r, sw), BF), name=name, compiler_params=_params(2))(core, slab, recv)


def _chips_comm(part):
    _, r, sw = part.shape

    def copies(ins, outs, sems):
        send_sems, recv_sems, local_sem = sems
        x, y, c = _position()
        mine = 2 * x + y
        local = pltpu.make_async_copy(ins[0].at[mine], outs[0].at[mine], local_sem)
        chips = [(1 - x, y), (x, 1 - y), (1 - x, 1 - y)]
        remote = [pltpu.make_async_remote_copy(
            src_ref=ins[0].at[2 * px + py], dst_ref=outs[0].at[mine],
            send_sem=send_sems.at[j], recv_sem=recv_sems.at[j],
            device_id=(px, py, c), device_id_type=MESH) for j, (px, py) in enumerate(chips)]
        return [local] + remote

    def start(*refs):
        for cp in copies(*refs):
            cp.start()

    def wait(*refs):
        for cp in copies(*refs):
            cp.wait()

    return _Comm([part], [_sds((4, r, sw), part.dtype)],
                 [pltpu.SemaphoreType.DMA((3,)), pltpu.SemaphoreType.DMA((3,)), pltpu.SemaphoreType.DMA],
                 start, wait)


def _small_allreduce_adamw(partial, w, m, v):
    nr = partial.shape[0]

    def kern(p_ref, w_ref, m_ref, v_ref, g_out, d_out, m_out, v_out, gath_ref, send_sems, recv_sems):
        x, y, c = _position()
        me = 4 * x + 2 * y + c
        gath_ref[me] = p_ref[...]
        copies = []
        for mask in range(1, N_DEV):
            peer = (x ^ (mask >> 2), y ^ ((mask >> 1) & 1), c ^ (mask & 1))
            copies.append(pltpu.make_async_remote_copy(
                src_ref=p_ref, dst_ref=gath_ref.at[me],
                send_sem=send_sems.at[mask - 1], recv_sem=recv_sems.at[mask - 1],
                device_id=peer, device_id_type=MESH))
        for cp in copies:
            cp.start()
        for cp in copies:
            cp.wait()
        tot = gath_ref[0]
        for k in range(1, N_DEV):
            tot = tot + gath_ref[k]
        _adam(tot, w_ref, m_ref, v_ref, g_out, d_out, m_out, v_out)

    vmem = pl.BlockSpec(memory_space=pltpu.VMEM)
    return pl.pallas_call(
        kern, in_specs=[vmem] * 4, out_specs=[vmem] * 4, out_shape=[_sds((nr, LANES), F32)] * 4,
        scratch_shapes=[pltpu.VMEM((N_DEV, nr, LANES), F32),
                        pltpu.SemaphoreType.DMA((N_DEV - 1,)), pltpu.SemaphoreType.DMA((N_DEV - 1,))],
        name="small_allreduce_adamw")(partial, w, m, v)


def kernel(x, norm_gain, w_in, rel_bias, pool_w, pool_scale, w_out_attn, w_out_pool, gate_bias, w_out, final_gain, loss_target, m_norm_gain, m_w_in, m_rel_bias, m_pool_w, m_pool_scale, m_w_out_attn, m_w_out_pool, m_gate_bias, m_w_out, m_final_gain, v_norm_gain, v_w_in, v_rel_bias, v_pool_w, v_pool_scale, v_w_out_attn, v_w_out_pool, v_gate_bias, v_w_out, v_final_gain):
    _, s_len, d = x.shape
    a = w_out_attn.shape[0]
    p = w_out_pool.shape[0]
    heads = a // HEAD_DIM
    ng = len(POOL_WINDOWS)
    cg = p // ng
    sw = d // N_DEV
    n_in = w_in.shape[1] * N_DEV
    assert a == p and a + p == d and cg == sw and n_in == 5 * d and w_in.shape[1] == 5 * sw
    assert s_len % TQ == 0 and rel_bias.shape == (heads, N_REL)
    tm = _tile(s_len, 1024)
    x2d = x.reshape(s_len, d)
    tgt = loss_target.reshape(s_len, d)

    g1 = norm_gain.reshape(1, d)
    g2 = final_gain.reshape(1, d)
    scale_row = pool_scale.reshape(1, p)
    tn = sw
    per = w_in.shape[1] // tn
    hbm = pl.BlockSpec(memory_space=pl.ANY)

    w_bf = w_in.astype(BF)
    hb, landed = _rms_fwd(x2d, g1, comm=_gather_comm([w_bf], cols=(0, tn)))
    win_rounds = []

    def store_bf16(acc, _, outs):
        outs[0][...] = acc.astype(BF)

    proj = None
    for r in range(per):
        win_rounds.append(landed)
        if r + 1 < per:
            comm = _gather_comm([w_bf], cols=((r + 1) * tn, tn))
        else:
            comm = _gather_comm([w_out_attn.astype(BF), w_out_pool.astype(BF), pool_w.astype(BF), gate_bias])
        ins = [(hb, pl.BlockSpec((tm, d), lambda i, j: (i, 0))),
               (landed, pl.BlockSpec((None, d, tn), lambda i, j: (j, 0, 0)))]
        if proj is not None:
            ins.append((proj, hbm))
        proj, landed, *rest = _mm(
            f"proj_{r}", (s_len // tm, N_DEV), ins,
            [(_sds((s_len, n_in), BF), pl.BlockSpec((tm, tn), lambda i, j, r=r: (i, per * j + r)))],
            NN, store_bf16, aliases={2: 0} if r else None, comm=comm)
    woa = landed.transpose(1, 0, 2).reshape(a, d)
    wop = rest[0].transpose(1, 0, 2).reshape(p, d)
    pw = rest[1].transpose(1, 0, 2, 3).reshape(ng, cg, cg)
    gb = rest[2].transpose(1, 0, 2).reshape(2, d)

    rb_pad = jnp.pad(rel_bias, ((0, 0), (0, N_REL_PAD - N_REL)))
    base = _bias_rows(rb_pad).reshape(heads, 1, SKEW)
    (att, ya), (wo_g,) = _attn_fwd(proj, base, a, comm=_gather_comm([w_out.astype(BF)]))
    wo = wo_g.reshape(d, d)
    u_blk, z_blk = 4 * a // p, 4 * a // p + 1
    dmean, ypre, yp = _pool_fwd(proj, pw, scale_row, p, u_blk, z_blk)

    ga_t, gp_t = (4 * a + 2 * p) // tn, (4 * a + 2 * p + d) // tn

    def gate_kernel(ya_ref, woa_ref, yp_ref, wop_ref, ga_ref, gp_ref, gb_ref, m_ref, a_ref, p_ref):
        am = jnp.dot(ya_ref[...], woa_ref[...], preferred_element_type=F32)
        pm = jnp.dot(yp_ref[...], wop_ref[...], preferred_element_type=F32)
        sa = _sig(ga_ref[...].astype(F32) + gb_ref[0:1, :])
        sp = _sig(gp_ref[...].astype(F32) + gb_ref[1:2, :])
        m_ref[...] = (sa * am + sp * pm).astype(BF)
        a_ref[...] = am.astype(BF)
        p_ref[...] = pm.astype(BF)

    tile_ij = pl.BlockSpec((tm, tn), lambda i, j: (i, j))
    merged, am, pm = pl.pallas_call(
        gate_kernel, grid=(s_len // tm, d // tn),
        in_specs=[pl.BlockSpec((tm, a), lambda i, j: (i, 0)), pl.BlockSpec((a, tn), lambda i, j: (0, j)),
                  pl.BlockSpec((tm, p), lambda i, j: (i, 0)), pl.BlockSpec((p, tn), lambda i, j: (0, j)),
                  pl.BlockSpec((tm, tn), lambda i, j: (i, ga_t + j)), pl.BlockSpec((tm, tn), lambda i, j: (i, gp_t + j)),
                  pl.BlockSpec((2, tn), lambda i, j: (0, j))],
        out_specs=[tile_ij] * 3, out_shape=[_sds((s_len, d), BF)] * 3,
        name="gate_merge", compiler_params=_params(2))(ya, woa, yp, wop, proj, proj, gb)

    def add_residual(acc, ex, outs):
        outs[0][...] = ex[0][...] + acc

    x2 = _mm("out_proj", (s_len // tm, d // tn),
             [(merged, pl.BlockSpec((tm, d), lambda i, j: (i, 0))), (wo, pl.BlockSpec((d, tn), lambda i, j: (0, j))),
              (x2d, tile_ij)],
             [(_sds((s_len, d), F32), tile_ij)], NN, add_residual)[0]

    dx2, dx2b, dg2, loss_part = _final_norm(x2, tgt, g2)

    tmb = _tile(s_len, 512)
    tile_ji = pl.BlockSpec((tmb, tn), lambda j, i: (i, j))

    def gate_bwd(dm, ex, outs):
        a_ref, p_ref, ga_ref, gp_ref, gb_ref = ex
        da_ref, dp_ref, dga_ref, dgp_ref, dgb_ref = outs
        i = pl.program_id(1)
        sa = _sig(ga_ref[...].astype(F32) + gb_ref[0:1, :])
        sp = _sig(gp_ref[...].astype(F32) + gb_ref[1:2, :])
        dga = dm * a_ref[...].astype(F32) * sa * (1.0 - sa)
        dgp = dm * p_ref[...].astype(F32) * sp * (1.0 - sp)
        da_ref[...] = (dm * sa).astype(BF)
        dp_ref[...] = (dm * sp).astype(BF)
        dga_ref[...] = dga.astype(BF)
        dgp_ref[...] = dgp.astype(BF)
        r = lax.broadcasted_iota(jnp.int32, (8, tn), 0)
        sums = jnp.where(r == 0, jnp.sum(dga, axis=0, keepdims=True),
                         jnp.where(r == 1, jnp.sum(dgp, axis=0, keepdims=True), 0.0))

        @pl.when(i == 0)
        def _():
            dgb_ref[...] = sums

        @pl.when(i > 0)
        def _():
            dgb_ref[...] += sums

    d_am, d_pm, dga, dgp, dgb8 = _mm(
        "gate_bwd", (d // tn, s_len // tmb),
        [(dx2b, pl.BlockSpec((tmb, d), lambda j, i: (i, 0))), (wo, pl.BlockSpec((tn, d), lambda j, i: (j, 0))),
         (am, tile_ji), (pm, tile_ji),
         (proj, pl.BlockSpec((tmb, tn), lambda j, i: (i, ga_t + j))),
         (proj, pl.BlockSpec((tmb, tn), lambda j, i: (i, gp_t + j))),
         (gb, pl.BlockSpec((2, tn), lambda j, i: (0, j)))],
        [(_sds((s_len, d), BF), tile_ji)] * 4 + [(_sds((8, d), F32), pl.BlockSpec((8, tn), lambda j, i: (0, j)))],
        NT, gate_bwd)

    za_t = 3 * a // tn

    def attn_gate_bwd(dya, ex, outs):
        silu, dsilu = _silu_and_grad(ex[0][...].astype(F32))
        outs[0][...] = (dya * silu).astype(BF)
        outs[1][...] = (dya * ex[1][...].astype(F32) * dsilu).astype(BF)

    datt, dza = _mm(
        "attn_gate_bwd", (s_len // tm, a // tn),
        [(d_am, pl.BlockSpec((tm, d), lambda i, j: (i, 0))), (woa, pl.BlockSpec((tn, d), lambda i, j: (j, 0))),
         (proj, pl.BlockSpec((tm, tn), lambda i, j: (i, za_t + j))), (att, tile_ij)],
        [(_sds((s_len, a), BF), tile_ij)] * 2, NT, attn_gate_bwd)

    zp_t = (4 * a + p) // tn

    def pool_gate_bwd(dyp, ex, outs):
        z_ref, y_ref, sc_ref = ex
        dzp_ref, dy_ref, dps_ref = outs
        i = pl.program_id(1)
        silu, dsilu = _silu_and_grad(z_ref[...].astype(F32))
        y = y_ref[...].astype(F32)
        sc = sc_ref[...]
        dyp0 = dyp * silu
        dzp_ref[...] = (dyp * (y * sc) * dsilu).astype(BF)
        dy_ref[...] = (dyp0 * sc).astype(BF)
        dps = jnp.sum(dyp0 * y, axis=0, keepdims=True)

        @pl.when(i == 0)
        def _():
            dps_ref[...] = dps

        @pl.when(i > 0)
        def _():
            dps_ref[...] += dps

    dzp, dy_pool, dps = _mm(
        "pool_gate_bwd", (p // tn, s_len // tmb),
        [(d_pm, pl.BlockSpec((tmb, d), lambda j, i: (i, 0))), (wop, pl.BlockSpec((tn, d), lambda j, i: (j, 0))),
         (proj, pl.BlockSpec((tmb, tn), lambda j, i: (i, zp_t + j))), (ypre, tile_ji),
         (scale_row, pl.BlockSpec((1, tn), lambda j, i: (0, j)))],
        [(_sds((s_len, p), BF), tile_ji)] * 2 + [(_sds((1, p), F32), pl.BlockSpec((1, tn), lambda j, i: (0, j)))],
        NT, pool_gate_bwd)

    du, dpw = _pool_bwd(dy_pool, dmean, pw)

    o_wop, o_wo, o_pool = a, d, 2 * d
    slab_a = _sds((2, 4, 2 * d + cg, sw), BF)
    slab_b = _sds((2, 4, 5 * d // 2, sw), BF)
    hbm = pl.BlockSpec(memory_space=pl.ANY)
    tmw = _tile(a, 1024)
    core = lax.axis_index("c").astype(jnp.int32).reshape(1)

    def pack_small(dpw_ref, dgb_ref, o_ref):
        rows = cg // N_DEV
        for j in range(N_DEV):
            for g in range(ng):
                o_ref[j % 2, j // 2, g * rows:(g + 1) * rows, :] = dpw_ref[g, j * rows:(j + 1) * rows, :].astype(BF)
            o_ref[j % 2, j // 2, ng * rows:, :] = jnp.concatenate(
                [dgb_ref[:, j * sw:(j + 1) * sw], jnp.zeros((cg - ng * rows - 8, sw), F32)], axis=0).astype(BF)

    slab = pl.pallas_call(
        pack_small, grid=(1,),
        in_specs=[pl.BlockSpec((ng, cg, cg), lambda i: (0, 0, 0)), pl.BlockSpec((8, d), lambda i: (0, 0))],
        out_specs=pl.BlockSpec((2, 4, cg, sw), lambda i: (0, 0, o_pool // cg, 0)), out_shape=slab_a,
        name="dw_small", compiler_params=_params(1))(dpw, dgb8)

    def into_slab(acc, _, outs):
        outs[0][...] = acc.astype(BF)

    def weight_grad(name, slab, lhs, rhs, grid, lhs_spec, rhs_spec, out_spec, comm=None):
        return _mm(name, grid, [(lhs, lhs_spec), (rhs, rhs_spec), (slab, hbm)], [(slab_a, out_spec)],
                   TN, into_slab, aliases={2: 0}, comm=comm)

    slab = weight_grad("dw_out", slab, merged, dx2b, (N_DEV, d // sw),
                       pl.BlockSpec((s_len, sw), lambda j, t: (0, j)), pl.BlockSpec((s_len, sw), lambda j, t: (0, t)),
                       pl.BlockSpec((None, None, sw, sw), lambda j, t: (j % 2, j // 2, o_wo // sw + t, 0)))[0]
    slab = weight_grad("dw_out_attn", slab, ya, d_am, (a // tmw, N_DEV),
                       pl.BlockSpec((s_len, tmw), lambda i, j: (0, i)), pl.BlockSpec((s_len, sw), lambda i, j: (0, j)),
                       pl.BlockSpec((None, None, tmw, sw), lambda i, j: (j % 2, j // 2, i, 0)))[0]
    slab = weight_grad("dw_out_pool", slab, yp, d_pm, (p // tmw, N_DEV),
                       pl.BlockSpec((s_len, tmw), lambda i, j: (0, i)), pl.BlockSpec((s_len, sw), lambda i, j: (0, j)),
                       pl.BlockSpec((None, None, tmw, sw), lambda i, j: (j % 2, j // 2, o_wop // tmw + i, 0)))[0]

    (dq, dk, dv, ddiag), (from_sibling_a,) = _attn_bwd(proj, datt, base, a, comm=_cores_comm(slab))
    chip_part_a = _add_core_partials("add_core_partials_a", slab, from_sibling_a, core, sw)
    drb = _bias_grad(ddiag.reshape(heads, SKEW))
    dproj = jnp.concatenate([dq, dk, dv, dza, du, dzp, dga, dgp], axis=1)

    tmd = _tile(d // 2, 1024)
    nrb = d // 2 // tmd

    def dw_in_rows(name, half, comm):
        return _mm(
            name, (nrb, n_in // sw),
            [(hb, pl.BlockSpec((s_len, tmd), lambda i, t: (0, half * nrb + i))),
             (dproj, pl.BlockSpec((s_len, sw), lambda i, t: (0, t)))],
            [(slab_b, pl.BlockSpec((None, None, tmd, sw),
                                   lambda i, t: ((t // per) % 2, (t // per) // 2, (t % per) * nrb + i, 0)))],
            TN, into_slab, comm=comm)

    slab_lo, parts_a = dw_in_rows("dw_in_lo", 0, _chips_comm(chip_part_a))
    slab_hi, from_sibling_lo = dw_in_rows("dw_in_hi", 1, _cores_comm(slab_lo))
    chip_part_lo = _add_core_partials("add_core_partials_lo", slab_lo, from_sibling_lo, core, 4 * sw)

    tk = w_in.shape[1]
    tnh = _tile(d, 1024)
    tmh = tm if s_len > tm else s_len // 2
    n_row = s_len // tmh

    def dh_rows(name, lo, hi, prev, comm):
        def body(in_refs, out_refs, scratch_refs):
            acc_ref = scratch_refs[0]
            k = pl.program_id(2)
            part = lax.dot_general(in_refs[0][:, 0:sw], in_refs[1][...], NT, preferred_element_type=F32)
            for r in range(1, per):
                part += lax.dot_general(in_refs[0][:, r * sw:(r + 1) * sw], in_refs[1 + r][...], NT,
                                        preferred_element_type=F32)

            @pl.when(k == 0)
            def _():
                acc_ref[...] = part

            @pl.when(k > 0)
            def _():
                acc_ref[...] += part

            @pl.when(k == N_DEV - 1)
            def _():
                out_refs[0][...] = acc_ref[...]

        ins = [(dproj, pl.BlockSpec((tmh, tk), lambda i, j, k: (lo + i, k)))]
        ins += [(w, pl.BlockSpec((None, tnh, sw), lambda i, j, k: (k, j, 0))) for w in win_rounds]
        if prev is not None:
            ins.append((prev, hbm))
        return _call(name, body, (hi - lo, d // tnh, N_DEV), ins,
                     [(_sds((s_len, d), F32), pl.BlockSpec((tmh, tnh), lambda i, j, k: (lo + i, j)))],
                     scratch=[pltpu.VMEM((tmh, tnh), F32)], aliases={1 + per: 0} if prev is not None else None,
                     comm=comm)

    dh, parts_lo, from_sibling_hi = dh_rows("dh_head", 0, n_row // 2, None,
                                            _both(_chips_comm(chip_part_lo), _cores_comm(slab_hi)))
    chip_part_hi = _add_core_partials("add_core_partials_hi", slab_hi, from_sibling_hi, core, 4 * sw)
    dh, parts_hi = dh_rows("dh_rest", n_row // 2, n_row, dh, _chips_comm(chip_part_hi))

    dx, dg1 = _rms_bwd(x2d, dh, dx2, g1)

    g_win, d_win, m_win, v_win = _adamw_row_halves("adamw_w_in", parts_lo, parts_hi, w_in, m_w_in, v_w_in)
    g_woa, d_woa, m_woa, v_woa = _adamw_shard("adamw_w_out_attn", parts_a, w_out_attn, m_w_out_attn, v_w_out_attn, 0)
    g_wop, d_wop, m_wop, v_wop = _adamw_shard("adamw_w_out_pool", parts_a, w_out_pool, m_w_out_pool, v_w_out_pool, o_wop)
    g_wo, d_wo, m_wo, v_wo = _adamw_shard("adamw_w_out", parts_a, w_out, m_w_out, v_w_out, o_wo)
    flat = lambda t: t.reshape(cg // 2, sw)
    pool_out = _adamw_shard("adamw_pool_w", parts_a, flat(pool_w), flat(m_pool_w), flat(v_pool_w), o_pool)
    g_pw, d_pw, m_pw, v_pw = [t.reshape(pool_w.shape) for t in pool_out]
    pad16 = lambda t: jnp.pad(t, ((0, 14), (0, 0)))
    gb_out = _adamw_shard("adamw_gate_bias", parts_a, pad16(gate_bias), pad16(m_gate_bias), pad16(v_gate_bias),
                          o_pool + cg // 2)
    g_gb, d_gb, m_gb, v_gb = [t[:2] for t in gb_out]

    def pack(n_gain, f_gain, scale, rb, last):
        rows = [n_gain.reshape(-1, LANES), f_gain.reshape(-1, LANES), scale.reshape(-1, LANES),
                rb.reshape(-1, LANES), last]
        return jnp.concatenate(rows, axis=0)

    pad_rb = lambda t: jnp.pad(t, ((0, 0), (0, N_REL_PAD - N_REL)))
    zeros8 = jnp.zeros((8, LANES), F32)
    loss_rows = jnp.pad(loss_part, ((0, 7), (0, 0)))
    small = _small_allreduce_adamw(
        pack(dg1, dg2, dps, drb, loss_rows),
        pack(norm_gain, final_gain, pool_scale, pad_rb(rel_bias), zeros8),
        pack(m_norm_gain, m_final_gain, m_pool_scale, pad_rb(m_rel_bias), zeros8),
        pack(v_norm_gain, v_final_gain, v_pool_scale, pad_rb(v_rel_bias), zeros8))

    n1, n2, n3 = d // LANES, 2 * d // LANES, (2 * d + p) // LANES
    n4 = n3 + heads * N_REL_PAD // LANES

    def unpack(t):
        return (t[:n1].reshape(d), t[n1:n2].reshape(d), t[n2:n3].reshape(p),
                t[n3:n4].reshape(heads, N_REL_PAD)[:, :N_REL])

    (g_ng, g_fg, g_ps, g_rb), (d_ng, d_fg, d_ps, d_rb), (m_ng, m_fg, m_ps, m_rb), (v_ng, v_fg, v_ps, v_rb) = [
        unpack(t) for t in small]
    loss = small[0][n4, 0]

    return (loss, dx.reshape(x.shape),
            g_ng, g_win, g_rb, g_pw, g_ps, g_woa, g_wop, g_gb, g_wo, g_fg,
            d_ng, d_win, d_rb, d_pw, d_ps, d_woa, d_wop, d_gb, d_wo, d_fg,
            m_ng, m_win, m_rb, m_pw, m_ps, m_woa, m_wop, m_gb, m_wo, m_fg,
            v_ng, v_win, v_rb, v_pw, v_ps, v_woa, v_wop, v_gb, v_wo, v_fg)
```

```python
import jax
import jax.numpy as jnp
from jax import lax
from jax.experimental import pallas as pl
from jax.experimental.pallas import tpu as pltpu

F32 = jnp.float32
BF = jnp.bfloat16
MESH = pl.DeviceIdType.MESH

N_DEV = 8
CHUNK = 64
N_LEFT_CHUNKS = 8
HEAD_DIM = 128
MAX_REL = 128
N_REL = 2 * MAX_REL + 1
N_REL_PAD = 384
POOL_WINDOWS = (2, 4, 8, 16)
HALO = 16
EPS = 1e-6
ADAM_LR = 0.001
ADAM_B1 = 0.9
ADAM_B2 = 0.999
ADAM_EPS = 1e-08
ADAM_WD = 0.01
ADAM_STEP = 10
NEG = -1e30
LANES = 128
TQ = N_LEFT_CHUNKS * CHUNK
TK = 2 * TQ
SKEW = 2 * TK
QS = 2 * CHUNK
KS = (N_LEFT_CHUNKS + 2) * CHUNK
VMEM_LIMIT = 52 * 1024 * 1024
NN = (((1,), (0,)), ((), ()))
NT = (((1,), (1,)), ((), ()))
TN = (((0,), (0,)), ((), ()))


def _params(n_grid):
    return pltpu.CompilerParams(dimension_semantics=("arbitrary",) * n_grid, vmem_limit_bytes=VMEM_LIMIT)


def _sig(z):
    return 1.0 / (1.0 + jnp.exp(-z))


def _silu_and_grad(z):
    s = _sig(z)
    return z * s, s * (1.0 + z * (1.0 - s))


def _tile(n, pref):
    t = min(n, pref)
    assert n % t == 0, (n, pref)
    return t


def _sds(shape, dtype):
    return jax.ShapeDtypeStruct(shape, dtype)


class _Comm:
    def __init__(self, ins, outs, scratch, start, wait, hooks=()):
        self.ins, self.outs, self.scratch = list(ins), list(outs), list(scratch)
        self.start, self.wait, self.hooks = start, wait, tuple(hooks)


def _call(name, body, grid, ins, outs, scratch=(), aliases=None, comm=None):
    n_in, n_out, n_scr = len(ins), len(outs), len(scratch)
    c_in = len(comm.ins) if comm else 0
    c_out = len(comm.outs) if comm else 0
    n_steps = 1
    for g in grid:
        n_steps *= g

    def kern(*refs):
        o0 = n_in + c_in
        s0 = o0 + n_out + c_out
        if comm:
            c_refs = (refs[n_in:o0], refs[o0 + n_out:s0], refs[s0 + n_scr:])
            step = pl.program_id(0)
            for ax in range(1, len(grid)):
                step = step * grid[ax] + pl.program_id(ax)

            @pl.when(step == 0)
            def _():
                comm.start(*c_refs)

            for frac, hook in comm.hooks:
                @pl.when(step == int(frac * n_steps))
                def _(hook=hook):
                    hook(*c_refs)

        body(refs[:n_in], refs[o0:o0 + n_out], refs[s0:s0 + n_scr])

        if comm:
            @pl.when(step == n_steps - 1)
            def _():
                comm.wait(*c_refs)

    hbm = pl.BlockSpec(memory_space=pl.ANY)
    return pl.pallas_call(
        kern, grid=grid,
        in_specs=[s for _, s in ins] + [hbm] * c_in, out_specs=[s for _, s in outs] + [hbm] * c_out,
        out_shape=[o for o, _ in outs] + (comm.outs if comm else []),
        scratch_shapes=list(scratch) + (comm.scratch if comm else []),
        name=name, compiler_params=_params(len(grid)), input_output_aliases=aliases or {},
    )(*([a for a, _ in ins] + (comm.ins if comm else [])))


def _mm(name, grid, ins, outs, dims, epi, aliases=None, comm=None):
    def body(in_refs, out_refs, _):
        acc = lax.dot_general(in_refs[0][...], in_refs[1][...], dims, preferred_element_type=F32)
        epi(acc, in_refs[2:], out_refs)

    return _call(name, body, grid, ins, outs, aliases=aliases, comm=comm)


def _rms_fwd(x, g, comm=None):
    s, d = x.shape
    tr = _tile(s, 256)

    def body(in_refs, out_refs, _):
        xv = in_refs[0][...]
        r = lax.rsqrt(jnp.mean(xv * xv, axis=-1, keepdims=True) + EPS)
        out_refs[0][...] = (xv * r * in_refs[1][...]).astype(BF)

    row = pl.BlockSpec((tr, d), lambda i: (i, 0))
    return _call("rms_fwd", body, (s // tr,), [(x, row), (g, pl.BlockSpec((1, d), lambda i: (0, 0)))],
                 [(_sds((s, d), BF), row)], comm=comm)


def _final_norm(x2, target, g):
    s, d = x2.shape
    tr = _tile(s, 128)

    def kern(x_ref, t_ref, g_ref, dx_ref, dxb_ref, dg_ref, loss_ref):
        i = pl.program_id(0)
        xv = x_ref[...]
        gv = g_ref[...]
        r = lax.rsqrt(jnp.mean(xv * xv, axis=-1, keepdims=True) + EPS)
        xhat = xv * r
        err = xhat * gv - t_ref[...]
        dy = err * (1.0 / d)
        gy = dy * gv
        dx = r * (gy - xhat * jnp.mean(gy * xhat, axis=-1, keepdims=True))
        dx_ref[...] = dx
        dxb_ref[...] = dx.astype(BF)
        dg = jnp.sum(dy * xhat, axis=0, keepdims=True)
        ls = jnp.broadcast_to(0.5 * jnp.sum(jnp.mean(err * err, axis=-1, keepdims=True)), (1, LANES))

        @pl.when(i == 0)
        def _():
            dg_ref[...] = dg
            loss_ref[...] = ls

        @pl.when(i > 0)
        def _():
            dg_ref[...] += dg
            loss_ref[...] += ls

    row = pl.BlockSpec((tr, d), lambda i: (i, 0))
    vec = pl.BlockSpec((1, d), lambda i: (0, 0))
    return pl.pallas_call(
        kern, grid=(s // tr,), in_specs=[row, row, vec],
        out_specs=[row, row, vec, pl.BlockSpec((1, LANES), lambda i: (0, 0))],
        out_shape=[_sds((s, d), F32), _sds((s, d), BF), _sds((1, d), F32), _sds((1, LANES), F32)],
        name="final_norm", compiler_params=_params(1))(x2, target, g)


def _rms_bwd(x, dh, dx2, g):
    s, d = x.shape
    tr = _tile(s, 128)

    def kern(x_ref, dh_ref, dx2_ref, g_ref, dx_ref, dg_ref):
        i = pl.program_id(0)
        xv = x_ref[...]
        r = lax.rsqrt(jnp.mean(xv * xv, axis=-1, keepdims=True) + EPS)
        xhat = xv * r
        dhv = dh_ref[...]
        gh = dhv * g_ref[...]
        dx_ref[...] = dx2_ref[...] + r * (gh - xhat * jnp.mean(gh * xhat, axis=-1, keepdims=True))
        dg = jnp.sum(dhv * xhat, axis=0, keepdims=True)

        @pl.when(i == 0)
        def _():
            dg_ref[...] = dg

        @pl.when(i > 0)
        def _():
            dg_ref[...] += dg

    row = pl.BlockSpec((tr, d), lambda i: (i, 0))
    vec = pl.BlockSpec((1, d), lambda i: (0, 0))
    return pl.pallas_call(
        kern, grid=(s // tr,), in_specs=[row, row, row, vec], out_specs=[row, vec],
        out_shape=[_sds((s, d), F32), _sds((1, d), F32)],
        name="rms_bwd", compiler_params=_params(1))(x, dh, dx2, g)


def _rel_index(j, backward):
    if backward:
        rel = 2 * TQ - 1 - j
    else:
        rel = TQ - jnp.where(j < TK, j, j - SKEW)
    return jnp.clip(rel, -MAX_REL, MAX_REL) + MAX_REL


def _bias_rows(rel_bias_pad):
    h = rel_bias_pad.shape[0]

    def kern(rb_ref, o_ref):
        j = lax.broadcasted_iota(jnp.int32, (N_REL_PAD, SKEW), 1)
        k = lax.broadcasted_iota(jnp.int32, (N_REL_PAD, SKEW), 0)
        onehot = (_rel_index(j, False) == k).astype(F32)
        o_ref[...] = jnp.dot(rb_ref[...], onehot, preferred_element_type=F32, precision=lax.Precision.HIGHEST)

    return pl.pallas_call(kern, out_shape=_sds((h, SKEW), F32), name="bias_rows")(rel_bias_pad)


def _bias_grad(ddiag):
    h = ddiag.shape[0]

    def kern(d_ref, o_ref):
        j = lax.broadcasted_iota(jnp.int32, (N_REL_PAD, SKEW), 1)
        k = lax.broadcasted_iota(jnp.int32, (N_REL_PAD, SKEW), 0)
        onehot = ((_rel_index(j, True) == k) & (j < TQ + TK - 1)).astype(F32)
        o_ref[...] = lax.dot_general(d_ref[...], onehot, NT, preferred_element_type=F32,
                                     precision=lax.Precision.HIGHEST)

    return pl.pallas_call(kern, out_shape=_sds((h, N_REL_PAD), F32), name="bias_grad")(ddiag)


def _bias_tile(row, first):
    t = pltpu.roll(jnp.broadcast_to(row, (TQ, SKEW)), 0, 1, stride=1, stride_axis=0)[:, :TK]
    r = lax.broadcasted_iota(jnp.int32, (TQ, TK), 0) // CHUNK
    col = lax.broadcasted_iota(jnp.int32, (TQ, TK), 1)
    dist = N_LEFT_CHUNKS + r - col // CHUNK
    keep = (dist >= 0) & (dist <= N_LEFT_CHUNKS) & jnp.logical_not(first & (col < TQ))
    return jnp.where(keep, t, NEG)


def _band(j):
    return pl.ds(j * QS, QS), pl.ds(j * QS, KS)


def _band_rows(prev_ref, cur_ref, j):
    return jnp.concatenate([prev_ref[j * QS:, :], cur_ref[:j * QS + KS - TQ, :]], axis=0)


def _scores(q, keys, tile):
    s = lax.dot_general(q, keys, NT, preferred_element_type=F32) * (HEAD_DIM ** -0.5) + tile
    m = jnp.max(s, axis=1, keepdims=True)
    p = jnp.exp(s - m)
    return p, jnp.sum(p, axis=1, keepdims=True)


def _attn_fwd(proj, base, a_width, comm=None):
    s_len = proj.shape[0]
    heads = a_width // HEAD_DIM
    nq = s_len // TQ
    kb, vb, zb = heads, 2 * heads, 3 * heads

    def kern(q_ref, kp_ref, kc_ref, vp_ref, vc_ref, z_ref, base_ref, att_ref, ya_ref, tile_ref):
        i = pl.program_id(1)

        @pl.when(i <= 1)
        def _():
            tile_ref[...] = _bias_tile(base_ref[...], i == 0)

        for j in range(TQ // QS):
            rows, cols = _band(j)
            p, l = _scores(q_ref[rows, :], _band_rows(kp_ref, kc_ref, j), tile_ref[rows, cols])
            o = jnp.dot(p.astype(BF), _band_rows(vp_ref, vc_ref, j), preferred_element_type=F32) / l
            att_ref[rows, :] = o.astype(BF)
            z = z_ref[rows, :].astype(F32)
            ya_ref[rows, :] = (o * (z * _sig(z))).astype(BF)

    blk = lambda off: pl.BlockSpec((TQ, HEAD_DIM), lambda h, i: (i, off + h))
    prev = lambda off: pl.BlockSpec((TQ, HEAD_DIM), lambda h, i: (jnp.maximum(i - 1, 0), off + h))
    out = pl.BlockSpec((TQ, HEAD_DIM), lambda h, i: (i, h))
    def body(in_refs, out_refs, scratch_refs):
        kern(*in_refs, *out_refs, *scratch_refs)

    act = _sds((s_len, a_width), BF)
    res = _call(
        "attn_fwd", body, (heads, nq),
        [(proj, blk(0)), (proj, prev(kb)), (proj, blk(kb)), (proj, prev(vb)), (proj, blk(vb)), (proj, blk(zb)),
         (base, pl.BlockSpec((None, 1, SKEW), lambda h, i: (h, 0, 0)))],
        [(act, out), (act, out)], scratch=[pltpu.VMEM((TQ, TK), F32)], comm=comm)
    return res[:2], res[2:]


def _attn_bwd(proj, datt, base, a_width, comm=None):
    s_len = proj.shape[0]
    heads = a_width // HEAD_DIM
    nq = s_len // TQ
    kb, vb = heads, 2 * heads
    scale = HEAD_DIM ** -0.5

    def body(in_refs, out_refs, scratch_refs):
        q_ref, kp_ref, kc_ref, vp_ref, vc_ref, do_ref, base_ref = in_refs
        dq_ref, dk_ref, dv_ref, dd_ref = out_refs
        tile_ref, dsacc_ref, ak_ref, av_ref = scratch_refs
        i = pl.program_id(1)

        @pl.when(i <= 1)
        def _():
            tile_ref[...] = _bias_tile(base_ref[...], i == 0)

        @pl.when(i == 0)
        def _():
            dsacc_ref[...] = jnp.zeros_like(dsacc_ref)
            ak_ref[...] = jnp.zeros_like(ak_ref)
            av_ref[...] = jnp.zeros_like(av_ref)

        @pl.when(i > 0)
        def _():
            ak_ref[:TQ, :] = ak_ref[TQ:, :]
            av_ref[:TQ, :] = av_ref[TQ:, :]
            ak_ref[TQ:, :] = jnp.zeros((TQ, HEAD_DIM), F32)
            av_ref[TQ:, :] = jnp.zeros((TQ, HEAD_DIM), F32)

        @pl.when(i < nq)
        def _():
            for j in range(TQ // QS):
                rows, cols = _band(j)
                q = q_ref[rows, :]
                do = do_ref[rows, :]
                keys = _band_rows(kp_ref, kc_ref, j)
                p, l = _scores(q, keys, tile_ref[rows, cols])
                p = p / l
                dp = lax.dot_general(do, _band_rows(vp_ref, vc_ref, j), NT, preferred_element_type=F32)
                ds = p * (dp - jnp.sum(p * dp, axis=1, keepdims=True))
                dsacc_ref[rows, cols] += ds
                dsb = ds.astype(BF)
                dq_ref[rows, :] = (jnp.dot(dsb, keys, preferred_element_type=F32) * scale).astype(BF)
                ak_ref[cols, :] += lax.dot_general(dsb, q, TN, preferred_element_type=F32) * scale
                av_ref[cols, :] += lax.dot_general(p.astype(BF), do, TN, preferred_element_type=F32)

        dk_ref[...] = ak_ref[:TQ, :].astype(BF)
        dv_ref[...] = av_ref[:TQ, :].astype(BF)

        @pl.when(i == nq)
        def _():
            acc = dsacc_ref[...]
            rr = lax.broadcasted_iota(jnp.int32, (TQ, TQ), 0)
            cc = lax.broadcasted_iota(jnp.int32, (TQ, TQ), 1)
            flip = (rr + cc == TQ - 1).astype(BF)
            hi = acc.astype(BF)
            lo = (acc - hi.astype(F32)).astype(BF)
            rev = jnp.dot(flip, hi, preferred_element_type=F32) + jnp.dot(flip, lo, preferred_element_type=F32)
            wide = jnp.concatenate([rev, jnp.zeros((TQ, SKEW - TK), F32)], axis=1)
            dd_ref[...] = jnp.sum(pltpu.roll(wide, 0, 1, stride=1, stride_axis=0), axis=0, keepdims=True)

    last = nq - 1
    cur = lambda off: pl.BlockSpec((TQ, HEAD_DIM), lambda h, i: (jnp.minimum(i, last), off + h))
    prev = lambda off: pl.BlockSpec((TQ, HEAD_DIM), lambda h, i: (jnp.maximum(jnp.minimum(i, last) - 1, 0), off + h))
    done = pl.BlockSpec((TQ, HEAD_DIM), lambda h, i: (jnp.maximum(i - 1, 0), h))
    row = pl.BlockSpec((None, 1, SKEW), lambda h, i: (h, 0, 0))
    act = _sds((s_len, a_width), BF)
    res = _call(
        "attn_bwd", body, (heads, nq + 1),
        [(proj, cur(0)), (proj, prev(kb)), (proj, cur(kb)), (proj, prev(vb)), (proj, cur(vb)), (datt, cur(0)),
         (base, row)],
        [(act, cur(0)), (act, done), (act, done), (_sds((heads, 1, SKEW), F32), row)],
        scratch=[pltpu.VMEM((TQ, TK), F32), pltpu.VMEM((TQ, TK), F32),
                 pltpu.VMEM((TK, HEAD_DIM), F32), pltpu.VMEM((TK, HEAD_DIM), F32)],
        comm=comm)
    return res[:4], res[4:]


def _pool_fwd(proj, pool_w, pool_scale, p_width, u_blk, z_blk):
    s_len = proj.shape[0]
    cg = p_width // len(POOL_WINDOWS)
    tt = _tile(s_len, 512)

    def kern(up_ref, uc_ref, z_ref, pw_ref, sc_ref, d_ref, y_ref, yp_ref):
        t = pl.program_id(0)
        row = lax.broadcasted_iota(jnp.int32, (tt, 1), 0) + t * tt
        for g, w in enumerate(POOL_WINDOWS):
            cs = slice(g * cg, (g + 1) * cg)
            prev = jnp.where(t == 0, 0.0, up_ref[:, cs].astype(F32))
            cur = uc_ref[:, cs].astype(F32)
            ws = jnp.concatenate([prev, cur], axis=0)
            sh = 1
            while sh < w:
                ws = ws + pltpu.roll(ws, sh, 0)
                sh *= 2
            cnt = jnp.minimum(row + 1, w).astype(F32)
            db = (ws[HALO:, :] / cnt - cur).astype(BF)
            y = jnp.dot(db, pw_ref[g], preferred_element_type=F32)
            d_ref[:, cs] = db
            y_ref[:, cs] = y.astype(BF)
            z = z_ref[:, cs].astype(F32)
            yp_ref[:, cs] = (y * sc_ref[:, cs] * (z * _sig(z))).astype(BF)

    full = pl.BlockSpec((tt, p_width), lambda t: (t, 0))
    return pl.pallas_call(
        kern, grid=(s_len // tt,),
        in_specs=[pl.BlockSpec((HALO, p_width), lambda t: (jnp.maximum(t * (tt // HALO) - 1, 0), u_blk)),
                  pl.BlockSpec((tt, p_width), lambda t: (t, u_blk)),
                  pl.BlockSpec((tt, p_width), lambda t: (t, z_blk)),
                  pl.BlockSpec((len(POOL_WINDOWS), cg, cg), lambda t: (0, 0, 0)),
                  pl.BlockSpec((1, p_width), lambda t: (0, 0))],
        out_specs=[full, full, full], out_shape=[_sds((s_len, p_width), BF)] * 3,
        name="pool_fwd", compiler_params=_params(1))(proj, proj, proj, pool_w, pool_scale)


def _pool_bwd(dy, dmean, pool_w):
    s_len, p_width = dy.shape
    ng = len(POOL_WINDOWS)
    cg = p_width // ng
    tt = _tile(s_len, 512)
    nt = s_len // tt

    def kern(dyc_ref, dyn_ref, d_ref, pw_ref, du_ref, dpw_ref):
        t = pl.program_id(0)

        @pl.when(t == 0)
        def _():
            dpw_ref[...] = jnp.zeros_like(dpw_ref)

        row = lax.broadcasted_iota(jnp.int32, (tt + HALO, 1), 0) + t * tt
        for g, w in enumerate(POOL_WINDOWS):
            cs = slice(g * cg, (g + 1) * cg)
            dyc = dyc_ref[:, cs]
            ddc = lax.dot_general(dyc, pw_ref[g], NT, preferred_element_type=F32)
            ddn = lax.dot_general(dyn_ref[:, cs], pw_ref[g], NT, preferred_element_type=F32)
            ddn = jnp.where(t == nt - 1, 0.0, ddn)
            cnt = jnp.minimum(row + 1, w).astype(F32)
            ws = jnp.concatenate([ddc, ddn], axis=0) / cnt
            sh = 1
            while sh < w:
                ws = ws + pltpu.roll(ws, tt + HALO - sh, 0)
                sh *= 2
            du_ref[:, cs] = (ws[:tt, :] - ddc).astype(BF)
            dpw_ref[g] += lax.dot_general(d_ref[:, cs], dyc, TN, preferred_element_type=F32)

    full = pl.BlockSpec((tt, p_width), lambda t: (t, 0))
    pw_spec = pl.BlockSpec((ng, cg, cg), lambda t: (0, 0, 0))
    return pl.pallas_call(
        kern, grid=(nt,),
        in_specs=[full,
                  pl.BlockSpec((HALO, p_width), lambda t: (jnp.minimum((t + 1) * (tt // HALO), s_len // HALO - 1), 0)),
                  full, pw_spec],
        out_specs=[full, pw_spec], out_shape=[_sds((s_len, p_width), BF), _sds((ng, cg, cg), F32)],
        name="pool_bwd", compiler_params=_params(1))(dy, dy, dmean, pool_w)


def _adam(g, w_ref, m_ref, v_ref, g_out, d_out, m_out, v_out):
    m = ADAM_B1 * m_ref[...] + (1.0 - ADAM_B1) * g
    v = ADAM_B2 * v_ref[...] + (1.0 - ADAM_B2) * (g * g)
    m_hat = m / (1.0 - ADAM_B1 ** ADAM_STEP)
    v_hat = v / (1.0 - ADAM_B2 ** ADAM_STEP)
    g_out[...] = g
    d_out[...] = -ADAM_LR * (m_hat / (jnp.sqrt(v_hat) + ADAM_EPS) + ADAM_WD * w_ref[...])
    m_out[...] = m
    v_out[...] = v


def _adamw_shard(name, parts, w, m, v, row_off):
    rw, cw = w.shape
    sw = parts.shape[2]
    tr = _tile(rw, 512)
    assert row_off % tr == 0 and cw % sw == 0

    def kern(b_ref, w_ref, m_ref, v_ref, g_out, d_out, m_out, v_out):
        b = b_ref[...].astype(F32)
        _adam(((b[0] + b[1]) + b[2]) + b[3], w_ref, m_ref, v_ref, g_out, d_out, m_out, v_out)

    blk = pl.BlockSpec((tr, sw), lambda ct, i: (i, ct))
    return pl.pallas_call(
        kern, grid=(cw // sw, rw // tr),
        in_specs=[pl.BlockSpec((4, tr, sw), lambda ct, i: (0, (row_off + ct * rw) // tr + i, 0)), blk, blk, blk],
        out_specs=[blk] * 4, out_shape=[_sds((rw, cw), F32)] * 4,
        name=name, compiler_params=_params(2))(parts, w, m, v)


def _adamw_row_halves(name, parts_lo, parts_hi, w, m, v):
    rw, cw = w.shape
    sw = parts_lo.shape[2]
    half = rw // 2
    tr = _tile(half, 512)
    nh = half // tr

    def kern(lo_ref, hi_ref, w_ref, m_ref, v_ref, g_out, d_out, m_out, v_out):
        i = pl.program_id(1)

        def update(b_ref):
            b = b_ref[...].astype(F32)
            _adam(((b[0] + b[1]) + b[2]) + b[3], w_ref, m_ref, v_ref, g_out, d_out, m_out, v_out)

        @pl.when(i < nh)
        def _():
            update(lo_ref)

        @pl.when(i >= nh)
        def _():
            update(hi_ref)

    blk = pl.BlockSpec((tr, sw), lambda ct, i: (i, ct))
    return pl.pallas_call(
        kern, grid=(cw // sw, rw // tr),
        in_specs=[pl.BlockSpec((4, tr, sw), lambda ct, i: (0, ct * nh + jnp.minimum(i, nh - 1), 0)),
                  pl.BlockSpec((4, tr, sw), lambda ct, i: (0, ct * nh + jnp.maximum(i - nh, 0), 0)), blk, blk, blk],
        out_specs=[blk] * 4, out_shape=[_sds((rw, cw), F32)] * 4,
        name=name, compiler_params=_params(2))(parts_lo, parts_hi, w, m, v)


def _both(c1, c2):
    n_in, n_out, n_sem = len(c1.ins), len(c1.outs), len(c1.scratch)

    def split(ins, outs, sems):
        return (ins[:n_in], outs[:n_out], sems[:n_sem]), (ins[n_in:], outs[n_out:], sems[n_sem:])

    def start(*refs):
        r1, r2 = split(*refs)
        c1.start(*r1)
        c2.start(*r2)

    def wait(*refs):
        r1, r2 = split(*refs)
        c1.wait(*r1)
        c2.wait(*r2)

    return _Comm(c1.ins + c2.ins, c1.outs + c2.outs, c1.scratch + c2.scratch, start, wait)


def _position():
    return lax.axis_index("x"), lax.axis_index("y"), lax.axis_index("c")


def _gather_comm(shards, cols=None):
    if cols is None:
        pieces = [(a, None) for a in range(len(shards))]
        shapes = [_sds(s.shape, s.dtype) for s in shards]
    else:
        half = shards[0].shape[0] // 2
        pieces = [(0, pl.ds(0, half)), (0, pl.ds(half, half))]
        shapes = [_sds((shards[0].shape[0], cols[1]), shards[0].dtype)]
    n = len(pieces)

    def plan(xs, outs, sems, only=None):
        send_sems, recv_sems, local_sems = sems
        x, y, c = _position()
        me, sibling = (x, y, c), (x, y, 1 - c)
        chips = [(1 - x, y), (x, 1 - y), (1 - x, 1 - y)]
        which = range(n) if only is None else only

        def source(v):
            a, rows = pieces[v]
            return xs[a] if rows is None else xs[a].at[rows, pl.ds(cols[0], cols[1])]

        def landing(v, block):
            a, rows = pieces[v]
            dst = outs[a].at[4 * block[0] + 2 * block[1] + block[2]]
            return dst if rows is None else dst.at[rows, :]

        def copy(v, k, block, to, own=False):
            dst = landing(v, block)
            return pltpu.make_async_remote_copy(
                src_ref=source(v) if own else dst, dst_ref=dst,
                send_sem=send_sems.at[7 * v + k], recv_sem=recv_sems.at[7 * v + k],
                device_id=to, device_id_type=MESH)

        by_chip = [(j, chip, v) for v in which for j, chip in enumerate(chips)]
        return dict(
            mine=lambda: [pltpu.make_async_copy(source(v), landing(v, me), local_sems.at[v]) for v in which],
            first=lambda: ([copy(v, 0, me, sibling, own=True) for v in which]
                           + [copy(v, 1 + j, me, (*chip, c), own=True) for j, chip, v in by_chip]),
            landed=lambda: [copy(v, 1 + j, (*chip, c), me) for j, chip, v in by_chip],
            passed=lambda: [copy(v, 4 + j, (*chip, c), sibling) for j, chip, v in by_chip],
            rest=lambda: ([copy(v, 0, sibling, me) for v in which]
                          + [copy(v, 4 + j, (*chip, 1 - c), me) for j, chip, v in by_chip]))

    def start(*refs):
        p = plan(*refs)
        for cp in p["mine"]() + p["first"]():
            cp.start()

    def pass_on(only):
        def hook(*refs):
            p = plan(*refs, only=only)
            for arrived, onward in zip(p["landed"](), p["passed"]()):
                arrived.wait_recv()
                onward.start()
        return hook

    def wait(*refs):
        if cols is not None:
            pass_on([1])(*refs)
        p = plan(*refs)
        for cp in p["rest"]():
            cp.wait_recv()
        for cp in p["first"]() + p["passed"]():
            cp.wait_send()
        for cp in p["mine"]():
            cp.wait()

    hooks = [(0.75, pass_on(None))] if cols is None else [(0.5, pass_on([0]))]
    return _Comm(shards, [_sds((N_DEV,) + s.shape, s.dtype) for s in shapes],
                 [pltpu.SemaphoreType.DMA((7 * n,)), pltpu.SemaphoreType.DMA((7 * n,)),
                  pltpu.SemaphoreType.DMA((n,))], start, wait, hooks)


def _cores_comm(slab):
    _, _, r, sw = slab.shape

    def copies(ins, outs, sems):
        x, y, c = _position()
        return [pltpu.make_async_remote_copy(
            src_ref=ins[0].at[1 - c], dst_ref=outs[0], send_sem=sems[0], recv_sem=sems[1],
            device_id=(x, y, 1 - c), device_id_type=MESH)]

    def start(*refs):
        for cp in copies(*refs):
            cp.start()

    def wait(*refs):
        for cp in copies(*refs):
            cp.wait()

    return _Comm([slab], [_sds((4, r, sw), slab.dtype)],
                 [pltpu.SemaphoreType.DMA, pltpu.SemaphoreType.DMA], start, wait)


def _add_core_partials(name, slab, recv, core, tr):
    _, _, r, sw = slab.shape

    def kern(c_ref, a_ref, b_ref, o_ref):
        o_ref[...] = (a_ref[...].astype(F32) + b_ref[...].astype(F32)).astype(BF)

    return pl.pallas_call(
        kern,
        grid_spec=pltpu.PrefetchScalarGridSpec(
            num_scalar_prefetch=1, grid=(4, r // tr),
            in_specs=[pl.BlockSpec((None, None, tr, sw), lambda k, i, c_ref: (c_ref[0], k, i, 0)),
                      pl.BlockSpec((None, tr, sw), lambda k, i, c_ref: (k, i, 0))],
            out_specs=pl.BlockSpec((None, tr, sw), lambda k, i, c_ref: (k, i, 0))),
        out_shape=_sds((4, r, sw), BF), name=name, compiler_params=_params(2))(core, slab, recv)


def _chips_comm(part):
    _, r, sw = part.shape

    def copies(ins, outs, sems):
        send_sems, recv_sems, local_sem = sems
        x, y, c = _position()
        mine = 2 * x + y
        local = pltpu.make_async_copy(ins[0].at[mine], outs[0].at[mine], local_sem)
        chips = [(1 - x, y), (x, 1 - y), (1 - x, 1 - y)]
        remote = [pltpu.make_async_remote_copy(
            src_ref=ins[0].at[2 * px + py], dst_ref=outs[0].at[mine],
            send_sem=send_sems.at[j], recv_sem=recv_sems.at[j],
            device_id=(px, py, c), device_id_type=MESH) for j, (px, py) in enumerate(chips)]
        return [local] + remote

    def start(*refs):
        for cp in copies(*refs):
            cp.start()

    def wait(*refs):
        for cp in copies(*refs):
            cp.wait()

    return _Comm([part], [_sds((4, r, sw), part.dtype)],
                 [pltpu.SemaphoreType.DMA((3,)), pltpu.SemaphoreType.DMA((3,)), pltpu.SemaphoreType.DMA],
                 start, wait)


def _small_allreduce_adamw(partial, w, m, v):
    nr = partial.shape[0]

    def kern(p_ref, w_ref, m_ref, v_ref, g_out, d_out, m_out, v_out, gath_ref, send_sems, recv_sems):
        x, y, c = _position()
        me = 4 * x + 2 * y + c
        gath_ref[me] = p_ref[...]
        copies = []
        for mask in range(1, N_DEV):
            peer = (x ^ (mask >> 2), y ^ ((mask >> 1) & 1), c ^ (mask & 1))
            copies.append(pltpu.make_async_remote_copy(
                src_ref=p_ref, dst_ref=gath_ref.at[me],
                send_sem=send_sems.at[mask - 1], recv_sem=recv_sems.at[mask - 1],
                device_id=peer, device_id_type=MESH))
        for cp in copies:
            cp.start()
        for cp in copies:
            cp.wait()
        tot = gath_ref[0]
        for k in range(1, N_DEV):
            tot = tot + gath_ref[k]
        _adam(tot, w_ref, m_ref, v_ref, g_out, d_out, m_out, v_out)

    vmem = pl.BlockSpec(memory_space=pltpu.VMEM)
    return pl.pallas_call(
        kern, in_specs=[vmem] * 4, out_specs=[vmem] * 4, out_shape=[_sds((nr, LANES), F32)] * 4,
        scratch_shapes=[pltpu.VMEM((N_DEV, nr, LANES), F32),
                        pltpu.SemaphoreType.DMA((N_DEV - 1,)), pltpu.SemaphoreType.DMA((N_DEV - 1,))],
        name="small_allreduce_adamw")(partial, w, m, v)


def kernel(x, norm_gain, w_in, rel_bias, pool_w, pool_scale, w_out_attn, w_out_pool, gate_bias, w_out, final_gain, loss_target, m_norm_gain, m_w_in, m_rel_bias, m_pool_w, m_pool_scale, m_w_out_attn, m_w_out_pool, m_gate_bias, m_w_out, m_final_gain, v_norm_gain, v_w_in, v_rel_bias, v_pool_w, v_pool_scale, v_w_out_attn, v_w_out_pool, v_gate_bias, v_w_out, v_final_gain):
    _, s_len, d = x.shape
    a = w_out_attn.shape[0]
    p = w_out_pool.shape[0]
    heads = a // HEAD_DIM
    ng = len(POOL_WINDOWS)
    cg = p // ng
    sw = d // N_DEV
    n_in = w_in.shape[1] * N_DEV
    assert a == p and a + p == d and cg == sw and n_in == 5 * d and w_in.shape[1] == 5 * sw
    assert s_len % TQ == 0 and rel_bias.shape == (heads, N_REL)
    tm = _tile(s_len, 1024)
    x2d = x.reshape(s_len, d)
    tgt = loss_target.reshape(s_len, d)

    g1 = norm_gain.reshape(1, d)
    g2 = final_gain.reshape(1, d)
    scale_row = pool_scale.reshape(1, p)
    tn = sw
    per = w_in.shape[1] // tn
    hbm = pl.BlockSpec(memory_space=pl.ANY)

    w_bf = w_in.astype(BF)
    hb, landed = _rms_fwd(x2d, g1, comm=_gather_comm([w_bf], cols=(0, tn)))
    win_rounds = []

    def store_bf16(acc, _, outs):
        outs[0][...] = acc.astype(BF)

    proj = None
    for r in range(per):
        win_rounds.append(landed)
        if r + 1 < per:
            comm = _gather_comm([w_bf], cols=((r + 1) * tn, tn))
        else:
            comm = _gather_comm([w_out_attn.astype(BF), w_out_pool.astype(BF), pool_w.astype(BF), gate_bias])
        ins = [(hb, pl.BlockSpec((tm, d), lambda i, j: (i, 0))),
               (landed, pl.BlockSpec((None, d, tn), lambda i, j: (j, 0, 0)))]
        if proj is not None:
            ins.append((proj, hbm))
        proj, landed, *rest = _mm(
            f"proj_{r}", (s_len // tm, N_DEV), ins,
            [(_sds((s_len, n_in), BF), pl.BlockSpec((tm, tn), lambda i, j, r=r: (i, per * j + r)))],
            NN, store_bf16, aliases={2: 0} if r else None, comm=comm)
    woa = landed.transpose(1, 0, 2).reshape(a, d)
    wop = rest[0].transpose(1, 0, 2).reshape(p, d)
    pw = rest[1].transpose(1, 0, 2, 3).reshape(ng, cg, cg)
    gb = rest[2].transpose(1, 0, 2).reshape(2, d)

    rb_pad = jnp.pad(rel_bias, ((0, 0), (0, N_REL_PAD - N_REL)))
    base = _bias_rows(rb_pad).reshape(heads, 1, SKEW)
    (att, ya), (wo_g,) = _attn_fwd(proj, base, a, comm=_gather_comm([w_out.astype(BF)]))
    wo = wo_g.reshape(d, d)
    u_blk, z_blk = 4 * a // p, 4 * a // p + 1
    dmean, ypre, yp = _pool_fwd(proj, pw, scale_row, p, u_blk, z_blk)

    ga_t, gp_t = (4 * a + 2 * p) // tn, (4 * a + 2 * p + d) // tn

    def gate_kernel(ya_ref, woa_ref, yp_ref, wop_ref, ga_ref, gp_ref, gb_ref, m_ref, a_ref, p_ref):
        am = jnp.dot(ya_ref[...], woa_ref[...], preferred_element_type=F32)
        pm = jnp.dot(yp_ref[...], wop_ref[...], preferred_element_type=F32)
        sa = _sig(ga_ref[...].astype(F32) + gb_ref[0:1, :])
        sp = _sig(gp_ref[...].astype(F32) + gb_ref[1:2, :])
        m_ref[...] = (sa * am + sp * pm).astype(BF)
        a_ref[...] = am.astype(BF)
        p_ref[...] = pm.astype(BF)

    tile_ij = pl.BlockSpec((tm, tn), lambda i, j: (i, j))
    merged, am, pm = pl.pallas_call(
        gate_kernel, grid=(s_len // tm, d // tn),
        in_specs=[pl.BlockSpec((tm, a), lambda i, j: (i, 0)), pl.BlockSpec((a, tn), lambda i, j: (0, j)),
                  pl.BlockSpec((tm, p), lambda i, j: (i, 0)), pl.BlockSpec((p, tn), lambda i, j: (0, j)),
                  pl.BlockSpec((tm, tn), lambda i, j: (i, ga_t + j)), pl.BlockSpec((tm, tn), lambda i, j: (i, gp_t + j)),
                  pl.BlockSpec((2, tn), lambda i, j: (0, j))],
        out_specs=[tile_ij] * 3, out_shape=[_sds((s_len, d), BF)] * 3,
        name="gate_merge", compiler_params=_params(2))(ya, woa, yp, wop, proj, proj, gb)

    def add_residual(acc, ex, outs):
        outs[0][...] = ex[0][...] + acc

    x2 = _mm("out_proj", (s_len // tm, d // tn),
             [(merged, pl.BlockSpec((tm, d), lambda i, j: (i, 0))), (wo, pl.BlockSpec((d, tn), lambda i, j: (0, j))),
              (x2d, tile_ij)],
             [(_sds((s_len, d), F32), tile_ij)], NN, add_residual)[0]

    dx2, dx2b, dg2, loss_part = _final_norm(x2, tgt, g2)

    tmb = _tile(s_len, 512)
    tile_ji = pl.BlockSpec((tmb, tn), lambda j, i: (i, j))

    def gate_bwd(dm, ex, outs):
        a_ref, p_ref, ga_ref, gp_ref, gb_ref = ex
        da_ref, dp_ref, dga_ref, dgp_ref, dgb_ref = outs
        i = pl.program_id(1)
        sa = _sig(ga_ref[...].astype(F32) + gb_ref[0:1, :])
        sp = _sig(gp_ref[...].astype(F32) + gb_ref[1:2, :])
        dga = dm * a_ref[...].astype(F32) * sa * (1.0 - sa)
        dgp = dm * p_ref[...].astype(F32) * sp * (1.0 - sp)
        da_ref[...] = (dm * sa).astype(BF)
        dp_ref[...] = (dm * sp).astype(BF)
        dga_ref[...] = dga.astype(BF)
        dgp_ref[...] = dgp.astype(BF)
        r = lax.broadcasted_iota(jnp.int32, (8, tn), 0)
        sums = jnp.where(r == 0, jnp.sum(dga, axis=0, keepdims=True),
                         jnp.where(r == 1, jnp.sum(dgp, axis=0, keepdims=True), 0.0))

        @pl.when(i == 0)
        def _():
            dgb_ref[...] = sums

        @pl.when(i > 0)
        def _():
            dgb_ref[...] += sums

    d_am, d_pm, dga, dgp, dgb8 = _mm(
        "gate_bwd", (d // tn, s_len // tmb),
        [(dx2b, pl.BlockSpec((tmb, d), lambda j, i: (i, 0))), (wo, pl.BlockSpec((tn, d), lambda j, i: (j, 0))),
         (am, tile_ji), (pm, tile_ji),
         (proj, pl.BlockSpec((tmb, tn), lambda j, i: (i, ga_t + j))),
         (proj, pl.BlockSpec((tmb, tn), lambda j, i: (i, gp_t + j))),
         (gb, pl.BlockSpec((2, tn), lambda j, i: (0, j)))],
        [(_sds((s_len, d), BF), tile_ji)] * 4 + [(_sds((8, d), F32), pl.BlockSpec((8, tn), lambda j, i: (0, j)))],
        NT, gate_bwd)

    za_t = 3 * a // tn

    def attn_gate_bwd(dya, ex, outs):
        silu, dsilu = _silu_and_grad(ex[0][...].astype(F32))
        outs[0][...] = (dya * silu).astype(BF)
        outs[1][...] = (dya * ex[1][...].astype(F32) * dsilu).astype(BF)

    datt, dza = _mm(
        "attn_gate_bwd", (s_len // tm, a // tn),
        [(d_am, pl.BlockSpec((tm, d), lambda i, j: (i, 0))), (woa, pl.BlockSpec((tn, d), lambda i, j: (j, 0))),
         (proj, pl.BlockSpec((tm, tn), lambda i, j: (i, za_t + j))), (att, tile_ij)],
        [(_sds((s_len, a), BF), tile_ij)] * 2, NT, attn_gate_bwd)

    zp_t = (4 * a + p) // tn

    def pool_gate_bwd(dyp, ex, outs):
        z_ref, y_ref, sc_ref = ex
        dzp_ref, dy_ref, dps_ref = outs
        i = pl.program_id(1)
        silu, dsilu = _silu_and_grad(z_ref[...].astype(F32))
        y = y_ref[...].astype(F32)
        sc = sc_ref[...]
        dyp0 = dyp * silu
        dzp_ref[...] = (dyp * (y * sc) * dsilu).astype(BF)
        dy_ref[...] = (dyp0 * sc).astype(BF)
        dps = jnp.sum(dyp0 * y, axis=0, keepdims=True)

        @pl.when(i == 0)
        def _():
            dps_ref[...] = dps

        @pl.when(i > 0)
        def _():
            dps_ref[...] += dps

    dzp, dy_pool, dps = _mm(
        "pool_gate_bwd", (p // tn, s_len // tmb),
        [(d_pm, pl.BlockSpec((tmb, d), lambda j, i: (i, 0))), (wop, pl.BlockSpec((tn, d), lambda j, i: (j, 0))),
         (proj, pl.BlockSpec((tmb, tn), lambda j, i: (i, zp_t + j))), (ypre, tile_ji),
         (scale_row, pl.BlockSpec((1, tn), lambda j, i: (0, j)))],
        [(_sds((s_len, p), BF), tile_ji)] * 2 + [(_sds((1, p), F32), pl.BlockSpec((1, tn), lambda j, i: (0, j)))],
        NT, pool_gate_bwd)

    du, dpw = _pool_bwd(dy_pool, dmean, pw)

    o_wop, o_wo, o_pool = a, d, 2 * d
    slab_a = _sds((2, 4, 2 * d + cg, sw), BF)
    slab_b = _sds((2, 4, 5 * d // 2, sw), BF)
    hbm = pl.BlockSpec(memory_space=pl.ANY)
    tmw = _tile(a, 1024)
    core = lax.axis_index("c").astype(jnp.int32).reshape(1)

    def pack_small(dpw_ref, dgb_ref, o_ref):
        rows = cg // N_DEV
        for j in range(N_DEV):
            for g in range(ng):
                o_ref[j % 2, j // 2, g * rows:(g + 1) * rows, :] = dpw_ref[g, j * rows:(j + 1) * rows, :].astype(BF)
            o_ref[j % 2, j // 2, ng * rows:, :] = jnp.concatenate(
                [dgb_ref[:, j * sw:(j + 1) * sw], jnp.zeros((cg - ng * rows - 8, sw), F32)], axis=0).astype(BF)

    slab = pl.pallas_call(
        pack_small, grid=(1,),
        in_specs=[pl.BlockSpec((ng, cg, cg), lambda i: (0, 0, 0)), pl.BlockSpec((8, d), lambda i: (0, 0))],
        out_specs=pl.BlockSpec((2, 4, cg, sw), lambda i: (0, 0, o_pool // cg, 0)), out_shape=slab_a,
        name="dw_small", compiler_params=_params(1))(dpw, dgb8)

    def into_slab(acc, _, outs):
        outs[0][...] = acc.astype(BF)

    def weight_grad(name, slab, lhs, rhs, grid, lhs_spec, rhs_spec, out_spec, comm=None):
        return _mm(name, grid, [(lhs, lhs_spec), (rhs, rhs_spec), (slab, hbm)], [(slab_a, out_spec)],
                   TN, into_slab, aliases={2: 0}, comm=comm)

    slab = weight_grad("dw_out", slab, merged, dx2b, (N_DEV, d // sw),
                       pl.BlockSpec((s_len, sw), lambda j, t: (0, j)), pl.BlockSpec((s_len, sw), lambda j, t: (0, t)),
                       pl.BlockSpec((None, None, sw, sw), lambda j, t: (j % 2, j // 2, o_wo // sw + t, 0)))[0]
    slab = weight_grad("dw_out_attn", slab, ya, d_am, (a // tmw, N_DEV),
                       pl.BlockSpec((s_len, tmw), lambda i, j: (0, i)), pl.BlockSpec((s_len, sw), lambda i, j: (0, j)),
                       pl.BlockSpec((None, None, tmw, sw), lambda i, j: (j % 2, j // 2, i, 0)))[0]
    slab = weight_grad("dw_out_pool", slab, yp, d_pm, (p // tmw, N_DEV),
                       pl.BlockSpec((s_len, tmw), lambda i, j: (0, i)), pl.BlockSpec((s_len, sw), lambda i, j: (0, j)),
                       pl.BlockSpec((None, None, tmw, sw), lambda i, j: (j % 2, j // 2, o_wop // tmw + i, 0)))[0]

    (dq, dk, dv, ddiag), (from_sibling_a,) = _attn_bwd(proj, datt, base, a, comm=_cores_comm(slab))
    chip_part_a = _add_core_partials("add_core_partials_a", slab, from_sibling_a, core, (2 * d + cg) // 2)
    drb = _bias_grad(ddiag.reshape(heads, SKEW))
    dproj = jnp.concatenate([dq, dk, dv, dza, du, dzp, dga, dgp], axis=1)

    tmd = _tile(d // 2, 1024)
    nrb = d // 2 // tmd

    def dw_in_rows(name, half, comm):
        return _mm(
            name, (nrb, n_in // sw),
            [(hb, pl.BlockSpec((s_len, tmd), lambda i, t: (0, half * nrb + i))),
             (dproj, pl.BlockSpec((s_len, sw), lambda i, t: (0, t)))],
            [(slab_b, pl.BlockSpec((None, None, tmd, sw),
                                   lambda i, t: ((t // per) % 2, (t // per) // 2, (t % per) * nrb + i, 0)))],
            TN, into_slab, comm=comm)

    slab_lo, parts_a = dw_in_rows("dw_in_lo", 0, _chips_comm(chip_part_a))
    slab_hi, from_sibling_lo = dw_in_rows("dw_in_hi", 1, _cores_comm(slab_lo))
    chip_part_lo = _add_core_partials("add_core_partials_lo", slab_lo, from_sibling_lo, core, 4 * sw)

    tk = w_in.shape[1]
    tnh = _tile(d, 1024)
    tmh = tm if s_len > tm else s_len // 2
    n_row = s_len // tmh

    def dh_rows(name, lo, hi, prev, comm):
        def body(in_refs, out_refs, scratch_refs):
            acc_ref = scratch_refs[0]
            k = pl.program_id(2)
            part = lax.dot_general(in_refs[0][:, 0:sw], in_refs[1][...], NT, preferred_element_type=F32)
            for r in range(1, per):
                part += lax.dot_general(in_refs[0][:, r * sw:(r + 1) * sw], in_refs[1 + r][...], NT,
                                        preferred_element_type=F32)

            @pl.when(k == 0)
            def _():
                acc_ref[...] = part

            @pl.when(k > 0)
            def _():
                acc_ref[...] += part

            @pl.when(k == N_DEV - 1)
            def _():
                out_refs[0][...] = acc_ref[...]

        ins = [(dproj, pl.BlockSpec((tmh, tk), lambda i, j, k: (lo + i, k)))]
        ins += [(w, pl.BlockSpec((None, tnh, sw), lambda i, j, k: (k, j, 0))) for w in win_rounds]
        if prev is not None:
            ins.append((prev, hbm))
        return _call(name, body, (hi - lo, d // tnh, N_DEV), ins,
                     [(_sds((s_len, d), F32), pl.BlockSpec((tmh, tnh), lambda i, j, k: (lo + i, j)))],
                     scratch=[pltpu.VMEM((tmh, tnh), F32)], aliases={1 + per: 0} if prev is not None else None,
                     comm=comm)

    dh, parts_lo, from_sibling_hi = dh_rows("dh_head", 0, n_row // 2, None,
                                            _both(_chips_comm(chip_part_lo), _cores_comm(slab_hi)))
    chip_part_hi = _add_core_partials("add_core_partials_hi", slab_hi, from_sibling_hi, core, 4 * sw)
    dh, parts_hi = dh_rows("dh_rest", n_row // 2, n_row, dh, _chips_comm(chip_part_hi))

    dx, dg1 = _rms_bwd(x2d, dh, dx2, g1)

    g_win, d_win, m_win, v_win = _adamw_row_halves("adamw_w_in", parts_lo, parts_hi, w_in, m_w_in, v_w_in)
    g_woa, d_woa, m_woa, v_woa = _adamw_shard("adamw_w_out_attn", parts_a, w_out_attn, m_w_out_attn, v_w_out_attn, 0)
    g_wop, d_wop, m_wop, v_wop = _adamw_shard("adamw_w_out_pool", parts_a, w_out_pool, m_w_out_pool, v_w_out_pool, o_wop)
    g_wo, d_wo, m_wo, v_wo = _adamw_shard("adamw_w_out", parts_a, w_out, m_w_out, v_w_out, o_wo)
    flat = lambda t: t.reshape(cg // 2, sw)
    pool_out = _adamw_shard("adamw_pool_w", parts_a, flat(pool_w), flat(m_pool_w), flat(v_pool_w), o_pool)
    g_pw, d_pw, m_pw, v_pw = [t.reshape(pool_w.shape) for t in pool_out]
    pad16 = lambda t: jnp.pad(t, ((0, 14), (0, 0)))
    gb_out = _adamw_shard("adamw_gate_bias", parts_a, pad16(gate_bias), pad16(m_gate_bias), pad16(v_gate_bias),
                          o_pool + cg // 2)
    g_gb, d_gb, m_gb, v_gb = [t[:2] for t in gb_out]

    def pack(n_gain, f_gain, scale, rb, last):
        rows = [n_gain.reshape(-1, LANES), f_gain.reshape(-1, LANES), scale.reshape(-1, LANES),
                rb.reshape(-1, LANES), last]
        return jnp.concatenate(rows, axis=0)

    pad_rb = lambda t: jnp.pad(t, ((0, 0), (0, N_REL_PAD - N_REL)))
    zeros8 = jnp.zeros((8, LANES), F32)
    loss_rows = jnp.pad(loss_part, ((0, 7), (0, 0)))
    small = _small_allreduce_adamw(
        pack(dg1, dg2, dps, drb, loss_rows),
        pack(norm_gain, final_gain, pool_scale, pad_rb(rel_bias), zeros8),
        pack(m_norm_gain, m_final_gain, m_pool_scale, pad_rb(m_rel_bias), zeros8),
        pack(v_norm_gain, v_final_gain, v_pool_scale, pad_rb(v_rel_bias), zeros8))

    n1, n2, n3 = d // LANES, 2 * d // LANES, (2 * d + p) // LANES
    n4 = n3 + heads * N_REL_PAD // LANES

    def unpack(t):
        return (t[:n1].reshape(d), t[n1:n2].reshape(d), t[n2:n3].reshape(p),
                t[n3:n4].reshape(heads, N_REL_PAD)[:, :N_REL])

    (g_ng, g_fg, g_ps, g_rb), (d_ng, d_fg, d_ps, d_rb), (m_ng, m_fg, m_ps, m_rb), (v_ng, v_fg, v_ps, v_rb) = [
        unpack(t) for t in small]
    loss = small[0][n4, 0]

    return (loss, dx.reshape(x.shape),
            g_ng, g_win, g_rb, g_pw, g_ps, g_woa, g_wop, g_gb, g_wo, g_fg,
            d_ng, d_win, d_rb, d_pw, d_ps, d_woa, d_wop, d_gb, d_wo, d_fg,
            m_ng, m_win, m_rb, m_pw, m_ps, m_woa, m_wop, m_gb, m_wo, m_fg,
            v_ng, v_win, v_rb, v_pw, v_ps, v_woa, v_wop, v_gb, v_wo, v_fg)
```

```python
import jax
import jax.numpy as jnp
from jax import lax
from jax.experimental import pallas as pl
from jax.experimental.pallas import tpu as pltpu

F32 = jnp.float32
BF = jnp.bfloat16
MESH = pl.DeviceIdType.MESH

N_DEV = 8
CHUNK = 64
N_LEFT_CHUNKS = 8
HEAD_DIM = 128
MAX_REL = 128
N_REL = 2 * MAX_REL + 1
N_REL_PAD = 384
POOL_WINDOWS = (2, 4, 8, 16)
HALO = 16
EPS = 1e-6
ADAM_LR = 0.001
ADAM_B1 = 0.9
ADAM_B2 = 0.999
ADAM_EPS = 1e-08
ADAM_WD = 0.01
ADAM_STEP = 10
NEG = -1e30
LANES = 128
TQ = N_LEFT_CHUNKS * CHUNK
TK = 2 * TQ
SKEW = 2 * TK
QS_FWD = 2 * CHUNK
QS_BWD = TQ
VMEM_LIMIT = 52 * 1024 * 1024
NN = (((1,), (0,)), ((), ()))
NT = (((1,), (1,)), ((), ()))
TN = (((0,), (0,)), ((), ()))


def _params(n_grid):
    return pltpu.CompilerParams(dimension_semantics=("arbitrary",) * n_grid, vmem_limit_bytes=VMEM_LIMIT)


def _sig(z):
    return 1.0 / (1.0 + jnp.exp(-z))


def _silu_and_grad(z):
    s = _sig(z)
    return z * s, s * (1.0 + z * (1.0 - s))


def _tile(n, pref):
    t = min(n, pref)
    assert n % t == 0, (n, pref)
    return t


def _sds(shape, dtype):
    return jax.ShapeDtypeStruct(shape, dtype)


class _Comm:
    def __init__(self, ins, outs, scratch, start, wait, hooks=()):
        self.ins, self.outs, self.scratch = list(ins), list(outs), list(scratch)
        self.start, self.wait, self.hooks = start, wait, tuple(hooks)


def _call(name, body, grid, ins, outs, scratch=(), aliases=None, comm=None):
    n_in, n_out, n_scr = len(ins), len(outs), len(scratch)
    c_in = len(comm.ins) if comm else 0
    c_out = len(comm.outs) if comm else 0
    n_steps = 1
    for g in grid:
        n_steps *= g

    def kern(*refs):
        o0 = n_in + c_in
        s0 = o0 + n_out + c_out
        if comm:
            c_refs = (refs[n_in:o0], refs[o0 + n_out:s0], refs[s0 + n_scr:])
            step = pl.program_id(0)
            for ax in range(1, len(grid)):
                step = step * grid[ax] + pl.program_id(ax)

            @pl.when(step == 0)
            def _():
                comm.start(*c_refs)

            for frac, hook in comm.hooks:
                @pl.when(step == int(frac * n_steps))
                def _(hook=hook):
                    hook(*c_refs)

        body(refs[:n_in], refs[o0:o0 + n_out], refs[s0:s0 + n_scr])

        if comm:
            @pl.when(step == n_steps - 1)
            def _():
                comm.wait(*c_refs)

    hbm = pl.BlockSpec(memory_space=pl.ANY)
    return pl.pallas_call(
        kern, grid=grid,
        in_specs=[s for _, s in ins] + [hbm] * c_in, out_specs=[s for _, s in outs] + [hbm] * c_out,
        out_shape=[o for o, _ in outs] + (comm.outs if comm else []),
        scratch_shapes=list(scratch) + (comm.scratch if comm else []),
        name=name, compiler_params=_params(len(grid)), input_output_aliases=aliases or {},
    )(*([a for a, _ in ins] + (comm.ins if comm else [])))


def _mm(name, grid, ins, outs, dims, epi, aliases=None, comm=None):
    def body(in_refs, out_refs, _):
        acc = lax.dot_general(in_refs[0][...], in_refs[1][...], dims, preferred_element_type=F32)
        epi(acc, in_refs[2:], out_refs)

    return _call(name, body, grid, ins, outs, aliases=aliases, comm=comm)


def _rms_fwd(x, g, comm=None):
    s, d = x.shape
    tr = _tile(s, 256)

    def body(in_refs, out_refs, _):
        xv = in_refs[0][...]
        r = lax.rsqrt(jnp.mean(xv * xv, axis=-1, keepdims=True) + EPS)
        out_refs[0][...] = (xv * r * in_refs[1][...]).astype(BF)

    row = pl.BlockSpec((tr, d), lambda i: (i, 0))
    return _call("rms_fwd", body, (s // tr,), [(x, row), (g, pl.BlockSpec((1, d), lambda i: (0, 0)))],
                 [(_sds((s, d), BF), row)], comm=comm)


def _final_norm(x2, target, g):
    s, d = x2.shape
    tr = _tile(s, 128)

    def kern(x_ref, t_ref, g_ref, dx_ref, dxb_ref, dg_ref, loss_ref):
        i = pl.program_id(0)
        xv = x_ref[...]
        gv = g_ref[...]
        r = lax.rsqrt(jnp.mean(xv * xv, axis=-1, keepdims=True) + EPS)
        xhat = xv * r
        err = xhat * gv - t_ref[...]
        dy = err * (1.0 / d)
        gy = dy * gv
        dx = r * (gy - xhat * jnp.mean(gy * xhat, axis=-1, keepdims=True))
        dx_ref[...] = dx
        dxb_ref[...] = dx.astype(BF)
        dg = jnp.sum(dy * xhat, axis=0, keepdims=True)
        ls = jnp.broadcast_to(0.5 * jnp.sum(jnp.mean(err * err, axis=-1, keepdims=True)), (1, LANES))

        @pl.when(i == 0)
        def _():
            dg_ref[...] = dg
            loss_ref[...] = ls

        @pl.when(i > 0)
        def _():
            dg_ref[...] += dg
            loss_ref[...] += ls

    row = pl.BlockSpec((tr, d), lambda i: (i, 0))
    vec = pl.BlockSpec((1, d), lambda i: (0, 0))
    return pl.pallas_call(
        kern, grid=(s // tr,), in_specs=[row, row, vec],
        out_specs=[row, row, vec, pl.BlockSpec((1, LANES), lambda i: (0, 0))],
        out_shape=[_sds((s, d), F32), _sds((s, d), BF), _sds((1, d), F32), _sds((1, LANES), F32)],
        name="final_norm", compiler_params=_params(1))(x2, target, g)


def _rms_bwd(x, dh, dx2, g):
    s, d = x.shape
    tr = _tile(s, 128)

    def kern(x_ref, dh_ref, dx2_ref, g_ref, dx_ref, dg_ref):
        i = pl.program_id(0)
        xv = x_ref[...]
        r = lax.rsqrt(jnp.mean(xv * xv, axis=-1, keepdims=True) + EPS)
        xhat = xv * r
        dhv = dh_ref[...]
        gh = dhv * g_ref[...]
        dx_ref[...] = dx2_ref[...] + r * (gh - xhat * jnp.mean(gh * xhat, axis=-1, keepdims=True))
        dg = jnp.sum(dhv * xhat, axis=0, keepdims=True)

        @pl.when(i == 0)
        def _():
            dg_ref[...] = dg

        @pl.when(i > 0)
        def _():
            dg_ref[...] += dg

    row = pl.BlockSpec((tr, d), lambda i: (i, 0))
    vec = pl.BlockSpec((1, d), lambda i: (0, 0))
    return pl.pallas_call(
        kern, grid=(s // tr,), in_specs=[row, row, row, vec], out_specs=[row, vec],
        out_shape=[_sds((s, d), F32), _sds((1, d), F32)],
        name="rms_bwd", compiler_params=_params(1))(x, dh, dx2, g)


def _rel_index(j, backward):
    if backward:
        rel = 2 * TQ - 1 - j
    else:
        rel = TQ - jnp.where(j < TK, j, j - SKEW)
    return jnp.clip(rel, -MAX_REL, MAX_REL) + MAX_REL


def _bias_rows(rel_bias_pad):
    h = rel_bias_pad.shape[0]

    def kern(rb_ref, o_ref):
        j = lax.broadcasted_iota(jnp.int32, (N_REL_PAD, SKEW), 1)
        k = lax.broadcasted_iota(jnp.int32, (N_REL_PAD, SKEW), 0)
        onehot = (_rel_index(j, False) == k).astype(F32)
        o_ref[...] = jnp.dot(rb_ref[...], onehot, preferred_element_type=F32, precision=lax.Precision.HIGHEST)

    return pl.pallas_call(kern, out_shape=_sds((h, SKEW), F32), name="bias_rows")(rel_bias_pad)


def _bias_grad(ddiag):
    h = ddiag.shape[0]

    def kern(d_ref, o_ref):
        j = lax.broadcasted_iota(jnp.int32, (N_REL_PAD, SKEW), 1)
        k = lax.broadcasted_iota(jnp.int32, (N_REL_PAD, SKEW), 0)
        onehot = ((_rel_index(j, True) == k) & (j < TQ + TK - 1)).astype(F32)
        o_ref[...] = lax.dot_general(d_ref[...], onehot, NT, preferred_element_type=F32,
                                     precision=lax.Precision.HIGHEST)

    return pl.pallas_call(kern, out_shape=_sds((h, N_REL_PAD), F32), name="bias_grad")(ddiag)


def _bias_tile(row, first):
    t = pltpu.roll(jnp.broadcast_to(row, (TQ, SKEW)), 0, 1, stride=1, stride_axis=0)[:, :TK]
    r = lax.broadcasted_iota(jnp.int32, (TQ, TK), 0) // CHUNK
    col = lax.broadcasted_iota(jnp.int32, (TQ, TK), 1)
    dist = N_LEFT_CHUNKS + r - col // CHUNK
    keep = (dist >= 0) & (dist <= N_LEFT_CHUNKS) & jnp.logical_not(first & (col < TQ))
    return jnp.where(keep, t, NEG)


def _band(j, qs):
    return pl.ds(j * qs, qs), pl.ds(j * qs, qs + TQ)


def _band_rows(prev_ref, cur_ref, j, qs):
    return jnp.concatenate([prev_ref[j * qs:, :], cur_ref[:(j + 1) * qs, :]], axis=0)


def _scores(q, keys, tile):
    s = lax.dot_general(q, keys, NT, preferred_element_type=F32) * (HEAD_DIM ** -0.5) + tile
    m = jnp.max(s, axis=1, keepdims=True)
    p = jnp.exp(s - m)
    return p, jnp.sum(p, axis=1, keepdims=True)


def _attn_fwd(proj, base, a_width, comm=None):
    s_len = proj.shape[0]
    heads = a_width // HEAD_DIM
    nq = s_len // TQ
    kb, vb, zb = heads, 2 * heads, 3 * heads

    def kern(q_ref, kp_ref, kc_ref, vp_ref, vc_ref, z_ref, base_ref, att_ref, ya_ref, tile_ref):
        i = pl.program_id(1)

        @pl.when(i <= 1)
        def _():
            tile_ref[...] = _bias_tile(base_ref[...], i == 0)

        for j in range(TQ // QS_FWD):
            rows, cols = _band(j, QS_FWD)
            p, l = _scores(q_ref[rows, :], _band_rows(kp_ref, kc_ref, j, QS_FWD), tile_ref[rows, cols])
            o = jnp.dot(p.astype(BF), _band_rows(vp_ref, vc_ref, j, QS_FWD), preferred_element_type=F32) / l
            att_ref[rows, :] = o.astype(BF)
            z = z_ref[rows, :].astype(F32)
            ya_ref[rows, :] = (o * (z * _sig(z))).astype(BF)

    blk = lambda off: pl.BlockSpec((TQ, HEAD_DIM), lambda h, i: (i, off + h))
    prev = lambda off: pl.BlockSpec((TQ, HEAD_DIM), lambda h, i: (jnp.maximum(i - 1, 0), off + h))
    out = pl.BlockSpec((TQ, HEAD_DIM), lambda h, i: (i, h))
    def body(in_refs, out_refs, scratch_refs):
        kern(*in_refs, *out_refs, *scratch_refs)

    act = _sds((s_len, a_width), BF)
    res = _call(
        "attn_fwd", body, (heads, nq),
        [(proj, blk(0)), (proj, prev(kb)), (proj, blk(kb)), (proj, prev(vb)), (proj, blk(vb)), (proj, blk(zb)),
         (base, pl.BlockSpec((None, 1, SKEW), lambda h, i: (h, 0, 0)))],
        [(act, out), (act, out)], scratch=[pltpu.VMEM((TQ, TK), F32)], comm=comm)
    return res[:2], res[2:]


def _attn_bwd(proj, datt, base, a_width, comm=None):
    s_len = proj.shape[0]
    heads = a_width // HEAD_DIM
    nq = s_len // TQ
    kb, vb = heads, 2 * heads
    scale = HEAD_DIM ** -0.5

    def body(in_refs, out_refs, scratch_refs):
        q_ref, kp_ref, kc_ref, vp_ref, vc_ref, do_ref, base_ref = in_refs
        dq_ref, dk_ref, dv_ref, dd_ref = out_refs
        tile_ref, dsacc_ref, ak_ref, av_ref = scratch_refs
        i = pl.program_id(1)

        @pl.when(i <= 1)
        def _():
            tile_ref[...] = _bias_tile(base_ref[...], i == 0)

        @pl.when(i == 0)
        def _():
            dsacc_ref[...] = jnp.zeros_like(dsacc_ref)
            ak_ref[...] = jnp.zeros_like(ak_ref)
            av_ref[...] = jnp.zeros_like(av_ref)

        @pl.when(i > 0)
        def _():
            ak_ref[:TQ, :] = ak_ref[TQ:, :]
            av_ref[:TQ, :] = av_ref[TQ:, :]
            ak_ref[TQ:, :] = jnp.zeros((TQ, HEAD_DIM), F32)
            av_ref[TQ:, :] = jnp.zeros((TQ, HEAD_DIM), F32)

        @pl.when(i < nq)
        def _():
            for j in range(TQ // QS_BWD):
                rows, cols = _band(j, QS_BWD)
                q = q_ref[rows, :]
                do = do_ref[rows, :]
                keys = _band_rows(kp_ref, kc_ref, j, QS_BWD)
                p, l = _scores(q, keys, tile_ref[rows, cols])
                p = p / l
                dp = lax.dot_general(do, _band_rows(vp_ref, vc_ref, j, QS_BWD), NT, preferred_element_type=F32)
                ds = p * (dp - jnp.sum(p * dp, axis=1, keepdims=True))
                dsacc_ref[rows, cols] += ds
                dsb = ds.astype(BF)
                dq_ref[rows, :] = (jnp.dot(dsb, keys, preferred_element_type=F32) * scale).astype(BF)
                ak_ref[cols, :] += lax.dot_general(dsb, q, TN, preferred_element_type=F32) * scale
                av_ref[cols, :] += lax.dot_general(p.astype(BF), do, TN, preferred_element_type=F32)

        dk_ref[...] = ak_ref[:TQ, :].astype(BF)
        dv_ref[...] = av_ref[:TQ, :].astype(BF)

        @pl.when(i == nq)
        def _():
            acc = dsacc_ref[...]
            rr = lax.broadcasted_iota(jnp.int32, (TQ, TQ), 0)
            cc = lax.broadcasted_iota(jnp.int32, (TQ, TQ), 1)
            flip = (rr + cc == TQ - 1).astype(BF)
            hi = acc.astype(BF)
            lo = (acc - hi.astype(F32)).astype(BF)
            rev = jnp.dot(flip, hi, preferred_element_type=F32) + jnp.dot(flip, lo, preferred_element_type=F32)
            wide = jnp.concatenate([rev, jnp.zeros((TQ, SKEW - TK), F32)], axis=1)
            dd_ref[...] = jnp.sum(pltpu.roll(wide, 0, 1, stride=1, stride_axis=0), axis=0, keepdims=True)

    last = nq - 1
    cur = lambda off: pl.BlockSpec((TQ, HEAD_DIM), lambda h, i: (jnp.minimum(i, last), off + h))
    prev = lambda off: pl.BlockSpec((TQ, HEAD_DIM), lambda h, i: (jnp.maximum(jnp.minimum(i, last) - 1, 0), off + h))
    done = pl.BlockSpec((TQ, HEAD_DIM), lambda h, i: (jnp.maximum(i - 1, 0), h))
    row = pl.BlockSpec((None, 1, SKEW), lambda h, i: (h, 0, 0))
    act = _sds((s_len, a_width), BF)
    res = _call(
        "attn_bwd", body, (heads, nq + 1),
        [(proj, cur(0)), (proj, prev(kb)), (proj, cur(kb)), (proj, prev(vb)), (proj, cur(vb)), (datt, cur(0)),
         (base, row)],
        [(act, cur(0)), (act, done), (act, done), (_sds((heads, 1, SKEW), F32), row)],
        scratch=[pltpu.VMEM((TQ, TK), F32), pltpu.VMEM((TQ, TK), F32),
                 pltpu.VMEM((TK, HEAD_DIM), F32), pltpu.VMEM((TK, HEAD_DIM), F32)],
        comm=comm)
    return res[:4], res[4:]


def _pool_fwd(proj, pool_w, pool_scale, p_width, u_blk, z_blk):
    s_len = proj.shape[0]
    cg = p_width // len(POOL_WINDOWS)
    tt = _tile(s_len, 512)

    def kern(up_ref, uc_ref, z_ref, pw_ref, sc_ref, d_ref, y_ref, yp_ref):
        t = pl.program_id(0)
        row = lax.broadcasted_iota(jnp.int32, (tt, 1), 0) + t * tt
        for g, w in enumerate(POOL_WINDOWS):
            cs = slice(g * cg, (g + 1) * cg)
            prev = jnp.where(t == 0, 0.0, up_ref[:, cs].astype(F32))
            cur = uc_ref[:, cs].astype(F32)
            ws = jnp.concatenate([prev, cur], axis=0)
            sh = 1
            while sh < w:
                ws = ws + pltpu.roll(ws, sh, 0)
                sh *= 2
            cnt = jnp.minimum(row + 1, w).astype(F32)
            db = (ws[HALO:, :] / cnt - cur).astype(BF)
            y = jnp.dot(db, pw_ref[g], preferred_element_type=F32)
            d_ref[:, cs] = db
            y_ref[:, cs] = y.astype(BF)
            z = z_ref[:, cs].astype(F32)
            yp_ref[:, cs] = (y * sc_ref[:, cs] * (z * _sig(z))).astype(BF)

    full = pl.BlockSpec((tt, p_width), lambda t: (t, 0))
    return pl.pallas_call(
        kern, grid=(s_len // tt,),
        in_specs=[pl.BlockSpec((HALO, p_width), lambda t: (jnp.maximum(t * (tt // HALO) - 1, 0), u_blk)),
                  pl.BlockSpec((tt, p_width), lambda t: (t, u_blk)),
                  pl.BlockSpec((tt, p_width), lambda t: (t, z_blk)),
                  pl.BlockSpec((len(POOL_WINDOWS), cg, cg), lambda t: (0, 0, 0)),
                  pl.BlockSpec((1, p_width), lambda t: (0, 0))],
        out_specs=[full, full, full], out_shape=[_sds((s_len, p_width), BF)] * 3,
        name="pool_fwd", compiler_params=_params(1))(proj, proj, proj, pool_w, pool_scale)


def _pool_bwd(dy, dmean, pool_w):
    s_len, p_width = dy.shape
    ng = len(POOL_WINDOWS)
    cg = p_width // ng
    tt = _tile(s_len, 512)
    nt = s_len // tt

    def kern(dyc_ref, dyn_ref, d_ref, pw_ref, du_ref, dpw_ref):
        t = pl.program_id(0)

        @pl.when(t == 0)
        def _():
            dpw_ref[...] = jnp.zeros_like(dpw_ref)

        row = lax.broadcasted_iota(jnp.int32, (tt + HALO, 1), 0) + t * tt
        for g, w in enumerate(POOL_WINDOWS):
            cs = slice(g * cg, (g + 1) * cg)
            dyc = dyc_ref[:, cs]
            ddc = lax.dot_general(dyc, pw_ref[g], NT, preferred_element_type=F32)
            ddn = lax.dot_general(dyn_ref[:, cs], pw_ref[g], NT, preferred_element_type=F32)
            ddn = jnp.where(t == nt - 1, 0.0, ddn)
            cnt = jnp.minimum(row + 1, w).astype(F32)
            ws = jnp.concatenate([ddc, ddn], axis=0) / cnt
            sh = 1
            while sh < w:
                ws = ws + pltpu.roll(ws, tt + HALO - sh, 0)
                sh *= 2
            du_ref[:, cs] = (ws[:tt, :] - ddc).astype(BF)
            dpw_ref[g] += lax.dot_general(d_ref[:, cs], dyc, TN, preferred_element_type=F32)

    full = pl.BlockSpec((tt, p_width), lambda t: (t, 0))
    pw_spec = pl.BlockSpec((ng, cg, cg), lambda t: (0, 0, 0))
    return pl.pallas_call(
        kern, grid=(nt,),
        in_specs=[full,
                  pl.BlockSpec((HALO, p_width), lambda t: (jnp.minimum((t + 1) * (tt // HALO), s_len // HALO - 1), 0)),
                  full, pw_spec],
        out_specs=[full, pw_spec], out_shape=[_sds((s_len, p_width), BF), _sds((ng, cg, cg), F32)],
        name="pool_bwd", compiler_params=_params(1))(dy, dy, dmean, pool_w)


def _adam(g, w_ref, m_ref, v_ref, g_out, d_out, m_out, v_out):
    m = ADAM_B1 * m_ref[...] + (1.0 - ADAM_B1) * g
    v = ADAM_B2 * v_ref[...] + (1.0 - ADAM_B2) * (g * g)
    m_hat = m / (1.0 - ADAM_B1 ** ADAM_STEP)
    v_hat = v / (1.0 - ADAM_B2 ** ADAM_STEP)
    g_out[...] = g
    d_out[...] = -ADAM_LR * (m_hat / (jnp.sqrt(v_hat) + ADAM_EPS) + ADAM_WD * w_ref[...])
    m_out[...] = m
    v_out[...] = v


def _adamw_shard(name, parts, w, m, v, row_off):
    rw, cw = w.shape
    sw = parts.shape[2]
    tr = _tile(rw, 512)
    assert row_off % tr == 0 and cw % sw == 0

    def kern(b_ref, w_ref, m_ref, v_ref, g_out, d_out, m_out, v_out):
        b = b_ref[...].astype(F32)
        _adam(((b[0] + b[1]) + b[2]) + b[3], w_ref, m_ref, v_ref, g_out, d_out, m_out, v_out)

    blk = pl.BlockSpec((tr, sw), lambda ct, i: (i, ct))
    return pl.pallas_call(
        kern, grid=(cw // sw, rw // tr),
        in_specs=[pl.BlockSpec((4, tr, sw), lambda ct, i: (0, (row_off + ct * rw) // tr + i, 0)), blk, blk, blk],
        out_specs=[blk] * 4, out_shape=[_sds((rw, cw), F32)] * 4,
        name=name, compiler_params=_params(2))(parts, w, m, v)


def _adamw_row_halves(name, parts_lo, parts_hi, w, m, v):
    rw, cw = w.shape
    sw = parts_lo.shape[2]
    half = rw // 2
    tr = _tile(half, 512)
    nh = half // tr

    def kern(lo_ref, hi_ref, w_ref, m_ref, v_ref, g_out, d_out, m_out, v_out):
        i = pl.program_id(1)

        def update(b_ref):
            b = b_ref[...].astype(F32)
            _adam(((b[0] + b[1]) + b[2]) + b[3], w_ref, m_ref, v_ref, g_out, d_out, m_out, v_out)

        @pl.when(i < nh)
        def _():
            update(lo_ref)

        @pl.when(i >= nh)
        def _():
            update(hi_ref)

    blk = pl.BlockSpec((tr, sw), lambda ct, i: (i, ct))
    return pl.pallas_call(
        kern, grid=(cw // sw, rw // tr),
        in_specs=[pl.BlockSpec((4, tr, sw), lambda ct, i: (0, ct * nh + jnp.minimum(i, nh - 1), 0)),
                  pl.BlockSpec((4, tr, sw), lambda ct, i: (0, ct * nh + jnp.maximum(i - nh, 0), 0)), blk, blk, blk],
        out_specs=[blk] * 4, out_shape=[_sds((rw, cw), F32)] * 4,
        name=name, compiler_params=_params(2))(parts_lo, parts_hi, w, m, v)


def _both(c1, c2):
    n_in, n_out, n_sem = len(c1.ins), len(c1.outs), len(c1.scratch)

    def split(ins, outs, sems):
        return (ins[:n_in], outs[:n_out], sems[:n_sem]), (ins[n_in:], outs[n_out:], sems[n_sem:])

    def start(*refs):
        r1, r2 = split(*refs)
        c1.start(*r1)
        c2.start(*r2)

    def wait(*refs):
        r1, r2 = split(*refs)
        c1.wait(*r1)
        c2.wait(*r2)

    return _Comm(c1.ins + c2.ins, c1.outs + c2.outs, c1.scratch + c2.scratch, start, wait)


def _position():
    return lax.axis_index("x"), lax.axis_index("y"), lax.axis_index("c")


def _gather_comm(shards, cols=None):
    if cols is None:
        pieces = [(a, None) for a in range(len(shards))]
        shapes = [_sds(s.shape, s.dtype) for s in shards]
    else:
        half = shards[0].shape[0] // 2
        pieces = [(0, pl.ds(0, half)), (0, pl.ds(half, half))]
        shapes = [_sds((shards[0].shape[0], cols[1]), shards[0].dtype)]
    n = len(pieces)

    def plan(xs, outs, sems, only=None):
        send_sems, recv_sems, local_sems = sems
        x, y, c = _position()
        me, sibling = (x, y, c), (x, y, 1 - c)
        chips = [(1 - x, y), (x, 1 - y), (1 - x, 1 - y)]
        which = range(n) if only is None else only

        def source(v):
            a, rows = pieces[v]
            return xs[a] if rows is None else xs[a].at[rows, pl.ds(cols[0], cols[1])]

        def landing(v, block):
            a, rows = pieces[v]
            dst = outs[a].at[4 * block[0] + 2 * block[1] + block[2]]
            return dst if rows is None else dst.at[rows, :]

        def copy(v, k, block, to, own=False):
            dst = landing(v, block)
            return pltpu.make_async_remote_copy(
                src_ref=source(v) if own else dst, dst_ref=dst,
                send_sem=send_sems.at[7 * v + k], recv_sem=recv_sems.at[7 * v + k],
                device_id=to, device_id_type=MESH)

        by_chip = [(j, chip, v) for v in which for j, chip in enumerate(chips)]
        return dict(
            mine=lambda: [pltpu.make_async_copy(source(v), landing(v, me), local_sems.at[v]) for v in which],
            first=lambda: ([copy(v, 0, me, sibling, own=True) for v in which]
                           + [copy(v, 1 + j, me, (*chip, c), own=True) for j, chip, v in by_chip]),
            landed=lambda: [copy(v, 1 + j, (*chip, c), me) for j, chip, v in by_chip],
            passed=lambda: [copy(v, 4 + j, (*chip, c), sibling) for j, chip, v in by_chip],
            rest=lambda: ([copy(v, 0, sibling, me) for v in which]
                          + [copy(v, 4 + j, (*chip, 1 - c), me) for j, chip, v in by_chip]))

    def start(*refs):
        p = plan(*refs)
        for cp in p["mine"]() + p["first"]():
            cp.start()

    def pass_on(only):
        def hook(*refs):
            p = plan(*refs, only=only)
            for arrived, onward in zip(p["landed"](), p["passed"]()):
                arrived.wait_recv()
                onward.start()
        return hook

    def wait(*refs):
        if cols is not None:
            pass_on([1])(*refs)
        p = plan(*refs)
        for cp in p["rest"]():
            cp.wait_recv()
        for cp in p["first"]() + p["passed"]():
            cp.wait_send()
        for cp in p["mine"]():
            cp.wait()

    hooks = [(0.75, pass_on(None))] if cols is None else [(0.5, pass_on([0]))]
    return _Comm(shards, [_sds((N_DEV,) + s.shape, s.dtype) for s in shapes],
                 [pltpu.SemaphoreType.DMA((7 * n,)), pltpu.SemaphoreType.DMA((7 * n,)),
                  pltpu.SemaphoreType.DMA((n,))], start, wait, hooks)


def _cores_comm(slab):
    _, _, r, sw = slab.shape

    def copies(ins, outs, sems):
        x, y, c = _position()
        return [pltpu.make_async_remote_copy(
            src_ref=ins[0].at[1 - c], dst_ref=outs[0], send_sem=sems[0], recv_sem=sems[1],
            device_id=(x, y, 1 - c), device_id_type=MESH)]

    def start(*refs):
        for cp in copies(*refs):
            cp.start()

    def wait(*refs):
        for cp in copies(*refs):
            cp.wait()

    return _Comm([slab], [_sds((4, r, sw), slab.dtype)],
                 [pltpu.SemaphoreType.DMA, pltpu.SemaphoreType.DMA], start, wait)


def _add_core_partials(name, slab, recv, core, tr):
    _, _, r, sw = slab.shape

    def kern(c_ref, a_ref, b_ref, o_ref):
        o_ref[...] = (a_ref[...].astype(F32) + b_ref[...].astype(F32)).astype(BF)

    return pl.pallas_call(
        kern,
        grid_spec=pltpu.PrefetchScalarGridSpec(
            num_scalar_prefetch=1, grid=(4, r // tr),
            in_specs=[pl.BlockSpec((None, None, tr, sw), lambda k, i, c_ref: (c_ref[0], k, i, 0)),
                      pl.BlockSpec((None, tr, sw), lambda k, i, c_ref: (k, i, 0))],
            out_specs=pl.BlockSpec((None, tr, sw), lambda k, i, c_ref: (k, i, 0))),
        out_shape=_sds((4, r, sw), BF), name=name, compiler_params=_params(2))(core, slab, recv)


def _chips_comm(part):
    _, r, sw = part.shape

    def copies(ins, outs, sems):
        send_sems, recv_sems, local_sem = sems
        x, y, c = _position()
        mine = 2 * x + y
        local = pltpu.make_async_copy(ins[0].at[mine], outs[0].at[mine], local_sem)
        chips = [(1 - x, y), (x, 1 - y), (1 - x, 1 - y)]
        remote = [pltpu.make_async_remote_copy(
            src_ref=ins[0].at[2 * px + py], dst_ref=outs[0].at[mine],
            send_sem=send_sems.at[j], recv_sem=recv_sems.at[j],
            device_id=(px, py, c), device_id_type=MESH) for j, (px, py) in enumerate(chips)]
        return [local] + remote

    def start(*refs):
        for cp in copies(*refs):
            cp.start()

    def wait(*refs):
        for cp in copies(*refs):
            cp.wait()

    return _Comm([part], [_sds((4, r, sw), part.dtype)],
                 [pltpu.SemaphoreType.DMA((3,)), pltpu.SemaphoreType.DMA((3,)), pltpu.SemaphoreType.DMA],
                 start, wait)


def _small_allreduce_adamw(partial, w, m, v):
    nr = partial.shape[0]

    def kern(p_ref, w_ref, m_ref, v_ref, g_out, d_out, m_out, v_out, gath_ref, send_sems, recv_sems):
        x, y, c = _position()
        me = 4 * x + 2 * y + c
        gath_ref[me] = p_ref[...]
        copies = []
        for mask in range(1, N_DEV):
            peer = (x ^ (mask >> 2), y ^ ((mask >> 1) & 1), c ^ (mask & 1))
            copies.append(pltpu.make_async_remote_copy(
                src_ref=p_ref, dst_ref=gath_ref.at[me],
                send_sem=send_sems.at[mask - 1], recv_sem=recv_sems.at[mask - 1],
                device_id=peer, device_id_type=MESH))
        for cp in copies:
            cp.start()
        for cp in copies:
            cp.wait()
        tot = gath_ref[0]
        for k in range(1, N_DEV):
            tot = tot + gath_ref[k]
        _adam(tot, w_ref, m_ref, v_ref, g_out, d_out, m_out, v_out)

    vmem = pl.BlockSpec(memory_space=pltpu.VMEM)
    return pl.pallas_call(
        kern, in_specs=[vmem] * 4, out_specs=[vmem] * 4, out_shape=[_sds((nr, LANES), F32)] * 4,
        scratch_shapes=[pltpu.VMEM((N_DEV, nr, LANES), F32),
                        pltpu.SemaphoreType.DMA((N_DEV - 1,)), pltpu.SemaphoreType.DMA((N_DEV - 1,))],
        name="small_allreduce_adamw")(partial, w, m, v)


def kernel(x, norm_gain, w_in, rel_bias, pool_w, pool_scale, w_out_attn, w_out_pool, gate_bias, w_out, final_gain, loss_target, m_norm_gain, m_w_in, m_rel_bias, m_pool_w, m_pool_scale, m_w_out_attn, m_w_out_pool, m_gate_bias, m_w_out, m_final_gain, v_norm_gain, v_w_in, v_rel_bias, v_pool_w, v_pool_scale, v_w_out_attn, v_w_out_pool, v_gate_bias, v_w_out, v_final_gain):
    _, s_len, d = x.shape
    a = w_out_attn.shape[0]
    p = w_out_pool.shape[0]
    heads = a // HEAD_DIM
    ng = len(POOL_WINDOWS)
    cg = p // ng
    sw = d // N_DEV
    n_in = w_in.shape[1] * N_DEV
    assert a == p and a + p == d and cg == sw and n_in == 5 * d and w_in.shape[1] == 5 * sw
    assert s_len % TQ == 0 and rel_bias.shape == (heads, N_REL)
    tm = _tile(s_len, 1024)
    x2d = x.reshape(s_len, d)
    tgt = loss_target.reshape(s_len, d)

    g1 = norm_gain.reshape(1, d)
    g2 = final_gain.reshape(1, d)
    scale_row = pool_scale.reshape(1, p)
    tn = sw
    per = w_in.shape[1] // tn
    hbm = pl.BlockSpec(memory_space=pl.ANY)

    w_bf = w_in.astype(BF)
    hb, landed = _rms_fwd(x2d, g1, comm=_gather_comm([w_bf], cols=(0, tn)))
    win_rounds = []

    def store_bf16(acc, _, outs):
        outs[0][...] = acc.astype(BF)

    proj = None
    for r in range(per):
        win_rounds.append(landed)
        if r + 1 < per:
            comm = _gather_comm([w_bf], cols=((r + 1) * tn, tn))
        else:
            comm = _gather_comm([w_out_attn.astype(BF), pool_w.astype(BF), gate_bias])
        ins = [(hb, pl.BlockSpec((tm, d), lambda i, j: (i, 0))),
               (landed, pl.BlockSpec((None, d, tn), lambda i, j: (j, 0, 0)))]
        if proj is not None:
            ins.append((proj, hbm))
        proj, landed, *rest = _mm(
            f"proj_{r}", (s_len // tm, N_DEV), ins,
            [(_sds((s_len, n_in), BF), pl.BlockSpec((tm, tn), lambda i, j, r=r: (i, per * j + r)))],
            NN, store_bf16, aliases={2: 0} if r else None, comm=comm)
    woa = landed.transpose(1, 0, 2).reshape(a, d)
    pw = rest[0].transpose(1, 0, 2, 3).reshape(ng, cg, cg)
    gb = rest[1].transpose(1, 0, 2).reshape(2, d)

    rb_pad = jnp.pad(rel_bias, ((0, 0), (0, N_REL_PAD - N_REL)))
    base = _bias_rows(rb_pad).reshape(heads, 1, SKEW)
    (att, ya), (wop_g,) = _attn_fwd(proj, base, a, comm=_gather_comm([w_out_pool.astype(BF)]))
    wop = wop_g.transpose(1, 0, 2).reshape(p, d)
    u_blk, z_blk = 4 * a // p, 4 * a // p + 1
    dmean, ypre, yp = _pool_fwd(proj, pw, scale_row, p, u_blk, z_blk)

    ga_t, gp_t = (4 * a + 2 * p) // tn, (4 * a + 2 * p + d) // tn

    def gate_kernel(in_refs, out_refs, _):
        ya_ref, woa_ref, yp_ref, wop_ref, ga_ref, gp_ref, gb_ref = in_refs
        m_ref, a_ref, p_ref = out_refs
        am = jnp.dot(ya_ref[...], woa_ref[...], preferred_element_type=F32)
        pm = jnp.dot(yp_ref[...], wop_ref[...], preferred_element_type=F32)
        sa = _sig(ga_ref[...].astype(F32) + gb_ref[0:1, :])
        sp = _sig(gp_ref[...].astype(F32) + gb_ref[1:2, :])
        m_ref[...] = (sa * am + sp * pm).astype(BF)
        a_ref[...] = am.astype(BF)
        p_ref[...] = pm.astype(BF)

    tile_ij = pl.BlockSpec((tm, tn), lambda i, j: (i, j))
    act_d = _sds((s_len, d), BF)
    merged, am, pm, wo_g = _call(
        "gate_merge", gate_kernel, (s_len // tm, d // tn),
        [(ya, pl.BlockSpec((tm, a), lambda i, j: (i, 0))), (woa, pl.BlockSpec((a, tn), lambda i, j: (0, j))),
         (yp, pl.BlockSpec((tm, p), lambda i, j: (i, 0))), (wop, pl.BlockSpec((p, tn), lambda i, j: (0, j))),
         (proj, pl.BlockSpec((tm, tn), lambda i, j: (i, ga_t + j))),
         (proj, pl.BlockSpec((tm, tn), lambda i, j: (i, gp_t + j))),
         (gb, pl.BlockSpec((2, tn), lambda i, j: (0, j)))],
        [(act_d, tile_ij)] * 3, comm=_gather_comm([w_out.astype(BF)]))
    wo = wo_g.reshape(d, d)

    def add_residual(acc, ex, outs):
        outs[0][...] = ex[0][...] + acc

    x2 = _mm("out_proj", (s_len // tm, d // tn),
             [(merged, pl.BlockSpec((tm, d), lambda i, j: (i, 0))), (wo, pl.BlockSpec((d, tn), lambda i, j: (0, j))),
              (x2d, tile_ij)],
             [(_sds((s_len, d), F32), tile_ij)], NN, add_residual)[0]

    dx2, dx2b, dg2, loss_part = _final_norm(x2, tgt, g2)

    tmb = _tile(s_len, 512)
    tile_ji = pl.BlockSpec((tmb, tn), lambda j, i: (i, j))

    def gate_bwd(dm, ex, outs):
        a_ref, p_ref, ga_ref, gp_ref, gb_ref = ex
        da_ref, dp_ref, dga_ref, dgp_ref, dgb_ref = outs
        i = pl.program_id(1)
        sa = _sig(ga_ref[...].astype(F32) + gb_ref[0:1, :])
        sp = _sig(gp_ref[...].astype(F32) + gb_ref[1:2, :])
        dga = dm * a_ref[...].astype(F32) * sa * (1.0 - sa)
        dgp = dm * p_ref[...].astype(F32) * sp * (1.0 - sp)
        da_ref[...] = (dm * sa).astype(BF)
        dp_ref[...] = (dm * sp).astype(BF)
        dga_ref[...] = dga.astype(BF)
        dgp_ref[...] = dgp.astype(BF)
        r = lax.broadcasted_iota(jnp.int32, (8, tn), 0)
        sums = jnp.where(r == 0, jnp.sum(dga, axis=0, keepdims=True),
                         jnp.where(r == 1, jnp.sum(dgp, axis=0, keepdims=True), 0.0))

        @pl.when(i == 0)
        def _():
            dgb_ref[...] = sums

        @pl.when(i > 0)
        def _():
            dgb_ref[...] += sums

    d_am, d_pm, dga, dgp, dgb8 = _mm(
        "gate_bwd", (d // tn, s_len // tmb),
        [(dx2b, pl.BlockSpec((tmb, d), lambda j, i: (i, 0))), (wo, pl.BlockSpec((tn, d), lambda j, i: (j, 0))),
         (am, tile_ji), (pm, tile_ji),
         (proj, pl.BlockSpec((tmb, tn), lambda j, i: (i, ga_t + j))),
         (proj, pl.BlockSpec((tmb, tn), lambda j, i: (i, gp_t + j))),
         (gb, pl.BlockSpec((2, tn), lambda j, i: (0, j)))],
        [(_sds((s_len, d), BF), tile_ji)] * 4 + [(_sds((8, d), F32), pl.BlockSpec((8, tn), lambda j, i: (0, j)))],
        NT, gate_bwd)

    za_t = 3 * a // tn

    def attn_gate_bwd(dya, ex, outs):
        silu, dsilu = _silu_and_grad(ex[0][...].astype(F32))
        outs[0][...] = (dya * silu).astype(BF)
        outs[1][...] = (dya * ex[1][...].astype(F32) * dsilu).astype(BF)

    datt, dza = _mm(
        "attn_gate_bwd", (s_len // tm, a // tn),
        [(d_am, pl.BlockSpec((tm, d), lambda i, j: (i, 0))), (woa, pl.BlockSpec((tn, d), lambda i, j: (j, 0))),
         (proj, pl.BlockSpec((tm, tn), lambda i, j: (i, za_t + j))), (att, tile_ij)],
        [(_sds((s_len, a), BF), tile_ij)] * 2, NT, attn_gate_bwd)

    zp_t = (4 * a + p) // tn

    def pool_gate_bwd(dyp, ex, outs):
        z_ref, y_ref, sc_ref = ex
        dzp_ref, dy_ref, dps_ref = outs
        i = pl.program_id(1)
        silu, dsilu = _silu_and_grad(z_ref[...].astype(F32))
        y = y_ref[...].astype(F32)
        sc = sc_ref[...]
        dyp0 = dyp * silu
        dzp_ref[...] = (dyp * (y * sc) * dsilu).astype(BF)
        dy_ref[...] = (dyp0 * sc).astype(BF)
        dps = jnp.sum(dyp0 * y, axis=0, keepdims=True)

        @pl.when(i == 0)
        def _():
            dps_ref[...] = dps

        @pl.when(i > 0)
        def _():
            dps_ref[...] += dps

    dzp, dy_pool, dps = _mm(
        "pool_gate_bwd", (p // tn, s_len // tmb),
        [(d_pm, pl.BlockSpec((tmb, d), lambda j, i: (i, 0))), (wop, pl.BlockSpec((tn, d), lambda j, i: (j, 0))),
         (proj, pl.BlockSpec((tmb, tn), lambda j, i: (i, zp_t + j))), (ypre, tile_ji),
         (scale_row, pl.BlockSpec((1, tn), lambda j, i: (0, j)))],
        [(_sds((s_len, p), BF), tile_ji)] * 2 + [(_sds((1, p), F32), pl.BlockSpec((1, tn), lambda j, i: (0, j)))],
        NT, pool_gate_bwd)

    du, dpw = _pool_bwd(dy_pool, dmean, pw)

    o_wop, o_wo, o_pool = a, d, 2 * d
    slab_a = _sds((2, 4, 2 * d + cg, sw), BF)
    slab_b = _sds((2, 4, 5 * d // 2, sw), BF)
    hbm = pl.BlockSpec(memory_space=pl.ANY)
    tmw = _tile(a, 1024)
    core = lax.axis_index("c").astype(jnp.int32).reshape(1)

    def pack_small(dpw_ref, dgb_ref, o_ref):
        rows = cg // N_DEV
        for j in range(N_DEV):
            for g in range(ng):
                o_ref[j % 2, j // 2, g * rows:(g + 1) * rows, :] = dpw_ref[g, j * rows:(j + 1) * rows, :].astype(BF)
            o_ref[j % 2, j // 2, ng * rows:, :] = jnp.concatenate(
                [dgb_ref[:, j * sw:(j + 1) * sw], jnp.zeros((cg - ng * rows - 8, sw), F32)], axis=0).astype(BF)

    slab = pl.pallas_call(
        pack_small, grid=(1,),
        in_specs=[pl.BlockSpec((ng, cg, cg), lambda i: (0, 0, 0)), pl.BlockSpec((8, d), lambda i: (0, 0))],
        out_specs=pl.BlockSpec((2, 4, cg, sw), lambda i: (0, 0, o_pool // cg, 0)), out_shape=slab_a,
        name="dw_small", compiler_params=_params(1))(dpw, dgb8)

    def into_slab(acc, _, outs):
        outs[0][...] = acc.astype(BF)

    def weight_grad(name, slab, lhs, rhs, grid, lhs_spec, rhs_spec, out_spec, comm=None):
        return _mm(name, grid, [(lhs, lhs_spec), (rhs, rhs_spec), (slab, hbm)], [(slab_a, out_spec)],
                   TN, into_slab, aliases={2: 0}, comm=comm)

    slab = weight_grad("dw_out", slab, merged, dx2b, (N_DEV, d // sw),
                       pl.BlockSpec((s_len, sw), lambda j, t: (0, j)), pl.BlockSpec((s_len, sw), lambda j, t: (0, t)),
                       pl.BlockSpec((None, None, sw, sw), lambda j, t: (j % 2, j // 2, o_wo // sw + t, 0)))[0]
    slab = weight_grad("dw_out_attn", slab, ya, d_am, (a // tmw, N_DEV),
                       pl.BlockSpec((s_len, tmw), lambda i, j: (0, i)), pl.BlockSpec((s_len, sw), lambda i, j: (0, j)),
                       pl.BlockSpec((None, None, tmw, sw), lambda i, j: (j % 2, j // 2, i, 0)))[0]
    slab = weight_grad("dw_out_pool", slab, yp, d_pm, (p // tmw, N_DEV),
                       pl.BlockSpec((s_len, tmw), lambda i, j: (0, i)), pl.BlockSpec((s_len, sw), lambda i, j: (0, j)),
                       pl.BlockSpec((None, None, tmw, sw), lambda i, j: (j % 2, j // 2, o_wop // tmw + i, 0)))[0]

    (dq, dk, dv, ddiag), (from_sibling_a,) = _attn_bwd(proj, datt, base, a, comm=_cores_comm(slab))
    chip_part_a = _add_core_partials("add_core_partials_a", slab, from_sibling_a, core, (2 * d + cg) // 2)
    drb = _bias_grad(ddiag.reshape(heads, SKEW))
    dproj = jnp.concatenate([dq, dk, dv, dza, du, dzp, dga, dgp], axis=1)

    tmd = _tile(d // 2, 1024)
    nrb = d // 2 // tmd

    def dw_in_rows(name, half, comm):
        return _mm(
            name, (nrb, n_in // sw),
            [(hb, pl.BlockSpec((s_len, tmd), lambda i, t: (0, half * nrb + i))),
             (dproj, pl.BlockSpec((s_len, sw), lambda i, t: (0, t)))],
            [(slab_b, pl.BlockSpec((None, None, tmd, sw),
                                   lambda i, t: ((t // per) % 2, (t // per) // 2, (t % per) * nrb + i, 0)))],
            TN, into_slab, comm=comm)

    slab_lo, parts_a = dw_in_rows("dw_in_lo", 0, _chips_comm(chip_part_a))
    slab_hi, from_sibling_lo = dw_in_rows("dw_in_hi", 1, _cores_comm(slab_lo))
    chip_part_lo = _add_core_partials("add_core_partials_lo", slab_lo, from_sibling_lo, core, 4 * sw)

    tk = w_in.shape[1]
    tnh = _tile(d, 1024)
    tmh = tm if s_len > tm else s_len // 2
    n_row = s_len // tmh

    def dh_rows(name, lo, hi, prev, comm):
        def body(in_refs, out_refs, scratch_refs):
            acc_ref = scratch_refs[0]
            k = pl.program_id(2)
            part = lax.dot_general(in_refs[0][:, 0:sw], in_refs[1][...], NT, preferred_element_type=F32)
            for r in range(1, per):
                part += lax.dot_general(in_refs[0][:, r * sw:(r + 1) * sw], in_refs[1 + r][...], NT,
                                        preferred_element_type=F32)

            @pl.when(k == 0)
            def _():
                acc_ref[...] = part

            @pl.when(k > 0)
            def _():
                acc_ref[...] += part

            @pl.when(k == N_DEV - 1)
            def _():
                out_refs[0][...] = acc_ref[...]

        ins = [(dproj, pl.BlockSpec((tmh, tk), lambda i, j, k: (lo + i, k)))]
        ins += [(w, pl.BlockSpec((None, tnh, sw), lambda i, j, k: (k, j, 0))) for w in win_rounds]
        if prev is not None:
            ins.append((prev, hbm))
        return _call(name, body, (hi - lo, d // tnh, N_DEV), ins,
                     [(_sds((s_len, d), F32), pl.BlockSpec((tmh, tnh), lambda i, j, k: (lo + i, j)))],
                     scratch=[pltpu.VMEM((tmh, tnh), F32)], aliases={1 + per: 0} if prev is not None else None,
                     comm=comm)

    dh, parts_lo, from_sibling_hi = dh_rows("dh_head", 0, n_row // 2, None,
                                            _both(_chips_comm(chip_part_lo), _cores_comm(slab_hi)))
    chip_part_hi = _add_core_partials("add_core_partials_hi", slab_hi, from_sibling_hi, core, 4 * sw)
    dh, parts_hi = dh_rows("dh_rest", n_row // 2, n_row, dh, _chips_comm(chip_part_hi))

    dx, dg1 = _rms_bwd(x2d, dh, dx2, g1)

    g_win, d_win, m_win, v_win = _adamw_row_halves("adamw_w_in", parts_lo, parts_hi, w_in, m_w_in, v_w_in)
    g_woa, d_woa, m_woa, v_woa = _adamw_shard("adamw_w_out_attn", parts_a, w_out_attn, m_w_out_attn, v_w_out_attn, 0)
    g_wop, d_wop, m_wop, v_wop = _adamw_shard("adamw_w_out_pool", parts_a, w_out_pool, m_w_out_pool, v_w_out_pool, o_wop)
    g_wo, d_wo, m_wo, v_wo = _adamw_shard("adamw_w_out", parts_a, w_out, m_w_out, v_w_out, o_wo)
    flat = lambda t: t.reshape(cg // 2, sw)
    pool_out = _adamw_shard("adamw_pool_w", parts_a, flat(pool_w), flat(m_pool_w), flat(v_pool_w), o_pool)
    g_pw, d_pw, m_pw, v_pw = [t.reshape(pool_w.shape) for t in pool_out]
    pad16 = lambda t: jnp.pad(t, ((0, 14), (0, 0)))
    gb_out = _adamw_shard("adamw_gate_bias", parts_a, pad16(gate_bias), pad16(m_gate_bias), pad16(v_gate_bias),
                          o_pool + cg // 2)
    g_gb, d_gb, m_gb, v_gb = [t[:2] for t in gb_out]

    def pack(n_gain, f_gain, scale, rb, last):
        rows = [n_gain.reshape(-1, LANES), f_gain.reshape(-1, LANES), scale.reshape(-1, LANES),
                rb.reshape(-1, LANES), last]
        return jnp.concatenate(rows, axis=0)

    pad_rb = lambda t: jnp.pad(t, ((0, 0), (0, N_REL_PAD - N_REL)))
    zeros8 = jnp.zeros((8, LANES), F32)
    loss_rows = jnp.pad(loss_part, ((0, 7), (0, 0)))
    small = _small_allreduce_adamw(
        pack(dg1, dg2, dps, drb, loss_rows),
        pack(norm_gain, final_gain, pool_scale, pad_rb(rel_bias), zeros8),
        pack(m_norm_gain, m_final_gain, m_pool_scale, pad_rb(m_rel_bias), zeros8),
        pack(v_norm_gain, v_final_gain, v_pool_scale, pad_rb(v_rel_bias), zeros8))

    n1, n2, n3 = d // LANES, 2 * d // LANES, (2 * d + p) // LANES
    n4 = n3 + heads * N_REL_PAD // LANES

    def unpack(t):
        return (t[:n1].reshape(d), t[n1:n2].reshape(d), t[n2:n3].reshape(p),
                t[n3:n4].reshape(heads, N_REL_PAD)[:, :N_REL])

    (g_ng, g_fg, g_ps, g_rb), (d_ng, d_fg, d_ps, d_rb), (m_ng, m_fg, m_ps, m_rb), (v_ng, v_fg, v_ps, v_rb) = [
        unpack(t) for t in small]
    loss = small[0][n4, 0]

    return (loss, dx.reshape(x.shape),
            g_ng, g_win, g_rb, g_pw, g_ps, g_woa, g_wop, g_gb, g_wo, g_fg,
            d_ng, d_win, d_rb, d_pw, d_ps, d_woa, d_wop, d_gb, d_wo, d_fg,
            m_ng, m_win, m_rb, m_pw, m_ps, m_woa, m_wop, m_gb, m_wo, m_fg,
            v_ng, v_win, v_rb, v_pw, v_ps, v_woa, v_wop, v_gb, v_wo, v_fg)
```

```python
import jax
import jax.numpy as jnp
from jax import lax
from jax.experimental import pallas as pl
from jax.experimental.pallas import tpu as pltpu

F32 = jnp.float32
BF = jnp.bfloat16
MESH = pl.DeviceIdType.MESH

N_DEV = 8
CHUNK = 64
N_LEFT_CHUNKS = 8
HEAD_DIM = 128
MAX_REL = 128
N_REL = 2 * MAX_REL + 1
N_REL_PAD = 384
POOL_WINDOWS = (2, 4, 8, 16)
HALO = 16
EPS = 1e-6
ADAM_LR = 0.001
ADAM_B1 = 0.9
ADAM_B2 = 0.999
ADAM_EPS = 1e-08
ADAM_WD = 0.01
ADAM_STEP = 10
NEG = -1e30
LANES = 128
TQ = N_LEFT_CHUNKS * CHUNK
TK = 2 * TQ
SKEW = 2 * TK
VMEM_LIMIT = 52 * 1024 * 1024

NN = (((1,), (0,)), ((), ()))
NT = (((1,), (1,)), ((), ()))
TN = (((0,), (0,)), ((), ()))


def _params(n_grid):
    return pltpu.CompilerParams(dimension_semantics=("arbitrary",) * n_grid, vmem_limit_bytes=VMEM_LIMIT)


def _sig(z):
    return 1.0 / (1.0 + jnp.exp(-z))


def _silu_and_grad(z):
    s = _sig(z)
    return z * s, s * (1.0 + z * (1.0 - s))


def _tile(n, pref):
    t = min(n, pref)
    assert n % t == 0, (n, pref)
    return t


def _sds(shape, dtype):
    return jax.ShapeDtypeStruct(shape, dtype)


class _Comm:
    def __init__(self, ins, outs, scratch, start, wait, hooks=()):
        self.ins, self.outs, self.scratch = list(ins), list(outs), list(scratch)
        self.start, self.wait, self.hooks = start, wait, tuple(hooks)


def _call(name, body, grid, ins, outs, scratch=(), aliases=None, comm=None):
    n_in, n_out, n_scr = len(ins), len(outs), len(scratch)
    c_in = len(comm.ins) if comm else 0
    c_out = len(comm.outs) if comm else 0
    n_steps = 1
    for g in grid:
        n_steps *= g

    def kern(*refs):
        o0 = n_in + c_in
        s0 = o0 + n_out + c_out
        if comm:
            c_refs = (refs[n_in:o0], refs[o0 + n_out:s0], refs[s0 + n_scr:])
            step = pl.program_id(0)
            for ax in range(1, len(grid)):
                step = step * grid[ax] + pl.program_id(ax)

            @pl.when(step == 0)
            def _():
                comm.start(*c_refs)

            for frac, hook in comm.hooks:
                @pl.when(step == int(frac * n_steps))
                def _(hook=hook):
                    hook(*c_refs)

        body(refs[:n_in], refs[o0:o0 + n_out], refs[s0:s0 + n_scr])

        if comm:
            @pl.when(step == n_steps - 1)
            def _():
                comm.wait(*c_refs)

    hbm = pl.BlockSpec(memory_space=pl.ANY)
    return pl.pallas_call(
        kern, grid=grid,
        in_specs=[s for _, s in ins] + [hbm] * c_in, out_specs=[s for _, s in outs] + [hbm] * c_out,
        out_shape=[o for o, _ in outs] + (comm.outs if comm else []),
        scratch_shapes=list(scratch) + (comm.scratch if comm else []),
        name=name, compiler_params=_params(len(grid)), input_output_aliases=aliases or {},
    )(*([a for a, _ in ins] + (comm.ins if comm else [])))


def _mm(name, grid, ins, outs, dims, epi, aliases=None, comm=None, nb=1):
    def body(in_refs, out_refs, _):
        if nb == 1:
            acc = lax.dot_general(in_refs[0][...], in_refs[1][...], dims, preferred_element_type=F32)
        else:
            kq = in_refs[0].shape[1] // nb
            acc = sum(lax.dot_general(in_refs[0][:, q * kq:(q + 1) * kq], in_refs[1 + q][...], dims,
                                      preferred_element_type=F32) for q in range(nb))
        epi(acc, in_refs[1 + nb:], out_refs)

    return _call(name, body, grid, ins, outs, aliases=aliases, comm=comm)


def _rms_fwd(x, g, comm=None):
    s, d = x.shape
    tr = _tile(s, 256)

    def body(in_refs, out_refs, _):
        xv = in_refs[0][...]
        r = lax.rsqrt(jnp.mean(xv * xv, axis=-1, keepdims=True) + EPS)
        out_refs[0][...] = (xv * r * in_refs[1][...]).astype(BF)

    row = pl.BlockSpec((tr, d), lambda i: (i, 0))
    return _call("rms_fwd", body, (s // tr,), [(x, row), (g, pl.BlockSpec((1, d), lambda i: (0, 0)))],
                 [(_sds((s, d), BF), row)], comm=comm)


def _final_norm(x2, target, g):
    s, d = x2.shape
    tr = _tile(s, 128)

    def kern(x_ref, t_ref, g_ref, dx_ref, dxb_ref, dg_ref, loss_ref):
        i = pl.program_id(0)
        xv = x_ref[...]
        gv = g_ref[...]
        r = lax.rsqrt(jnp.mean(xv * xv, axis=-1, keepdims=True) + EPS)
        xhat = xv * r
        err = xhat * gv - t_ref[...]
        dy = err * (1.0 / d)
        gy = dy * gv
        dx = r * (gy - xhat * jnp.mean(gy * xhat, axis=-1, keepdims=True))
        dx_ref[...] = dx
        dxb_ref[...] = dx.astype(BF)
        dg = jnp.sum(dy * xhat, axis=0, keepdims=True)
        ls = jnp.broadcast_to(0.5 * jnp.sum(jnp.mean(err * err, axis=-1, keepdims=True)), (1, LANES))

        @pl.when(i == 0)
        def _():
            dg_ref[...] = dg
            loss_ref[...] = ls

        @pl.when(i > 0)
        def _():
            dg_ref[...] += dg
            loss_ref[...] += ls

    row = pl.BlockSpec((tr, d), lambda i: (i, 0))
    vec = pl.BlockSpec((1, d), lambda i: (0, 0))
    return pl.pallas_call(
        kern, grid=(s // tr,), in_specs=[row, row, vec],
        out_specs=[row, row, vec, pl.BlockSpec((1, LANES), lambda i: (0, 0))],
        out_shape=[_sds((s, d), F32), _sds((s, d), BF), _sds((1, d), F32), _sds((1, LANES), F32)],
        name="final_norm", compiler_params=_params(1))(x2, target, g)


def _rms_bwd(x, dh, dx2, g):
    s, d = x.shape
    tr = _tile(s, 128)

    def kern(x_ref, dh_ref, dx2_ref, g_ref, dx_ref, dg_ref):
        i = pl.program_id(0)
        xv = x_ref[...]
        r = lax.rsqrt(jnp.mean(xv * xv, axis=-1, keepdims=True) + EPS)
        xhat = xv * r
        dhv = dh_ref[...]
        gh = dhv * g_ref[...]
        dx_ref[...] = dx2_ref[...] + r * (gh - xhat * jnp.mean(gh * xhat, axis=-1, keepdims=True))
        dg = jnp.sum(dhv * xhat, axis=0, keepdims=True)

        @pl.when(i == 0)
        def _():
            dg_ref[...] = dg

        @pl.when(i > 0)
        def _():
            dg_ref[...] += dg

    row = pl.BlockSpec((tr, d), lambda i: (i, 0))
    vec = pl.BlockSpec((1, d), lambda i: (0, 0))
    return pl.pallas_call(
        kern, grid=(s // tr,), in_specs=[row, row, row, vec], out_specs=[row, vec],
        out_shape=[_sds((s, d), F32), _sds((1, d), F32)],
        name="rms_bwd", compiler_params=_params(1))(x, dh, dx2, g)


def _rel_index(j, backward):
    if backward:
        rel = 2 * TQ - 1 - j
    else:
        rel = TQ - jnp.where(j < TK, j, j - SKEW)
    return jnp.clip(rel, -MAX_REL, MAX_REL) + MAX_REL


def _bias_rows(rel_bias_pad):
    h = rel_bias_pad.shape[0]

    def kern(rb_ref, o_ref):
        j = lax.broadcasted_iota(jnp.int32, (N_REL_PAD, SKEW), 1)
        k = lax.broadcasted_iota(jnp.int32, (N_REL_PAD, SKEW), 0)
        onehot = (_rel_index(j, False) == k).astype(F32)
        o_ref[...] = jnp.dot(rb_ref[...], onehot, preferred_element_type=F32, precision=lax.Precision.HIGHEST)

    return pl.pallas_call(kern, out_shape=_sds((h, SKEW), F32), name="bias_rows")(rel_bias_pad)


def _bias_grad(ddiag):
    h = ddiag.shape[0]

    def kern(d_ref, o_ref):
        j = lax.broadcasted_iota(jnp.int32, (N_REL_PAD, SKEW), 1)
        k = lax.broadcasted_iota(jnp.int32, (N_REL_PAD, SKEW), 0)
        onehot = ((_rel_index(j, True) == k) & (j < TQ + TK - 1)).astype(F32)
        o_ref[...] = lax.dot_general(d_ref[...], onehot, NT, preferred_element_type=F32,
                                     precision=lax.Precision.HIGHEST)

    return pl.pallas_call(kern, out_shape=_sds((h, N_REL_PAD), F32), name="bias_grad")(ddiag)


def _bias_tile(row, first):
    t = pltpu.roll(jnp.broadcast_to(row, (TQ, SKEW)), 0, 1, stride=1, stride_axis=0)[:, :TK]
    r = lax.broadcasted_iota(jnp.int32, (TQ, TK), 0) // CHUNK
    col = lax.broadcasted_iota(jnp.int32, (TQ, TK), 1)
    dist = N_LEFT_CHUNKS + r - col // CHUNK
    keep = (dist >= 0) & (dist <= N_LEFT_CHUNKS) & jnp.logical_not(first & (col < TQ))
    return jnp.where(keep, t, NEG)


def _scores(q, keys, tile):
    s = lax.dot_general(q, keys, NT, preferred_element_type=F32) * (HEAD_DIM ** -0.5) + tile
    m = jnp.max(s, axis=1, keepdims=True)
    p = jnp.exp(s - m)
    return p, jnp.sum(p, axis=1, keepdims=True)


def _attn_fwd(proj, base, a_width, comm=None):
    s_len = proj.shape[0]
    heads = a_width // HEAD_DIM
    nq = s_len // TQ
    kb, vb, zb = heads, 2 * heads, 3 * heads

    def kern(q_ref, kp_ref, kc_ref, vp_ref, vc_ref, z_ref, base_ref, att_ref, ya_ref, tile_ref):
        i = pl.program_id(1)

        @pl.when(i <= 1)
        def _():
            tile_ref[...] = _bias_tile(base_ref[...], i == 0)

        kcat = jnp.concatenate([kp_ref[...], kc_ref[...]], axis=0)
        vcat = jnp.concatenate([vp_ref[...], vc_ref[...]], axis=0)
        p, l = _scores(q_ref[...], kcat, tile_ref[...])
        o = jnp.dot(p.astype(BF), vcat, preferred_element_type=F32) / l
        att_ref[...] = o.astype(BF)
        z = z_ref[...].astype(F32)
        ya_ref[...] = (o * (z * _sig(z))).astype(BF)

    blk = lambda off: pl.BlockSpec((TQ, HEAD_DIM), lambda h, i: (i, off + h))
    prev = lambda off: pl.BlockSpec((TQ, HEAD_DIM), lambda h, i: (jnp.maximum(i - 1, 0), off + h))
    out = pl.BlockSpec((TQ, HEAD_DIM), lambda h, i: (i, h))
    def body(in_refs, out_refs, scratch_refs):
        kern(*in_refs, *out_refs, *scratch_refs)

    act = _sds((s_len, a_width), BF)
    res = _call(
        "attn_fwd", body, (heads, nq),
        [(proj, blk(0)), (proj, prev(kb)), (proj, blk(kb)), (proj, prev(vb)), (proj, blk(vb)), (proj, blk(zb)),
         (base, pl.BlockSpec((None, 1, SKEW), lambda h, i: (h, 0, 0)))],
        [(act, out), (act, out)], scratch=[pltpu.VMEM((TQ, TK), F32)], comm=comm)
    return res[:2], res[2:]


def _attn_bwd(proj, datt, base, a_width, dproj, comm=None):
    s_len = proj.shape[0]
    heads = a_width // HEAD_DIM
    nq = s_len // TQ
    kb, vb = heads, 2 * heads
    scale = HEAD_DIM ** -0.5

    def body(in_refs, out_refs, scratch_refs):
        q_ref, kp_ref, kc_ref, vp_ref, vc_ref, do_ref, base_ref, _ = in_refs
        dq_ref, dk_ref, dv_ref, dd_ref = out_refs
        tile_ref, dsacc_ref, ak_ref, av_ref = scratch_refs
        i = pl.program_id(1)

        @pl.when(i <= 1)
        def _():
            tile_ref[...] = _bias_tile(base_ref[...], i == 0)

        @pl.when(i == 0)
        def _():
            dsacc_ref[...] = jnp.zeros_like(dsacc_ref)
            ak_ref[...] = jnp.zeros_like(ak_ref)
            av_ref[...] = jnp.zeros_like(av_ref)

        @pl.when(i < nq)
        def _():
            q = q_ref[...]
            do = do_ref[...]
            kcat = jnp.concatenate([kp_ref[...], kc_ref[...]], axis=0)
            vcat = jnp.concatenate([vp_ref[...], vc_ref[...]], axis=0)
            p, l = _scores(q, kcat, tile_ref[...])
            p = p / l
            dp = lax.dot_general(do, vcat, NT, preferred_element_type=F32)
            ds = p * (dp - jnp.sum(p * dp, axis=1, keepdims=True))
            dsacc_ref[...] += ds
            dsb = ds.astype(BF)
            dq_ref[...] = (jnp.dot(dsb, kcat, preferred_element_type=F32) * scale).astype(BF)
            dkc = lax.dot_general(dsb, q, TN, preferred_element_type=F32) * scale
            dvc = lax.dot_general(p.astype(BF), do, TN, preferred_element_type=F32)
            dk_ref[...] = (ak_ref[...] + dkc[:TQ]).astype(BF)
            dv_ref[...] = (av_ref[...] + dvc[:TQ]).astype(BF)
            ak_ref[...] = dkc[TQ:]
            av_ref[...] = dvc[TQ:]

        @pl.when(i == nq)
        def _():
            dk_ref[...] = ak_ref[...].astype(BF)
            dv_ref[...] = av_ref[...].astype(BF)
            acc = dsacc_ref[...]
            rr = lax.broadcasted_iota(jnp.int32, (TQ, TQ), 0)
            cc = lax.broadcasted_iota(jnp.int32, (TQ, TQ), 1)
            flip = (rr + cc == TQ - 1).astype(BF)
            hi = acc.astype(BF)
            lo = (acc - hi.astype(F32)).astype(BF)
            rev = jnp.dot(flip, hi, preferred_element_type=F32) + jnp.dot(flip, lo, preferred_element_type=F32)
            wide = jnp.concatenate([rev, jnp.zeros((TQ, SKEW - TK), F32)], axis=1)
            dd_ref[...] = jnp.sum(pltpu.roll(wide, 0, 1, stride=1, stride_axis=0), axis=0, keepdims=True)

    last = nq - 1
    cur = lambda off: pl.BlockSpec((TQ, HEAD_DIM), lambda h, i: (jnp.minimum(i, last), off + h))
    prev = lambda off: pl.BlockSpec((TQ, HEAD_DIM), lambda h, i: (jnp.maximum(jnp.minimum(i, last) - 1, 0), off + h))
    done = pl.BlockSpec((TQ, HEAD_DIM), lambda h, i: (jnp.maximum(i - 1, 0), h))
    row = pl.BlockSpec((None, 1, SKEW), lambda h, i: (h, 0, 0))
    act = _sds((s_len, a_width), BF)
    res = _call(
        "attn_bwd", body, (heads, nq + 1),
        [(proj, cur(0)), (proj, prev(kb)), (proj, cur(kb)), (proj, prev(vb)), (proj, cur(vb)), (datt, cur(0)),
         (base, row), (dproj, pl.BlockSpec(memory_space=pl.ANY))],
        [(_sds(dproj.shape, dproj.dtype), cur(0)), (act, done), (act, done), (_sds((heads, 1, SKEW), F32), row)],
        aliases={7: 0},
        scratch=[pltpu.VMEM((TQ, TK), F32), pltpu.VMEM((TQ, TK), F32),
                 pltpu.VMEM((TQ, HEAD_DIM), F32), pltpu.VMEM((TQ, HEAD_DIM), F32)],
        comm=comm)
    return res[:4], res[4:]


def _pool_fwd(proj, pool_w, pool_scale, p_width, u_blk, z_blk):
    s_len = proj.shape[0]
    cg = p_width // len(POOL_WINDOWS)
    tt = _tile(s_len, 512)

    def kern(up_ref, uc_ref, z_ref, pw_ref, sc_ref, d_ref, y_ref, yp_ref):
        t = pl.program_id(0)
        row = lax.broadcasted_iota(jnp.int32, (tt, 1), 0) + t * tt
        for g, w in enumerate(POOL_WINDOWS):
            cs = slice(g * cg, (g + 1) * cg)
            prev = jnp.where(t == 0, 0.0, up_ref[:, cs].astype(F32))
            cur = uc_ref[:, cs].astype(F32)
            ws = jnp.concatenate([prev, cur], axis=0)
            sh = 1
            while sh < w:
                ws = ws + pltpu.roll(ws, sh, 0)
                sh *= 2
            cnt = jnp.minimum(row + 1, w).astype(F32)
            db = (ws[HALO:, :] / cnt - cur).astype(BF)
            y = jnp.dot(db, pw_ref[g], preferred_element_type=F32)
            d_ref[:, cs] = db
            y_ref[:, cs] = y.astype(BF)
            z = z_ref[:, cs].astype(F32)
            yp_ref[:, cs] = (y * sc_ref[:, cs] * (z * _sig(z))).astype(BF)

    full = pl.BlockSpec((tt, p_width), lambda t: (t, 0))
    return pl.pallas_call(
        kern, grid=(s_len // tt,),
        in_specs=[pl.BlockSpec((HALO, p_width), lambda t: (jnp.maximum(t * (tt // HALO) - 1, 0), u_blk)),
                  pl.BlockSpec((tt, p_width), lambda t: (t, u_blk)),
                  pl.BlockSpec((tt, p_width), lambda t: (t, z_blk)),
                  pl.BlockSpec((len(POOL_WINDOWS), cg, cg), lambda t: (0, 0, 0)),
                  pl.BlockSpec((1, p_width), lambda t: (0, 0))],
        out_specs=[full, full, full], out_shape=[_sds((s_len, p_width), BF)] * 3,
        name="pool_fwd", compiler_params=_params(1))(proj, proj, proj, pool_w, pool_scale)


def _pool_bwd(dy, dmean, pool_w, dproj, u_blk):
    s_len, p_width = dy.shape
    ng = len(POOL_WINDOWS)
    cg = p_width // ng
    tt = _tile(s_len, 512)
    nt = s_len // tt

    def kern(dyc_ref, dyn_ref, d_ref, pw_ref, _, du_ref, dpw_ref):
        t = pl.program_id(0)

        @pl.when(t == 0)
        def _():
            dpw_ref[...] = jnp.zeros_like(dpw_ref)

        row = lax.broadcasted_iota(jnp.int32, (tt + HALO, 1), 0) + t * tt
        for g, w in enumerate(POOL_WINDOWS):
            cs = slice(g * cg, (g + 1) * cg)
            dyc = dyc_ref[:, cs]
            ddc = lax.dot_general(dyc, pw_ref[g], NT, preferred_element_type=F32)
            ddn = lax.dot_general(dyn_ref[:, cs], pw_ref[g], NT, preferred_element_type=F32)
            ddn = jnp.where(t == nt - 1, 0.0, ddn)
            cnt = jnp.minimum(row + 1, w).astype(F32)
            ws = jnp.concatenate([ddc, ddn], axis=0) / cnt
            sh = 1
            while sh < w:
                ws = ws + pltpu.roll(ws, tt + HALO - sh, 0)
                sh *= 2
            du_ref[:, cs] = (ws[:tt, :] - ddc).astype(BF)
            dpw_ref[g] += lax.dot_general(d_ref[:, cs], dyc, TN, preferred_element_type=F32)

    full = pl.BlockSpec((tt, p_width), lambda t: (t, 0))
    pw_spec = pl.BlockSpec((ng, cg, cg), lambda t: (0, 0, 0))
    return pl.pallas_call(
        kern, grid=(nt,),
        in_specs=[full,
                  pl.BlockSpec((HALO, p_width), lambda t: (jnp.minimum((t + 1) * (tt // HALO), s_len // HALO - 1), 0)),
                  full, pw_spec, pl.BlockSpec(memory_space=pl.ANY)],
        out_specs=[pl.BlockSpec((tt, p_width), lambda t: (t, u_blk)), pw_spec],
        out_shape=[_sds(dproj.shape, dproj.dtype), _sds((ng, cg, cg), F32)],
        input_output_aliases={4: 0},
        name="pool_bwd", compiler_params=_params(1))(dy, dy, dmean, pool_w, dproj)


def _adam(g, w_ref, m_ref, v_ref, g_out, d_out, m_out, v_out):
    m = ADAM_B1 * m_ref[...] + (1.0 - ADAM_B1) * g
    v = ADAM_B2 * v_ref[...] + (1.0 - ADAM_B2) * (g * g)
    m_hat = m / (1.0 - ADAM_B1 ** ADAM_STEP)
    v_hat = v / (1.0 - ADAM_B2 ** ADAM_STEP)
    g_out[...] = g
    d_out[...] = -ADAM_LR * (m_hat / (jnp.sqrt(v_hat) + ADAM_EPS) + ADAM_WD * w_ref[...])
    m_out[...] = m
    v_out[...] = v


def _adamw_shard(name, parts, w, m, v, row_off):
    rw, cw = w.shape
    sw = parts.shape[2]
    tr = _tile(rw, 512)
    assert row_off % tr == 0 and cw % sw == 0

    def kern(b_ref, w_ref, m_ref, v_ref, g_out, d_out, m_out, v_out):
        b = b_ref[...].astype(F32)
        _adam(((b[0] + b[1]) + b[2]) + b[3], w_ref, m_ref, v_ref, g_out, d_out, m_out, v_out)

    blk = pl.BlockSpec((tr, sw), lambda ct, i: (i, ct))
    return pl.pallas_call(
        kern, grid=(cw // sw, rw // tr),
        in_specs=[pl.BlockSpec((4, tr, sw), lambda ct, i: (0, (row_off + ct * rw) // tr + i, 0)), blk, blk, blk],
        out_specs=[blk] * 4, out_shape=[_sds((rw, cw), F32)] * 4,
        name=name, compiler_params=_params(2))(parts, w, m, v)


def _adamw_row_halves(name, parts_lo, parts_hi, w, m, v):
    rw, cw = w.shape
    sw = parts_lo.shape[2]
    half = rw // 2
    tr = _tile(half, 512)
    nh = half // tr

    def kern(lo_ref, hi_ref, w_ref, m_ref, v_ref, g_out, d_out, m_out, v_out):
        i = pl.program_id(1)

        def update(b_ref):
            b = b_ref[...].astype(F32)
            _adam(((b[0] + b[1]) + b[2]) + b[3], w_ref, m_ref, v_ref, g_out, d_out, m_out, v_out)

        @pl.when(i < nh)
        def _():
            update(lo_ref)

        @pl.when(i >= nh)
        def _():
            update(hi_ref)

    blk = pl.BlockSpec((tr, sw), lambda ct, i: (i, ct))
    return pl.pallas_call(
        kern, grid=(cw // sw, rw // tr),
        in_specs=[pl.BlockSpec((4, tr, sw), lambda ct, i: (0, ct * nh + jnp.minimum(i, nh - 1), 0)),
                  pl.BlockSpec((4, tr, sw), lambda ct, i: (0, ct * nh + jnp.maximum(i - nh, 0), 0)), blk, blk, blk],
        out_specs=[blk] * 4, out_shape=[_sds((rw, cw), F32)] * 4,
        name=name, compiler_params=_params(2))(parts_lo, parts_hi, w, m, v)


def _both(c1, c2):
    n_in, n_out, n_sem = len(c1.ins), len(c1.outs), len(c1.scratch)

    def split(ins, outs, sems):
        return (ins[:n_in], outs[:n_out], sems[:n_sem]), (ins[n_in:], outs[n_out:], sems[n_sem:])

    def start(*refs):
        r1, r2 = split(*refs)
        c1.start(*r1)
        c2.start(*r2)

    def wait(*refs):
        r1, r2 = split(*refs)
        c1.wait(*r1)
        c2.wait(*r2)

    def of(which, hook):
        return lambda *refs: hook(*split(*refs)[which])

    hooks = [(f, of(0, h)) for f, h in c1.hooks] + [(f, of(1, h)) for f, h in c2.hooks]
    return _Comm(c1.ins + c2.ins, c1.outs + c2.outs, c1.scratch + c2.scratch, start, wait, hooks)


def _position():
    return lax.axis_index("x"), lax.axis_index("y"), lax.axis_index("c")


def _gather_comm(shards, cols=None):
    if cols is None:
        pieces = [(a, None) for a in range(len(shards))]
        shapes = [_sds(s.shape, s.dtype) for s in shards]
    else:
        half = shards[0].shape[0] // 2
        pieces = [(0, pl.ds(0, half)), (0, pl.ds(half, half))]
        shapes = [_sds((shards[0].shape[0], cols[1]), shards[0].dtype)]
    n = len(pieces)

    def plan(xs, outs, sems, only=None):
        send_sems, recv_sems, local_sems = sems
        x, y, c = _position()
        me, sibling = (x, y, c), (x, y, 1 - c)
        chips = [(1 - x, y), (x, 1 - y), (1 - x, 1 - y)]
        which = range(n) if only is None else only

        def source(v):
            a, rows = pieces[v]
            return xs[a] if rows is None else xs[a].at[rows, pl.ds(cols[0], cols[1])]

        def landing(v, block):
            a, rows = pieces[v]
            dst = outs[a].at[4 * block[0] + 2 * block[1] + block[2]]
            return dst if rows is None else dst.at[rows, :]

        def copy(v, k, block, to, own=False):
            dst = landing(v, block)
            return pltpu.make_async_remote_copy(
                src_ref=source(v) if own else dst, dst_ref=dst,
                send_sem=send_sems.at[7 * v + k], recv_sem=recv_sems.at[7 * v + k],
                device_id=to, device_id_type=MESH)

        by_chip = [(j, chip, v) for v in which for j, chip in enumerate(chips)]
        return dict(
            mine=lambda: [pltpu.make_async_copy(source(v), landing(v, me), local_sems.at[v]) for v in which],
            first=lambda: ([copy(v, 0, me, sibling, own=True) for v in which]
                           + [copy(v, 1 + j, me, (*chip, c), own=True) for j, chip, v in by_chip]),
            landed=lambda: [copy(v, 1 + j, (*chip, c), me) for j, chip, v in by_chip],
            passed=lambda: [copy(v, 4 + j, (*chip, c), sibling) for j, chip, v in by_chip],
            rest=lambda: ([copy(v, 0, sibling, me) for v in which]
                          + [copy(v, 4 + j, (*chip, 1 - c), me) for j, chip, v in by_chip]))

    def start(*refs):
        p = plan(*refs)
        for cp in p["mine"]() + p["first"]():
            cp.start()

    def pass_on(only):
        def hook(*refs):
            p = plan(*refs, only=only)
            for arrived, onward in zip(p["landed"](), p["passed"]()):
                arrived.wait_recv()
                onward.start()
        return hook

    def wait(*refs):
        if cols is not None:
            pass_on([1])(*refs)
        p = plan(*refs)
        for cp in p["rest"]():
            cp.wait_recv()
        for cp in p["first"]() + p["passed"]():
            cp.wait_send()
        for cp in p["mine"]():
            cp.wait()

    hooks = [(0.75, pass_on(None))] if cols is None else [(0.5, pass_on([0]))]
    return _Comm(shards, [_sds((N_DEV,) + s.shape, s.dtype) for s in shapes],
                 [pltpu.SemaphoreType.DMA((7 * n,)), pltpu.SemaphoreType.DMA((7 * n,)),
                  pltpu.SemaphoreType.DMA((n,))], start, wait, hooks)


def _cores_comm(slab):
    _, _, r, sw = slab.shape

    def copies(ins, outs, sems):
        x, y, c = _position()
        return [pltpu.make_async_remote_copy(
            src_ref=ins[0].at[1 - c], dst_ref=outs[0], send_sem=sems[0], recv_sem=sems[1],
            device_id=(x, y, 1 - c), device_id_type=MESH)]

    def start(*refs):
        for cp in copies(*refs):
            cp.start()

    def wait(*refs):
        for cp in copies(*refs):
            cp.wait()

    return _Comm([slab], [_sds((4, r, sw), slab.dtype)],
                 [pltpu.SemaphoreType.DMA, pltpu.SemaphoreType.DMA], start, wait)


def _add_core_partials(name, slab, recv, core, tr):
    _, _, r, sw = slab.shape

    def kern(c_ref, a_ref, b_ref, o_ref):
        o_ref[...] = (a_ref[...].astype(F32) + b_ref[...].astype(F32)).astype(BF)

    return pl.pallas_call(
        kern,
        grid_spec=pltpu.PrefetchScalarGridSpec(
            num_scalar_prefetch=1, grid=(4, r // tr),
            in_specs=[pl.BlockSpec((None, None, tr, sw), lambda k, i, c_ref: (c_ref[0], k, i, 0)),
                      pl.BlockSpec((None, tr, sw), lambda k, i, c_ref: (k, i, 0))],
            out_specs=pl.BlockSpec((None, tr, sw), lambda k, i, c_ref: (k, i, 0))),
        out_shape=_sds((4, r, sw), BF), name=name, compiler_params=_params(2))(core, slab, recv)


def _chips_comm(part):
    _, r, sw = part.shape

    def copies(ins, outs, sems):
        send_sems, recv_sems, local_sem = sems
        x, y, c = _position()
        mine = 2 * x + y
        local = pltpu.make_async_copy(ins[0].at[mine], outs[0].at[mine], local_sem)
        chips = [(1 - x, y), (x, 1 - y), (1 - x, 1 - y)]
        remote = [pltpu.make_async_remote_copy(
            src_ref=ins[0].at[2 * px + py], dst_ref=outs[0].at[mine],
            send_sem=send_sems.at[j], recv_sem=recv_sems.at[j],
            device_id=(px, py, c), device_id_type=MESH) for j, (px, py) in enumerate(chips)]
        return [local] + remote

    def start(*refs):
        for cp in copies(*refs):
            cp.start()

    def wait(*refs):
        for cp in copies(*refs):
            cp.wait()

    return _Comm([part], [_sds((4, r, sw), part.dtype)],
                 [pltpu.SemaphoreType.DMA((3,)), pltpu.SemaphoreType.DMA((3,)), pltpu.SemaphoreType.DMA],
                 start, wait)


def _small_allreduce_adamw(partial, w, m, v):
    nr = partial.shape[0]

    def kern(p_ref, w_ref, m_ref, v_ref, g_out, d_out, m_out, v_out, gath_ref, send_sems, recv_sems):
        x, y, c = _position()
        me = 4 * x + 2 * y + c
        gath_ref[me] = p_ref[...]
        copies = []
        for mask in range(1, N_DEV):
            peer = (x ^ (mask >> 2), y ^ ((mask >> 1) & 1), c ^ (mask & 1))
            copies.append(pltpu.make_async_remote_copy(
                src_ref=p_ref, dst_ref=gath_ref.at[me],
                send_sem=send_sems.at[mask - 1], recv_sem=recv_sems.at[mask - 1],
                device_id=peer, device_id_type=MESH))
        for cp in copies:
            cp.start()
        for cp in copies:
            cp.wait()
        tot = gath_ref[0]
        for k in range(1, N_DEV):
            tot = tot + gath_ref[k]
        _adam(tot, w_ref, m_ref, v_ref, g_out, d_out, m_out, v_out)

    vmem = pl.BlockSpec(memory_space=pltpu.VMEM)
    return pl.pallas_call(
        kern, in_specs=[vmem] * 4, out_specs=[vmem] * 4, out_shape=[_sds((nr, LANES), F32)] * 4,
        scratch_shapes=[pltpu.VMEM((N_DEV, nr, LANES), F32),
                        pltpu.SemaphoreType.DMA((N_DEV - 1,)), pltpu.SemaphoreType.DMA((N_DEV - 1,))],
        name="small_allreduce_adamw")(partial, w, m, v)


def kernel(x, norm_gain, w_in, rel_bias, pool_w, pool_scale, w_out_attn, w_out_pool, gate_bias, w_out, final_gain, loss_target, m_norm_gain, m_w_in, m_rel_bias, m_pool_w, m_pool_scale, m_w_out_attn, m_w_out_pool, m_gate_bias, m_w_out, m_final_gain, v_norm_gain, v_w_in, v_rel_bias, v_pool_w, v_pool_scale, v_w_out_attn, v_w_out_pool, v_gate_bias, v_w_out, v_final_gain):
    _, s_len, d = x.shape
    a = w_out_attn.shape[0]
    p = w_out_pool.shape[0]
    heads = a // HEAD_DIM
    ng = len(POOL_WINDOWS)
    cg = p // ng
    sw = d // N_DEV
    n_in = w_in.shape[1] * N_DEV
    assert a == p and a + p == d and cg == sw and n_in == 5 * d and w_in.shape[1] == 5 * sw
    assert s_len % TQ == 0 and rel_bias.shape == (heads, N_REL)
    tm = _tile(s_len, 1024)
    x2d = x.reshape(s_len, d)
    tgt = loss_target.reshape(s_len, d)

    g1 = norm_gain.reshape(1, d)
    g2 = final_gain.reshape(1, d)
    scale_row = pool_scale.reshape(1, p)
    tn = sw
    per = w_in.shape[1] // tn
    hbm = pl.BlockSpec(memory_space=pl.ANY)

    w_bf = w_in.astype(BF)
    hb, landed = _rms_fwd(x2d, g1, comm=_gather_comm([w_bf], cols=(0, tn)))
    win_rounds = []

    def store_bf16(acc, _, outs):
        outs[0][...] = acc.astype(BF)

    gate0, n_gate = (4 * a + 2 * p) // tn, d // tn

    def pos(t):
        g = t - gate0
        return jnp.where(t < gate0, t, jnp.where(g < n_gate, gate0 + 2 * g, gate0 + 2 * (g - n_gate) + 1))

    proj = None
    for r in range(per):
        win_rounds.append(landed)
        if r + 1 < per:
            comm = _gather_comm([w_bf], cols=((r + 1) * tn, tn))
        else:
            comm = _gather_comm([w_out_attn.astype(BF), pool_w.astype(BF), gate_bias])
        ins = [(hb, pl.BlockSpec((tm, d), lambda i, j: (i, 0))),
               (landed, pl.BlockSpec((None, d, tn), lambda i, j: (j, 0, 0)))]
        if proj is not None:
            ins.append((proj, hbm))
        proj, landed, *rest = _mm(
            f"proj_{r}", (s_len // tm, N_DEV), ins,
            [(_sds((s_len, n_in), BF), pl.BlockSpec((tm, tn), lambda i, j, r=r: (i, pos(per * j + r))))],
            NN, store_bf16, aliases={2: 0} if r else None, comm=comm)
    woa_g = landed
    woa = woa_g.transpose(1, 0, 2).reshape(a, d)
    pw = rest[0].transpose(1, 0, 2, 3).reshape(ng, cg, cg)
    gb = rest[1].transpose(1, 0, 2).reshape(2, d)

    rb_pad = jnp.pad(rel_bias, ((0, 0), (0, N_REL_PAD - N_REL)))
    base = _bias_rows(rb_pad).reshape(heads, 1, SKEW)
    wo_bf = w_out.astype(BF)
    (att, ya), (wop_g, wo_lo_g) = _attn_fwd(
        proj, base, a, comm=_both(_gather_comm([w_out_pool.astype(BF)]), _gather_comm([wo_bf], cols=(0, d // 2))))
    wop = wop_g.transpose(1, 0, 2).reshape(p, d)
    u_blk, z_blk = 4 * a // p, 4 * a // p + 1
    dmean, ypre, yp = _pool_fwd(proj, pw, scale_row, p, u_blk, z_blk)

    def gate_kernel(in_refs, out_refs, _):
        ya_ref, woa_ref, yp_ref, wop_ref, ga_ref, gp_ref, gb_ref = in_refs
        m_ref, a_ref, p_ref = out_refs
        am = jnp.dot(ya_ref[...], woa_ref[...], preferred_element_type=F32)
        pm = jnp.dot(yp_ref[...], wop_ref[...], preferred_element_type=F32)
        sa = _sig(ga_ref[...].astype(F32) + gb_ref[0:1, :])
        sp = _sig(gp_ref[...].astype(F32) + gb_ref[1:2, :])
        m_ref[...] = (sa * am + sp * pm).astype(BF)
        a_ref[...] = am.astype(BF)
        p_ref[...] = pm.astype(BF)

    tile_ij = pl.BlockSpec((tm, tn), lambda i, j: (i, j))
    act_d = _sds((s_len, d), BF)
    merged, am, pm, wo_hi_g = _call(
        "gate_merge", gate_kernel, (s_len // tm, d // tn),
        [(ya, pl.BlockSpec((tm, a), lambda i, j: (i, 0))), (woa_g, pl.BlockSpec((None, a, tn), lambda i, j: (j, 0, 0))),
         (yp, pl.BlockSpec((tm, p), lambda i, j: (i, 0))), (wop_g, pl.BlockSpec((None, p, tn), lambda i, j: (j, 0, 0))),
         (proj, pl.BlockSpec((tm, tn), lambda i, j: (i, gate0 + 2 * j))),
         (proj, pl.BlockSpec((tm, tn), lambda i, j: (i, gate0 + 2 * j + 1))),
         (gb, pl.BlockSpec((2, tn), lambda i, j: (0, j)))],
        [(act_d, tile_ij)] * 3, comm=_gather_comm([wo_bf], cols=(d // 2, d // 2)))
    wo_halves = [wo_lo_g.reshape(d, d // 2), wo_hi_g.reshape(d, d // 2)]

    def out_proj(in_refs, out_refs, _):
        m_ref, lo_ref, hi_ref, x_ref = in_refs
        j = pl.program_id(1)

        @pl.when(j < n_half)
        def _():
            out_refs[0][...] = x_ref[...] + jnp.dot(m_ref[...], lo_ref[...], preferred_element_type=F32)

        @pl.when(j >= n_half)
        def _():
            out_refs[0][...] = x_ref[...] + jnp.dot(m_ref[...], hi_ref[...], preferred_element_type=F32)

    n_half = d // 2 // tn
    x2 = _call("out_proj", out_proj, (s_len // tm, d // tn),
               [(merged, pl.BlockSpec((tm, d), lambda i, j: (i, 0))),
                (wo_halves[0], pl.BlockSpec((d, tn), lambda i, j: (0, jnp.minimum(j, n_half - 1)))),
                (wo_halves[1], pl.BlockSpec((d, tn), lambda i, j: (0, jnp.maximum(j - n_half, 0)))),
                (x2d, tile_ij)],
               [(_sds((s_len, d), F32), tile_ij)])[0]

    dx2, dx2b, dg2, loss_part = _final_norm(x2, tgt, g2)

    tmb = _tile(s_len, 512)
    tile_ji = pl.BlockSpec((tmb, tn), lambda j, i: (i, j))

    def gate_bwd(dm, ex, outs):
        a_ref, p_ref, ga_ref, gp_ref, gb_ref = ex
        da_ref, dp_ref, dgate_ref, dgb_ref = outs
        i = pl.program_id(1)
        sa = _sig(ga_ref[...].astype(F32) + gb_ref[0:1, :])
        sp = _sig(gp_ref[...].astype(F32) + gb_ref[1:2, :])
        dga = dm * a_ref[...].astype(F32) * sa * (1.0 - sa)
        dgp = dm * p_ref[...].astype(F32) * sp * (1.0 - sp)
        da_ref[...] = (dm * sa).astype(BF)
        dp_ref[...] = (dm * sp).astype(BF)
        dgate_ref[:, :tn] = dga.astype(BF)
        dgate_ref[:, tn:] = dgp.astype(BF)
        r = lax.broadcasted_iota(jnp.int32, (8, tn), 0)
        sums = jnp.where(r == 0, jnp.sum(dga, axis=0, keepdims=True),
                         jnp.where(r == 1, jnp.sum(dgp, axis=0, keepdims=True), 0.0))

        @pl.when(i == 0)
        def _():
            dgb_ref[...] = sums

        @pl.when(i > 0)
        def _():
            dgb_ref[...] += sums

    dproj_shape = _sds((s_len, n_in), BF)
    d_am, d_pm, dproj, dgb8 = _mm(
        "gate_bwd", (d // tn, s_len // tmb),
        [(dx2b, pl.BlockSpec((tmb, d), lambda j, i: (i, 0)))]
        + [(w, pl.BlockSpec((tn, d // 2), lambda j, i: (j, 0))) for w in wo_halves]
        + [(am, tile_ji), (pm, tile_ji),
           (proj, pl.BlockSpec((tmb, tn), lambda j, i: (i, gate0 + 2 * j))),
           (proj, pl.BlockSpec((tmb, tn), lambda j, i: (i, gate0 + 2 * j + 1))),
           (gb, pl.BlockSpec((2, tn), lambda j, i: (0, j)))],
        [(_sds((s_len, d), BF), tile_ji)] * 2
        + [(dproj_shape, pl.BlockSpec((tmb, 2 * tn), lambda j, i: (i, gate0 // 2 + j))),
           (_sds((8, d), F32), pl.BlockSpec((8, tn), lambda j, i: (0, j)))],
        NT, gate_bwd, nb=2)

    za_t = 3 * a // tn

    def attn_gate_bwd(dya, ex, outs):
        silu, dsilu = _silu_and_grad(ex[0][...].astype(F32))
        outs[0][...] = (dya * silu).astype(BF)
        outs[1][...] = (dya * ex[1][...].astype(F32) * dsilu).astype(BF)

    datt, dproj = _mm(
        "attn_gate_bwd", (s_len // tm, a // tn),
        [(d_am, pl.BlockSpec((tm, d), lambda i, j: (i, 0))), (woa, pl.BlockSpec((tn, d), lambda i, j: (j, 0))),
         (proj, pl.BlockSpec((tm, tn), lambda i, j: (i, za_t + j))), (att, tile_ij), (dproj, hbm)],
        [(_sds((s_len, a), BF), tile_ij), (dproj_shape, pl.BlockSpec((tm, tn), lambda i, j: (i, za_t + j)))],
        NT, attn_gate_bwd, aliases={4: 1})

    zp_t = (4 * a + p) // tn

    def pool_gate_bwd(dyp, ex, outs):
        z_ref, y_ref, sc_ref, _ = ex
        dzp_ref, dy_ref, dps_ref = outs
        i = pl.program_id(1)
        silu, dsilu = _silu_and_grad(z_ref[...].astype(F32))
        y = y_ref[...].astype(F32)
        sc = sc_ref[...]
        dyp0 = dyp * silu
        dzp_ref[...] = (dyp * (y * sc) * dsilu).astype(BF)
        dy_ref[...] = (dyp0 * sc).astype(BF)
        dps = jnp.sum(dyp0 * y, axis=0, keepdims=True)

        @pl.when(i == 0)
        def _():
            dps_ref[...] = dps

        @pl.when(i > 0)
        def _():
            dps_ref[...] += dps

    dproj, dy_pool, dps = _mm(
        "pool_gate_bwd", (p // tn, s_len // tmb),
        [(d_pm, pl.BlockSpec((tmb, d), lambda j, i: (i, 0))), (wop, pl.BlockSpec((tn, d), lambda j, i: (j, 0))),
         (proj, pl.BlockSpec((tmb, tn), lambda j, i: (i, zp_t + j))), (ypre, tile_ji),
         (scale_row, pl.BlockSpec((1, tn), lambda j, i: (0, j))), (dproj, hbm)],
        [(dproj_shape, pl.BlockSpec((tmb, tn), lambda j, i: (i, zp_t + j))), (_sds((s_len, p), BF), tile_ji),
         (_sds((1, p), F32), pl.BlockSpec((1, tn), lambda j, i: (0, j)))],
        NT, pool_gate_bwd, aliases={5: 0})

    dproj, dpw = _pool_bwd(dy_pool, dmean, pw, dproj, u_blk)

    o_wop, o_wo, o_pool = a, d, 2 * d
    slab_a = _sds((2, 4, 2 * d + cg, sw), BF)
    slab_b = _sds((2, 4, 5 * d // 2, sw), BF)
    hbm = pl.BlockSpec(memory_space=pl.ANY)
    tmw = _tile(a, 1024)
    core = lax.axis_index("c").astype(jnp.int32).reshape(1)

    def pack_small(dpw_ref, dgb_ref, o_ref):
        rows = cg // N_DEV
        for j in range(N_DEV):
            for g in range(ng):
                o_ref[j % 2, j // 2, g * rows:(g + 1) * rows, :] = dpw_ref[g, j * rows:(j + 1) * rows, :].astype(BF)
            o_ref[j % 2, j // 2, ng * rows:, :] = jnp.concatenate(
                [dgb_ref[:, j * sw:(j + 1) * sw], jnp.zeros((cg - ng * rows - 8, sw), F32)], axis=0).astype(BF)

    slab = pl.pallas_call(
        pack_small, grid=(1,),
        in_specs=[pl.BlockSpec((ng, cg, cg), lambda i: (0, 0, 0)), pl.BlockSpec((8, d), lambda i: (0, 0))],
        out_specs=pl.BlockSpec((2, 4, cg, sw), lambda i: (0, 0, o_pool // cg, 0)), out_shape=slab_a,
        name="dw_small", compiler_params=_params(1))(dpw, dgb8)

    def into_slab(acc, _, outs):
        outs[0][...] = acc.astype(BF)

    def weight_grad(name, slab, lhs, rhs, grid, lhs_spec, rhs_spec, out_spec, comm=None):
        return _mm(name, grid, [(lhs, lhs_spec), (rhs, rhs_spec), (slab, hbm)], [(slab_a, out_spec)],
                   TN, into_slab, aliases={2: 0}, comm=comm)

    slab = weight_grad("dw_out", slab, merged, dx2b, (N_DEV, d // sw),
                       pl.BlockSpec((s_len, sw), lambda j, t: (0, j)), pl.BlockSpec((s_len, sw), lambda j, t: (0, t)),
                       pl.BlockSpec((None, None, sw, sw), lambda j, t: (j % 2, j // 2, o_wo // sw + t, 0)))[0]
    slab = weight_grad("dw_out_attn", slab, ya, d_am, (a // tmw, N_DEV),
                       pl.BlockSpec((s_len, tmw), lambda i, j: (0, i)), pl.BlockSpec((s_len, sw), lambda i, j: (0, j)),
                       pl.BlockSpec((None, None, tmw, sw), lambda i, j: (j % 2, j // 2, i, 0)))[0]
    slab = weight_grad("dw_out_pool", slab, yp, d_pm, (p // tmw, N_DEV),
                       pl.BlockSpec((s_len, tmw), lambda i, j: (0, i)), pl.BlockSpec((s_len, sw), lambda i, j: (0, j)),
                       pl.BlockSpec((None, None, tmw, sw), lambda i, j: (j % 2, j // 2, o_wop // tmw + i, 0)))[0]

    (dproj, dk, dv, ddiag), (from_sibling_a,) = _attn_bwd(proj, datt, base, a, dproj, comm=_cores_comm(slab))
    chip_part_a = _add_core_partials("add_core_partials_a", slab, from_sibling_a, core, (2 * d + cg) // 2)
    drb = _bias_grad(ddiag.reshape(heads, SKEW))
    dproj = lax.dynamic_update_slice(dproj, dk, (0, a))
    dproj = lax.dynamic_update_slice(dproj, dv, (0, 2 * a))

    tmd = _tile(d // 2, 1024)
    nrb = d // 2 // tmd

    def dw_in_rows(name, half, comm):
        return _mm(
            name, (nrb, n_in // sw),
            [(hb, pl.BlockSpec((s_len, tmd), lambda i, t: (0, half * nrb + i))),
             (dproj, pl.BlockSpec((s_len, sw), lambda i, t: (0, pos(t))))],
            [(slab_b, pl.BlockSpec((None, None, tmd, sw),
                                   lambda i, t: ((t // per) % 2, (t // per) // 2, (t % per) * nrb + i, 0)))],
            TN, into_slab, comm=comm)

    slab_lo, parts_a = dw_in_rows("dw_in_lo", 0, _chips_comm(chip_part_a))
    slab_hi, from_sibling_lo = dw_in_rows("dw_in_hi", 1, _cores_comm(slab_lo))
    chip_part_lo = _add_core_partials("add_core_partials_lo", slab_lo, from_sibling_lo, core, 4 * sw)

    tk = w_in.shape[1]
    tnh = _tile(d, 1024)
    tmh = tm if s_len > tm else s_len // 2
    n_row = s_len // tmh

    def dh_rows(name, lo, hi, prev, comm):
        def body(in_refs, out_refs, scratch_refs):
            acc_ref = scratch_refs[0]
            k = pl.program_id(2)
            part = lax.dot_general(in_refs[0][...], in_refs[per][...], NT, preferred_element_type=F32)
            for r in range(1, per):
                part += lax.dot_general(in_refs[r][...], in_refs[per + r][...], NT, preferred_element_type=F32)

            @pl.when(k == 0)
            def _():
                acc_ref[...] = part

            @pl.when(k > 0)
            def _():
                acc_ref[...] += part

            @pl.when(k == N_DEV - 1)
            def _():
                out_refs[0][...] = acc_ref[...]

        ins = [(dproj, pl.BlockSpec((tmh, sw), lambda i, j, k, r=r: (lo + i, pos(per * k + r)))) for r in range(per)]
        ins += [(w, pl.BlockSpec((None, tnh, sw), lambda i, j, k: (k, j, 0))) for w in win_rounds]
        if prev is not None:
            ins.append((prev, hbm))
        return _call(name, body, (hi - lo, d // tnh, N_DEV), ins,
                     [(_sds((s_len, d), F32), pl.BlockSpec((tmh, tnh), lambda i, j, k: (lo + i, j)))],
                     scratch=[pltpu.VMEM((tmh, tnh), F32)], aliases={2 * per: 0} if prev is not None else None,
                     comm=comm)

    dh, parts_lo, from_sibling_hi = dh_rows("dh_head", 0, n_row // 2, None,
                                            _both(_chips_comm(chip_part_lo), _cores_comm(slab_hi)))
    chip_part_hi = _add_core_partials("add_core_partials_hi", slab_hi, from_sibling_hi, core, 4 * sw)
    dh, parts_hi = dh_rows("dh_rest", n_row // 2, n_row, dh, _chips_comm(chip_part_hi))

    dx, dg1 = _rms_bwd(x2d, dh, dx2, g1)

    g_win, d_win, m_win, v_win = _adamw_row_halves("adamw_w_in", parts_lo, parts_hi, w_in, m_w_in, v_w_in)
    g_woa, d_woa, m_woa, v_woa = _adamw_shard("adamw_w_out_attn", parts_a, w_out_attn, m_w_out_attn, v_w_out_attn, 0)
    g_wop, d_wop, m_wop, v_wop = _adamw_shard("adamw_w_out_pool", parts_a, w_out_pool, m_w_out_pool, v_w_out_pool, o_wop)
    g_wo, d_wo, m_wo, v_wo = _adamw_shard("adamw_w_out", parts_a, w_out, m_w_out, v_w_out, o_wo)
    flat = lambda t: t.reshape(cg // 2, sw)
    pool_out = _adamw_shard("adamw_pool_w", parts_a, flat(pool_w), flat(m_pool_w), flat(v_pool_w), o_pool)
    g_pw, d_pw, m_pw, v_pw = [t.reshape(pool_w.shape) for t in pool_out]
    pad16 = lambda t: jnp.pad(t, ((0, 14), (0, 0)))
    gb_out = _adamw_shard("adamw_gate_bias", parts_a, pad16(gate_bias), pad16(m_gate_bias), pad16(v_gate_bias),
                          o_pool + cg // 2)
    g_gb, d_gb, m_gb, v_gb = [t[:2] for t in gb_out]

    def pack(n_gain, f_gain, scale, rb, last):
        rows = [n_gain.reshape(-1, LANES), f_gain.reshape(-1, LANES), scale.reshape(-1, LANES),
                rb.reshape(-1, LANES), last]
        return jnp.concatenate(rows, axis=0)

    pad_rb = lambda t: jnp.pad(t, ((0, 0), (0, N_REL_PAD - N_REL)))
    zeros8 = jnp.zeros((8, LANES), F32)
    loss_rows = jnp.pad(loss_part, ((0, 7), (0, 0)))
    small = _small_allreduce_adamw(
        pack(dg1, dg2, dps, drb, loss_rows),
        pack(norm_gain, final_gain, pool_scale, pad_rb(rel_bias), zeros8),
        pack(m_norm_gain, m_final_gain, m_pool_scale, pad_rb(m_rel_bias), zeros8),
        pack(v_norm_gain, v_final_gain, v_pool_scale, pad_rb(v_rel_bias), zeros8))

    n1, n2, n3 = d // LANES, 2 * d // LANES, (2 * d + p) // LANES
    n4 = n3 + heads * N_REL_PAD // LANES

    def unpack(t):
        return (t[:n1].reshape(d), t[n1:n2].reshape(d), t[n2:n3].reshape(p),
                t[n3:n4].reshape(heads, N_REL_PAD)[:, :N_REL])

    (g_ng, g_fg, g_ps, g_rb), (d_ng, d_fg, d_ps, d_rb), (m_ng, m_fg, m_ps, m_rb), (v_ng, v_fg, v_ps, v_rb) = [
        unpack(t) for t in small]
    loss = small[0][n4, 0]

    return (loss, dx.reshape(x.shape),
            g_ng, g_win, g_rb, g_pw, g_ps, g_woa, g_wop, g_gb, g_wo, g_fg,
            d_ng, d_win, d_rb, d_pw, d_ps, d_woa, d_wop, d_gb, d_wo, d_fg,
            m_ng, m_win, m_rb, m_pw, m_ps, m_woa, m_wop, m_gb, m_wo, m_fg,
            v_ng, v_win, v_rb, v_pw, v_ps, v_woa, v_wop, v_gb, v_wo, v_fg)
```

```python
import jax
import jax.numpy as jnp
from jax import lax
from jax.experimental import pallas as pl
from jax.experimental.pallas import tpu as pltpu

F32 = jnp.float32
BF = jnp.bfloat16
MESH = pl.DeviceIdType.MESH

N_DEV = 8
CHUNK = 64
N_LEFT_CHUNKS = 8
HEAD_DIM = 128
MAX_REL = 128
N_REL = 2 * MAX_REL + 1
N_REL_PAD = 384
POOL_WINDOWS = (2, 4, 8, 16)
HALO = 16
EPS = 1e-6
ADAM_LR = 0.001
ADAM_B1 = 0.9
ADAM_B2 = 0.999
ADAM_EPS = 1e-08
ADAM_WD = 0.01
ADAM_STEP = 10
NEG = -1e30
LANES = 128
TQ = N_LEFT_CHUNKS * CHUNK
TK = 2 * TQ
SKEW = 2 * TK
VMEM_LIMIT = 52 * 1024 * 1024

NN = (((1,), (0,)), ((), ()))
NT = (((1,), (1,)), ((), ()))
TN = (((0,), (0,)), ((), ()))


def _params(n_grid):
    return pltpu.CompilerParams(dimension_semantics=("arbitrary",) * n_grid, vmem_limit_bytes=VMEM_LIMIT)


def _sig(z):
    return 1.0 / (1.0 + jnp.exp(-z))


def _silu_and_grad(z):
    s = _sig(z)
    return z * s, s * (1.0 + z * (1.0 - s))


def _tile(n, pref):
    t = min(n, pref)
    assert n % t == 0, (n, pref)
    return t


def _sds(shape, dtype):
    return jax.ShapeDtypeStruct(shape, dtype)


class _Comm:
    def __init__(self, ins, outs, scratch, start, wait, hooks=()):
        self.ins, self.outs, self.scratch = list(ins), list(outs), list(scratch)
        self.start, self.wait, self.hooks = start, wait, tuple(hooks)


def _call(name, body, grid, ins, outs, scratch=(), aliases=None, comm=None):
    n_in, n_out, n_scr = len(ins), len(outs), len(scratch)
    c_in = len(comm.ins) if comm else 0
    c_out = len(comm.outs) if comm else 0
    n_steps = 1
    for g in grid:
        n_steps *= g

    def kern(*refs):
        o0 = n_in + c_in
        s0 = o0 + n_out + c_out
        if comm:
            c_refs = (refs[n_in:o0], refs[o0 + n_out:s0], refs[s0 + n_scr:])
            step = pl.program_id(0)
            for ax in range(1, len(grid)):
                step = step * grid[ax] + pl.program_id(ax)

            @pl.when(step == 0)
            def _():
                comm.start(*c_refs)

            for frac, hook in comm.hooks:
                @pl.when(step == int(frac * n_steps))
                def _(hook=hook):
                    hook(*c_refs)

        body(refs[:n_in], refs[o0:o0 + n_out], refs[s0:s0 + n_scr])

        if comm:
            @pl.when(step == n_steps - 1)
            def _():
                comm.wait(*c_refs)

    hbm = pl.BlockSpec(memory_space=pl.ANY)
    return pl.pallas_call(
        kern, grid=grid,
        in_specs=[s for _, s in ins] + [hbm] * c_in, out_specs=[s for _, s in outs] + [hbm] * c_out,
        out_shape=[o for o, _ in outs] + (comm.outs if comm else []),
        scratch_shapes=list(scratch) + (comm.scratch if comm else []),
        name=name, compiler_params=_params(len(grid)), input_output_aliases=aliases or {},
    )(*([a for a, _ in ins] + (comm.ins if comm else [])))


def _mm(name, grid, ins, outs, dims, epi, aliases=None, comm=None, nb=1):
    def body(in_refs, out_refs, _):
        if nb == 1:
            acc = lax.dot_general(in_refs[0][...], in_refs[1][...], dims, preferred_element_type=F32)
        else:
            kq = in_refs[0].shape[1] // nb
            acc = sum(lax.dot_general(in_refs[0][:, q * kq:(q + 1) * kq], in_refs[1 + q][...], dims,
                                      preferred_element_type=F32) for q in range(nb))
        epi(acc, in_refs[1 + nb:], out_refs)

    return _call(name, body, grid, ins, outs, aliases=aliases, comm=comm)


def _rms_fwd(x, g, comm=None):
    s, d = x.shape
    tr = _tile(s, 256)

    def body(in_refs, out_refs, _):
        xv = in_refs[0][...]
        r = lax.rsqrt(jnp.mean(xv * xv, axis=-1, keepdims=True) + EPS)
        out_refs[0][...] = (xv * r * in_refs[1][...]).astype(BF)

    row = pl.BlockSpec((tr, d), lambda i: (i, 0))
    return _call("rms_fwd", body, (s // tr,), [(x, row), (g, pl.BlockSpec((1, d), lambda i: (0, 0)))],
                 [(_sds((s, d), BF), row)], comm=comm)


def _final_norm(x2, target, g):
    s, d = x2.shape
    tr = _tile(s, 128)

    def kern(x_ref, t_ref, g_ref, dx_ref, dxb_ref, dg_ref, loss_ref):
        i = pl.program_id(0)
        xv = x_ref[...]
        gv = g_ref[...]
        r = lax.rsqrt(jnp.mean(xv * xv, axis=-1, keepdims=True) + EPS)
        xhat = xv * r
        err = xhat * gv - t_ref[...]
        dy = err * (1.0 / d)
        gy = dy * gv
        dx = r * (gy - xhat * jnp.mean(gy * xhat, axis=-1, keepdims=True))
        dx_ref[...] = dx
        dxb_ref[...] = dx.astype(BF)
        dg = jnp.sum(dy * xhat, axis=0, keepdims=True)
        ls = jnp.broadcast_to(0.5 * jnp.sum(jnp.mean(err * err, axis=-1, keepdims=True)), (1, LANES))

        @pl.when(i == 0)
        def _():
            dg_ref[...] = dg
            loss_ref[...] = ls

        @pl.when(i > 0)
        def _():
            dg_ref[...] += dg
            loss_ref[...] += ls

    row = pl.BlockSpec((tr, d), lambda i: (i, 0))
    vec = pl.BlockSpec((1, d), lambda i: (0, 0))
    return pl.pallas_call(
        kern, grid=(s // tr,), in_specs=[row, row, vec],
        out_specs=[row, row, vec, pl.BlockSpec((1, LANES), lambda i: (0, 0))],
        out_shape=[_sds((s, d), F32), _sds((s, d), BF), _sds((1, d), F32), _sds((1, LANES), F32)],
        name="final_norm", compiler_params=_params(1))(x2, target, g)


def _rms_bwd(x, dh, dx2, g):
    s, d = x.shape
    tr = _tile(s, 128)

    def kern(x_ref, dh_ref, dx2_ref, g_ref, dx_ref, dg_ref):
        i = pl.program_id(0)
        xv = x_ref[...]
        r = lax.rsqrt(jnp.mean(xv * xv, axis=-1, keepdims=True) + EPS)
        xhat = xv * r
        dhv = dh_ref[...]
        gh = dhv * g_ref[...]
        dx_ref[...] = dx2_ref[...] + r * (gh - xhat * jnp.mean(gh * xhat, axis=-1, keepdims=True))
        dg = jnp.sum(dhv * xhat, axis=0, keepdims=True)

        @pl.when(i == 0)
        def _():
            dg_ref[...] = dg

        @pl.when(i > 0)
        def _():
            dg_ref[...] += dg

    row = pl.BlockSpec((tr, d), lambda i: (i, 0))
    vec = pl.BlockSpec((1, d), lambda i: (0, 0))
    return pl.pallas_call(
        kern, grid=(s // tr,), in_specs=[row, row, row, vec], out_specs=[row, vec],
        out_shape=[_sds((s, d), F32), _sds((1, d), F32)],
        name="rms_bwd", compiler_params=_params(1))(x, dh, dx2, g)


def _rel_index(j, backward):
    if backward:
        rel = 2 * TQ - 1 - j
    else:
        rel = TQ - jnp.where(j < TK, j, j - SKEW)
    return jnp.clip(rel, -MAX_REL, MAX_REL) + MAX_REL


def _bias_rows(rel_bias_pad):
    h = rel_bias_pad.shape[0]

    def kern(rb_ref, o_ref):
        j = lax.broadcasted_iota(jnp.int32, (N_REL_PAD, SKEW), 1)
        k = lax.broadcasted_iota(jnp.int32, (N_REL_PAD, SKEW), 0)
        onehot = (_rel_index(j, False) == k).astype(F32)
        o_ref[...] = jnp.dot(rb_ref[...], onehot, preferred_element_type=F32, precision=lax.Precision.HIGHEST)

    return pl.pallas_call(kern, out_shape=_sds((h, SKEW), F32), name="bias_rows")(rel_bias_pad)


def _bias_grad(ddiag):
    h = ddiag.shape[0]

    def kern(d_ref, o_ref):
        j = lax.broadcasted_iota(jnp.int32, (N_REL_PAD, SKEW), 1)
        k = lax.broadcasted_iota(jnp.int32, (N_REL_PAD, SKEW), 0)
        onehot = ((_rel_index(j, True) == k) & (j < TQ + TK - 1)).astype(F32)
        o_ref[...] = lax.dot_general(d_ref[...], onehot, NT, preferred_element_type=F32,
                                     precision=lax.Precision.HIGHEST)

    return pl.pallas_call(kern, out_shape=_sds((h, N_REL_PAD), F32), name="bias_grad")(ddiag)


def _bias_tile(row, first):
    t = pltpu.roll(jnp.broadcast_to(row, (TQ, SKEW)), 0, 1, stride=1, stride_axis=0)[:, :TK]
    r = lax.broadcasted_iota(jnp.int32, (TQ, TK), 0) // CHUNK
    col = lax.broadcasted_iota(jnp.int32, (TQ, TK), 1)
    dist = N_LEFT_CHUNKS + r - col // CHUNK
    keep = (dist >= 0) & (dist <= N_LEFT_CHUNKS) & jnp.logical_not(first & (col < TQ))
    return jnp.where(keep, t, NEG)


def _scores(q, keys, tile):
    s = lax.dot_general(q, keys, NT, preferred_element_type=F32) * (HEAD_DIM ** -0.5) + tile
    m = jnp.max(s, axis=1, keepdims=True)
    p = jnp.exp(s - m)
    return p, jnp.sum(p, axis=1, keepdims=True)


def _attn_fwd(proj, base, a_width, comm=None):
    s_len = proj.shape[0]
    heads = a_width // HEAD_DIM
    nq = s_len // TQ
    kb, vb, zb = heads, 2 * heads, 3 * heads

    def kern(q_ref, kp_ref, kc_ref, vp_ref, vc_ref, z_ref, base_ref, att_ref, ya_ref, tile_ref):
        i = pl.program_id(1)

        @pl.when(i <= 1)
        def _():
            tile_ref[...] = _bias_tile(base_ref[...], i == 0)

        kcat = jnp.concatenate([kp_ref[...], kc_ref[...]], axis=0)
        vcat = jnp.concatenate([vp_ref[...], vc_ref[...]], axis=0)
        p, l = _scores(q_ref[...], kcat, tile_ref[...])
        o = jnp.dot(p.astype(BF), vcat, preferred_element_type=F32) / l
        att_ref[...] = o.astype(BF)
        z = z_ref[...].astype(F32)
        ya_ref[...] = (o * (z * _sig(z))).astype(BF)

    blk = lambda off: pl.BlockSpec((TQ, HEAD_DIM), lambda h, i: (i, off + h))
    prev = lambda off: pl.BlockSpec((TQ, HEAD_DIM), lambda h, i: (jnp.maximum(i - 1, 0), off + h))
    out = pl.BlockSpec((TQ, HEAD_DIM), lambda h, i: (i, h))
    def body(in_refs, out_refs, scratch_refs):
        kern(*in_refs, *out_refs, *scratch_refs)

    act = _sds((s_len, a_width), BF)
    res = _call(
        "attn_fwd", body, (heads, nq),
        [(proj, blk(0)), (proj, prev(kb)), (proj, blk(kb)), (proj, prev(vb)), (proj, blk(vb)), (proj, blk(zb)),
         (base, pl.BlockSpec((None, 1, SKEW), lambda h, i: (h, 0, 0)))],
        [(act, out), (act, out)], scratch=[pltpu.VMEM((TQ, TK), F32)], comm=comm)
    return res[:2], res[2:]


def _attn_bwd(proj, datt, base, a_width, dproj, comm=None):
    s_len = proj.shape[0]
    heads = a_width // HEAD_DIM
    nq = s_len // TQ
    kb, vb = heads, 2 * heads
    scale = HEAD_DIM ** -0.5

    def body(in_refs, out_refs, scratch_refs):
        q_ref, kp_ref, kc_ref, vp_ref, vc_ref, do_ref, base_ref, _ = in_refs
        dq_ref, dk_ref, dv_ref, dd_ref = out_refs
        tile_ref, dsacc_ref, ak_ref, av_ref = scratch_refs
        i = pl.program_id(1)

        @pl.when(i <= 1)
        def _():
            tile_ref[...] = _bias_tile(base_ref[...], i == 0)

        @pl.when(i == 0)
        def _():
            dsacc_ref[...] = jnp.zeros_like(dsacc_ref)
            ak_ref[...] = jnp.zeros_like(ak_ref)
            av_ref[...] = jnp.zeros_like(av_ref)

        @pl.when(i < nq)
        def _():
            q = q_ref[...]
            do = do_ref[...]
            kcat = jnp.concatenate([kp_ref[...], kc_ref[...]], axis=0)
            vcat = jnp.concatenate([vp_ref[...], vc_ref[...]], axis=0)
            p, l = _scores(q, kcat, tile_ref[...])
            p = p / l
            dp = lax.dot_general(do, vcat, NT, preferred_element_type=F32)
            ds = p * (dp - jnp.sum(p * dp, axis=1, keepdims=True))
            dsacc_ref[...] += ds
            dsb = ds.astype(BF)
            dq_ref[...] = (jnp.dot(dsb, kcat, preferred_element_type=F32) * scale).astype(BF)
            dkc = lax.dot_general(dsb, q, TN, preferred_element_type=F32) * scale
            dvc = lax.dot_general(p.astype(BF), do, TN, preferred_element_type=F32)
            dk_ref[...] = (ak_ref[...] + dkc[:TQ]).astype(BF)
            dv_ref[...] = (av_ref[...] + dvc[:TQ]).astype(BF)
            ak_ref[...] = dkc[TQ:]
            av_ref[...] = dvc[TQ:]

        @pl.when(i == nq)
        def _():
            dk_ref[...] = ak_ref[...].astype(BF)
            dv_ref[...] = av_ref[...].astype(BF)
            acc = dsacc_ref[...]
            rr = lax.broadcasted_iota(jnp.int32, (TQ, TQ), 0)
            cc = lax.broadcasted_iota(jnp.int32, (TQ, TQ), 1)
            flip = (rr + cc == TQ - 1).astype(BF)
            hi = acc.astype(BF)
            lo = (acc - hi.astype(F32)).astype(BF)
            rev = jnp.dot(flip, hi, preferred_element_type=F32) + jnp.dot(flip, lo, preferred_element_type=F32)
            wide = jnp.concatenate([rev, jnp.zeros((TQ, SKEW - TK), F32)], axis=1)
            dd_ref[...] = jnp.sum(pltpu.roll(wide, 0, 1, stride=1, stride_axis=0), axis=0, keepdims=True)

    last = nq - 1
    cur = lambda off: pl.BlockSpec((TQ, HEAD_DIM), lambda h, i: (jnp.minimum(i, last), off + h))
    prev = lambda off: pl.BlockSpec((TQ, HEAD_DIM), lambda h, i: (jnp.maximum(jnp.minimum(i, last) - 1, 0), off + h))
    done = pl.BlockSpec((TQ, HEAD_DIM), lambda h, i: (jnp.maximum(i - 1, 0), h))
    row = pl.BlockSpec((None, 1, SKEW), lambda h, i: (h, 0, 0))
    act = _sds((s_len, a_width), BF)
    res = _call(
        "attn_bwd", body, (heads, nq + 1),
        [(proj, cur(0)), (proj, prev(kb)), (proj, cur(kb)), (proj, prev(vb)), (proj, cur(vb)), (datt, cur(0)),
         (base, row), (dproj, pl.BlockSpec(memory_space=pl.ANY))],
        [(_sds(dproj.shape, dproj.dtype), cur(0)), (act, done), (act, done), (_sds((heads, 1, SKEW), F32), row)],
        aliases={7: 0},
        scratch=[pltpu.VMEM((TQ, TK), F32), pltpu.VMEM((TQ, TK), F32),
                 pltpu.VMEM((TQ, HEAD_DIM), F32), pltpu.VMEM((TQ, HEAD_DIM), F32)],
        comm=comm)
    return res[:4], res[4:]


def _pool_fwd(proj, pool_w, pool_scale, p_width, u_blk, z_blk):
    s_len = proj.shape[0]
    cg = p_width // len(POOL_WINDOWS)
    tt = _tile(s_len, 512)

    def kern(up_ref, uc_ref, z_ref, pw_ref, sc_ref, d_ref, y_ref, yp_ref):
        t = pl.program_id(0)
        row = lax.broadcasted_iota(jnp.int32, (tt, 1), 0) + t * tt
        for g, w in enumerate(POOL_WINDOWS):
            cs = slice(g * cg, (g + 1) * cg)
            prev = jnp.where(t == 0, 0.0, up_ref[:, cs].astype(F32))
            cur = uc_ref[:, cs].astype(F32)
            ws = jnp.concatenate([prev, cur], axis=0)
            sh = 1
            while sh < w:
                ws = ws + pltpu.roll(ws, sh, 0)
                sh *= 2
            cnt = jnp.minimum(row + 1, w).astype(F32)
            db = (ws[HALO:, :] / cnt - cur).astype(BF)
            y = jnp.dot(db, pw_ref[g], preferred_element_type=F32)
            d_ref[:, cs] = db
            y_ref[:, cs] = y.astype(BF)
            z = z_ref[:, cs].astype(F32)
            yp_ref[:, cs] = (y * sc_ref[:, cs] * (z * _sig(z))).astype(BF)

    full = pl.BlockSpec((tt, p_width), lambda t: (t, 0))
    return pl.pallas_call(
        kern, grid=(s_len // tt,),
        in_specs=[pl.BlockSpec((HALO, p_width), lambda t: (jnp.maximum(t * (tt // HALO) - 1, 0), u_blk)),
                  pl.BlockSpec((tt, p_width), lambda t: (t, u_blk)),
                  pl.BlockSpec((tt, p_width), lambda t: (t, z_blk)),
                  pl.BlockSpec((len(POOL_WINDOWS), cg, cg), lambda t: (0, 0, 0)),
                  pl.BlockSpec((1, p_width), lambda t: (0, 0))],
        out_specs=[full, full, full], out_shape=[_sds((s_len, p_width), BF)] * 3,
        name="pool_fwd", compiler_params=_params(1))(proj, proj, proj, pool_w, pool_scale)


def _pool_bwd(dy, dmean, pool_w, dproj, u_blk):
    s_len, p_width = dy.shape
    ng = len(POOL_WINDOWS)
    cg = p_width // ng
    tt = _tile(s_len, 512)
    nt = s_len // tt

    def kern(dyc_ref, dyn_ref, d_ref, pw_ref, _, du_ref, dpw_ref):
        t = pl.program_id(0)

        @pl.when(t == 0)
        def _():
            dpw_ref[...] = jnp.zeros_like(dpw_ref)

        row = lax.broadcasted_iota(jnp.int32, (tt + HALO, 1), 0) + t * tt
        for g, w in enumerate(POOL_WINDOWS):
            cs = slice(g * cg, (g + 1) * cg)
            dyc = dyc_ref[:, cs]
            ddc = lax.dot_general(dyc, pw_ref[g], NT, preferred_element_type=F32)
            ddn = lax.dot_general(dyn_ref[:, cs], pw_ref[g], NT, preferred_element_type=F32)
            ddn = jnp.where(t == nt - 1, 0.0, ddn)
            cnt = jnp.minimum(row + 1, w).astype(F32)
            ws = jnp.concatenate([ddc, ddn], axis=0) / cnt
            sh = 1
            while sh < w:
                ws = ws + pltpu.roll(ws, tt + HALO - sh, 0)
                sh *= 2
            du_ref[:, cs] = (ws[:tt, :] - ddc).astype(BF)
            dpw_ref[g] += lax.dot_general(d_ref[:, cs], dyc, TN, preferred_element_type=F32)

    full = pl.BlockSpec((tt, p_width), lambda t: (t, 0))
    pw_spec = pl.BlockSpec((ng, cg, cg), lambda t: (0, 0, 0))
    return pl.pallas_call(
        kern, grid=(nt,),
        in_specs=[full,
                  pl.BlockSpec((HALO, p_width), lambda t: (jnp.minimum((t + 1) * (tt // HALO), s_len // HALO - 1), 0)),
                  full, pw_spec, pl.BlockSpec(memory_space=pl.ANY)],
        out_specs=[pl.BlockSpec((tt, p_width), lambda t: (t, u_blk)), pw_spec],
        out_shape=[_sds(dproj.shape, dproj.dtype), _sds((ng, cg, cg), F32)],
        input_output_aliases={4: 0},
        name="pool_bwd", compiler_params=_params(1))(dy, dy, dmean, pool_w, dproj)


def _adam(g, w_ref, m_ref, v_ref, g_out, d_out, m_out, v_out):
    m = ADAM_B1 * m_ref[...] + (1.0 - ADAM_B1) * g
    v = ADAM_B2 * v_ref[...] + (1.0 - ADAM_B2) * (g * g)
    m_hat = m / (1.0 - ADAM_B1 ** ADAM_STEP)
    v_hat = v / (1.0 - ADAM_B2 ** ADAM_STEP)
    g_out[...] = g
    d_out[...] = -ADAM_LR * (m_hat / (jnp.sqrt(v_hat) + ADAM_EPS) + ADAM_WD * w_ref[...])
    m_out[...] = m
    v_out[...] = v


def _adamw_shard(name, parts, w, m, v, row_off):
    rw, cw = w.shape
    sw = parts.shape[2]
    tr = _tile(rw, 512)
    assert row_off % tr == 0 and cw % sw == 0

    def kern(b_ref, w_ref, m_ref, v_ref, g_out, d_out, m_out, v_out):
        b = b_ref[...].astype(F32)
        _adam(((b[0] + b[1]) + b[2]) + b[3], w_ref, m_ref, v_ref, g_out, d_out, m_out, v_out)

    blk = pl.BlockSpec((tr, sw), lambda ct, i: (i, ct))
    return pl.pallas_call(
        kern, grid=(cw // sw, rw // tr),
        in_specs=[pl.BlockSpec((4, tr, sw), lambda ct, i: (0, (row_off + ct * rw) // tr + i, 0)), blk, blk, blk],
        out_specs=[blk] * 4, out_shape=[_sds((rw, cw), F32)] * 4,
        name=name, compiler_params=_params(2))(parts, w, m, v)


def _adamw_row_halves(name, parts_lo, parts_hi, w, m, v):
    rw, cw = w.shape
    sw = parts_lo.shape[2]
    half = rw // 2
    tr = _tile(half, 512)
    nh = half // tr

    def kern(lo_ref, hi_ref, w_ref, m_ref, v_ref, g_out, d_out, m_out, v_out):
        i = pl.program_id(1)

        def update(b_ref):
            b = b_ref[...].astype(F32)
            _adam(((b[0] + b[1]) + b[2]) + b[3], w_ref, m_ref, v_ref, g_out, d_out, m_out, v_out)

        @pl.when(i < nh)
        def _():
            update(lo_ref)

        @pl.when(i >= nh)
        def _():
            update(hi_ref)

    blk = pl.BlockSpec((tr, sw), lambda ct, i: (i, ct))
    return pl.pallas_call(
        kern, grid=(cw // sw, rw // tr),
        in_specs=[pl.BlockSpec((4, tr, sw), lambda ct, i: (0, ct * nh + jnp.minimum(i, nh - 1), 0)),
                  pl.BlockSpec((4, tr, sw), lambda ct, i: (0, ct * nh + jnp.maximum(i - nh, 0), 0)), blk, blk, blk],
        out_specs=[blk] * 4, out_shape=[_sds((rw, cw), F32)] * 4,
        name=name, compiler_params=_params(2))(parts_lo, parts_hi, w, m, v)


def _both(c1, c2):
    n_in, n_out, n_sem = len(c1.ins), len(c1.outs), len(c1.scratch)

    def split(ins, outs, sems):
        return (ins[:n_in], outs[:n_out], sems[:n_sem]), (ins[n_in:], outs[n_out:], sems[n_sem:])

    def start(*refs):
        r1, r2 = split(*refs)
        c1.start(*r1)
        c2.start(*r2)

    def wait(*refs):
        r1, r2 = split(*refs)
        c1.wait(*r1)
        c2.wait(*r2)

    def of(which, hook):
        return lambda *refs: hook(*split(*refs)[which])

    hooks = [(f, of(0, h)) for f, h in c1.hooks] + [(f, of(1, h)) for f, h in c2.hooks]
    return _Comm(c1.ins + c2.ins, c1.outs + c2.outs, c1.scratch + c2.scratch, start, wait, hooks)


def _position():
    return lax.axis_index("x"), lax.axis_index("y"), lax.axis_index("c")


def _gather_comm(shards, cols=None, pass_at=None):
    if cols is None:
        pieces = [(a, None) for a in range(len(shards))]
        shapes = [_sds(s.shape, s.dtype) for s in shards]
    else:
        half = shards[0].shape[0] // 2
        pieces = [(0, pl.ds(0, half)), (0, pl.ds(half, half))]
        shapes = [_sds((shards[0].shape[0], cols[1]), shards[0].dtype)]
    n = len(pieces)

    def plan(xs, outs, sems, only=None):
        send_sems, recv_sems, local_sems = sems
        x, y, c = _position()
        me, sibling = (x, y, c), (x, y, 1 - c)
        chips = [(1 - x, y), (x, 1 - y), (1 - x, 1 - y)]
        which = range(n) if only is None else only

        def source(v):
            a, rows = pieces[v]
            return xs[a] if rows is None else xs[a].at[rows, pl.ds(cols[0], cols[1])]

        def landing(v, block):
            a, rows = pieces[v]
            dst = outs[a].at[4 * block[0] + 2 * block[1] + block[2]]
            return dst if rows is None else dst.at[rows, :]

        def copy(v, k, block, to, own=False):
            dst = landing(v, block)
            return pltpu.make_async_remote_copy(
                src_ref=source(v) if own else dst, dst_ref=dst,
                send_sem=send_sems.at[7 * v + k], recv_sem=recv_sems.at[7 * v + k],
                device_id=to, device_id_type=MESH)

        by_chip = [(j, chip, v) for v in which for j, chip in enumerate(chips)]
        return dict(
            mine=lambda: [pltpu.make_async_copy(source(v), landing(v, me), local_sems.at[v]) for v in which],
            first=lambda: ([copy(v, 0, me, sibling, own=True) for v in which]
                           + [copy(v, 1 + j, me, (*chip, c), own=True) for j, chip, v in by_chip]),
            landed=lambda: [copy(v, 1 + j, (*chip, c), me) for j, chip, v in by_chip],
            passed=lambda: [copy(v, 4 + j, (*chip, c), sibling) for j, chip, v in by_chip],
            rest=lambda: ([copy(v, 0, sibling, me) for v in which]
                          + [copy(v, 4 + j, (*chip, 1 - c), me) for j, chip, v in by_chip]))

    def start(*refs):
        p = plan(*refs)
        for cp in p["mine"]() + p["first"]():
            cp.start()

    def pass_on(only):
        def hook(*refs):
            p = plan(*refs, only=only)
            for arrived, onward in zip(p["landed"](), p["passed"]()):
                arrived.wait_recv()
                onward.start()
        return hook

    def wait(*refs):
        if cols is not None:
            pass_on([1])(*refs)
        p = plan(*refs)
        for cp in p["rest"]():
            cp.wait_recv()
        for cp in p["first"]() + p["passed"]():
            cp.wait_send()
        for cp in p["mine"]():
            cp.wait()

    if cols is None:
        hooks = [(0.75 if pass_at is None else pass_at, pass_on(None))]
    else:
        hooks = [(0.5 if pass_at is None else pass_at, pass_on([0]))]
    return _Comm(shards, [_sds((N_DEV,) + s.shape, s.dtype) for s in shapes],
                 [pltpu.SemaphoreType.DMA((7 * n,)), pltpu.SemaphoreType.DMA((7 * n,)),
                  pltpu.SemaphoreType.DMA((n,))], start, wait, hooks)


def _cores_comm(slab):
    _, _, r, sw = slab.shape

    def copies(ins, outs, sems):
        x, y, c = _position()
        return [pltpu.make_async_remote_copy(
            src_ref=ins[0].at[1 - c], dst_ref=outs[0], send_sem=sems[0], recv_sem=sems[1],
            device_id=(x, y, 1 - c), device_id_type=MESH)]

    def start(*refs):
        for cp in copies(*refs):
            cp.start()

    def wait(*refs):
        for cp in copies(*refs):
            cp.wait()

    return _Comm([slab], [_sds((4, r, sw), slab.dtype)],
                 [pltpu.SemaphoreType.DMA, pltpu.SemaphoreType.DMA], start, wait)


def _add_core_partials(name, slab, recv, core, tr):
    _, _, r, sw = slab.shape

    def kern(c_ref, a_ref, b_ref, o_ref):
        o_ref[...] = (a_ref[...].astype(F32) + b_ref[...].astype(F32)).astype(BF)

    return pl.pallas_call(
        kern,
        grid_spec=pltpu.PrefetchScalarGridSpec(
            num_scalar_prefetch=1, grid=(4, r // tr),
            in_specs=[pl.BlockSpec((None, None, tr, sw), lambda k, i, c_ref: (c_ref[0], k, i, 0)),
                      pl.BlockSpec((None, tr, sw), lambda k, i, c_ref: (k, i, 0))],
            out_specs=pl.BlockSpec((None, tr, sw), lambda k, i, c_ref: (k, i, 0))),
        out_shape=_sds((4, r, sw), BF), name=name, compiler_params=_params(2))(core, slab, recv)


def _chips_comm(part):
    _, r, sw = part.shape

    def copies(ins, outs, sems):
        send_sems, recv_sems, local_sem = sems
        x, y, c = _position()
        mine = 2 * x + y
        local = pltpu.make_async_copy(ins[0].at[mine], outs[0].at[mine], local_sem)
        chips = [(1 - x, y), (x, 1 - y), (1 - x, 1 - y)]
        remote = [pltpu.make_async_remote_copy(
            src_ref=ins[0].at[2 * px + py], dst_ref=outs[0].at[mine],
            send_sem=send_sems.at[j], recv_sem=recv_sems.at[j],
            device_id=(px, py, c), device_id_type=MESH) for j, (px, py) in enumerate(chips)]
        return [local] + remote

    def start(*refs):
        for cp in copies(*refs):
            cp.start()

    def wait(*refs):
        for cp in copies(*refs):
            cp.wait()

    return _Comm([part], [_sds((4, r, sw), part.dtype)],
                 [pltpu.SemaphoreType.DMA((3,)), pltpu.SemaphoreType.DMA((3,)), pltpu.SemaphoreType.DMA],
                 start, wait)


def _small_allreduce_adamw(partial, w, m, v):
    nr = partial.shape[0]

    def kern(p_ref, w_ref, m_ref, v_ref, g_out, d_out, m_out, v_out, gath_ref, send_sems, recv_sems):
        x, y, c = _position()
        me = 4 * x + 2 * y + c
        gath_ref[me] = p_ref[...]
        copies = []
        for mask in range(1, N_DEV):
            peer = (x ^ (mask >> 2), y ^ ((mask >> 1) & 1), c ^ (mask & 1))
            copies.append(pltpu.make_async_remote_copy(
                src_ref=p_ref, dst_ref=gath_ref.at[me],
                send_sem=send_sems.at[mask - 1], recv_sem=recv_sems.at[mask - 1],
                device_id=peer, device_id_type=MESH))
        for cp in copies:
            cp.start()
        for cp in copies:
            cp.wait()
        tot = gath_ref[0]
        for k in range(1, N_DEV):
            tot = tot + gath_ref[k]
        _adam(tot, w_ref, m_ref, v_ref, g_out, d_out, m_out, v_out)

    vmem = pl.BlockSpec(memory_space=pltpu.VMEM)
    return pl.pallas_call(
        kern, in_specs=[vmem] * 4, out_specs=[vmem] * 4, out_shape=[_sds((nr, LANES), F32)] * 4,
        scratch_shapes=[pltpu.VMEM((N_DEV, nr, LANES), F32),
                        pltpu.SemaphoreType.DMA((N_DEV - 1,)), pltpu.SemaphoreType.DMA((N_DEV - 1,))],
        name="small_allreduce_adamw")(partial, w, m, v)


def kernel(x, norm_gain, w_in, rel_bias, pool_w, pool_scale, w_out_attn, w_out_pool, gate_bias, w_out, final_gain, loss_target, m_norm_gain, m_w_in, m_rel_bias, m_pool_w, m_pool_scale, m_w_out_attn, m_w_out_pool, m_gate_bias, m_w_out, m_final_gain, v_norm_gain, v_w_in, v_rel_bias, v_pool_w, v_pool_scale, v_w_out_attn, v_w_out_pool, v_gate_bias, v_w_out, v_final_gain):
    _, s_len, d = x.shape
    a = w_out_attn.shape[0]
    p = w_out_pool.shape[0]
    heads = a // HEAD_DIM
    ng = len(POOL_WINDOWS)
    cg = p // ng
    sw = d // N_DEV
    n_in = w_in.shape[1] * N_DEV
    assert a == p and a + p == d and cg == sw and n_in == 5 * d and w_in.shape[1] == 5 * sw
    assert s_len % TQ == 0 and rel_bias.shape == (heads, N_REL)
    tm = _tile(s_len, 1024)
    x2d = x.reshape(s_len, d)
    tgt = loss_target.reshape(s_len, d)

    g1 = norm_gain.reshape(1, d)
    g2 = final_gain.reshape(1, d)
    scale_row = pool_scale.reshape(1, p)
    tn = sw
    per = w_in.shape[1] // tn
    hbm = pl.BlockSpec(memory_space=pl.ANY)

    w_bf = [w_in[:, r * tn:(r + 1) * tn].astype(BF) for r in range(per)]
    hb, landed = _rms_fwd(x2d, g1, comm=_gather_comm([w_bf[0]], cols=(0, tn)))
    win_rounds = []

    def store_bf16(acc, _, outs):
        outs[0][...] = acc.astype(BF)

    gate0, n_gate = (4 * a + 2 * p) // tn, d // tn

    def pos(t):
        g = t - gate0
        return jnp.where(t < gate0, t, jnp.where(g < n_gate, gate0 + 2 * g, gate0 + 2 * (g - n_gate) + 1))

    proj = None
    for r in range(per):
        win_rounds.append(landed)
        if r + 1 < per:
            comm = _gather_comm([w_bf[r + 1]], cols=(0, tn))
        else:
            comm = _gather_comm([w_out_attn.astype(BF), pool_w.astype(BF), gate_bias])
        ins = [(hb, pl.BlockSpec((tm, d), lambda i, j: (i, 0))),
               (landed, pl.BlockSpec((None, d, tn), lambda i, j: (j, 0, 0)))]
        if proj is not None:
            ins.append((proj, hbm))
        proj, landed, *rest = _mm(
            f"proj_{r}", (s_len // tm, N_DEV), ins,
            [(_sds((s_len, n_in), BF), pl.BlockSpec((tm, tn), lambda i, j, r=r: (i, pos(per * j + r))))],
            NN, store_bf16, aliases={2: 0} if r else None, comm=comm)
    woa_g = landed
    woa = woa_g.transpose(1, 0, 2).reshape(a, d)
    pw = rest[0].transpose(1, 0, 2, 3).reshape(ng, cg, cg)
    gb = rest[1].transpose(1, 0, 2).reshape(2, d)

    rb_pad = jnp.pad(rel_bias, ((0, 0), (0, N_REL_PAD - N_REL)))
    base = _bias_rows(rb_pad).reshape(heads, 1, SKEW)
    wo_bf = w_out.astype(BF)
    (att, ya), (wop_g, wo_lo_g) = _attn_fwd(
        proj, base, a, comm=_both(_gather_comm([w_out_pool.astype(BF)], pass_at=0.9),
                                  _gather_comm([wo_bf], cols=(0, d // 2), pass_at=0.9)))
    wop = wop_g.transpose(1, 0, 2).reshape(p, d)
    u_blk, z_blk = 4 * a // p, 4 * a // p + 1
    dmean, ypre, yp = _pool_fwd(proj, pw, scale_row, p, u_blk, z_blk)

    def gate_kernel(in_refs, out_refs, _):
        ya_ref, woa_ref, yp_ref, wop_ref, ga_ref, gp_ref, gb_ref = in_refs
        m_ref, a_ref, p_ref = out_refs
        am = jnp.dot(ya_ref[...], woa_ref[...], preferred_element_type=F32)
        pm = jnp.dot(yp_ref[...], wop_ref[...], preferred_element_type=F32)
        sa = _sig(ga_ref[...].astype(F32) + gb_ref[0:1, :])
        sp = _sig(gp_ref[...].astype(F32) + gb_ref[1:2, :])
        m_ref[...] = (sa * am + sp * pm).astype(BF)
        a_ref[...] = am.astype(BF)
        p_ref[...] = pm.astype(BF)

    tile_ij = pl.BlockSpec((tm, tn), lambda i, j: (i, j))
    act_d = _sds((s_len, d), BF)
    merged, am, pm, wo_hi_g = _call(
        "gate_merge", gate_kernel, (s_len // tm, d // tn),
        [(ya, pl.BlockSpec((tm, a), lambda i, j: (i, 0))), (woa_g, pl.BlockSpec((None, a, tn), lambda i, j: (j, 0, 0))),
         (yp, pl.BlockSpec((tm, p), lambda i, j: (i, 0))), (wop_g, pl.BlockSpec((None, p, tn), lambda i, j: (j, 0, 0))),
         (proj, pl.BlockSpec((tm, tn), lambda i, j: (i, gate0 + 2 * j))),
         (proj, pl.BlockSpec((tm, tn), lambda i, j: (i, gate0 + 2 * j + 1))),
         (gb, pl.BlockSpec((2, tn), lambda i, j: (0, j)))],
        [(act_d, tile_ij)] * 3, comm=_gather_comm([wo_bf], cols=(d // 2, d // 2), pass_at=0.9))
    wo_halves = [wo_lo_g.reshape(d, d // 2), wo_hi_g.reshape(d, d // 2)]

    def out_proj(in_refs, out_refs, _):
        m_ref, lo_ref, hi_ref, x_ref = in_refs
        j = pl.program_id(1)

        @pl.when(j < n_half)
        def _():
            out_refs[0][...] = x_ref[...] + jnp.dot(m_ref[...], lo_ref[...], preferred_element_type=F32)

        @pl.when(j >= n_half)
        def _():
            out_refs[0][...] = x_ref[...] + jnp.dot(m_ref[...], hi_ref[...], preferred_element_type=F32)

    n_half = d // 2 // tn
    x2 = _call("out_proj", out_proj, (s_len // tm, d // tn),
               [(merged, pl.BlockSpec((tm, d), lambda i, j: (i, 0))),
                (wo_halves[0], pl.BlockSpec((d, tn), lambda i, j: (0, jnp.minimum(j, n_half - 1)))),
                (wo_halves[1], pl.BlockSpec((d, tn), lambda i, j: (0, jnp.maximum(j - n_half, 0)))),
                (x2d, tile_ij)],
               [(_sds((s_len, d), F32), tile_ij)])[0]

    dx2, dx2b, dg2, loss_part = _final_norm(x2, tgt, g2)

    tmb = _tile(s_len, 512)
    tile_ji = pl.BlockSpec((tmb, tn), lambda j, i: (i, j))

    def gate_bwd(dm, ex, outs):
        a_ref, p_ref, ga_ref, gp_ref, gb_ref = ex
        da_ref, dp_ref, dgate_ref, dgb_ref = outs
        i = pl.program_id(1)
        sa = _sig(ga_ref[...].astype(F32) + gb_ref[0:1, :])
        sp = _sig(gp_ref[...].astype(F32) + gb_ref[1:2, :])
        dga = dm * a_ref[...].astype(F32) * sa * (1.0 - sa)
        dgp = dm * p_ref[...].astype(F32) * sp * (1.0 - sp)
        da_ref[...] = (dm * sa).astype(BF)
        dp_ref[...] = (dm * sp).astype(BF)
        dgate_ref[:, :tn] = dga.astype(BF)
        dgate_ref[:, tn:] = dgp.astype(BF)
        r = lax.broadcasted_iota(jnp.int32, (8, tn), 0)
        sums = jnp.where(r == 0, jnp.sum(dga, axis=0, keepdims=True),
                         jnp.where(r == 1, jnp.sum(dgp, axis=0, keepdims=True), 0.0))

        @pl.when(i == 0)
        def _():
            dgb_ref[...] = sums

        @pl.when(i > 0)
        def _():
            dgb_ref[...] += sums

    dproj_shape = _sds((s_len, n_in), BF)
    d_am, d_pm, dproj, dgb8 = _mm(
        "gate_bwd", (d // tn, s_len // tmb),
        [(dx2b, pl.BlockSpec((tmb, d), lambda j, i: (i, 0)))]
        + [(w, pl.BlockSpec((tn, d // 2), lambda j, i: (j, 0))) for w in wo_halves]
        + [(am, tile_ji), (pm, tile_ji),
           (proj, pl.BlockSpec((tmb, tn), lambda j, i: (i, gate0 + 2 * j))),
           (proj, pl.BlockSpec((tmb, tn), lambda j, i: (i, gate0 + 2 * j + 1))),
           (gb, pl.BlockSpec((2, tn), lambda j, i: (0, j)))],
        [(_sds((s_len, d), BF), tile_ji)] * 2
        + [(dproj_shape, pl.BlockSpec((tmb, 2 * tn), lambda j, i: (i, gate0 // 2 + j))),
           (_sds((8, d), F32), pl.BlockSpec((8, tn), lambda j, i: (0, j)))],
        NT, gate_bwd, nb=2)

    za_t = 3 * a // tn

    def attn_gate_bwd(dya, ex, outs):
        silu, dsilu = _silu_and_grad(ex[0][...].astype(F32))
        outs[0][...] = (dya * silu).astype(BF)
        outs[1][...] = (dya * ex[1][...].astype(F32) * dsilu).astype(BF)

    datt, dproj = _mm(
        "attn_gate_bwd", (s_len // tm, a // tn),
        [(d_am, pl.BlockSpec((tm, d), lambda i, j: (i, 0))), (woa, pl.BlockSpec((tn, d), lambda i, j: (j, 0))),
         (proj, pl.BlockSpec((tm, tn), lambda i, j: (i, za_t + j))), (att, tile_ij), (dproj, hbm)],
        [(_sds((s_len, a), BF), tile_ij), (dproj_shape, pl.BlockSpec((tm, tn), lambda i, j: (i, za_t + j)))],
        NT, attn_gate_bwd, aliases={4: 1})

    zp_t = (4 * a + p) // tn

    def pool_gate_bwd(dyp, ex, outs):
        z_ref, y_ref, sc_ref, _ = ex
        dzp_ref, dy_ref, dps_ref = outs
        i = pl.program_id(1)
        silu, dsilu = _silu_and_grad(z_ref[...].astype(F32))
        y = y_ref[...].astype(F32)
        sc = sc_ref[...]
        dyp0 = dyp * silu
        dzp_ref[...] = (dyp * (y * sc) * dsilu).astype(BF)
        dy_ref[...] = (dyp0 * sc).astype(BF)
        dps = jnp.sum(dyp0 * y, axis=0, keepdims=True)

        @pl.when(i == 0)
        def _():
            dps_ref[...] = dps

        @pl.when(i > 0)
        def _():
            dps_ref[...] += dps

    dproj, dy_pool, dps = _mm(
        "pool_gate_bwd", (p // tn, s_len // tmb),
        [(d_pm, pl.BlockSpec((tmb, d), lambda j, i: (i, 0))), (wop, pl.BlockSpec((tn, d), lambda j, i: (j, 0))),
         (proj, pl.BlockSpec((tmb, tn), lambda j, i: (i, zp_t + j))), (ypre, tile_ji),
         (scale_row, pl.BlockSpec((1, tn), lambda j, i: (0, j))), (dproj, hbm)],
        [(dproj_shape, pl.BlockSpec((tmb, tn), lambda j, i: (i, zp_t + j))), (_sds((s_len, p), BF), tile_ji),
         (_sds((1, p), F32), pl.BlockSpec((1, tn), lambda j, i: (0, j)))],
        NT, pool_gate_bwd, aliases={5: 0})

    dproj, dpw = _pool_bwd(dy_pool, dmean, pw, dproj, u_blk)

    o_wop, o_wo, o_pool = a, d, 2 * d
    slab_a = _sds((2, 4, 2 * d + cg, sw), BF)
    slab_b = _sds((2, 4, 5 * d // 2, sw), BF)
    hbm = pl.BlockSpec(memory_space=pl.ANY)
    tmw = _tile(a, 1024)
    core = lax.axis_index("c").astype(jnp.int32).reshape(1)

    def pack_small(dpw_ref, dgb_ref, o_ref):
        rows = cg // N_DEV
        for j in range(N_DEV):
            for g in range(ng):
                o_ref[j % 2, j // 2, g * rows:(g + 1) * rows, :] = dpw_ref[g, j * rows:(j + 1) * rows, :].astype(BF)
            o_ref[j % 2, j // 2, ng * rows:, :] = jnp.concatenate(
                [dgb_ref[:, j * sw:(j + 1) * sw], jnp.zeros((cg - ng * rows - 8, sw), F32)], axis=0).astype(BF)

    slab = pl.pallas_call(
        pack_small, grid=(1,),
        in_specs=[pl.BlockSpec((ng, cg, cg), lambda i: (0, 0, 0)), pl.BlockSpec((8, d), lambda i: (0, 0))],
        out_specs=pl.BlockSpec((2, 4, cg, sw), lambda i: (0, 0, o_pool // cg, 0)), out_shape=slab_a,
        name="dw_small", compiler_params=_params(1))(dpw, dgb8)

    def into_slab(acc, _, outs):
        outs[0][...] = acc.astype(BF)

    def weight_grad(name, slab, lhs, rhs, grid, lhs_spec, rhs_spec, out_spec, comm=None):
        return _mm(name, grid, [(lhs, lhs_spec), (rhs, rhs_spec), (slab, hbm)], [(slab_a, out_spec)],
                   TN, into_slab, aliases={2: 0}, comm=comm)

    slab = weight_grad("dw_out", slab, merged, dx2b, (N_DEV, d // sw),
                       pl.BlockSpec((s_len, sw), lambda j, t: (0, j)), pl.BlockSpec((s_len, sw), lambda j, t: (0, t)),
                       pl.BlockSpec((None, None, sw, sw), lambda j, t: (j % 2, j // 2, o_wo // sw + t, 0)))[0]
    slab = weight_grad("dw_out_attn", slab, ya, d_am, (a // tmw, N_DEV),
                       pl.BlockSpec((s_len, tmw), lambda i, j: (0, i)), pl.BlockSpec((s_len, sw), lambda i, j: (0, j)),
                       pl.BlockSpec((None, None, tmw, sw), lambda i, j: (j % 2, j // 2, i, 0)))[0]
    slab = weight_grad("dw_out_pool", slab, yp, d_pm, (p // tmw, N_DEV),
                       pl.BlockSpec((s_len, tmw), lambda i, j: (0, i)), pl.BlockSpec((s_len, sw), lambda i, j: (0, j)),
                       pl.BlockSpec((None, None, tmw, sw), lambda i, j: (j % 2, j // 2, o_wop // tmw + i, 0)))[0]

    (dproj, dk, dv, ddiag), (from_sibling_a,) = _attn_bwd(proj, datt, base, a, dproj, comm=_cores_comm(slab))
    chip_part_a = _add_core_partials("add_core_partials_a", slab, from_sibling_a, core, (2 * d + cg) // 2)
    drb = _bias_grad(ddiag.reshape(heads, SKEW))
    dproj = lax.dynamic_update_slice(dproj, dk, (0, a))
    dproj = lax.dynamic_update_slice(dproj, dv, (0, 2 * a))

    tmd = _tile(d // 2, 1024)
    nrb = d // 2 // tmd

    def dw_in_rows(name, half, comm):
        return _mm(
            name, (nrb, n_in // sw),
            [(hb, pl.BlockSpec((s_len, tmd), lambda i, t: (0, half * nrb + i))),
             (dproj, pl.BlockSpec((s_len, sw), lambda i, t: (0, pos(t))))],
            [(slab_b, pl.BlockSpec((None, None, tmd, sw),
                                   lambda i, t: ((t // per) % 2, (t // per) // 2, (t % per) * nrb + i, 0)))],
            TN, into_slab, comm=comm)

    slab_lo, parts_a = dw_in_rows("dw_in_lo", 0, _chips_comm(chip_part_a))
    slab_hi, from_sibling_lo = dw_in_rows("dw_in_hi", 1, _cores_comm(slab_lo))
    chip_part_lo = _add_core_partials("add_core_partials_lo", slab_lo, from_sibling_lo, core, 4 * sw)

    tk = w_in.shape[1]
    tnh = _tile(d, 1024)
    tmh = tm if s_len > tm else s_len // 2
    n_row = s_len // tmh

    def dh_rows(name, lo, hi, prev, comm):
        def body(in_refs, out_refs, scratch_refs):
            acc_ref = scratch_refs[0]
            k = pl.program_id(2)
            part = lax.dot_general(in_refs[0][...], in_refs[per][...], NT, preferred_element_type=F32)
            for r in range(1, per):
                part += lax.dot_general(in_refs[r][...], in_refs[per + r][...], NT, preferred_element_type=F32)

            @pl.when(k == 0)
            def _():
                acc_ref[...] = part

            @pl.when(k > 0)
            def _():
                acc_ref[...] += part

            @pl.when(k == N_DEV - 1)
            def _():
                out_refs[0][...] = acc_ref[...]

        ins = [(dproj, pl.BlockSpec((tmh, sw), lambda i, j, k, r=r: (lo + i, pos(per * k + r)))) for r in range(per)]
        ins += [(w, pl.BlockSpec((None, tnh, sw), lambda i, j, k: (k, j, 0))) for w in win_rounds]
        if prev is not None:
            ins.append((prev, hbm))
        return _call(name, body, (hi - lo, d // tnh, N_DEV), ins,
                     [(_sds((s_len, d), F32), pl.BlockSpec((tmh, tnh), lambda i, j, k: (lo + i, j)))],
                     scratch=[pltpu.VMEM((tmh, tnh), F32)], aliases={2 * per: 0} if prev is not None else None,
                     comm=comm)

    dh, parts_lo, from_sibling_hi = dh_rows("dh_head", 0, n_row // 2, None,
                                            _both(_chips_comm(chip_part_lo), _cores_comm(slab_hi)))
    chip_part_hi = _add_core_partials("add_core_partials_hi", slab_hi, from_sibling_hi, core, 4 * sw)
    dh, parts_hi = dh_rows("dh_rest", n_row // 2, n_row, dh, _chips_comm(chip_part_hi))

    dx, dg1 = _rms_bwd(x2d, dh, dx2, g1)

    g_win, d_win, m_win, v_win = _adamw_row_halves("adamw_w_in", parts_lo, parts_hi, w_in, m_w_in, v_w_in)
    g_woa, d_woa, m_woa, v_woa = _adamw_shard("adamw_w_out_attn", parts_a, w_out_attn, m_w_out_attn, v_w_out_attn, 0)
    g_wop, d_wop, m_wop, v_wop = _adamw_shard("adamw_w_out_pool", parts_a, w_out_pool, m_w_out_pool, v_w_out_pool, o_wop)
    g_wo, d_wo, m_wo, v_wo = _adamw_shard("adamw_w_out", parts_a, w_out, m_w_out, v_w_out, o_wo)
    flat = lambda t: t.reshape(cg // 2, sw)
    pool_out = _adamw_shard("adamw_pool_w", parts_a, flat(pool_w), flat(m_pool_w), flat(v_pool_w), o_pool)
    g_pw, d_pw, m_pw, v_pw = [t.reshape(pool_w.shape) for t in pool_out]
    pad16 = lambda t: jnp.pad(t, ((0, 14), (0, 0)))
    gb_out = _adamw_shard("adamw_gate_bias", parts_a, pad16(gate_bias), pad16(m_gate_bias), pad16(v_gate_bias),
                          o_pool + cg // 2)
    g_gb, d_gb, m_gb, v_gb = [t[:2] for t in gb_out]

    def pack(n_gain, f_gain, scale, rb, last):
        rows = [n_gain.reshape(-1, LANES), f_gain.reshape(-1, LANES), scale.reshape(-1, LANES),
                rb.reshape(-1, LANES), last]
        return jnp.concatenate(rows, axis=0)

    pad_rb = lambda t: jnp.pad(t, ((0, 0), (0, N_REL_PAD - N_REL)))
    zeros8 = jnp.zeros((8, LANES), F32)
    loss_rows = jnp.pad(loss_part, ((0, 7), (0, 0)))
    small = _small_allreduce_adamw(
        pack(dg1, dg2, dps, drb, loss_rows),
        pack(norm_gain, final_gain, pool_scale, pad_rb(rel_bias), zeros8),
        pack(m_norm_gain, m_final_gain, m_pool_scale, pad_rb(m_rel_bias), zeros8),
        pack(v_norm_gain, v_final_gain, v_pool_scale, pad_rb(v_rel_bias), zeros8))

    n1, n2, n3 = d // LANES, 2 * d // LANES, (2 * d + p) // LANES
    n4 = n3 + heads * N_REL_PAD // LANES

    def unpack(t):
        return (t[:n1].reshape(d), t[n1:n2].reshape(d), t[n2:n3].reshape(p),
                t[n3:n4].reshape(heads, N_REL_PAD)[:, :N_REL])

    (g_ng, g_fg, g_ps, g_rb), (d_ng, d_fg, d_ps, d_rb), (m_ng, m_fg, m_ps, m_rb), (v_ng, v_fg, v_ps, v_rb) = [
        unpack(t) for t in small]
    loss = small[0][n4, 0]

    return (loss, dx.reshape(x.shape),
            g_ng, g_win, g_rb, g_pw, g_ps, g_woa, g_wop, g_gb, g_wo, g_fg,
            d_ng, d_win, d_rb, d_pw, d_ps, d_woa, d_wop, d_gb, d_wo, d_fg,
            m_ng, m_win, m_rb, m_pw, m_ps, m_woa, m_wop, m_gb, m_wo, m_fg,
            v_ng, v_win, v_rb, v_pw, v_ps, v_woa, v_wop, v_gb, v_wo, v_fg)
```

```python
import jax
import jax.numpy as jnp
from jax import lax
from jax.experimental import pallas as pl
from jax.experimental.pallas import tpu as pltpu

F32 = jnp.float32
BF = jnp.bfloat16
MESH = pl.DeviceIdType.MESH

N_DEV = 8
CHUNK = 64
N_LEFT_CHUNKS = 8
HEAD_DIM = 128
MAX_REL = 128
N_REL = 2 * MAX_REL + 1
N_REL_PAD = 384
POOL_WINDOWS = (2, 4, 8, 16)
HALO = 16
EPS = 1e-6
ADAM_LR = 0.001
ADAM_B1 = 0.9
ADAM_B2 = 0.999
ADAM_EPS = 1e-08
ADAM_WD = 0.01
ADAM_STEP = 10
NEG = -1e30
LANES = 128
TQ = N_LEFT_CHUNKS * CHUNK
TK = 2 * TQ
SKEW = 2 * TK
VMEM_LIMIT = 52 * 1024 * 1024

NN = (((1,), (0,)), ((), ()))
NT = (((1,), (1,)), ((), ()))
TN = (((0,), (0,)), ((), ()))


def _params(n_grid):
    return pltpu.CompilerParams(dimension_semantics=("arbitrary",) * n_grid, vmem_limit_bytes=VMEM_LIMIT)


def _sig(z):
    return 1.0 / (1.0 + jnp.exp(-z))


def _silu_and_grad(z):
    s = _sig(z)
    return z * s, s * (1.0 + z * (1.0 - s))


def _tile(n, pref):
    t = min(n, pref)
    assert n % t == 0, (n, pref)
    return t


def _sds(shape, dtype):
    return jax.ShapeDtypeStruct(shape, dtype)


class _Comm:
    def __init__(self, ins, outs, scratch, start, wait, hooks=()):
        self.ins, self.outs, self.scratch = list(ins), list(outs), list(scratch)
        self.start, self.wait, self.hooks = start, wait, tuple(hooks)


def _call(name, body, grid, ins, outs, scratch=(), aliases=None, comm=None):
    n_in, n_out, n_scr = len(ins), len(outs), len(scratch)
    c_in = len(comm.ins) if comm else 0
    c_out = len(comm.outs) if comm else 0
    n_steps = 1
    for g in grid:
        n_steps *= g

    def kern(*refs):
        o0 = n_in + c_in
        s0 = o0 + n_out + c_out
        if comm:
            c_refs = (refs[n_in:o0], refs[o0 + n_out:s0], refs[s0 + n_scr:])
            step = pl.program_id(0)
            for ax in range(1, len(grid)):
                step = step * grid[ax] + pl.program_id(ax)

            @pl.when(step == 0)
            def _():
                comm.start(*c_refs)

            for frac, hook in comm.hooks:
                @pl.when(step == int(frac * n_steps))
                def _(hook=hook):
                    hook(*c_refs)

        body(refs[:n_in], refs[o0:o0 + n_out], refs[s0:s0 + n_scr])

        if comm:
            @pl.when(step == n_steps - 1)
            def _():
                comm.wait(*c_refs)

    hbm = pl.BlockSpec(memory_space=pl.ANY)
    return pl.pallas_call(
        kern, grid=grid,
        in_specs=[s for _, s in ins] + [hbm] * c_in, out_specs=[s for _, s in outs] + [hbm] * c_out,
        out_shape=[o for o, _ in outs] + (comm.outs if comm else []),
        scratch_shapes=list(scratch) + (comm.scratch if comm else []),
        name=name, compiler_params=_params(len(grid)), input_output_aliases=aliases or {},
    )(*([a for a, _ in ins] + (comm.ins if comm else [])))


def _mm(name, grid, ins, outs, dims, epi, aliases=None, comm=None, nb=1):
    def body(in_refs, out_refs, _):
        if nb == 1:
            acc = lax.dot_general(in_refs[0][...], in_refs[1][...], dims, preferred_element_type=F32)
        else:
            kq = in_refs[0].shape[1] // nb
            acc = sum(lax.dot_general(in_refs[0][:, q * kq:(q + 1) * kq], in_refs[1 + q][...], dims,
                                      preferred_element_type=F32) for q in range(nb))
        epi(acc, in_refs[1 + nb:], out_refs)

    return _call(name, body, grid, ins, outs, aliases=aliases, comm=comm)


def _rms_fwd(x, g, comm=None):
    s, d = x.shape
    tr = _tile(s, 256)

    def body(in_refs, out_refs, _):
        xv = in_refs[0][...]
        r = lax.rsqrt(jnp.mean(xv * xv, axis=-1, keepdims=True) + EPS)
        out_refs[0][...] = (xv * r * in_refs[1][...]).astype(BF)

    row = pl.BlockSpec((tr, d), lambda i: (i, 0))
    return _call("rms_fwd", body, (s // tr,), [(x, row), (g, pl.BlockSpec((1, d), lambda i: (0, 0)))],
                 [(_sds((s, d), BF), row)], comm=comm)


def _final_norm(x2, target, g):
    s, d = x2.shape
    tr = _tile(s, 128)

    def kern(x_ref, t_ref, g_ref, dx_ref, dxb_ref, dg_ref, loss_ref):
        i = pl.program_id(0)
        xv = x_ref[...]
        gv = g_ref[...]
        r = lax.rsqrt(jnp.mean(xv * xv, axis=-1, keepdims=True) + EPS)
        xhat = xv * r
        err = xhat * gv - t_ref[...]
        dy = err * (1.0 / d)
        gy = dy * gv
        dx = r * (gy - xhat * jnp.mean(gy * xhat, axis=-1, keepdims=True))
        dx_ref[...] = dx
        dxb_ref[...] = dx.astype(BF)
        dg = jnp.sum(dy * xhat, axis=0, keepdims=True)
        ls = jnp.broadcast_to(0.5 * jnp.sum(jnp.mean(err * err, axis=-1, keepdims=True)), (1, LANES))

        @pl.when(i == 0)
        def _():
            dg_ref[...] = dg
            loss_ref[...] = ls

        @pl.when(i > 0)
        def _():
            dg_ref[...] += dg
            loss_ref[...] += ls

    row = pl.BlockSpec((tr, d), lambda i: (i, 0))
    vec = pl.BlockSpec((1, d), lambda i: (0, 0))
    return pl.pallas_call(
        kern, grid=(s // tr,), in_specs=[row, row, vec],
        out_specs=[row, row, vec, pl.BlockSpec((1, LANES), lambda i: (0, 0))],
        out_shape=[_sds((s, d), F32), _sds((s, d), BF), _sds((1, d), F32), _sds((1, LANES), F32)],
        name="final_norm", compiler_params=_params(1))(x2, target, g)


def _rms_bwd(x, dh, dx2, g):
    s, d = x.shape
    tr = _tile(s, 128)

    def kern(x_ref, dh_ref, dx2_ref, g_ref, dx_ref, dg_ref):
        i = pl.program_id(0)
        xv = x_ref[...]
        r = lax.rsqrt(jnp.mean(xv * xv, axis=-1, keepdims=True) + EPS)
        xhat = xv * r
        dhv = dh_ref[...]
        gh = dhv * g_ref[...]
        dx_ref[...] = dx2_ref[...] + r * (gh - xhat * jnp.mean(gh * xhat, axis=-1, keepdims=True))
        dg = jnp.sum(dhv * xhat, axis=0, keepdims=True)

        @pl.when(i == 0)
        def _():
            dg_ref[...] = dg

        @pl.when(i > 0)
        def _():
            dg_ref[...] += dg

    row = pl.BlockSpec((tr, d), lambda i: (i, 0))
    vec = pl.BlockSpec((1, d), lambda i: (0, 0))
    return pl.pallas_call(
        kern, grid=(s // tr,), in_specs=[row, row, row, vec], out_specs=[row, vec],
        out_shape=[_sds((s, d), F32), _sds((1, d), F32)],
        name="rms_bwd", compiler_params=_params(1))(x, dh, dx2, g)


def _rel_index(j, backward):
    if backward:
        rel = 2 * TQ - 1 - j
    else:
        rel = TQ - jnp.where(j < TK, j, j - SKEW)
    return jnp.clip(rel, -MAX_REL, MAX_REL) + MAX_REL


def _bias_rows(rel_bias_pad):
    h = rel_bias_pad.shape[0]

    def kern(rb_ref, o_ref):
        j = lax.broadcasted_iota(jnp.int32, (N_REL_PAD, SKEW), 1)
        k = lax.broadcasted_iota(jnp.int32, (N_REL_PAD, SKEW), 0)
        onehot = (_rel_index(j, False) == k).astype(F32)
        o_ref[...] = jnp.dot(rb_ref[...], onehot, preferred_element_type=F32, precision=lax.Precision.HIGHEST)

    return pl.pallas_call(kern, out_shape=_sds((h, SKEW), F32), name="bias_rows")(rel_bias_pad)


def _bias_grad(ddiag):
    h = ddiag.shape[0]

    def kern(d_ref, o_ref):
        j = lax.broadcasted_iota(jnp.int32, (N_REL_PAD, SKEW), 1)
        k = lax.broadcasted_iota(jnp.int32, (N_REL_PAD, SKEW), 0)
        onehot = ((_rel_index(j, True) == k) & (j < TQ + TK - 1)).astype(F32)
        o_ref[...] = lax.dot_general(d_ref[...], onehot, NT, preferred_element_type=F32,
                                     precision=lax.Precision.HIGHEST)

    return pl.pallas_call(kern, out_shape=_sds((h, N_REL_PAD), F32), name="bias_grad")(ddiag)


def _bias_tile(row, first):
    t = pltpu.roll(jnp.broadcast_to(row, (TQ, SKEW)), 0, 1, stride=1, stride_axis=0)[:, :TK]
    r = lax.broadcasted_iota(jnp.int32, (TQ, TK), 0) // CHUNK
    col = lax.broadcasted_iota(jnp.int32, (TQ, TK), 1)
    dist = N_LEFT_CHUNKS + r - col // CHUNK
    keep = (dist >= 0) & (dist <= N_LEFT_CHUNKS) & jnp.logical_not(first & (col < TQ))
    return jnp.where(keep, t, NEG)


def _scores(q, keys, tile):
    s = lax.dot_general(q, keys, NT, preferred_element_type=F32) * (HEAD_DIM ** -0.5) + tile
    m = jnp.max(s, axis=1, keepdims=True)
    p = jnp.exp(s - m)
    return p, jnp.sum(p, axis=1, keepdims=True)


def _attn_fwd(proj, base, a_width, comm=None):
    s_len = proj.shape[0]
    heads = a_width // HEAD_DIM
    nq = s_len // TQ
    kb, vb, zb = heads, 2 * heads, 3 * heads

    def kern(q_ref, kp_ref, kc_ref, vp_ref, vc_ref, z_ref, base_ref, att_ref, ya_ref, tile_ref):
        i = pl.program_id(1)

        @pl.when(i <= 1)
        def _():
            tile_ref[...] = _bias_tile(base_ref[...], i == 0)

        kcat = jnp.concatenate([kp_ref[...], kc_ref[...]], axis=0)
        vcat = jnp.concatenate([vp_ref[...], vc_ref[...]], axis=0)
        p, l = _scores(q_ref[...], kcat, tile_ref[...])
        o = jnp.dot(p.astype(BF), vcat, preferred_element_type=F32) / l
        att_ref[...] = o.astype(BF)
        z = z_ref[...].astype(F32)
        ya_ref[...] = (o * (z * _sig(z))).astype(BF)

    blk = lambda off: pl.BlockSpec((TQ, HEAD_DIM), lambda h, i: (i, off + h))
    prev = lambda off: pl.BlockSpec((TQ, HEAD_DIM), lambda h, i: (jnp.maximum(i - 1, 0), off + h))
    out = pl.BlockSpec((TQ, HEAD_DIM), lambda h, i: (i, h))
    def body(in_refs, out_refs, scratch_refs):
        kern(*in_refs, *out_refs, *scratch_refs)

    act = _sds((s_len, a_width), BF)
    res = _call(
        "attn_fwd", body, (heads, nq),
        [(proj, blk(0)), (proj, prev(kb)), (proj, blk(kb)), (proj, prev(vb)), (proj, blk(vb)), (proj, blk(zb)),
         (base, pl.BlockSpec((None, 1, SKEW), lambda h, i: (h, 0, 0)))],
        [(act, out), (act, out)], scratch=[pltpu.VMEM((TQ, TK), F32)], comm=comm)
    return res[:2], res[2:]


def _attn_bwd(proj, datt, base, a_width, dproj, comm=None):
    s_len = proj.shape[0]
    heads = a_width // HEAD_DIM
    nq = s_len // TQ
    kb, vb = heads, 2 * heads
    scale = HEAD_DIM ** -0.5

    def body(in_refs, out_refs, scratch_refs):
        q_ref, kp_ref, kc_ref, vp_ref, vc_ref, do_ref, base_ref, _ = in_refs
        dq_ref, dk_ref, dv_ref, dd_ref = out_refs
        tile_ref, dsacc_ref, ak_ref, av_ref = scratch_refs
        i = pl.program_id(1)

        @pl.when(i <= 1)
        def _():
            tile_ref[...] = _bias_tile(base_ref[...], i == 0)

        @pl.when(i == 0)
        def _():
            dsacc_ref[...] = jnp.zeros_like(dsacc_ref)
            ak_ref[...] = jnp.zeros_like(ak_ref)
            av_ref[...] = jnp.zeros_like(av_ref)

        @pl.when(i < nq)
        def _():
            q = q_ref[...]
            do = do_ref[...]
            kcat = jnp.concatenate([kp_ref[...], kc_ref[...]], axis=0)
            vcat = jnp.concatenate([vp_ref[...], vc_ref[...]], axis=0)
            p, l = _scores(q, kcat, tile_ref[...])
            p = p / l
            dp = lax.dot_general(do, vcat, NT, preferred_element_type=F32)
            ds = p * (dp - jnp.sum(p * dp, axis=1, keepdims=True))
            dsacc_ref[...] += ds
            dsb = ds.astype(BF)
            dq_ref[...] = (jnp.dot(dsb, kcat, preferred_element_type=F32) * scale).astype(BF)
            dkc = lax.dot_general(dsb, q, TN, preferred_element_type=F32) * scale
            dvc = lax.dot_general(p.astype(BF), do, TN, preferred_element_type=F32)
            dk_ref[...] = (ak_ref[...] + dkc[:TQ]).astype(BF)
            dv_ref[...] = (av_ref[...] + dvc[:TQ]).astype(BF)
            ak_ref[...] = dkc[TQ:]
            av_ref[...] = dvc[TQ:]

        @pl.when(i == nq)
        def _():
            dk_ref[...] = ak_ref[...].astype(BF)
            dv_ref[...] = av_ref[...].astype(BF)
            acc = dsacc_ref[...]
            rr = lax.broadcasted_iota(jnp.int32, (TQ, TQ), 0)
            cc = lax.broadcasted_iota(jnp.int32, (TQ, TQ), 1)
            flip = (rr + cc == TQ - 1).astype(BF)
            hi = acc.astype(BF)
            lo = (acc - hi.astype(F32)).astype(BF)
            rev = jnp.dot(flip, hi, preferred_element_type=F32) + jnp.dot(flip, lo, preferred_element_type=F32)
            wide = jnp.concatenate([rev, jnp.zeros((TQ, SKEW - TK), F32)], axis=1)
            dd_ref[...] = jnp.sum(pltpu.roll(wide, 0, 1, stride=1, stride_axis=0), axis=0, keepdims=True)

    last = nq - 1
    cur = lambda off: pl.BlockSpec((TQ, HEAD_DIM), lambda h, i: (jnp.minimum(i, last), off + h))
    prev = lambda off: pl.BlockSpec((TQ, HEAD_DIM), lambda h, i: (jnp.maximum(jnp.minimum(i, last) - 1, 0), off + h))
    done = pl.BlockSpec((TQ, HEAD_DIM), lambda h, i: (jnp.maximum(i - 1, 0), h))
    row = pl.BlockSpec((None, 1, SKEW), lambda h, i: (h, 0, 0))
    act = _sds((s_len, a_width), BF)
    res = _call(
        "attn_bwd", body, (heads, nq + 1),
        [(proj, cur(0)), (proj, prev(kb)), (proj, cur(kb)), (proj, prev(vb)), (proj, cur(vb)), (datt, cur(0)),
         (base, row), (dproj, pl.BlockSpec(memory_space=pl.ANY))],
        [(_sds(dproj.shape, dproj.dtype), cur(0)), (act, done), (act, done), (_sds((heads, 1, SKEW), F32), row)],
        aliases={7: 0},
        scratch=[pltpu.VMEM((TQ, TK), F32), pltpu.VMEM((TQ, TK), F32),
                 pltpu.VMEM((TQ, HEAD_DIM), F32), pltpu.VMEM((TQ, HEAD_DIM), F32)],
        comm=comm)
    return res[:4], res[4:]


def _pool_fwd(proj, pool_w, pool_scale, p_width, u_blk, z_blk):
    s_len = proj.shape[0]
    cg = p_width // len(POOL_WINDOWS)
    tt = _tile(s_len, 512)

    def kern(up_ref, uc_ref, z_ref, pw_ref, sc_ref, d_ref, y_ref, yp_ref):
        t = pl.program_id(0)
        row = lax.broadcasted_iota(jnp.int32, (tt, 1), 0) + t * tt
        for g, w in enumerate(POOL_WINDOWS):
            cs = slice(g * cg, (g + 1) * cg)
            prev = jnp.where(t == 0, 0.0, up_ref[:, cs].astype(F32))
            cur = uc_ref[:, cs].astype(F32)
            ws = jnp.concatenate([prev, cur], axis=0)
            sh = 1
            while sh < w:
                ws = ws + pltpu.roll(ws, sh, 0)
                sh *= 2
            cnt = jnp.minimum(row + 1, w).astype(F32)
            db = (ws[HALO:, :] / cnt - cur).astype(BF)
            y = jnp.dot(db, pw_ref[g], preferred_element_type=F32)
            d_ref[:, cs] = db
            y_ref[:, cs] = y.astype(BF)
            z = z_ref[:, cs].astype(F32)
            yp_ref[:, cs] = (y * sc_ref[:, cs] * (z * _sig(z))).astype(BF)

    full = pl.BlockSpec((tt, p_width), lambda t: (t, 0))
    return pl.pallas_call(
        kern, grid=(s_len // tt,),
        in_specs=[pl.BlockSpec((HALO, p_width), lambda t: (jnp.maximum(t * (tt // HALO) - 1, 0), u_blk)),
                  pl.BlockSpec((tt, p_width), lambda t: (t, u_blk)),
                  pl.BlockSpec((tt, p_width), lambda t: (t, z_blk)),
                  pl.BlockSpec((len(POOL_WINDOWS), cg, cg), lambda t: (0, 0, 0)),
                  pl.BlockSpec((1, p_width), lambda t: (0, 0))],
        out_specs=[full, full, full], out_shape=[_sds((s_len, p_width), BF)] * 3,
        name="pool_fwd", compiler_params=_params(1))(proj, proj, proj, pool_w, pool_scale)


def _pool_bwd(dy, dmean, pool_w, dproj, u_blk):
    s_len, p_width = dy.shape
    ng = len(POOL_WINDOWS)
    cg = p_width // ng
    tt = _tile(s_len, 512)
    nt = s_len // tt

    def kern(dyc_ref, dyn_ref, d_ref, pw_ref, _, du_ref, dpw_ref):
        t = pl.program_id(0)

        @pl.when(t == 0)
        def _():
            dpw_ref[...] = jnp.zeros_like(dpw_ref)

        row = lax.broadcasted_iota(jnp.int32, (tt + HALO, 1), 0) + t * tt
        for g, w in enumerate(POOL_WINDOWS):
            cs = slice(g * cg, (g + 1) * cg)
            dyc = dyc_ref[:, cs]
            ddc = lax.dot_general(dyc, pw_ref[g], NT, preferred_element_type=F32)
            ddn = lax.dot_general(dyn_ref[:, cs], pw_ref[g], NT, preferred_element_type=F32)
            ddn = jnp.where(t == nt - 1, 0.0, ddn)
            cnt = jnp.minimum(row + 1, w).astype(F32)
            ws = jnp.concatenate([ddc, ddn], axis=0) / cnt
            sh = 1
            while sh < w:
                ws = ws + pltpu.roll(ws, tt + HALO - sh, 0)
                sh *= 2
            du_ref[:, cs] = (ws[:tt, :] - ddc).astype(BF)
            dpw_ref[g] += lax.dot_general(d_ref[:, cs], dyc, TN, preferred_element_type=F32)

    full = pl.BlockSpec((tt, p_width), lambda t: (t, 0))
    pw_spec = pl.BlockSpec((ng, cg, cg), lambda t: (0, 0, 0))
    return pl.pallas_call(
        kern, grid=(nt,),
        in_specs=[full,
                  pl.BlockSpec((HALO, p_width), lambda t: (jnp.minimum((t + 1) * (tt // HALO), s_len // HALO - 1), 0)),
                  full, pw_spec, pl.BlockSpec(memory_space=pl.ANY)],
        out_specs=[pl.BlockSpec((tt, p_width), lambda t: (t, u_blk)), pw_spec],
        out_shape=[_sds(dproj.shape, dproj.dtype), _sds((ng, cg, cg), F32)],
        input_output_aliases={4: 0},
        name="pool_bwd", compiler_params=_params(1))(dy, dy, dmean, pool_w, dproj)


def _adam(g, w_ref, m_ref, v_ref, g_out, d_out, m_out, v_out):
    m = ADAM_B1 * m_ref[...] + (1.0 - ADAM_B1) * g
    v = ADAM_B2 * v_ref[...] + (1.0 - ADAM_B2) * (g * g)
    m_hat = m / (1.0 - ADAM_B1 ** ADAM_STEP)
    v_hat = v / (1.0 - ADAM_B2 ** ADAM_STEP)
    g_out[...] = g
    d_out[...] = -ADAM_LR * (m_hat / (jnp.sqrt(v_hat) + ADAM_EPS) + ADAM_WD * w_ref[...])
    m_out[...] = m
    v_out[...] = v


def _adamw_shard(name, parts, w, m, v, row_off):
    rw, cw = w.shape
    sw = parts.shape[2]
    tr = _tile(rw, 512)
    assert row_off % tr == 0 and cw % sw == 0

    def kern(b_ref, w_ref, m_ref, v_ref, g_out, d_out, m_out, v_out):
        b = b_ref[...].astype(F32)
        _adam(((b[0] + b[1]) + b[2]) + b[3], w_ref, m_ref, v_ref, g_out, d_out, m_out, v_out)

    blk = pl.BlockSpec((tr, sw), lambda ct, i: (i, ct))
    return pl.pallas_call(
        kern, grid=(cw // sw, rw // tr),
        in_specs=[pl.BlockSpec((4, tr, sw), lambda ct, i: (0, (row_off + ct * rw) // tr + i, 0)), blk, blk, blk],
        out_specs=[blk] * 4, out_shape=[_sds((rw, cw), F32)] * 4,
        name=name, compiler_params=_params(2))(parts, w, m, v)


def _adamw_row_halves(name, parts_lo, parts_hi, w, m, v):
    rw, cw = w.shape
    sw = parts_lo.shape[2]
    half = rw // 2
    tr = _tile(half, 512)
    nh = half // tr

    def kern(lo_ref, hi_ref, w_ref, m_ref, v_ref, g_out, d_out, m_out, v_out):
        i = pl.program_id(1)

        def update(b_ref):
            b = b_ref[...].astype(F32)
            _adam(((b[0] + b[1]) + b[2]) + b[3], w_ref, m_ref, v_ref, g_out, d_out, m_out, v_out)

        @pl.when(i < nh)
        def _():
            update(lo_ref)

        @pl.when(i >= nh)
        def _():
            update(hi_ref)

    blk = pl.BlockSpec((tr, sw), lambda ct, i: (i, ct))
    return pl.pallas_call(
        kern, grid=(cw // sw, rw // tr),
        in_specs=[pl.BlockSpec((4, tr, sw), lambda ct, i: (0, ct * nh + jnp.minimum(i, nh - 1), 0)),
                  pl.BlockSpec((4, tr, sw), lambda ct, i: (0, ct * nh + jnp.maximum(i - nh, 0), 0)), blk, blk, blk],
        out_specs=[blk] * 4, out_shape=[_sds((rw, cw), F32)] * 4,
        name=name, compiler_params=_params(2))(parts_lo, parts_hi, w, m, v)


def _both(c1, c2):
    n_in, n_out, n_sem = len(c1.ins), len(c1.outs), len(c1.scratch)

    def split(ins, outs, sems):
        return (ins[:n_in], outs[:n_out], sems[:n_sem]), (ins[n_in:], outs[n_out:], sems[n_sem:])

    def start(*refs):
        r1, r2 = split(*refs)
        c1.start(*r1)
        c2.start(*r2)

    def wait(*refs):
        r1, r2 = split(*refs)
        c1.wait(*r1)
        c2.wait(*r2)

    def of(which, hook):
        return lambda *refs: hook(*split(*refs)[which])

    hooks = [(f, of(0, h)) for f, h in c1.hooks] + [(f, of(1, h)) for f, h in c2.hooks]
    return _Comm(c1.ins + c2.ins, c1.outs + c2.outs, c1.scratch + c2.scratch, start, wait, hooks)


def _position():
    return lax.axis_index("x"), lax.axis_index("y"), lax.axis_index("c")


def _gather_comm(shards, cols=None, pass_at=None):
    if cols is None:
        pieces = [(a, None) for a in range(len(shards))]
        shapes = [_sds(s.shape, s.dtype) for s in shards]
    else:
        half = shards[0].shape[0] // 2
        pieces = [(0, pl.ds(0, half)), (0, pl.ds(half, half))]
        shapes = [_sds((shards[0].shape[0], cols[1]), shards[0].dtype)]
    n = len(pieces)

    def plan(xs, outs, sems, only=None):
        send_sems, recv_sems, local_sems = sems
        x, y, c = _position()
        me, sibling = (x, y, c), (x, y, 1 - c)
        chips = [(1 - x, y), (x, 1 - y), (1 - x, 1 - y)]
        which = range(n) if only is None else only

        def source(v):
            a, rows = pieces[v]
            return xs[a] if rows is None else xs[a].at[rows, pl.ds(cols[0], cols[1])]

        def landing(v, block):
            a, rows = pieces[v]
            dst = outs[a].at[4 * block[0] + 2 * block[1] + block[2]]
            return dst if rows is None else dst.at[rows, :]

        def copy(v, k, block, to, own=False):
            dst = landing(v, block)
            return pltpu.make_async_remote_copy(
                src_ref=source(v) if own else dst, dst_ref=dst,
                send_sem=send_sems.at[7 * v + k], recv_sem=recv_sems.at[7 * v + k],
                device_id=to, device_id_type=MESH)

        by_chip = [(j, chip, v) for v in which for j, chip in enumerate(chips)]
        return dict(
            mine=lambda: [pltpu.make_async_copy(source(v), landing(v, me), local_sems.at[v]) for v in which],
            first=lambda: ([copy(v, 0, me, sibling, own=True) for v in which]
                           + [copy(v, 1 + j, me, (*chip, c), own=True) for j, chip, v in by_chip]),
            landed=lambda: [copy(v, 1 + j, (*chip, c), me) for j, chip, v in by_chip],
            passed=lambda: [copy(v, 4 + j, (*chip, c), sibling) for j, chip, v in by_chip],
            rest=lambda: ([copy(v, 0, sibling, me) for v in which]
                          + [copy(v, 4 + j, (*chip, 1 - c), me) for j, chip, v in by_chip]))

    def start(*refs):
        p = plan(*refs)
        for cp in p["mine"]() + p["first"]():
            cp.start()

    def pass_on(only):
        def hook(*refs):
            p = plan(*refs, only=only)
            for arrived, onward in zip(p["landed"](), p["passed"]()):
                arrived.wait_recv()
                onward.start()
        return hook

    in_halves = cols is not None and pass_at is None

    def wait(*refs):
        if in_halves:
            pass_on([1])(*refs)
        p = plan(*refs)
        for cp in p["rest"]():
            cp.wait_recv()
        for cp in p["first"]() + p["passed"]():
            cp.wait_send()
        for cp in p["mine"]():
            cp.wait()

    hooks = [(0.5, pass_on([0]))] if in_halves else [(0.75 if pass_at is None else pass_at, pass_on(None))]
    return _Comm(shards, [_sds((N_DEV,) + s.shape, s.dtype) for s in shapes],
                 [pltpu.SemaphoreType.DMA((7 * n,)), pltpu.SemaphoreType.DMA((7 * n,)),
                  pltpu.SemaphoreType.DMA((n,))], start, wait, hooks)


def _cores_comm(slab):
    _, _, r, sw = slab.shape

    def copies(ins, outs, sems):
        x, y, c = _position()
        return [pltpu.make_async_remote_copy(
            src_ref=ins[0].at[1 - c], dst_ref=outs[0], send_sem=sems[0], recv_sem=sems[1],
            device_id=(x, y, 1 - c), device_id_type=MESH)]

    def start(*refs):
        for cp in copies(*refs):
            cp.start()

    def wait(*refs):
        for cp in copies(*refs):
            cp.wait()

    return _Comm([slab], [_sds((4, r, sw), slab.dtype)],
                 [pltpu.SemaphoreType.DMA, pltpu.SemaphoreType.DMA], start, wait)


def _add_core_partials(name, slab, recv, core, tr):
    _, _, r, sw = slab.shape

    def kern(c_ref, a_ref, b_ref, o_ref):
        o_ref[...] = (a_ref[...].astype(F32) + b_ref[...].astype(F32)).astype(BF)

    return pl.pallas_call(
        kern,
        grid_spec=pltpu.PrefetchScalarGridSpec(
            num_scalar_prefetch=1, grid=(4, r // tr),
            in_specs=[pl.BlockSpec((None, None, tr, sw), lambda k, i, c_ref: (c_ref[0], k, i, 0)),
                      pl.BlockSpec((None, tr, sw), lambda k, i, c_ref: (k, i, 0))],
            out_specs=pl.BlockSpec((None, tr, sw), lambda k, i, c_ref: (k, i, 0))),
        out_shape=_sds((4, r, sw), BF), name=name, compiler_params=_params(2))(core, slab, recv)


def _chips_comm(part):
    _, r, sw = part.shape

    def copies(ins, outs, sems):
        send_sems, recv_sems, local_sem = sems
        x, y, c = _position()
        mine = 2 * x + y
        local = pltpu.make_async_copy(ins[0].at[mine], outs[0].at[mine], local_sem)
        chips = [(1 - x, y), (x, 1 - y), (1 - x, 1 - y)]
        remote = [pltpu.make_async_remote_copy(
            src_ref=ins[0].at[2 * px + py], dst_ref=outs[0].at[mine],
            send_sem=send_sems.at[j], recv_sem=recv_sems.at[j],
            device_id=(px, py, c), device_id_type=MESH) for j, (px, py) in enumerate(chips)]
        return [local] + remote

    def start(*refs):
        for cp in copies(*refs):
            cp.start()

    def wait(*refs):
        for cp in copies(*refs):
            cp.wait()

    return _Comm([part], [_sds((4, r, sw), part.dtype)],
                 [pltpu.SemaphoreType.DMA((3,)), pltpu.SemaphoreType.DMA((3,)), pltpu.SemaphoreType.DMA],
                 start, wait)


def _small_allreduce_adamw(partial, w, m, v):
    nr = partial.shape[0]

    def kern(p_ref, w_ref, m_ref, v_ref, g_out, d_out, m_out, v_out, gath_ref, send_sems, recv_sems):
        x, y, c = _position()
        me = 4 * x + 2 * y + c
        gath_ref[me] = p_ref[...]
        copies = []
        for mask in range(1, N_DEV):
            peer = (x ^ (mask >> 2), y ^ ((mask >> 1) & 1), c ^ (mask & 1))
            copies.append(pltpu.make_async_remote_copy(
                src_ref=p_ref, dst_ref=gath_ref.at[me],
                send_sem=send_sems.at[mask - 1], recv_sem=recv_sems.at[mask - 1],
                device_id=peer, device_id_type=MESH))
        for cp in copies:
            cp.start()
        for cp in copies:
            cp.wait()
        tot = gath_ref[0]
        for k in range(1, N_DEV):
            tot = tot + gath_ref[k]
        _adam(tot, w_ref, m_ref, v_ref, g_out, d_out, m_out, v_out)

    vmem = pl.BlockSpec(memory_space=pltpu.VMEM)
    return pl.pallas_call(
        kern, in_specs=[vmem] * 4, out_specs=[vmem] * 4, out_shape=[_sds((nr, LANES), F32)] * 4,
        scratch_shapes=[pltpu.VMEM((N_DEV, nr, LANES), F32),
                        pltpu.SemaphoreType.DMA((N_DEV - 1,)), pltpu.SemaphoreType.DMA((N_DEV - 1,))],
        name="small_allreduce_adamw")(partial, w, m, v)


def kernel(x, norm_gain, w_in, rel_bias, pool_w, pool_scale, w_out_attn, w_out_pool, gate_bias, w_out, final_gain, loss_target, m_norm_gain, m_w_in, m_rel_bias, m_pool_w, m_pool_scale, m_w_out_attn, m_w_out_pool, m_gate_bias, m_w_out, m_final_gain, v_norm_gain, v_w_in, v_rel_bias, v_pool_w, v_pool_scale, v_w_out_attn, v_w_out_pool, v_gate_bias, v_w_out, v_final_gain):
    _, s_len, d = x.shape
    a = w_out_attn.shape[0]
    p = w_out_pool.shape[0]
    heads = a // HEAD_DIM
    ng = len(POOL_WINDOWS)
    cg = p // ng
    sw = d // N_DEV
    n_in = w_in.shape[1] * N_DEV
    assert a == p and a + p == d and cg == sw and n_in == 5 * d and w_in.shape[1] == 5 * sw
    assert s_len % TQ == 0 and rel_bias.shape == (heads, N_REL)
    tm = _tile(s_len, 1024)
    x2d = x.reshape(s_len, d)
    tgt = loss_target.reshape(s_len, d)

    g1 = norm_gain.reshape(1, d)
    g2 = final_gain.reshape(1, d)
    scale_row = pool_scale.reshape(1, p)
    tn = sw
    per = w_in.shape[1] // tn
    hbm = pl.BlockSpec(memory_space=pl.ANY)

    w_bf = [w_in[:, r * tn:(r + 1) * tn].astype(BF) for r in range(per)]
    hb, landed = _rms_fwd(x2d, g1, comm=_gather_comm([w_bf[0]], cols=(0, tn)))
    win_rounds = []

    def store_bf16(acc, _, outs):
        outs[0][...] = acc.astype(BF)

    gate0, n_gate = (4 * a + 2 * p) // tn, d // tn

    def pos(t):
        g = t - gate0
        return jnp.where(t < gate0, t, jnp.where(g < n_gate, gate0 + 2 * g, gate0 + 2 * (g - n_gate) + 1))

    proj = None
    for r in range(per):
        win_rounds.append(landed)
        if r + 1 < per:
            comm = _gather_comm([w_bf[r + 1]], cols=(0, tn))
        else:
            comm = _gather_comm([w_out_attn.astype(BF), pool_w.astype(BF), gate_bias])
        ins = [(hb, pl.BlockSpec((tm, d), lambda i, j: (i, 0))),
               (landed, pl.BlockSpec((None, d, tn), lambda i, j: (j, 0, 0)))]
        if proj is not None:
            ins.append((proj, hbm))
        proj, landed, *rest = _mm(
            f"proj_{r}", (s_len // tm, N_DEV), ins,
            [(_sds((s_len, n_in), BF), pl.BlockSpec((tm, tn), lambda i, j, r=r: (i, pos(per * j + r))))],
            NN, store_bf16, aliases={2: 0} if r else None, comm=comm)
    woa_g = landed
    woa = woa_g.transpose(1, 0, 2).reshape(a, d)
    pw = rest[0].transpose(1, 0, 2, 3).reshape(ng, cg, cg)
    gb = rest[1].transpose(1, 0, 2).reshape(2, d)

    rb_pad = jnp.pad(rel_bias, ((0, 0), (0, N_REL_PAD - N_REL)))
    base = _bias_rows(rb_pad).reshape(heads, 1, SKEW)
    wo_bf = w_out.astype(BF)
    (att, ya), (wop_g, wo_lo_g) = _attn_fwd(
        proj, base, a, comm=_both(_gather_comm([w_out_pool.astype(BF)], pass_at=0.9),
                                  _gather_comm([wo_bf], cols=(0, d // 2), pass_at=0.9)))
    wop = wop_g.transpose(1, 0, 2).reshape(p, d)
    u_blk, z_blk = 4 * a // p, 4 * a // p + 1
    dmean, ypre, yp = _pool_fwd(proj, pw, scale_row, p, u_blk, z_blk)

    def gate_kernel(in_refs, out_refs, _):
        ya_ref, woa_ref, yp_ref, wop_ref, ga_ref, gp_ref, gb_ref = in_refs
        m_ref, a_ref, p_ref = out_refs
        am = jnp.dot(ya_ref[...], woa_ref[...], preferred_element_type=F32)
        pm = jnp.dot(yp_ref[...], wop_ref[...], preferred_element_type=F32)
        sa = _sig(ga_ref[...].astype(F32) + gb_ref[0:1, :])
        sp = _sig(gp_ref[...].astype(F32) + gb_ref[1:2, :])
        m_ref[...] = (sa * am + sp * pm).astype(BF)
        a_ref[...] = am.astype(BF)
        p_ref[...] = pm.astype(BF)

    tile_ij = pl.BlockSpec((tm, tn), lambda i, j: (i, j))
    act_d = _sds((s_len, d), BF)
    merged, am, pm, wo_hi_g = _call(
        "gate_merge", gate_kernel, (s_len // tm, d // tn),
        [(ya, pl.BlockSpec((tm, a), lambda i, j: (i, 0))), (woa_g, pl.BlockSpec((None, a, tn), lambda i, j: (j, 0, 0))),
         (yp, pl.BlockSpec((tm, p), lambda i, j: (i, 0))), (wop_g, pl.BlockSpec((None, p, tn), lambda i, j: (j, 0, 0))),
         (proj, pl.BlockSpec((tm, tn), lambda i, j: (i, gate0 + 2 * j))),
         (proj, pl.BlockSpec((tm, tn), lambda i, j: (i, gate0 + 2 * j + 1))),
         (gb, pl.BlockSpec((2, tn), lambda i, j: (0, j)))],
        [(act_d, tile_ij)] * 3, comm=_gather_comm([wo_bf], cols=(d // 2, d // 2), pass_at=0.9))
    wo_halves = [wo_lo_g.reshape(d, d // 2), wo_hi_g.reshape(d, d // 2)]

    def out_proj(in_refs, out_refs, _):
        m_ref, lo_ref, hi_ref, x_ref = in_refs
        j = pl.program_id(1)

        @pl.when(j < n_half)
        def _():
            out_refs[0][...] = x_ref[...] + jnp.dot(m_ref[...], lo_ref[...], preferred_element_type=F32)

        @pl.when(j >= n_half)
        def _():
            out_refs[0][...] = x_ref[...] + jnp.dot(m_ref[...], hi_ref[...], preferred_element_type=F32)

    n_half = d // 2 // tn
    x2 = _call("out_proj", out_proj, (s_len // tm, d // tn),
               [(merged, pl.BlockSpec((tm, d), lambda i, j: (i, 0))),
                (wo_halves[0], pl.BlockSpec((d, tn), lambda i, j: (0, jnp.minimum(j, n_half - 1)))),
                (wo_halves[1], pl.BlockSpec((d, tn), lambda i, j: (0, jnp.maximum(j - n_half, 0)))),
                (x2d, tile_ij)],
               [(_sds((s_len, d), F32), tile_ij)])[0]

    dx2, dx2b, dg2, loss_part = _final_norm(x2, tgt, g2)

    tmb = _tile(s_len, 512)
    tile_ji = pl.BlockSpec((tmb, tn), lambda j, i: (i, j))

    def gate_bwd(dm, ex, outs):
        a_ref, p_ref, ga_ref, gp_ref, gb_ref = ex
        da_ref, dp_ref, dgate_ref, dgb_ref = outs
        i = pl.program_id(1)
        sa = _sig(ga_ref[...].astype(F32) + gb_ref[0:1, :])
        sp = _sig(gp_ref[...].astype(F32) + gb_ref[1:2, :])
        dga = dm * a_ref[...].astype(F32) * sa * (1.0 - sa)
        dgp = dm * p_ref[...].astype(F32) * sp * (1.0 - sp)
        da_ref[...] = (dm * sa).astype(BF)
        dp_ref[...] = (dm * sp).astype(BF)
        dgate_ref[:, :tn] = dga.astype(BF)
        dgate_ref[:, tn:] = dgp.astype(BF)
        r = lax.broadcasted_iota(jnp.int32, (8, tn), 0)
        sums = jnp.where(r == 0, jnp.sum(dga, axis=0, keepdims=True),
                         jnp.where(r == 1, jnp.sum(dgp, axis=0, keepdims=True), 0.0))

        @pl.when(i == 0)
        def _():
            dgb_ref[...] = sums

        @pl.when(i > 0)
        def _():
            dgb_ref[...] += sums

    dproj_shape = _sds((s_len, n_in), BF)
    d_am, d_pm, dproj, dgb8 = _mm(
        "gate_bwd", (d // tn, s_len // tmb),
        [(dx2b, pl.BlockSpec((tmb, d), lambda j, i: (i, 0)))]
        + [(w, pl.BlockSpec((tn, d // 2), lambda j, i: (j, 0))) for w in wo_halves]
        + [(am, tile_ji), (pm, tile_ji),
           (proj, pl.BlockSpec((tmb, tn), lambda j, i: (i, gate0 + 2 * j))),
           (proj, pl.BlockSpec((tmb, tn), lambda j, i: (i, gate0 + 2 * j + 1))),
           (gb, pl.BlockSpec((2, tn), lambda j, i: (0, j)))],
        [(_sds((s_len, d), BF), tile_ji)] * 2
        + [(dproj_shape, pl.BlockSpec((tmb, 2 * tn), lambda j, i: (i, gate0 // 2 + j))),
           (_sds((8, d), F32), pl.BlockSpec((8, tn), lambda j, i: (0, j)))],
        NT, gate_bwd, nb=2)

    za_t = 3 * a // tn

    def attn_gate_bwd(dya, ex, outs):
        silu, dsilu = _silu_and_grad(ex[0][...].astype(F32))
        outs[0][...] = (dya * silu).astype(BF)
        outs[1][...] = (dya * ex[1][...].astype(F32) * dsilu).astype(BF)

    datt, dproj = _mm(
        "attn_gate_bwd", (s_len // tm, a // tn),
        [(d_am, pl.BlockSpec((tm, d), lambda i, j: (i, 0))), (woa, pl.BlockSpec((tn, d), lambda i, j: (j, 0))),
         (proj, pl.BlockSpec((tm, tn), lambda i, j: (i, za_t + j))), (att, tile_ij), (dproj, hbm)],
        [(_sds((s_len, a), BF), tile_ij), (dproj_shape, pl.BlockSpec((tm, tn), lambda i, j: (i, za_t + j)))],
        NT, attn_gate_bwd, aliases={4: 1})

    zp_t = (4 * a + p) // tn

    def pool_gate_bwd(dyp, ex, outs):
        z_ref, y_ref, sc_ref, _ = ex
        dzp_ref, dy_ref, dps_ref = outs
        i = pl.program_id(1)
        silu, dsilu = _silu_and_grad(z_ref[...].astype(F32))
        y = y_ref[...].astype(F32)
        sc = sc_ref[...]
        dyp0 = dyp * silu
        dzp_ref[...] = (dyp * (y * sc) * dsilu).astype(BF)
        dy_ref[...] = (dyp0 * sc).astype(BF)
        dps = jnp.sum(dyp0 * y, axis=0, keepdims=True)

        @pl.when(i == 0)
        def _():
            dps_ref[...] = dps

        @pl.when(i > 0)
        def _():
            dps_ref[...] += dps

    dproj, dy_pool, dps = _mm(
        "pool_gate_bwd", (p // tn, s_len // tmb),
        [(d_pm, pl.BlockSpec((tmb, d), lambda j, i: (i, 0))), (wop, pl.BlockSpec((tn, d), lambda j, i: (j, 0))),
         (proj, pl.BlockSpec((tmb, tn), lambda j, i: (i, zp_t + j))), (ypre, tile_ji),
         (scale_row, pl.BlockSpec((1, tn), lambda j, i: (0, j))), (dproj, hbm)],
        [(dproj_shape, pl.BlockSpec((tmb, tn), lambda j, i: (i, zp_t + j))), (_sds((s_len, p), BF), tile_ji),
         (_sds((1, p), F32), pl.BlockSpec((1, tn), lambda j, i: (0, j)))],
        NT, pool_gate_bwd, aliases={5: 0})

    dproj, dpw = _pool_bwd(dy_pool, dmean, pw, dproj, u_blk)

    o_wop, o_wo, o_pool = a, d, 2 * d
    slab_a = _sds((2, 4, 2 * d + cg, sw), BF)
    slab_b = _sds((2, 4, 5 * d // 2, sw), BF)
    hbm = pl.BlockSpec(memory_space=pl.ANY)
    tmw = _tile(a, 1024)
    core = lax.axis_index("c").astype(jnp.int32).reshape(1)

    def pack_small(dpw_ref, dgb_ref, o_ref):
        rows = cg // N_DEV
        for j in range(N_DEV):
            for g in range(ng):
                o_ref[j % 2, j // 2, g * rows:(g + 1) * rows, :] = dpw_ref[g, j * rows:(j + 1) * rows, :].astype(BF)
            o_ref[j % 2, j // 2, ng * rows:, :] = jnp.concatenate(
                [dgb_ref[:, j * sw:(j + 1) * sw], jnp.zeros((cg - ng * rows - 8, sw), F32)], axis=0).astype(BF)

    slab = pl.pallas_call(
        pack_small, grid=(1,),
        in_specs=[pl.BlockSpec((ng, cg, cg), lambda i: (0, 0, 0)), pl.BlockSpec((8, d), lambda i: (0, 0))],
        out_specs=pl.BlockSpec((2, 4, cg, sw), lambda i: (0, 0, o_pool // cg, 0)), out_shape=slab_a,
        name="dw_small", compiler_params=_params(1))(dpw, dgb8)

    def into_slab(acc, _, outs):
        outs[0][...] = acc.astype(BF)

    def weight_grad(name, slab, lhs, rhs, grid, lhs_spec, rhs_spec, out_spec, epi=into_slab):
        return _mm(name, grid, [(lhs, lhs_spec), (rhs, rhs_spec), (slab, hbm)], [(slab_a, out_spec)],
                   TN, epi, aliases={2: 0})

    def into_both_cores(acc, _, outs):
        outs[0][0] = acc[:sw].astype(BF)
        outs[0][1] = acc[sw:].astype(BF)

    slab = weight_grad("dw_out", slab, merged, dx2b, (N_DEV // 2, d // sw),
                       pl.BlockSpec((s_len, 2 * sw), lambda k, t: (0, k)),
                       pl.BlockSpec((s_len, sw), lambda k, t: (0, t)),
                       pl.BlockSpec((2, None, sw, sw), lambda k, t: (0, k, o_wo // sw + t, 0)), into_both_cores)[0]
    slab = weight_grad("dw_out_attn", slab, ya, d_am, (a // tmw, N_DEV),
                       pl.BlockSpec((s_len, tmw), lambda i, j: (0, i)), pl.BlockSpec((s_len, sw), lambda i, j: (0, j)),
                       pl.BlockSpec((None, None, tmw, sw), lambda i, j: (j % 2, j // 2, i, 0)))[0]
    slab = weight_grad("dw_out_pool", slab, yp, d_pm, (p // tmw, N_DEV),
                       pl.BlockSpec((s_len, tmw), lambda i, j: (0, i)), pl.BlockSpec((s_len, sw), lambda i, j: (0, j)),
                       pl.BlockSpec((None, None, tmw, sw), lambda i, j: (j % 2, j // 2, o_wop // tmw + i, 0)))[0]

    (dproj, dk, dv, ddiag), (from_sibling_a,) = _attn_bwd(proj, datt, base, a, dproj, comm=_cores_comm(slab))
    chip_part_a = _add_core_partials("add_core_partials_a", slab, from_sibling_a, core, (2 * d + cg) // 2)
    drb = _bias_grad(ddiag.reshape(heads, SKEW))
    dproj = lax.dynamic_update_slice(dproj, dk, (0, a))
    dproj = lax.dynamic_update_slice(dproj, dv, (0, 2 * a))

    tmd = _tile(d // 2, 1024)
    nrb = d // 2 // tmd

    def dw_in_rows(name, half, comm):
        return _mm(
            name, (nrb, n_in // sw),
            [(hb, pl.BlockSpec((s_len, tmd), lambda i, t: (0, half * nrb + i))),
             (dproj, pl.BlockSpec((s_len, sw), lambda i, t: (0, pos(t))))],
            [(slab_b, pl.BlockSpec((None, None, tmd, sw),
                                   lambda i, t: ((t // per) % 2, (t // per) // 2, (t % per) * nrb + i, 0)))],
            TN, into_slab, comm=comm)

    slab_lo, parts_a = dw_in_rows("dw_in_lo", 0, _chips_comm(chip_part_a))
    slab_hi, from_sibling_lo = dw_in_rows("dw_in_hi", 1, _cores_comm(slab_lo))
    chip_part_lo = _add_core_partials("add_core_partials_lo", slab_lo, from_sibling_lo, core, 4 * sw)

    tk = w_in.shape[1]
    tnh = _tile(d, 1024)
    tmh = tm if s_len > tm else s_len // 2
    n_row = s_len // tmh

    def dh_rows(name, lo, hi, prev, comm):
        def body(in_refs, out_refs, scratch_refs):
            acc_ref = scratch_refs[0]
            k = pl.program_id(2)
            part = lax.dot_general(in_refs[0][...], in_refs[per][...], NT, preferred_element_type=F32)
            for r in range(1, per):
                part += lax.dot_general(in_refs[r][...], in_refs[per + r][...], NT, preferred_element_type=F32)

            @pl.when(k == 0)
            def _():
                acc_ref[...] = part

            @pl.when(k > 0)
            def _():
                acc_ref[...] += part

            @pl.when(k == N_DEV - 1)
            def _():
                out_refs[0][...] = acc_ref[...]

        ins = [(dproj, pl.BlockSpec((tmh, sw), lambda i, j, k, r=r: (lo + i, pos(per * k + r)))) for r in range(per)]
        ins += [(w, pl.BlockSpec((None, tnh, sw), lambda i, j, k: (k, j, 0))) for w in win_rounds]
        if prev is not None:
            ins.append((prev, hbm))
        return _call(name, body, (hi - lo, d // tnh, N_DEV), ins,
                     [(_sds((s_len, d), F32), pl.BlockSpec((tmh, tnh), lambda i, j, k: (lo + i, j)))],
                     scratch=[pltpu.VMEM((tmh, tnh), F32)], aliases={2 * per: 0} if prev is not None else None,
                     comm=comm)

    dh, parts_lo, from_sibling_hi = dh_rows("dh_head", 0, n_row // 2, None,
                                            _both(_chips_comm(chip_part_lo), _cores_comm(slab_hi)))
    chip_part_hi = _add_core_partials("add_core_partials_hi", slab_hi, from_sibling_hi, core, 4 * sw)
    dh, parts_hi = dh_rows("dh_rest", n_row // 2, n_row, dh, _chips_comm(chip_part_hi))

    dx, dg1 = _rms_bwd(x2d, dh, dx2, g1)

    g_win, d_win, m_win, v_win = _adamw_row_halves("adamw_w_in", parts_lo, parts_hi, w_in, m_w_in, v_w_in)
    g_woa, d_woa, m_woa, v_woa = _adamw_shard("adamw_w_out_attn", parts_a, w_out_attn, m_w_out_attn, v_w_out_attn, 0)
    g_wop, d_wop, m_wop, v_wop = _adamw_shard("adamw_w_out_pool", parts_a, w_out_pool, m_w_out_pool, v_w_out_pool, o_wop)
    g_wo, d_wo, m_wo, v_wo = _adamw_shard("adamw_w_out", parts_a, w_out, m_w_out, v_w_out, o_wo)
    flat = lambda t: t.reshape(cg // 2, sw)
    pool_out = _adamw_shard("adamw_pool_w", parts_a, flat(pool_w), flat(m_pool_w), flat(v_pool_w), o_pool)
    g_pw, d_pw, m_pw, v_pw = [t.reshape(pool_w.shape) for t in pool_out]
    pad16 = lambda t: jnp.pad(t, ((0, 14), (0, 0)))
    gb_out = _adamw_shard("adamw_gate_bias", parts_a, pad16(gate_bias), pad16(m_gate_bias), pad16(v_gate_bias),
                          o_pool + cg // 2)
    g_gb, d_gb, m_gb, v_gb = [t[:2] for t in gb_out]

    def pack(n_gain, f_gain, scale, rb, last):
        rows = [n_gain.reshape(-1, LANES), f_gain.reshape(-1, LANES), scale.reshape(-1, LANES),
                rb.reshape(-1, LANES), last]
        return jnp.concatenate(rows, axis=0)

    pad_rb = lambda t: jnp.pad(t, ((0, 0), (0, N_REL_PAD - N_REL)))
    zeros8 = jnp.zeros((8, LANES), F32)
    loss_rows = jnp.pad(loss_part, ((0, 7), (0, 0)))
    small = _small_allreduce_adamw(
        pack(dg1, dg2, dps, drb, loss_rows),
        pack(norm_gain, final_gain, pool_scale, pad_rb(rel_bias), zeros8),
        pack(m_norm_gain, m_final_gain, m_pool_scale, pad_rb(m_rel_bias), zeros8),
        pack(v_norm_gain, v_final_gain, v_pool_scale, pad_rb(v_rel_bias), zeros8))

    n1, n2, n3 = d // LANES, 2 * d // LANES, (2 * d + p) // LANES
    n4 = n3 + heads * N_REL_PAD // LANES

    def unpack(t):
        return (t[:n1].reshape(d), t[n1:n2].reshape(d), t[n2:n3].reshape(p),
                t[n3:n4].reshape(heads, N_REL_PAD)[:, :N_REL])

    (g_ng, g_fg, g_ps, g_rb), (d_ng, d_fg, d_ps, d_rb), (m_ng, m_fg, m_ps, m_rb), (v_ng, v_fg, v_ps, v_rb) = [
        unpack(t) for t in small]
    loss = small[0][n4, 0]

    return (loss, dx.reshape(x.shape),
            g_ng, g_win, g_rb, g_pw, g_ps, g_woa, g_wop, g_gb, g_wo, g_fg,
            d_ng, d_win, d_rb, d_pw, d_ps, d_woa, d_wop, d_gb, d_wo, d_fg,
            m_ng, m_win, m_rb, m_pw, m_ps, m_woa, m_wop, m_gb, m_wo, m_fg,
            v_ng, v_win, v_rb, v_pw, v_ps, v_woa, v_wop, v_gb, v_wo, v_fg)
```

```python
import jax
import jax.numpy as jnp
from jax import lax
from jax.experimental import pallas as pl
from jax.experimental.pallas import tpu as pltpu

F32 = jnp.float32
BF = jnp.bfloat16
MESH = pl.DeviceIdType.MESH

N_DEV = 8
CHUNK = 64
N_LEFT_CHUNKS = 8
HEAD_DIM = 128
MAX_REL = 128
N_REL = 2 * MAX_REL + 1
N_REL_PAD = 384
POOL_WINDOWS = (2, 4, 8, 16)
HALO = 16
EPS = 1e-6
ADAM_LR = 0.001
ADAM_B1 = 0.9
ADAM_B2 = 0.999
ADAM_EPS = 1e-08
ADAM_WD = 0.01
ADAM_STEP = 10
NEG = -1e30
LANES = 128
TQ = N_LEFT_CHUNKS * CHUNK
TK = 2 * TQ
SKEW = 2 * TK
VMEM_LIMIT = 52 * 1024 * 1024

NN = (((1,), (0,)), ((), ()))
NT = (((1,), (1,)), ((), ()))
TN = (((0,), (0,)), ((), ()))


def _params(n_grid):
    return pltpu.CompilerParams(dimension_semantics=("arbitrary",) * n_grid, vmem_limit_bytes=VMEM_LIMIT)


def _sig(z):
    return 1.0 / (1.0 + jnp.exp(-z))


def _silu_and_grad(z):
    s = _sig(z)
    return z * s, s * (1.0 + z * (1.0 - s))


def _tile(n, pref):
    t = min(n, pref)
    assert n % t == 0, (n, pref)
    return t


def _sds(shape, dtype):
    return jax.ShapeDtypeStruct(shape, dtype)


class _Comm:
    def __init__(self, ins, outs, scratch, start, wait, hooks=()):
        self.ins, self.outs, self.scratch = list(ins), list(outs), list(scratch)
        self.start, self.wait, self.hooks = start, wait, tuple(hooks)


def _call(name, body, grid, ins, outs, scratch=(), aliases=None, comm=None):
    n_in, n_out, n_scr = len(ins), len(outs), len(scratch)
    c_in = len(comm.ins) if comm else 0
    c_out = len(comm.outs) if comm else 0
    n_steps = 1
    for g in grid:
        n_steps *= g

    def kern(*refs):
        o0 = n_in + c_in
        s0 = o0 + n_out + c_out
        if comm:
            c_refs = (refs[n_in:o0], refs[o0 + n_out:s0], refs[s0 + n_scr:])
            step = pl.program_id(0)
            for ax in range(1, len(grid)):
                step = step * grid[ax] + pl.program_id(ax)

            @pl.when(step == 0)
            def _():
                comm.start(*c_refs)

            for frac, hook in comm.hooks:
                @pl.when(step == int(frac * n_steps))
                def _(hook=hook):
                    hook(*c_refs)

        body(refs[:n_in], refs[o0:o0 + n_out], refs[s0:s0 + n_scr])

        if comm:
            @pl.when(step == n_steps - 1)
            def _():
                comm.wait(*c_refs)

    hbm = pl.BlockSpec(memory_space=pl.ANY)
    return pl.pallas_call(
        kern, grid=grid,
        in_specs=[s for _, s in ins] + [hbm] * c_in, out_specs=[s for _, s in outs] + [hbm] * c_out,
        out_shape=[o for o, _ in outs] + (comm.outs if comm else []),
        scratch_shapes=list(scratch) + (comm.scratch if comm else []),
        name=name, compiler_params=_params(len(grid)), input_output_aliases=aliases or {},
    )(*([a for a, _ in ins] + (comm.ins if comm else [])))


def _mm(name, grid, ins, outs, dims, epi, aliases=None, comm=None, nb=1):
    def body(in_refs, out_refs, _):
        if nb == 1:
            acc = lax.dot_general(in_refs[0][...], in_refs[1][...], dims, preferred_element_type=F32)
        else:
            kq = in_refs[0].shape[1] // nb
            acc = sum(lax.dot_general(in_refs[0][:, q * kq:(q + 1) * kq], in_refs[1 + q][...], dims,
                                      preferred_element_type=F32) for q in range(nb))
        epi(acc, in_refs[1 + nb:], out_refs)

    return _call(name, body, grid, ins, outs, aliases=aliases, comm=comm)


def _rms_fwd(x, g, comm=None):
    s, d = x.shape
    tr = _tile(s, 256)

    def body(in_refs, out_refs, _):
        xv = in_refs[0][...]
        r = lax.rsqrt(jnp.mean(xv * xv, axis=-1, keepdims=True) + EPS)
        out_refs[0][...] = (xv * r * in_refs[1][...]).astype(BF)

    row = pl.BlockSpec((tr, d), lambda i: (i, 0))
    return _call("rms_fwd", body, (s // tr,), [(x, row), (g, pl.BlockSpec((1, d), lambda i: (0, 0)))],
                 [(_sds((s, d), BF), row)], comm=comm)


def _final_norm(x2, target, g):
    s, d = x2.shape
    tr = _tile(s, 128)

    def kern(x_ref, t_ref, g_ref, dx_ref, dxb_ref, dg_ref, loss_ref):
        i = pl.program_id(0)
        xv = x_ref[...]
        gv = g_ref[...]
        r = lax.rsqrt(jnp.mean(xv * xv, axis=-1, keepdims=True) + EPS)
        xhat = xv * r
        err = xhat * gv - t_ref[...]
        dy = err * (1.0 / d)
        gy = dy * gv
        dx = r * (gy - xhat * jnp.mean(gy * xhat, axis=-1, keepdims=True))
        dx_ref[...] = dx
        dxb_ref[...] = dx.astype(BF)
        dg = jnp.sum(dy * xhat, axis=0, keepdims=True)
        ls = jnp.broadcast_to(0.5 * jnp.sum(jnp.mean(err * err, axis=-1, keepdims=True)), (1, LANES))

        @pl.when(i == 0)
        def _():
            dg_ref[...] = dg
            loss_ref[...] = ls

        @pl.when(i > 0)
        def _():
            dg_ref[...] += dg
            loss_ref[...] += ls

    row = pl.BlockSpec((tr, d), lambda i: (i, 0))
    vec = pl.BlockSpec((1, d), lambda i: (0, 0))
    return pl.pallas_call(
        kern, grid=(s // tr,), in_specs=[row, row, vec],
        out_specs=[row, row, vec, pl.BlockSpec((1, LANES), lambda i: (0, 0))],
        out_shape=[_sds((s, d), F32), _sds((s, d), BF), _sds((1, d), F32), _sds((1, LANES), F32)],
        name="final_norm", compiler_params=_params(1))(x2, target, g)


def _rms_bwd(x, dh, dx2, g):
    s, d = x.shape
    tr = _tile(s, 128)

    def kern(x_ref, dh_ref, dx2_ref, g_ref, dx_ref, dg_ref):
        i = pl.program_id(0)
        xv = x_ref[...]
        r = lax.rsqrt(jnp.mean(xv * xv, axis=-1, keepdims=True) + EPS)
        xhat = xv * r
        dhv = dh_ref[...]
        gh = dhv * g_ref[...]
        dx_ref[...] = dx2_ref[...] + r * (gh - xhat * jnp.mean(gh * xhat, axis=-1, keepdims=True))
        dg = jnp.sum(dhv * xhat, axis=0, keepdims=True)

        @pl.when(i == 0)
        def _():
            dg_ref[...] = dg

        @pl.when(i > 0)
        def _():
            dg_ref[...] += dg

    row = pl.BlockSpec((tr, d), lambda i: (i, 0))
    vec = pl.BlockSpec((1, d), lambda i: (0, 0))
    return pl.pallas_call(
        kern, grid=(s // tr,), in_specs=[row, row, row, vec], out_specs=[row, vec],
        out_shape=[_sds((s, d), F32), _sds((1, d), F32)],
        name="rms_bwd", compiler_params=_params(1))(x, dh, dx2, g)


def _rel_index(j, backward):
    if backward:
        rel = 2 * TQ - 1 - j
    else:
        rel = TQ - jnp.where(j < TK, j, j - SKEW)
    return jnp.clip(rel, -MAX_REL, MAX_REL) + MAX_REL


def _bias_rows(rel_bias_pad):
    h = rel_bias_pad.shape[0]

    def kern(rb_ref, o_ref):
        j = lax.broadcasted_iota(jnp.int32, (N_REL_PAD, SKEW), 1)
        k = lax.broadcasted_iota(jnp.int32, (N_REL_PAD, SKEW), 0)
        onehot = (_rel_index(j, False) == k).astype(F32)
        o_ref[...] = jnp.dot(rb_ref[...], onehot, preferred_element_type=F32, precision=lax.Precision.HIGHEST)

    return pl.pallas_call(kern, out_shape=_sds((h, SKEW), F32), name="bias_rows")(rel_bias_pad)


def _bias_grad(ddiag):
    h = ddiag.shape[0]

    def kern(d_ref, o_ref):
        j = lax.broadcasted_iota(jnp.int32, (N_REL_PAD, SKEW), 1)
        k = lax.broadcasted_iota(jnp.int32, (N_REL_PAD, SKEW), 0)
        onehot = ((_rel_index(j, True) == k) & (j < TQ + TK - 1)).astype(F32)
        o_ref[...] = lax.dot_general(d_ref[...], onehot, NT, preferred_element_type=F32,
                                     precision=lax.Precision.HIGHEST)

    return pl.pallas_call(kern, out_shape=_sds((h, N_REL_PAD), F32), name="bias_grad")(ddiag)


def _bias_tile(row):
    t = pltpu.roll(jnp.broadcast_to(row, (TQ, SKEW)), 0, 1, stride=1, stride_axis=0)[:, :TK]
    r = lax.broadcasted_iota(jnp.int32, (TQ, TK), 0) // CHUNK
    c = lax.broadcasted_iota(jnp.int32, (TQ, TK), 1) // CHUNK
    dist = N_LEFT_CHUNKS + r - c
    return jnp.where((dist >= 0) & (dist <= N_LEFT_CHUNKS), t, NEG)


def _scores(q, keys, tile):
    s = lax.dot_general(q, keys, NT, preferred_element_type=F32) * (HEAD_DIM ** -0.5) + tile
    m = jnp.max(s, axis=1, keepdims=True)
    p = jnp.exp(s - m)
    return p, jnp.sum(p, axis=1, keepdims=True)


def _attn_fwd(proj, base, a_width, comm=None):
    s_len = proj.shape[0]
    heads = a_width // HEAD_DIM
    nq = s_len // TQ
    kb, vb, zb = heads, 2 * heads, 3 * heads

    def kern(q_ref, kp_ref, kc_ref, vp_ref, vc_ref, z_ref, base_ref, att_ref, ya_ref, tile_ref):
        i = pl.program_id(1)

        def attend(keys, vals, tile):
            p, l = _scores(q_ref[...], keys, tile)
            o = jnp.dot(p.astype(BF), vals, preferred_element_type=F32) / l
            att_ref[...] = o.astype(BF)
            z = z_ref[...].astype(F32)
            ya_ref[...] = (o * (z * _sig(z))).astype(BF)

        @pl.when(i == 0)
        def _():
            tile_ref[...] = _bias_tile(base_ref[...])
            attend(kc_ref[...], vc_ref[...], tile_ref[:, TQ:])

        @pl.when(i > 0)
        def _():
            attend(jnp.concatenate([kp_ref[...], kc_ref[...]], axis=0),
                   jnp.concatenate([vp_ref[...], vc_ref[...]], axis=0), tile_ref[...])

    blk = lambda off: pl.BlockSpec((TQ, HEAD_DIM), lambda h, i: (i, off + h))
    prev = lambda off: pl.BlockSpec((TQ, HEAD_DIM), lambda h, i: (jnp.maximum(i - 1, 0), off + h))
    out = pl.BlockSpec((TQ, HEAD_DIM), lambda h, i: (i, h))
    def body(in_refs, out_refs, scratch_refs):
        kern(*in_refs, *out_refs, *scratch_refs)

    act = _sds((s_len, a_width), BF)
    res = _call(
        "attn_fwd", body, (heads, nq),
        [(proj, blk(0)), (proj, prev(kb)), (proj, blk(kb)), (proj, prev(vb)), (proj, blk(vb)), (proj, blk(zb)),
         (base, pl.BlockSpec((None, 1, SKEW), lambda h, i: (h, 0, 0)))],
        [(act, out), (act, out)], scratch=[pltpu.VMEM((TQ, TK), F32)], comm=comm)
    return res[:2], res[2:]


def _attn_bwd(proj, datt, base, a_width, dproj, comm=None):
    s_len = proj.shape[0]
    heads = a_width // HEAD_DIM
    nq = s_len // TQ
    kb, vb = heads, 2 * heads
    scale = HEAD_DIM ** -0.5

    def body(in_refs, out_refs, scratch_refs):
        q_ref, kp_ref, kc_ref, vp_ref, vc_ref, do_ref, base_ref, _ = in_refs
        dq_ref, dk_ref, dv_ref, dd_ref = out_refs
        tile_ref, dsacc_ref, ak_ref, av_ref = scratch_refs
        i = pl.program_id(1)

        def backward(keys, vals, tile, cols):
            q = q_ref[...]
            do = do_ref[...]
            p, l = _scores(q, keys, tile)
            p = p / l
            dp = lax.dot_general(do, vals, NT, preferred_element_type=F32)
            ds = p * (dp - jnp.sum(p * dp, axis=1, keepdims=True))
            dsacc_ref[:, cols] += ds
            dsb = ds.astype(BF)
            dq_ref[...] = (jnp.dot(dsb, keys, preferred_element_type=F32) * scale).astype(BF)
            return (lax.dot_general(dsb, q, TN, preferred_element_type=F32) * scale,
                    lax.dot_general(p.astype(BF), do, TN, preferred_element_type=F32))

        @pl.when(i == 0)
        def _():
            tile_ref[...] = _bias_tile(base_ref[...])
            dsacc_ref[...] = jnp.zeros_like(dsacc_ref)
            ak_ref[...], av_ref[...] = backward(kc_ref[...], vc_ref[...], tile_ref[:, TQ:], slice(TQ, TK))
            dk_ref[...] = jnp.zeros_like(dk_ref)
            dv_ref[...] = jnp.zeros_like(dv_ref)

        @pl.when((i > 0) & (i < nq))
        def _():
            dkc, dvc = backward(jnp.concatenate([kp_ref[...], kc_ref[...]], axis=0),
                                jnp.concatenate([vp_ref[...], vc_ref[...]], axis=0), tile_ref[...], slice(0, TK))
            dk_ref[...] = (ak_ref[...] + dkc[:TQ]).astype(BF)
            dv_ref[...] = (av_ref[...] + dvc[:TQ]).astype(BF)
            ak_ref[...] = dkc[TQ:]
            av_ref[...] = dvc[TQ:]

        @pl.when(i == nq)
        def _():
            dk_ref[...] = ak_ref[...].astype(BF)
            dv_ref[...] = av_ref[...].astype(BF)
            acc = dsacc_ref[...]
            rr = lax.broadcasted_iota(jnp.int32, (TQ, TQ), 0)
            cc = lax.broadcasted_iota(jnp.int32, (TQ, TQ), 1)
            flip = (rr + cc == TQ - 1).astype(BF)
            hi = acc.astype(BF)
            lo = (acc - hi.astype(F32)).astype(BF)
            rev = jnp.dot(flip, hi, preferred_element_type=F32) + jnp.dot(flip, lo, preferred_element_type=F32)
            wide = jnp.concatenate([rev, jnp.zeros((TQ, SKEW - TK), F32)], axis=1)
            dd_ref[...] = jnp.sum(pltpu.roll(wide, 0, 1, stride=1, stride_axis=0), axis=0, keepdims=True)

    last = nq - 1
    cur = lambda off: pl.BlockSpec((TQ, HEAD_DIM), lambda h, i: (jnp.minimum(i, last), off + h))
    prev = lambda off: pl.BlockSpec((TQ, HEAD_DIM), lambda h, i: (jnp.maximum(jnp.minimum(i, last) - 1, 0), off + h))
    done = pl.BlockSpec((TQ, HEAD_DIM), lambda h, i: (jnp.maximum(i - 1, 0), h))
    row = pl.BlockSpec((None, 1, SKEW), lambda h, i: (h, 0, 0))
    act = _sds((s_len, a_width), BF)
    res = _call(
        "attn_bwd", body, (heads, nq + 1),
        [(proj, cur(0)), (proj, prev(kb)), (proj, cur(kb)), (proj, prev(vb)), (proj, cur(vb)), (datt, cur(0)),
         (base, row), (dproj, pl.BlockSpec(memory_space=pl.ANY))],
        [(_sds(dproj.shape, dproj.dtype), cur(0)), (act, done), (act, done), (_sds((heads, 1, SKEW), F32), row)],
        aliases={7: 0},
        scratch=[pltpu.VMEM((TQ, TK), F32), pltpu.VMEM((TQ, TK), F32),
                 pltpu.VMEM((TQ, HEAD_DIM), F32), pltpu.VMEM((TQ, HEAD_DIM), F32)],
        comm=comm)
    return res[:4], res[4:]


def _pool_fwd(proj, pool_w, pool_scale, p_width, u_blk, z_blk):
    s_len = proj.shape[0]
    cg = p_width // len(POOL_WINDOWS)
    tt = _tile(s_len, 512)

    def kern(up_ref, uc_ref, z_ref, pw_ref, sc_ref, d_ref, y_ref, yp_ref):
        t = pl.program_id(0)
        row = lax.broadcasted_iota(jnp.int32, (tt, 1), 0) + t * tt
        for g, w in enumerate(POOL_WINDOWS):
            cs = slice(g * cg, (g + 1) * cg)
            prev = jnp.where(t == 0, 0.0, up_ref[:, cs].astype(F32))
            cur = uc_ref[:, cs].astype(F32)
            ws = jnp.concatenate([prev, cur], axis=0)
            sh = 1
            while sh < w:
                ws = ws + pltpu.roll(ws, sh, 0)
                sh *= 2
            cnt = jnp.minimum(row + 1, w).astype(F32)
            db = (ws[HALO:, :] / cnt - cur).astype(BF)
            y = jnp.dot(db, pw_ref[g], preferred_element_type=F32)
            d_ref[:, cs] = db
            y_ref[:, cs] = y.astype(BF)
            z = z_ref[:, cs].astype(F32)
            yp_ref[:, cs] = (y * sc_ref[:, cs] * (z * _sig(z))).astype(BF)

    full = pl.BlockSpec((tt, p_width), lambda t: (t, 0))
    return pl.pallas_call(
        kern, grid=(s_len // tt,),
        in_specs=[pl.BlockSpec((HALO, p_width), lambda t: (jnp.maximum(t * (tt // HALO) - 1, 0), u_blk)),
                  pl.BlockSpec((tt, p_width), lambda t: (t, u_blk)),
                  pl.BlockSpec((tt, p_width), lambda t: (t, z_blk)),
                  pl.BlockSpec((len(POOL_WINDOWS), cg, cg), lambda t: (0, 0, 0)),
                  pl.BlockSpec((1, p_width), lambda t: (0, 0))],
        out_specs=[full, full, full], out_shape=[_sds((s_len, p_width), BF)] * 3,
        name="pool_fwd", compiler_params=_params(1))(proj, proj, proj, pool_w, pool_scale)


def _pool_bwd(dy, dmean, pool_w, dproj, u_blk):
    s_len, p_width = dy.shape
    ng = len(POOL_WINDOWS)
    cg = p_width // ng
    tt = _tile(s_len, 512)
    nt = s_len // tt

    def kern(dyc_ref, dyn_ref, d_ref, pw_ref, _, du_ref, dpw_ref):
        t = pl.program_id(0)

        @pl.when(t == 0)
        def _():
            dpw_ref[...] = jnp.zeros_like(dpw_ref)

        row = lax.broadcasted_iota(jnp.int32, (tt + HALO, 1), 0) + t * tt
        for g, w in enumerate(POOL_WINDOWS):
            cs = slice(g * cg, (g + 1) * cg)
            dyc = dyc_ref[:, cs]
            ddc = lax.dot_general(dyc, pw_ref[g], NT, preferred_element_type=F32)
            ddn = lax.dot_general(dyn_ref[:, cs], pw_ref[g], NT, preferred_element_type=F32)
            ddn = jnp.where(t == nt - 1, 0.0, ddn)
            cnt = jnp.minimum(row + 1, w).astype(F32)
            ws = jnp.concatenate([ddc, ddn], axis=0) / cnt
            sh = 1
            while sh < w:
                ws = ws + pltpu.roll(ws, tt + HALO - sh, 0)
                sh *= 2
            du_ref[:, cs] = (ws[:tt, :] - ddc).astype(BF)
            dpw_ref[g] += lax.dot_general(d_ref[:, cs], dyc, TN, preferred_element_type=F32)

    full = pl.BlockSpec((tt, p_width), lambda t: (t, 0))
    pw_spec = pl.BlockSpec((ng, cg, cg), lambda t: (0, 0, 0))
    return pl.pallas_call(
        kern, grid=(nt,),
        in_specs=[full,
                  pl.BlockSpec((HALO, p_width), lambda t: (jnp.minimum((t + 1) * (tt // HALO), s_len // HALO - 1), 0)),
                  full, pw_spec, pl.BlockSpec(memory_space=pl.ANY)],
        out_specs=[pl.BlockSpec((tt, p_width), lambda t: (t, u_blk)), pw_spec],
        out_shape=[_sds(dproj.shape, dproj.dtype), _sds((ng, cg, cg), F32)],
        input_output_aliases={4: 0},
        name="pool_bwd", compiler_params=_params(1))(dy, dy, dmean, pool_w, dproj)


def _adam(g, w_ref, m_ref, v_ref, g_out, d_out, m_out, v_out):
    m = ADAM_B1 * m_ref[...] + (1.0 - ADAM_B1) * g
    v = ADAM_B2 * v_ref[...] + (1.0 - ADAM_B2) * (g * g)
    m_hat = m / (1.0 - ADAM_B1 ** ADAM_STEP)
    v_hat = v / (1.0 - ADAM_B2 ** ADAM_STEP)
    g_out[...] = g
    d_out[...] = -ADAM_LR * (m_hat / (jnp.sqrt(v_hat) + ADAM_EPS) + ADAM_WD * w_ref[...])
    m_out[...] = m
    v_out[...] = v


def _adamw_shard(name, parts, w, m, v, row_off):
    rw, cw = w.shape
    sw = parts.shape[2]
    tr = _tile(rw, 512)
    assert row_off % tr == 0 and cw % sw == 0

    def kern(b_ref, w_ref, m_ref, v_ref, g_out, d_out, m_out, v_out):
        b = b_ref[...].astype(F32)
        _adam(((b[0] + b[1]) + b[2]) + b[3], w_ref, m_ref, v_ref, g_out, d_out, m_out, v_out)

    blk = pl.BlockSpec((tr, sw), lambda ct, i: (i, ct))
    return pl.pallas_call(
        kern, grid=(cw // sw, rw // tr),
        in_specs=[pl.BlockSpec((4, tr, sw), lambda ct, i: (0, (row_off + ct * rw) // tr + i, 0)), blk, blk, blk],
        out_specs=[blk] * 4, out_shape=[_sds((rw, cw), F32)] * 4,
        name=name, compiler_params=_params(2))(parts, w, m, v)


def _adamw_row_halves(name, parts_lo, parts_hi, w, m, v):
    rw, cw = w.shape
    sw = parts_lo.shape[2]
    half = rw // 2
    tr = _tile(half, 512)
    nh = half // tr

    def kern(lo_ref, hi_ref, w_ref, m_ref, v_ref, g_out, d_out, m_out, v_out):
        i = pl.program_id(1)

        def update(b_ref):
            b = b_ref[...].astype(F32)
            _adam(((b[0] + b[1]) + b[2]) + b[3], w_ref, m_ref, v_ref, g_out, d_out, m_out, v_out)

        @pl.when(i < nh)
        def _():
            update(lo_ref)

        @pl.when(i >= nh)
        def _():
            update(hi_ref)

    blk = pl.BlockSpec((tr, sw), lambda ct, i: (i, ct))
    return pl.pallas_call(
        kern, grid=(cw // sw, rw // tr),
        in_specs=[pl.BlockSpec((4, tr, sw), lambda ct, i: (0, ct * nh + jnp.minimum(i, nh - 1), 0)),
                  pl.BlockSpec((4, tr, sw), lambda ct, i: (0, ct * nh + jnp.maximum(i - nh, 0), 0)), blk, blk, blk],
        out_specs=[blk] * 4, out_shape=[_sds((rw, cw), F32)] * 4,
        name=name, compiler_params=_params(2))(parts_lo, parts_hi, w, m, v)


def _both(c1, c2):
    n_in, n_out, n_sem = len(c1.ins), len(c1.outs), len(c1.scratch)

    def split(ins, outs, sems):
        return (ins[:n_in], outs[:n_out], sems[:n_sem]), (ins[n_in:], outs[n_out:], sems[n_sem:])

    def start(*refs):
        r1, r2 = split(*refs)
        c1.start(*r1)
        c2.start(*r2)

    def wait(*refs):
        r1, r2 = split(*refs)
        c1.wait(*r1)
        c2.wait(*r2)

    def of(which, hook):
        return lambda *refs: hook(*split(*refs)[which])

    hooks = [(f, of(0, h)) for f, h in c1.hooks] + [(f, of(1, h)) for f, h in c2.hooks]
    return _Comm(c1.ins + c2.ins, c1.outs + c2.outs, c1.scratch + c2.scratch, start, wait, hooks)


def _position():
    return lax.axis_index("x"), lax.axis_index("y"), lax.axis_index("c")


def _gather_comm(shards, cols=None, pass_at=None):
    if cols is None:
        pieces = [(a, None) for a in range(len(shards))]
        shapes = [_sds(s.shape, s.dtype) for s in shards]
    else:
        half = shards[0].shape[0] // 2
        pieces = [(0, pl.ds(0, half)), (0, pl.ds(half, half))]
        shapes = [_sds((shards[0].shape[0], cols[1]), shards[0].dtype)]
    n = len(pieces)

    def plan(xs, outs, sems, only=None):
        send_sems, recv_sems, local_sems = sems
        x, y, c = _position()
        me, sibling = (x, y, c), (x, y, 1 - c)
        chips = [(1 - x, y), (x, 1 - y), (1 - x, 1 - y)]
        which = range(n) if only is None else only

        def source(v):
            a, rows = pieces[v]
            return xs[a] if rows is None else xs[a].at[rows, pl.ds(cols[0], cols[1])]

        def landing(v, block):
            a, rows = pieces[v]
            dst = outs[a].at[4 * block[0] + 2 * block[1] + block[2]]
            return dst if rows is None else dst.at[rows, :]

        def copy(v, k, block, to, own=False):
            dst = landing(v, block)
            return pltpu.make_async_remote_copy(
                src_ref=source(v) if own else dst, dst_ref=dst,
                send_sem=send_sems.at[7 * v + k], recv_sem=recv_sems.at[7 * v + k],
                device_id=to, device_id_type=MESH)

        by_chip = [(j, chip, v) for v in which for j, chip in enumerate(chips)]
        return dict(
            mine=lambda: [pltpu.make_async_copy(source(v), landing(v, me), local_sems.at[v]) for v in which],
            first=lambda: ([copy(v, 0, me, sibling, own=True) for v in which]
                           + [copy(v, 1 + j, me, (*chip, c), own=True) for j, chip, v in by_chip]),
            landed=lambda: [copy(v, 1 + j, (*chip, c), me) for j, chip, v in by_chip],
            passed=lambda: [copy(v, 4 + j, (*chip, c), sibling) for j, chip, v in by_chip],
            rest=lambda: ([copy(v, 0, sibling, me) for v in which]
                          + [copy(v, 4 + j, (*chip, 1 - c), me) for j, chip, v in by_chip]))

    def start(*refs):
        p = plan(*refs)
        for cp in p["mine"]() + p["first"]():
            cp.start()

    def pass_on(only):
        def hook(*refs):
            p = plan(*refs, only=only)
            for arrived, onward in zip(p["landed"](), p["passed"]()):
                arrived.wait_recv()
                onward.start()
        return hook

    in_halves = cols is not None and pass_at is None

    def wait(*refs):
        if in_halves:
            pass_on([1])(*refs)
        p = plan(*refs)
        for cp in p["rest"]():
            cp.wait_recv()
        for cp in p["first"]() + p["passed"]():
            cp.wait_send()
        for cp in p["mine"]():
            cp.wait()

    hooks = [(0.5, pass_on([0]))] if in_halves else [(0.75 if pass_at is None else pass_at, pass_on(None))]
    return _Comm(shards, [_sds((N_DEV,) + s.shape, s.dtype) for s in shapes],
                 [pltpu.SemaphoreType.DMA((7 * n,)), pltpu.SemaphoreType.DMA((7 * n,)),
                  pltpu.SemaphoreType.DMA((n,))], start, wait, hooks)


def _cores_comm(slab):
    _, _, r, sw = slab.shape

    def copies(ins, outs, sems):
        x, y, c = _position()
        return [pltpu.make_async_remote_copy(
            src_ref=ins[0].at[1 - c], dst_ref=outs[0], send_sem=sems[0], recv_sem=sems[1],
            device_id=(x, y, 1 - c), device_id_type=MESH)]

    def start(*refs):
        for cp in copies(*refs):
            cp.start()

    def wait(*refs):
        for cp in copies(*refs):
            cp.wait()

    return _Comm([slab], [_sds((4, r, sw), slab.dtype)],
                 [pltpu.SemaphoreType.DMA, pltpu.SemaphoreType.DMA], start, wait)


def _add_core_partials(name, slab, recv, core, tr):
    _, _, r, sw = slab.shape

    def kern(c_ref, a_ref, b_ref, o_ref):
        o_ref[...] = (a_ref[...].astype(F32) + b_ref[...].astype(F32)).astype(BF)

    return pl.pallas_call(
        kern,
        grid_spec=pltpu.PrefetchScalarGridSpec(
            num_scalar_prefetch=1, grid=(4, r // tr),
            in_specs=[pl.BlockSpec((None, None, tr, sw), lambda k, i, c_ref: (c_ref[0], k, i, 0)),
                      pl.BlockSpec((None, tr, sw), lambda k, i, c_ref: (k, i, 0))],
            out_specs=pl.BlockSpec((None, tr, sw), lambda k, i, c_ref: (k, i, 0))),
        out_shape=_sds((4, r, sw), BF), name=name, compiler_params=_params(2))(core, slab, recv)


def _chips_comm(part):
    _, r, sw = part.shape

    def copies(ins, outs, sems):
        send_sems, recv_sems, local_sem = sems
        x, y, c = _position()
        mine = 2 * x + y
        local = pltpu.make_async_copy(ins[0].at[mine], outs[0].at[mine], local_sem)
        chips = [(1 - x, y), (x, 1 - y), (1 - x, 1 - y)]
        remote = [pltpu.make_async_remote_copy(
            src_ref=ins[0].at[2 * px + py], dst_ref=outs[0].at[mine],
            send_sem=send_sems.at[j], recv_sem=recv_sems.at[j],
            device_id=(px, py, c), device_id_type=MESH) for j, (px, py) in enumerate(chips)]
        return [local] + remote

    def start(*refs):
        for cp in copies(*refs):
            cp.start()

    def wait(*refs):
        for cp in copies(*refs):
            cp.wait()

    return _Comm([part], [_sds((4, r, sw), part.dtype)],
                 [pltpu.SemaphoreType.DMA((3,)), pltpu.SemaphoreType.DMA((3,)), pltpu.SemaphoreType.DMA],
                 start, wait)


def _small_allreduce_adamw(partial, w, m, v):
    nr = partial.shape[0]

    def kern(p_ref, w_ref, m_ref, v_ref, g_out, d_out, m_out, v_out, gath_ref, send_sems, recv_sems):
        x, y, c = _position()
        me = 4 * x + 2 * y + c
        gath_ref[me] = p_ref[...]
        copies = []
        for mask in range(1, N_DEV):
            peer = (x ^ (mask >> 2), y ^ ((mask >> 1) & 1), c ^ (mask & 1))
            copies.append(pltpu.make_async_remote_copy(
                src_ref=p_ref, dst_ref=gath_ref.at[me],
                send_sem=send_sems.at[mask - 1], recv_sem=recv_sems.at[mask - 1],
                device_id=peer, device_id_type=MESH))
        for cp in copies:
            cp.start()
        for cp in copies:
            cp.wait()
        tot = gath_ref[0]
        for k in range(1, N_DEV):
            tot = tot + gath_ref[k]
        _adam(tot, w_ref, m_ref, v_ref, g_out, d_out, m_out, v_out)

    vmem = pl.BlockSpec(memory_space=pltpu.VMEM)
    return pl.pallas_call(
        kern, in_specs=[vmem] * 4, out_specs=[vmem] * 4, out_shape=[_sds((nr, LANES), F32)] * 4,
        scratch_shapes=[pltpu.VMEM((N_DEV, nr, LANES), F32),
                        pltpu.SemaphoreType.DMA((N_DEV - 1,)), pltpu.SemaphoreType.DMA((N_DEV - 1,))],
        name="small_allreduce_adamw")(partial, w, m, v)


def kernel(x, norm_gain, w_in, rel_bias, pool_w, pool_scale, w_out_attn, w_out_pool, gate_bias, w_out, final_gain, loss_target, m_norm_gain, m_w_in, m_rel_bias, m_pool_w, m_pool_scale, m_w_out_attn, m_w_out_pool, m_gate_bias, m_w_out, m_final_gain, v_norm_gain, v_w_in, v_rel_bias, v_pool_w, v_pool_scale, v_w_out_attn, v_w_out_pool, v_gate_bias, v_w_out, v_final_gain):
    _, s_len, d = x.shape
    a = w_out_attn.shape[0]
    p = w_out_pool.shape[0]
    heads = a // HEAD_DIM
    ng = len(POOL_WINDOWS)
    cg = p // ng
    sw = d // N_DEV
    n_in = w_in.shape[1] * N_DEV
    assert a == p and a + p == d and cg == sw and n_in == 5 * d and w_in.shape[1] == 5 * sw
    assert s_len % TQ == 0 and rel_bias.shape == (heads, N_REL)
    tm = _tile(s_len, 1024)
    x2d = x.reshape(s_len, d)
    tgt = loss_target.reshape(s_len, d)

    g1 = norm_gain.reshape(1, d)
    g2 = final_gain.reshape(1, d)
    scale_row = pool_scale.reshape(1, p)
    tn = sw
    per = w_in.shape[1] // tn
    hbm = pl.BlockSpec(memory_space=pl.ANY)

    w_bf = [w_in[:, r * tn:(r + 1) * tn].astype(BF) for r in range(per)]
    hb, landed = _rms_fwd(x2d, g1, comm=_gather_comm([w_bf[0]], cols=(0, tn)))
    win_rounds = []

    def store_bf16(acc, _, outs):
        outs[0][...] = acc.astype(BF)

    gate0, n_gate = (4 * a + 2 * p) // tn, d // tn

    def pos(t):
        g = t - gate0
        return jnp.where(t < gate0, t, jnp.where(g < n_gate, gate0 + 2 * g, gate0 + 2 * (g - n_gate) + 1))

    proj = None
    for r in range(per):
        win_rounds.append(landed)
        if r + 1 < per:
            comm = _gather_comm([w_bf[r + 1]], cols=(0, tn))
        else:
            comm = _gather_comm([w_out_attn.astype(BF), pool_w.astype(BF), gate_bias])
        ins = [(hb, pl.BlockSpec((tm, d), lambda i, j: (i, 0))),
               (landed, pl.BlockSpec((None, d, tn), lambda i, j: (j, 0, 0)))]
        if proj is not None:
            ins.append((proj, hbm))
        proj, landed, *rest = _mm(
            f"proj_{r}", (s_len // tm, N_DEV), ins,
            [(_sds((s_len, n_in), BF), pl.BlockSpec((tm, tn), lambda i, j, r=r: (i, pos(per * j + r))))],
            NN, store_bf16, aliases={2: 0} if r else None, comm=comm)
    woa_g = landed
    woa = woa_g.transpose(1, 0, 2).reshape(a, d)
    pw = rest[0].transpose(1, 0, 2, 3).reshape(ng, cg, cg)
    gb = rest[1].transpose(1, 0, 2).reshape(2, d)

    rb_pad = jnp.pad(rel_bias, ((0, 0), (0, N_REL_PAD - N_REL)))
    base = _bias_rows(rb_pad).reshape(heads, 1, SKEW)
    wo_bf = w_out.astype(BF)
    (att, ya), (wop_g, wo_lo_g) = _attn_fwd(
        proj, base, a, comm=_both(_gather_comm([w_out_pool.astype(BF)], pass_at=0.9),
                                  _gather_comm([wo_bf], cols=(0, d // 2), pass_at=0.9)))
    wop = wop_g.transpose(1, 0, 2).reshape(p, d)
    u_blk, z_blk = 4 * a // p, 4 * a // p + 1
    dmean, ypre, yp = _pool_fwd(proj, pw, scale_row, p, u_blk, z_blk)

    def gate_kernel(in_refs, out_refs, _):
        ya_ref, woa_ref, yp_ref, wop_ref, ga_ref, gp_ref, gb_ref = in_refs
        m_ref, a_ref, p_ref = out_refs
        am = jnp.dot(ya_ref[...], woa_ref[...], preferred_element_type=F32)
        pm = jnp.dot(yp_ref[...], wop_ref[...], preferred_element_type=F32)
        sa = _sig(ga_ref[...].astype(F32) + gb_ref[0:1, :])
        sp = _sig(gp_ref[...].astype(F32) + gb_ref[1:2, :])
        m_ref[...] = (sa * am + sp * pm).astype(BF)
        a_ref[...] = am.astype(BF)
        p_ref[...] = pm.astype(BF)

    tile_ij = pl.BlockSpec((tm, tn), lambda i, j: (i, j))
    act_d = _sds((s_len, d), BF)
    merged, am, pm, wo_hi_g = _call(
        "gate_merge", gate_kernel, (s_len // tm, d // tn),
        [(ya, pl.BlockSpec((tm, a), lambda i, j: (i, 0))), (woa_g, pl.BlockSpec((None, a, tn), lambda i, j: (j, 0, 0))),
         (yp, pl.BlockSpec((tm, p), lambda i, j: (i, 0))), (wop_g, pl.BlockSpec((None, p, tn), lambda i, j: (j, 0, 0))),
         (proj, pl.BlockSpec((tm, tn), lambda i, j: (i, gate0 + 2 * j))),
         (proj, pl.BlockSpec((tm, tn), lambda i, j: (i, gate0 + 2 * j + 1))),
         (gb, pl.BlockSpec((2, tn), lambda i, j: (0, j)))],
        [(act_d, tile_ij)] * 3, comm=_gather_comm([wo_bf], cols=(d // 2, d // 2), pass_at=0.9))
    wo_halves = [wo_lo_g.reshape(d, d // 2), wo_hi_g.reshape(d, d // 2)]

    def out_proj(in_refs, out_refs, _):
        m_ref, lo_ref, hi_ref, x_ref = in_refs
        j = pl.program_id(1)

        @pl.when(j < n_half)
        def _():
            out_refs[0][...] = x_ref[...] + jnp.dot(m_ref[...], lo_ref[...], preferred_element_type=F32)

        @pl.when(j >= n_half)
        def _():
            out_refs[0][...] = x_ref[...] + jnp.dot(m_ref[...], hi_ref[...], preferred_element_type=F32)

    n_half = d // 2 // tn
    x2 = _call("out_proj", out_proj, (s_len // tm, d // tn),
               [(merged, pl.BlockSpec((tm, d), lambda i, j: (i, 0))),
                (wo_halves[0], pl.BlockSpec((d, tn), lambda i, j: (0, jnp.minimum(j, n_half - 1)))),
                (wo_halves[1], pl.BlockSpec((d, tn), lambda i, j: (0, jnp.maximum(j - n_half, 0)))),
                (x2d, tile_ij)],
               [(_sds((s_len, d), F32), tile_ij)])[0]

    dx2, dx2b, dg2, loss_part = _final_norm(x2, tgt, g2)

    tmb = _tile(s_len, 512)
    tile_ji = pl.BlockSpec((tmb, tn), lambda j, i: (i, j))

    def gate_bwd(dm, ex, outs):
        a_ref, p_ref, ga_ref, gp_ref, gb_ref = ex
        da_ref, dp_ref, dgate_ref, dgb_ref = outs
        i = pl.program_id(1)
        sa = _sig(ga_ref[...].astype(F32) + gb_ref[0:1, :])
        sp = _sig(gp_ref[...].astype(F32) + gb_ref[1:2, :])
        dga = dm * a_ref[...].astype(F32) * sa * (1.0 - sa)
        dgp = dm * p_ref[...].astype(F32) * sp * (1.0 - sp)
        da_ref[...] = (dm * sa).astype(BF)
        dp_ref[...] = (dm * sp).astype(BF)
        dgate_ref[:, :tn] = dga.astype(BF)
        dgate_ref[:, tn:] = dgp.astype(BF)
        r = lax.broadcasted_iota(jnp.int32, (8, tn), 0)
        sums = jnp.where(r == 0, jnp.sum(dga, axis=0, keepdims=True),
                         jnp.where(r == 1, jnp.sum(dgp, axis=0, keepdims=True), 0.0))

        @pl.when(i == 0)
        def _():
            dgb_ref[...] = sums

        @pl.when(i > 0)
        def _():
            dgb_ref[...] += sums

    dproj_shape = _sds((s_len, n_in), BF)
    d_am, d_pm, dproj, dgb8 = _mm(
        "gate_bwd", (d // tn, s_len // tmb),
        [(dx2b, pl.BlockSpec((tmb, d), lambda j, i: (i, 0)))]
        + [(w, pl.BlockSpec((tn, d // 2), lambda j, i: (j, 0))) for w in wo_halves]
        + [(am, tile_ji), (pm, tile_ji),
           (proj, pl.BlockSpec((tmb, tn), lambda j, i: (i, gate0 + 2 * j))),
           (proj, pl.BlockSpec((tmb, tn), lambda j, i: (i, gate0 + 2 * j + 1))),
           (gb, pl.BlockSpec((2, tn), lambda j, i: (0, j)))],
        [(_sds((s_len, d), BF), tile_ji)] * 2
        + [(dproj_shape, pl.BlockSpec((tmb, 2 * tn), lambda j, i: (i, gate0 // 2 + j))),
           (_sds((8, d), F32), pl.BlockSpec((8, tn), lambda j, i: (0, j)))],
        NT, gate_bwd, nb=2)

    za_t = 3 * a // tn

    def attn_gate_bwd(dya, ex, outs):
        silu, dsilu = _silu_and_grad(ex[0][...].astype(F32))
        outs[0][...] = (dya * silu).astype(BF)
        outs[1][...] = (dya * ex[1][...].astype(F32) * dsilu).astype(BF)

    datt, dproj = _mm(
        "attn_gate_bwd", (s_len // tm, a // tn),
        [(d_am, pl.BlockSpec((tm, d), lambda i, j: (i, 0))), (woa, pl.BlockSpec((tn, d), lambda i, j: (j, 0))),
         (proj, pl.BlockSpec((tm, tn), lambda i, j: (i, za_t + j))), (att, tile_ij), (dproj, hbm)],
        [(_sds((s_len, a), BF), tile_ij), (dproj_shape, pl.BlockSpec((tm, tn), lambda i, j: (i, za_t + j)))],
        NT, attn_gate_bwd, aliases={4: 1})

    zp_t = (4 * a + p) // tn

    def pool_gate_bwd(dyp, ex, outs):
        z_ref, y_ref, sc_ref, _ = ex
        dzp_ref, dy_ref, dps_ref = outs
        i = pl.program_id(1)
        silu, dsilu = _silu_and_grad(z_ref[...].astype(F32))
        y = y_ref[...].astype(F32)
        sc = sc_ref[...]
        dyp0 = dyp * silu
        dzp_ref[...] = (dyp * (y * sc) * dsilu).astype(BF)
        dy_ref[...] = (dyp0 * sc).astype(BF)
        dps = jnp.sum(dyp0 * y, axis=0, keepdims=True)

        @pl.when(i == 0)
        def _():
            dps_ref[...] = dps

        @pl.when(i > 0)
        def _():
            dps_ref[...] += dps

    dproj, dy_pool, dps = _mm(
        "pool_gate_bwd", (p // tn, s_len // tmb),
        [(d_pm, pl.BlockSpec((tmb, d), lambda j, i: (i, 0))), (wop, pl.BlockSpec((tn, d), lambda j, i: (j, 0))),
         (proj, pl.BlockSpec((tmb, tn), lambda j, i: (i, zp_t + j))), (ypre, tile_ji),
         (scale_row, pl.BlockSpec((1, tn), lambda j, i: (0, j))), (dproj, hbm)],
        [(dproj_shape, pl.BlockSpec((tmb, tn), lambda j, i: (i, zp_t + j))), (_sds((s_len, p), BF), tile_ji),
         (_sds((1, p), F32), pl.BlockSpec((1, tn), lambda j, i: (0, j)))],
        NT, pool_gate_bwd, aliases={5: 0})

    dproj, dpw = _pool_bwd(dy_pool, dmean, pw, dproj, u_blk)

    o_wop, o_wo, o_pool = a, d, 2 * d
    slab_a = _sds((2, 4, 2 * d + cg, sw), BF)
    slab_b = _sds((2, 4, 5 * d // 2, sw), BF)
    hbm = pl.BlockSpec(memory_space=pl.ANY)
    tmw = _tile(a, 1024)
    core = lax.axis_index("c").astype(jnp.int32).reshape(1)

    def pack_small(dpw_ref, dgb_ref, o_ref):
        rows = cg // N_DEV
        for j in range(N_DEV):
            for g in range(ng):
                o_ref[j % 2, j // 2, g * rows:(g + 1) * rows, :] = dpw_ref[g, j * rows:(j + 1) * rows, :].astype(BF)
            o_ref[j % 2, j // 2, ng * rows:, :] = jnp.concatenate(
                [dgb_ref[:, j * sw:(j + 1) * sw], jnp.zeros((cg - ng * rows - 8, sw), F32)], axis=0).astype(BF)

    slab = pl.pallas_call(
        pack_small, grid=(1,),
        in_specs=[pl.BlockSpec((ng, cg, cg), lambda i: (0, 0, 0)), pl.BlockSpec((8, d), lambda i: (0, 0))],
        out_specs=pl.BlockSpec((2, 4, cg, sw), lambda i: (0, 0, o_pool // cg, 0)), out_shape=slab_a,
        name="dw_small", compiler_params=_params(1))(dpw, dgb8)

    def into_slab(acc, _, outs):
        outs[0][...] = acc.astype(BF)

    def weight_grad(name, slab, lhs, rhs, grid, lhs_spec, rhs_spec, out_spec, epi=into_slab):
        return _mm(name, grid, [(lhs, lhs_spec), (rhs, rhs_spec), (slab, hbm)], [(slab_a, out_spec)],
                   TN, epi, aliases={2: 0})

    def into_both_cores(acc, _, outs):
        outs[0][0] = acc[:sw].astype(BF)
        outs[0][1] = acc[sw:].astype(BF)

    slab = weight_grad("dw_out", slab, merged, dx2b, (N_DEV // 2, d // sw),
                       pl.BlockSpec((s_len, 2 * sw), lambda k, t: (0, k)),
                       pl.BlockSpec((s_len, sw), lambda k, t: (0, t)),
                       pl.BlockSpec((2, None, sw, sw), lambda k, t: (0, k, o_wo // sw + t, 0)), into_both_cores)[0]
    slab = weight_grad("dw_out_attn", slab, ya, d_am, (a // tmw, N_DEV),
                       pl.BlockSpec((s_len, tmw), lambda i, j: (0, i)), pl.BlockSpec((s_len, sw), lambda i, j: (0, j)),
                       pl.BlockSpec((None, None, tmw, sw), lambda i, j: (j % 2, j // 2, i, 0)))[0]
    slab = weight_grad("dw_out_pool", slab, yp, d_pm, (p // tmw, N_DEV),
                       pl.BlockSpec((s_len, tmw), lambda i, j: (0, i)), pl.BlockSpec((s_len, sw), lambda i, j: (0, j)),
                       pl.BlockSpec((None, None, tmw, sw), lambda i, j: (j % 2, j // 2, o_wop // tmw + i, 0)))[0]

    (dproj, dk, dv, ddiag), (from_sibling_a,) = _attn_bwd(proj, datt, base, a, dproj, comm=_cores_comm(slab))
    chip_part_a = _add_core_partials("add_core_partials_a", slab, from_sibling_a, core, (2 * d + cg) // 2)
    drb = _bias_grad(ddiag.reshape(heads, SKEW))
    dproj = lax.dynamic_update_slice(dproj, dk, (0, a))
    dproj = lax.dynamic_update_slice(dproj, dv, (0, 2 * a))

    tmd = _tile(d // 2, 1024)
    nrb = d // 2 // tmd

    def dw_in_rows(name, half, comm):
        return _mm(
            name, (nrb, n_in // sw),
            [(hb, pl.BlockSpec((s_len, tmd), lambda i, t: (0, half * nrb + i))),
             (dproj, pl.BlockSpec((s_len, sw), lambda i, t: (0, pos(t))))],
            [(slab_b, pl.BlockSpec((None, None, tmd, sw),
                                   lambda i, t: ((t // per) % 2, (t // per) // 2, (t % per) * nrb + i, 0)))],
            TN, into_slab, comm=comm)

    slab_lo, parts_a = dw_in_rows("dw_in_lo", 0, _chips_comm(chip_part_a))
    slab_hi, from_sibling_lo = dw_in_rows("dw_in_hi", 1, _cores_comm(slab_lo))
    chip_part_lo = _add_core_partials("add_core_partials_lo", slab_lo, from_sibling_lo, core, 4 * sw)

    tnh = _tile(d, 1024)
    tmh = tm if s_len > tm else s_len // 2
    n_row = s_len // tmh

    def dh_rows(name, lo, hi, prev, comm):
        def body(in_refs, out_refs, scratch_refs):
            acc_ref = scratch_refs[0]
            k = pl.program_id(2)
            part = lax.dot_general(in_refs[0][...], in_refs[per][...], NT, preferred_element_type=F32)
            for r in range(1, per):
                part += lax.dot_general(in_refs[r][...], in_refs[per + r][...], NT, preferred_element_type=F32)

            @pl.when(k == 0)
            def _():
                acc_ref[...] = part

            @pl.when(k > 0)
            def _():
                acc_ref[...] += part

            @pl.when(k == N_DEV - 1)
            def _():
                out_refs[0][...] = acc_ref[...]

        ins = [(dproj, pl.BlockSpec((tmh, sw), lambda i, j, k, r=r: (lo + i, pos(per * k + r)))) for r in range(per)]
        ins += [(w, pl.BlockSpec((None, tnh, sw), lambda i, j, k: (k, j, 0))) for w in win_rounds]
        if prev is not None:
            ins.append((prev, hbm))
        return _call(name, body, (hi - lo, d // tnh, N_DEV), ins,
                     [(_sds((s_len, d), F32), pl.BlockSpec((tmh, tnh), lambda i, j, k: (lo + i, j)))],
                     scratch=[pltpu.VMEM((tmh, tnh), F32)], aliases={2 * per: 0} if prev is not None else None,
                     comm=comm)

    dh, parts_lo, from_sibling_hi = dh_rows("dh_head", 0, n_row // 2, None,
                                            _both(_chips_comm(chip_part_lo), _cores_comm(slab_hi)))
    chip_part_hi = _add_core_partials("add_core_partials_hi", slab_hi, from_sibling_hi, core, 4 * sw)
    dh, parts_hi = dh_rows("dh_rest", n_row // 2, n_row, dh, _chips_comm(chip_part_hi))

    dx, dg1 = _rms_bwd(x2d, dh, dx2, g1)

    g_win, d_win, m_win, v_win = _adamw_row_halves("adamw_w_in", parts_lo, parts_hi, w_in, m_w_in, v_w_in)
    g_woa, d_woa, m_woa, v_woa = _adamw_shard("adamw_w_out_attn", parts_a, w_out_attn, m_w_out_attn, v_w_out_attn, 0)
    g_wop, d_wop, m_wop, v_wop = _adamw_shard("adamw_w_out_pool", parts_a, w_out_pool, m_w_out_pool, v_w_out_pool, o_wop)
    g_wo, d_wo, m_wo, v_wo = _adamw_shard("adamw_w_out", parts_a, w_out, m_w_out, v_w_out, o_wo)
    flat = lambda t: t.reshape(cg // 2, sw)
    pool_out = _adamw_shard("adamw_pool_w", parts_a, flat(pool_w), flat(m_pool_w), flat(v_pool_w), o_pool)
    g_pw, d_pw, m_pw, v_pw = [t.reshape(pool_w.shape) for t in pool_out]
    pad16 = lambda t: jnp.pad(t, ((0, 14), (0, 0)))
    gb_out = _adamw_shard("adamw_gate_bias", parts_a, pad16(gate_bias), pad16(m_gate_bias), pad16(v_gate_bias),
                          o_pool + cg // 2)
    g_gb, d_gb, m_gb, v_gb = [t[:2] for t in gb_out]

    def pack(n_gain, f_gain, scale, rb, last):
        rows = [n_gain.reshape(-1, LANES), f_gain.reshape(-1, LANES), scale.reshape(-1, LANES),
                rb.reshape(-1, LANES), last]
        return jnp.concatenate(rows, axis=0)

    pad_rb = lambda t: jnp.pad(t, ((0, 0), (0, N_REL_PAD - N_REL)))
    zeros8 = jnp.zeros((8, LANES), F32)
    loss_rows = jnp.pad(loss_part, ((0, 7), (0, 0)))
    small = _small_allreduce_adamw(
        pack(dg1, dg2, dps, drb, loss_rows),
        pack(norm_gain, final_gain, pool_scale, pad_rb(rel_bias), zeros8),
        pack(m_norm_gain, m_final_gain, m_pool_scale, pad_rb(m_rel_bias), zeros8),
        pack(v_norm_gain, v_final_gain, v_pool_scale, pad_rb(v_rel_bias), zeros8))

    n1, n2, n3 = d // LANES, 2 * d // LANES, (2 * d + p) // LANES
    n4 = n3 + heads * N_REL_PAD // LANES

    def unpack(t):
        return (t[:n1].reshape(d), t[n1:n2].reshape(d), t[n2:n3].reshape(p),
                t[n3:n4].reshape(heads, N_REL_PAD)[:, :N_REL])

    (g_ng, g_fg, g_ps, g_rb), (d_ng, d_fg, d_ps, d_rb), (m_ng, m_fg, m_ps, m_rb), (v_ng, v_fg, v_ps, v_rb) = [
        unpack(t) for t in small]
    loss = small[0][n4, 0]

    return (loss, dx.reshape(x.shape),
            g_ng, g_win, g_rb, g_pw, g_ps, g_woa, g_wop, g_gb, g_wo, g_fg,
            d_ng, d_win, d_rb, d_pw, d_ps, d_woa, d_wop, d_gb, d_wo, d_fg,
            m_ng, m_win, m_rb, m_pw, m_ps, m_woa, m_wop, m_gb, m_wo, m_fg,
            v_ng, v_win, v_rb, v_pw, v_ps, v_woa, v_wop, v_gb, v_wo, v_fg)
```

```python
import jax
import jax.numpy as jnp
from jax import lax
from jax.experimental import pallas as pl
from jax.experimental.pallas import tpu as pltpu

F32 = jnp.float32
BF = jnp.bfloat16
MESH = pl.DeviceIdType.MESH

N_DEV = 8
CHUNK = 64
N_LEFT_CHUNKS = 8
HEAD_DIM = 128
MAX_REL = 128
N_REL = 2 * MAX_REL + 1
N_REL_PAD = 384
POOL_WINDOWS = (2, 4, 8, 16)
HALO = 16
EPS = 1e-6
ADAM_LR = 0.001
ADAM_B1 = 0.9
ADAM_B2 = 0.999
ADAM_EPS = 1e-08
ADAM_WD = 0.01
ADAM_STEP = 10
NEG = -1e30
LANES = 128
TQ = N_LEFT_CHUNKS * CHUNK
TK = 2 * TQ
SKEW = 2 * TK
VMEM_LIMIT = 52 * 1024 * 1024

NN = (((1,), (0,)), ((), ()))
NT = (((1,), (1,)), ((), ()))
TN = (((0,), (0,)), ((), ()))


def _params(n_grid):
    return pltpu.CompilerParams(dimension_semantics=("arbitrary",) * n_grid, vmem_limit_bytes=VMEM_LIMIT)


def _sig(z):
    return 1.0 / (1.0 + jnp.exp(-z))


def _silu_and_grad(z):
    s = _sig(z)
    return z * s, s * (1.0 + z * (1.0 - s))


def _tile(n, pref):
    t = min(n, pref)
    assert n % t == 0, (n, pref)
    return t


def _sds(shape, dtype):
    return jax.ShapeDtypeStruct(shape, dtype)


class _Comm:
    def __init__(self, ins, outs, scratch, start, wait, hooks=()):
        self.ins, self.outs, self.scratch = list(ins), list(outs), list(scratch)
        self.start, self.wait, self.hooks = start, wait, tuple(hooks)


def _call(name, body, grid, ins, outs, scratch=(), aliases=None, comm=None):
    n_in, n_out, n_scr = len(ins), len(outs), len(scratch)
    c_in = len(comm.ins) if comm else 0
    c_out = len(comm.outs) if comm else 0
    n_steps = 1
    for g in grid:
        n_steps *= g

    def kern(*refs):
        o0 = n_in + c_in
        s0 = o0 + n_out + c_out
        if comm:
            c_refs = (refs[n_in:o0], refs[o0 + n_out:s0], refs[s0 + n_scr:])
            step = pl.program_id(0)
            for ax in range(1, len(grid)):
                step = step * grid[ax] + pl.program_id(ax)

            @pl.when(step == 0)
            def _():
                comm.start(*c_refs)

            for frac, hook in comm.hooks:
                @pl.when(step == int(frac * n_steps))
                def _(hook=hook):
                    hook(*c_refs)

        body(refs[:n_in], refs[o0:o0 + n_out], refs[s0:s0 + n_scr])

        if comm:
            @pl.when(step == n_steps - 1)
            def _():
                comm.wait(*c_refs)

    hbm = pl.BlockSpec(memory_space=pl.ANY)
    return pl.pallas_call(
        kern, grid=grid,
        in_specs=[s for _, s in ins] + [hbm] * c_in, out_specs=[s for _, s in outs] + [hbm] * c_out,
        out_shape=[o for o, _ in outs] + (comm.outs if comm else []),
        scratch_shapes=list(scratch) + (comm.scratch if comm else []),
        name=name, compiler_params=_params(len(grid)), input_output_aliases=aliases or {},
    )(*([a for a, _ in ins] + (comm.ins if comm else [])))


def _mm(name, grid, ins, outs, dims, epi, aliases=None, comm=None, nb=1):
    def body(in_refs, out_refs, _):
        if nb == 1:
            acc = lax.dot_general(in_refs[0][...], in_refs[1][...], dims, preferred_element_type=F32)
        else:
            acc, k0 = 0.0, 0
            for b_ref in in_refs[1:1 + nb]:
                k1 = k0 + b_ref.shape[1]
                acc = acc + lax.dot_general(in_refs[0][:, k0:k1], b_ref[...], dims, preferred_element_type=F32)
                k0 = k1
        epi(acc, in_refs[1 + nb:], out_refs)

    return _call(name, body, grid, ins, outs, aliases=aliases, comm=comm)


def _rms_fwd(x, g, comm=None):
    s, d = x.shape
    tr = _tile(s, 256)

    def body(in_refs, out_refs, _):
        xv = in_refs[0][...]
        r = lax.rsqrt(jnp.mean(xv * xv, axis=-1, keepdims=True) + EPS)
        out_refs[0][...] = (xv * r * in_refs[1][...]).astype(BF)

    row = pl.BlockSpec((tr, d), lambda i: (i, 0))
    return _call("rms_fwd", body, (s // tr,), [(x, row), (g, pl.BlockSpec((1, d), lambda i: (0, 0)))],
                 [(_sds((s, d), BF), row)], comm=comm)


def _final_norm(x2, target, g):
    s, d = x2.shape
    tr = _tile(s, 128)

    def kern(x_ref, t_ref, g_ref, dx_ref, dxb_ref, dg_ref, loss_ref):
        i = pl.program_id(0)
        xv = x_ref[...]
        gv = g_ref[...]
        r = lax.rsqrt(jnp.mean(xv * xv, axis=-1, keepdims=True) + EPS)
        xhat = xv * r
        err = xhat * gv - t_ref[...]
        dy = err * (1.0 / d)
        gy = dy * gv
        dx = r * (gy - xhat * jnp.mean(gy * xhat, axis=-1, keepdims=True))
        dx_ref[...] = dx
        dxb_ref[...] = dx.astype(BF)
        dg = jnp.sum(dy * xhat, axis=0, keepdims=True)
        ls = jnp.broadcast_to(0.5 * jnp.sum(jnp.mean(err * err, axis=-1, keepdims=True)), (1, LANES))

        @pl.when(i == 0)
        def _():
            dg_ref[...] = dg
            loss_ref[...] = ls

        @pl.when(i > 0)
        def _():
            dg_ref[...] += dg
            loss_ref[...] += ls

    row = pl.BlockSpec((tr, d), lambda i: (i, 0))
    vec = pl.BlockSpec((1, d), lambda i: (0, 0))
    return pl.pallas_call(
        kern, grid=(s // tr,), in_specs=[row, row, vec],
        out_specs=[row, row, vec, pl.BlockSpec((1, LANES), lambda i: (0, 0))],
        out_shape=[_sds((s, d), F32), _sds((s, d), BF), _sds((1, d), F32), _sds((1, LANES), F32)],
        name="final_norm", compiler_params=_params(1))(x2, target, g)


def _rms_bwd(x, dh, dx2, g):
    s, d = x.shape
    tr = _tile(s, 128)

    def kern(x_ref, dh_ref, dx2_ref, g_ref, dx_ref, dg_ref):
        i = pl.program_id(0)
        xv = x_ref[...]
        r = lax.rsqrt(jnp.mean(xv * xv, axis=-1, keepdims=True) + EPS)
        xhat = xv * r
        dhv = dh_ref[...]
        gh = dhv * g_ref[...]
        dx_ref[...] = dx2_ref[...] + r * (gh - xhat * jnp.mean(gh * xhat, axis=-1, keepdims=True))
        dg = jnp.sum(dhv * xhat, axis=0, keepdims=True)

        @pl.when(i == 0)
        def _():
            dg_ref[...] = dg

        @pl.when(i > 0)
        def _():
            dg_ref[...] += dg

    row = pl.BlockSpec((tr, d), lambda i: (i, 0))
    vec = pl.BlockSpec((1, d), lambda i: (0, 0))
    return pl.pallas_call(
        kern, grid=(s // tr,), in_specs=[row, row, row, vec], out_specs=[row, vec],
        out_shape=[_sds((s, d), F32), _sds((1, d), F32)],
        name="rms_bwd", compiler_params=_params(1))(x, dh, dx2, g)


def _rel_index(j, backward):
    if backward:
        rel = 2 * TQ - 1 - j
    else:
        rel = TQ - jnp.where(j < TK, j, j - SKEW)
    return jnp.clip(rel, -MAX_REL, MAX_REL) + MAX_REL


def _bias_rows(rel_bias_pad):
    h = rel_bias_pad.shape[0]

    def kern(rb_ref, o_ref):
        j = lax.broadcasted_iota(jnp.int32, (N_REL_PAD, SKEW), 1)
        k = lax.broadcasted_iota(jnp.int32, (N_REL_PAD, SKEW), 0)
        onehot = (_rel_index(j, False) == k).astype(F32)
        o_ref[...] = jnp.dot(rb_ref[...], onehot, preferred_element_type=F32, precision=lax.Precision.HIGHEST)

    return pl.pallas_call(kern, out_shape=_sds((h, SKEW), F32), name="bias_rows")(rel_bias_pad)


def _bias_grad(ddiag):
    h = ddiag.shape[0]

    def kern(d_ref, o_ref):
        j = lax.broadcasted_iota(jnp.int32, (N_REL_PAD, SKEW), 1)
        k = lax.broadcasted_iota(jnp.int32, (N_REL_PAD, SKEW), 0)
        onehot = ((_rel_index(j, True) == k) & (j < TQ + TK - 1)).astype(F32)
        o_ref[...] = lax.dot_general(d_ref[...], onehot, NT, preferred_element_type=F32,
                                     precision=lax.Precision.HIGHEST)

    return pl.pallas_call(kern, out_shape=_sds((h, N_REL_PAD), F32), name="bias_grad")(ddiag)


def _bias_tile(row):
    t = pltpu.roll(jnp.broadcast_to(row, (TQ, SKEW)), 0, 1, stride=1, stride_axis=0)[:, :TK]
    r = lax.broadcasted_iota(jnp.int32, (TQ, TK), 0) // CHUNK
    c = lax.broadcasted_iota(jnp.int32, (TQ, TK), 1) // CHUNK
    dist = N_LEFT_CHUNKS + r - c
    return jnp.where((dist >= 0) & (dist <= N_LEFT_CHUNKS), t, NEG)


def _scores(q, keys, tile):
    s = lax.dot_general(q, keys, NT, preferred_element_type=F32) * (HEAD_DIM ** -0.5) + tile
    m = jnp.max(s, axis=1, keepdims=True)
    p = jnp.exp(s - m)
    return p, jnp.sum(p, axis=1, keepdims=True)


HQ = TQ // 2
BANDS = ((0, 0, TQ + HQ), (1, HQ, TK))
BANDS_FIRST = ((0, TQ, TQ + HQ), (1, TQ, TK))


def _key_rows(prev_ref, cur_ref, first, end):
    if first >= TQ:
        return cur_ref[first - TQ:end - TQ, :]
    return jnp.concatenate([prev_ref[first:, :], cur_ref[:end - TQ, :]], axis=0)


def _attn_fwd(proj, base, a_width, comm=None):
    s_len = proj.shape[0]
    heads = a_width // HEAD_DIM
    nq = s_len // TQ
    kb, vb, zb = heads, 2 * heads, 3 * heads

    def kern(q_ref, kp_ref, kc_ref, vp_ref, vc_ref, z_ref, base_ref, att_ref, ya_ref, tile_ref):
        i = pl.program_id(1)

        def attend(bands):
            for half, first, end in bands:
                rows = slice(half * HQ, (half + 1) * HQ)
                p, l = _scores(q_ref[rows, :], _key_rows(kp_ref, kc_ref, first, end), tile_ref[rows, first:end])
                o = jnp.dot(p.astype(BF), _key_rows(vp_ref, vc_ref, first, end), preferred_element_type=F32) / l
                att_ref[rows, :] = o.astype(BF)
                z = z_ref[rows, :].astype(F32)
                ya_ref[rows, :] = (o * (z * _sig(z))).astype(BF)

        @pl.when(i == 0)
        def _():
            tile_ref[...] = _bias_tile(base_ref[...])
            attend(BANDS_FIRST)

        @pl.when(i > 0)
        def _():
            attend(BANDS)

    blk = lambda off: pl.BlockSpec((TQ, HEAD_DIM), lambda h, i: (i, off + h))
    prev = lambda off: pl.BlockSpec((TQ, HEAD_DIM), lambda h, i: (jnp.maximum(i - 1, 0), off + h))
    out = pl.BlockSpec((TQ, HEAD_DIM), lambda h, i: (i, h))
    def body(in_refs, out_refs, scratch_refs):
        kern(*in_refs, *out_refs, *scratch_refs)

    act = _sds((s_len, a_width), BF)
    res = _call(
        "attn_fwd", body, (heads, nq),
        [(proj, blk(0)), (proj, prev(kb)), (proj, blk(kb)), (proj, prev(vb)), (proj, blk(vb)), (proj, blk(zb)),
         (base, pl.BlockSpec((None, 1, SKEW), lambda h, i: (h, 0, 0)))],
        [(act, out), (act, out)], scratch=[pltpu.VMEM((TQ, TK), F32)], comm=comm)
    return res[:2], res[2:]


def _attn_bwd(proj, datt, base, a_width, dproj, comm=None):
    s_len = proj.shape[0]
    heads = a_width // HEAD_DIM
    nq = s_len // TQ
    kb, vb = heads, 2 * heads
    scale = HEAD_DIM ** -0.5

    def body(in_refs, out_refs, scratch_refs):
        q_ref, kp_ref, kc_ref, vp_ref, vc_ref, do_ref, base_ref, _ = in_refs
        dq_ref, dk_ref, dv_ref, dd_ref = out_refs
        tile_ref, dsacc_ref, ak_ref, av_ref = scratch_refs
        i = pl.program_id(1)

        def backward(keys, vals, tile, cols):
            q = q_ref[...]
            do = do_ref[...]
            p, l = _scores(q, keys, tile)
            p = p / l
            dp = lax.dot_general(do, vals, NT, preferred_element_type=F32)
            ds = p * (dp - jnp.sum(p * dp, axis=1, keepdims=True))
            dsacc_ref[:, cols] += ds
            dsb = ds.astype(BF)
            dq_ref[...] = (jnp.dot(dsb, keys, preferred_element_type=F32) * scale).astype(BF)
            return (lax.dot_general(dsb, q, TN, preferred_element_type=F32) * scale,
                    lax.dot_general(p.astype(BF), do, TN, preferred_element_type=F32))

        @pl.when(i == 0)
        def _():
            tile_ref[...] = _bias_tile(base_ref[...])
            dsacc_ref[...] = jnp.zeros_like(dsacc_ref)
            ak_ref[...], av_ref[...] = backward(kc_ref[...], vc_ref[...], tile_ref[:, TQ:], slice(TQ, TK))
            dk_ref[...] = jnp.zeros_like(dk_ref)
            dv_ref[...] = jnp.zeros_like(dv_ref)

        @pl.when((i > 0) & (i < nq))
        def _():
            dkc, dvc = backward(jnp.concatenate([kp_ref[...], kc_ref[...]], axis=0),
                                jnp.concatenate([vp_ref[...], vc_ref[...]], axis=0), tile_ref[...], slice(0, TK))
            dk_ref[...] = (ak_ref[...] + dkc[:TQ]).astype(BF)
            dv_ref[...] = (av_ref[...] + dvc[:TQ]).astype(BF)
            ak_ref[...] = dkc[TQ:]
            av_ref[...] = dvc[TQ:]

        @pl.when(i == nq)
        def _():
            dk_ref[...] = ak_ref[...].astype(BF)
            dv_ref[...] = av_ref[...].astype(BF)
            acc = dsacc_ref[...]
            rr = lax.broadcasted_iota(jnp.int32, (TQ, TQ), 0)
            cc = lax.broadcasted_iota(jnp.int32, (TQ, TQ), 1)
            flip = (rr + cc == TQ - 1).astype(BF)
            hi = acc.astype(BF)
            lo = (acc - hi.astype(F32)).astype(BF)
            rev = jnp.dot(flip, hi, preferred_element_type=F32) + jnp.dot(flip, lo, preferred_element_type=F32)
            wide = jnp.concatenate([rev, jnp.zeros((TQ, SKEW - TK), F32)], axis=1)
            dd_ref[...] = jnp.sum(pltpu.roll(wide, 0, 1, stride=1, stride_axis=0), axis=0, keepdims=True)

    last = nq - 1
    cur = lambda off: pl.BlockSpec((TQ, HEAD_DIM), lambda h, i: (jnp.minimum(i, last), off + h))
    prev = lambda off: pl.BlockSpec((TQ, HEAD_DIM), lambda h, i: (jnp.maximum(jnp.minimum(i, last) - 1, 0), off + h))
    done = pl.BlockSpec((TQ, HEAD_DIM), lambda h, i: (jnp.maximum(i - 1, 0), h))
    row = pl.BlockSpec((None, 1, SKEW), lambda h, i: (h, 0, 0))
    act = _sds((s_len, a_width), BF)
    res = _call(
        "attn_bwd", body, (heads, nq + 1),
        [(proj, cur(0)), (proj, prev(kb)), (proj, cur(kb)), (proj, prev(vb)), (proj, cur(vb)), (datt, cur(0)),
         (base, row), (dproj, pl.BlockSpec(memory_space=pl.ANY))],
        [(_sds(dproj.shape, dproj.dtype), cur(0)), (act, done), (act, done), (_sds((heads, 1, SKEW), F32), row)],
        aliases={7: 0},
        scratch=[pltpu.VMEM((TQ, TK), F32), pltpu.VMEM((TQ, TK), F32),
                 pltpu.VMEM((TQ, HEAD_DIM), F32), pltpu.VMEM((TQ, HEAD_DIM), F32)],
        comm=comm)
    return res[:4], res[4:]


def _pool_fwd(proj, pool_w, pool_scale, p_width, u_blk, z_blk):
    s_len = proj.shape[0]
    cg = p_width // len(POOL_WINDOWS)
    tt = _tile(s_len, 512)

    def kern(up_ref, uc_ref, z_ref, pw_ref, sc_ref, d_ref, y_ref, yp_ref):
        t = pl.program_id(0)
        row = lax.broadcasted_iota(jnp.int32, (tt, 1), 0) + t * tt
        for g, w in enumerate(POOL_WINDOWS):
            cs = slice(g * cg, (g + 1) * cg)
            prev = jnp.where(t == 0, 0.0, up_ref[:, cs].astype(F32))
            cur = uc_ref[:, cs].astype(F32)
            ws = jnp.concatenate([prev, cur], axis=0)
            sh = 1
            while sh < w:
                ws = ws + pltpu.roll(ws, sh, 0)
                sh *= 2
            cnt = jnp.minimum(row + 1, w).astype(F32)
            db = (ws[HALO:, :] / cnt - cur).astype(BF)
            y = jnp.dot(db, pw_ref[g], preferred_element_type=F32)
            d_ref[:, cs] = db
            y_ref[:, cs] = y.astype(BF)
            z = z_ref[:, cs].astype(F32)
            yp_ref[:, cs] = (y * sc_ref[:, cs] * (z * _sig(z))).astype(BF)

    full = pl.BlockSpec((tt, p_width), lambda t: (t, 0))
    return pl.pallas_call(
        kern, grid=(s_len // tt,),
        in_specs=[pl.BlockSpec((HALO, p_width), lambda t: (jnp.maximum(t * (tt // HALO) - 1, 0), u_blk)),
                  pl.BlockSpec((tt, p_width), lambda t: (t, u_blk)),
                  pl.BlockSpec((tt, p_width), lambda t: (t, z_blk)),
                  pl.BlockSpec((len(POOL_WINDOWS), cg, cg), lambda t: (0, 0, 0)),
                  pl.BlockSpec((1, p_width), lambda t: (0, 0))],
        out_specs=[full, full, full], out_shape=[_sds((s_len, p_width), BF)] * 3,
        name="pool_fwd", compiler_params=_params(1))(proj, proj, proj, pool_w, pool_scale)


def _pool_bwd(dy, dmean, pool_w, dproj, u_blk):
    s_len, p_width = dy.shape
    ng = len(POOL_WINDOWS)
    cg = p_width // ng
    tt = _tile(s_len, 512)
    nt = s_len // tt

    def kern(dyc_ref, dyn_ref, d_ref, pw_ref, _, du_ref, dpw_ref):
        t = pl.program_id(0)

        @pl.when(t == 0)
        def _():
            dpw_ref[...] = jnp.zeros_like(dpw_ref)

        row = lax.broadcasted_iota(jnp.int32, (tt + HALO, 1), 0) + t * tt
        for g, w in enumerate(POOL_WINDOWS):
            cs = slice(g * cg, (g + 1) * cg)
            dyc = dyc_ref[:, cs]
            ddc = lax.dot_general(dyc, pw_ref[g], NT, preferred_element_type=F32)
            ddn = lax.dot_general(dyn_ref[:, cs], pw_ref[g], NT, preferred_element_type=F32)
            ddn = jnp.where(t == nt - 1, 0.0, ddn)
            cnt = jnp.minimum(row + 1, w).astype(F32)
            ws = jnp.concatenate([ddc, ddn], axis=0) / cnt
            sh = 1
            while sh < w:
                ws = ws + pltpu.roll(ws, tt + HALO - sh, 0)
                sh *= 2
            du_ref[:, cs] = (ws[:tt, :] - ddc).astype(BF)
            dpw_ref[g] += lax.dot_general(d_ref[:, cs], dyc, TN, preferred_element_type=F32)

    full = pl.BlockSpec((tt, p_width), lambda t: (t, 0))
    pw_spec = pl.BlockSpec((ng, cg, cg), lambda t: (0, 0, 0))
    return pl.pallas_call(
        kern, grid=(nt,),
        in_specs=[full,
                  pl.BlockSpec((HALO, p_width), lambda t: (jnp.minimum((t + 1) * (tt // HALO), s_len // HALO - 1), 0)),
                  full, pw_spec, pl.BlockSpec(memory_space=pl.ANY)],
        out_specs=[pl.BlockSpec((tt, p_width), lambda t: (t, u_blk)), pw_spec],
        out_shape=[_sds(dproj.shape, dproj.dtype), _sds((ng, cg, cg), F32)],
        input_output_aliases={4: 0},
        name="pool_bwd", compiler_params=_params(1))(dy, dy, dmean, pool_w, dproj)


def _adam(g, w_ref, m_ref, v_ref, g_out, d_out, m_out, v_out):
    m = ADAM_B1 * m_ref[...] + (1.0 - ADAM_B1) * g
    v = ADAM_B2 * v_ref[...] + (1.0 - ADAM_B2) * (g * g)
    m_hat = m / (1.0 - ADAM_B1 ** ADAM_STEP)
    v_hat = v / (1.0 - ADAM_B2 ** ADAM_STEP)
    g_out[...] = g
    d_out[...] = -ADAM_LR * (m_hat / (jnp.sqrt(v_hat) + ADAM_EPS) + ADAM_WD * w_ref[...])
    m_out[...] = m
    v_out[...] = v


def _adamw_shard(name, parts, w, m, v, row_off):
    rw, cw = w.shape
    sw = parts.shape[2]
    tr = _tile(rw, 512)
    assert row_off % tr == 0 and cw % sw == 0

    def kern(b_ref, w_ref, m_ref, v_ref, g_out, d_out, m_out, v_out):
        b = b_ref[...].astype(F32)
        _adam(((b[0] + b[1]) + b[2]) + b[3], w_ref, m_ref, v_ref, g_out, d_out, m_out, v_out)

    blk = pl.BlockSpec((tr, sw), lambda ct, i: (i, ct))
    return pl.pallas_call(
        kern, grid=(cw // sw, rw // tr),
        in_specs=[pl.BlockSpec((4, tr, sw), lambda ct, i: (0, (row_off + ct * rw) // tr + i, 0)), blk, blk, blk],
        out_specs=[blk] * 4, out_shape=[_sds((rw, cw), F32)] * 4,
        name=name, compiler_params=_params(2))(parts, w, m, v)


def _adamw_row_halves(name, parts_lo, parts_hi, w, m, v):
    rw, cw = w.shape
    sw = parts_lo.shape[2]
    half = rw // 2
    tr = _tile(half, 512)
    nh = half // tr

    def kern(lo_ref, hi_ref, w_ref, m_ref, v_ref, g_out, d_out, m_out, v_out):
        i = pl.program_id(1)

        def update(b_ref):
            b = b_ref[...].astype(F32)
            _adam(((b[0] + b[1]) + b[2]) + b[3], w_ref, m_ref, v_ref, g_out, d_out, m_out, v_out)

        @pl.when(i < nh)
        def _():
            update(lo_ref)

        @pl.when(i >= nh)
        def _():
            update(hi_ref)

    blk = pl.BlockSpec((tr, sw), lambda ct, i: (i, ct))
    return pl.pallas_call(
        kern, grid=(cw // sw, rw // tr),
        in_specs=[pl.BlockSpec((4, tr, sw), lambda ct, i: (0, ct * nh + jnp.minimum(i, nh - 1), 0)),
                  pl.BlockSpec((4, tr, sw), lambda ct, i: (0, ct * nh + jnp.maximum(i - nh, 0), 0)), blk, blk, blk],
        out_specs=[blk] * 4, out_shape=[_sds((rw, cw), F32)] * 4,
        name=name, compiler_params=_params(2))(parts_lo, parts_hi, w, m, v)


def _both(c1, c2):
    n_in, n_out, n_sem = len(c1.ins), len(c1.outs), len(c1.scratch)

    def split(ins, outs, sems):
        return (ins[:n_in], outs[:n_out], sems[:n_sem]), (ins[n_in:], outs[n_out:], sems[n_sem:])

    def start(*refs):
        r1, r2 = split(*refs)
        c1.start(*r1)
        c2.start(*r2)

    def wait(*refs):
        r1, r2 = split(*refs)
        c1.wait(*r1)
        c2.wait(*r2)

    def of(which, hook):
        return lambda *refs: hook(*split(*refs)[which])

    hooks = [(f, of(0, h)) for f, h in c1.hooks] + [(f, of(1, h)) for f, h in c2.hooks]
    return _Comm(c1.ins + c2.ins, c1.outs + c2.outs, c1.scratch + c2.scratch, start, wait, hooks)


def _position():
    return lax.axis_index("x"), lax.axis_index("y"), lax.axis_index("c")


def _gather_comm(shards, cols=None, pass_at=None):
    if cols is None:
        pieces = [(a, None) for a in range(len(shards))]
        shapes = [_sds(s.shape, s.dtype) for s in shards]
    else:
        half = shards[0].shape[0] // 2
        pieces = [(0, pl.ds(0, half)), (0, pl.ds(half, half))]
        shapes = [_sds((shards[0].shape[0], cols[1]), shards[0].dtype)]
    n = len(pieces)

    def plan(xs, outs, sems, only=None):
        send_sems, recv_sems, local_sems = sems
        x, y, c = _position()
        me, sibling = (x, y, c), (x, y, 1 - c)
        chips = [(1 - x, y), (x, 1 - y), (1 - x, 1 - y)]
        which = range(n) if only is None else only

        def source(v):
            a, rows = pieces[v]
            return xs[a] if rows is None else xs[a].at[rows, pl.ds(cols[0], cols[1])]

        def landing(v, block):
            a, rows = pieces[v]
            dst = outs[a].at[4 * block[0] + 2 * block[1] + block[2]]
            return dst if rows is None else dst.at[rows, :]

        def copy(v, k, block, to, own=False):
            dst = landing(v, block)
            return pltpu.make_async_remote_copy(
                src_ref=source(v) if own else dst, dst_ref=dst,
                send_sem=send_sems.at[7 * v + k], recv_sem=recv_sems.at[7 * v + k],
                device_id=to, device_id_type=MESH)

        by_chip = [(j, chip, v) for v in which for j, chip in enumerate(chips)]
        return dict(
            mine=lambda: [pltpu.make_async_copy(source(v), landing(v, me), local_sems.at[v]) for v in which],
            first=lambda: ([copy(v, 0, me, sibling, own=True) for v in which]
                           + [copy(v, 1 + j, me, (*chip, c), own=True) for j, chip, v in by_chip]),
            landed=lambda: [copy(v, 1 + j, (*chip, c), me) for j, chip, v in by_chip],
            passed=lambda: [copy(v, 4 + j, (*chip, c), sibling) for j, chip, v in by_chip],
            rest=lambda: ([copy(v, 0, sibling, me) for v in which]
                          + [copy(v, 4 + j, (*chip, 1 - c), me) for j, chip, v in by_chip]))

    def start(*refs):
        p = plan(*refs)
        for cp in p["mine"]() + p["first"]():
            cp.start()

    def pass_on(only):
        def hook(*refs):
            p = plan(*refs, only=only)
            for arrived, onward in zip(p["landed"](), p["passed"]()):
                arrived.wait_recv()
                onward.start()
        return hook

    in_halves = cols is not None and pass_at is None

    def wait(*refs):
        if in_halves:
            pass_on([1])(*refs)
        p = plan(*refs)
        for cp in p["rest"]():
            cp.wait_recv()
        for cp in p["first"]() + p["passed"]():
            cp.wait_send()
        for cp in p["mine"]():
            cp.wait()

    hooks = [(0.5, pass_on([0]))] if in_halves else [(0.75 if pass_at is None else pass_at, pass_on(None))]
    return _Comm(shards, [_sds((N_DEV,) + s.shape, s.dtype) for s in shapes],
                 [pltpu.SemaphoreType.DMA((7 * n,)), pltpu.SemaphoreType.DMA((7 * n,)),
                  pltpu.SemaphoreType.DMA((n,))], start, wait, hooks)


def _cores_comm(slab):
    _, _, r, sw = slab.shape

    def copies(ins, outs, sems):
        x, y, c = _position()
        return [pltpu.make_async_remote_copy(
            src_ref=ins[0].at[1 - c], dst_ref=outs[0], send_sem=sems[0], recv_sem=sems[1],
            device_id=(x, y, 1 - c), device_id_type=MESH)]

    def start(*refs):
        for cp in copies(*refs):
            cp.start()

    def wait(*refs):
        for cp in copies(*refs):
            cp.wait()

    return _Comm([slab], [_sds((4, r, sw), slab.dtype)],
                 [pltpu.SemaphoreType.DMA, pltpu.SemaphoreType.DMA], start, wait)


def _add_core_partials(name, slab, recv, core, tr):
    _, _, r, sw = slab.shape

    def kern(c_ref, a_ref, b_ref, o_ref):
        o_ref[...] = (a_ref[...].astype(F32) + b_ref[...].astype(F32)).astype(BF)

    return pl.pallas_call(
        kern,
        grid_spec=pltpu.PrefetchScalarGridSpec(
            num_scalar_prefetch=1, grid=(4, r // tr),
            in_specs=[pl.BlockSpec((None, None, tr, sw), lambda k, i, c_ref: (c_ref[0], k, i, 0)),
                      pl.BlockSpec((None, tr, sw), lambda k, i, c_ref: (k, i, 0))],
            out_specs=pl.BlockSpec((None, tr, sw), lambda k, i, c_ref: (k, i, 0))),
        out_shape=_sds((4, r, sw), BF), name=name, compiler_params=_params(2))(core, slab, recv)


def _chips_comm(part):
    _, r, sw = part.shape

    def copies(ins, outs, sems):
        send_sems, recv_sems, local_sem = sems
        x, y, c = _position()
        mine = 2 * x + y
        local = pltpu.make_async_copy(ins[0].at[mine], outs[0].at[mine], local_sem)
        chips = [(1 - x, y), (x, 1 - y), (1 - x, 1 - y)]
        remote = [pltpu.make_async_remote_copy(
            src_ref=ins[0].at[2 * px + py], dst_ref=outs[0].at[mine],
            send_sem=send_sems.at[j], recv_sem=recv_sems.at[j],
            device_id=(px, py, c), device_id_type=MESH) for j, (px, py) in enumerate(chips)]
        return [local] + remote

    def start(*refs):
        for cp in copies(*refs):
            cp.start()

    def wait(*refs):
        for cp in copies(*refs):
            cp.wait()

    return _Comm([part], [_sds((4, r, sw), part.dtype)],
                 [pltpu.SemaphoreType.DMA((3,)), pltpu.SemaphoreType.DMA((3,)), pltpu.SemaphoreType.DMA],
                 start, wait)


def _small_allreduce_adamw(partial, w, m, v):
    nr = partial.shape[0]

    def kern(p_ref, w_ref, m_ref, v_ref, g_out, d_out, m_out, v_out, gath_ref, send_sems, recv_sems):
        x, y, c = _position()
        me = 4 * x + 2 * y + c
        gath_ref[me] = p_ref[...]
        copies = []
        for mask in range(1, N_DEV):
            peer = (x ^ (mask >> 2), y ^ ((mask >> 1) & 1), c ^ (mask & 1))
            copies.append(pltpu.make_async_remote_copy(
                src_ref=p_ref, dst_ref=gath_ref.at[me],
                send_sem=send_sems.at[mask - 1], recv_sem=recv_sems.at[mask - 1],
                device_id=peer, device_id_type=MESH))
        for cp in copies:
            cp.start()
        for cp in copies:
            cp.wait()
        tot = gath_ref[0]
        for k in range(1, N_DEV):
            tot = tot + gath_ref[k]
        _adam(tot, w_ref, m_ref, v_ref, g_out, d_out, m_out, v_out)

    vmem = pl.BlockSpec(memory_space=pltpu.VMEM)
    return pl.pallas_call(
        kern, in_specs=[vmem] * 4, out_specs=[vmem] * 4, out_shape=[_sds((nr, LANES), F32)] * 4,
        scratch_shapes=[pltpu.VMEM((N_DEV, nr, LANES), F32),
                        pltpu.SemaphoreType.DMA((N_DEV - 1,)), pltpu.SemaphoreType.DMA((N_DEV - 1,))],
        name="small_allreduce_adamw")(partial, w, m, v)


def kernel(x, norm_gain, w_in, rel_bias, pool_w, pool_scale, w_out_attn, w_out_pool, gate_bias, w_out, final_gain, loss_target, m_norm_gain, m_w_in, m_rel_bias, m_pool_w, m_pool_scale, m_w_out_attn, m_w_out_pool, m_gate_bias, m_w_out, m_final_gain, v_norm_gain, v_w_in, v_rel_bias, v_pool_w, v_pool_scale, v_w_out_attn, v_w_out_pool, v_gate_bias, v_w_out, v_final_gain):
    _, s_len, d = x.shape
    a = w_out_attn.shape[0]
    p = w_out_pool.shape[0]
    heads = a // HEAD_DIM
    ng = len(POOL_WINDOWS)
    cg = p // ng
    sw = d // N_DEV
    n_in = w_in.shape[1] * N_DEV
    assert a == p and a + p == d and cg == sw and n_in == 5 * d and w_in.shape[1] == 5 * sw
    assert s_len % TQ == 0 and rel_bias.shape == (heads, N_REL)
    tm = _tile(s_len, 1024)
    x2d = x.reshape(s_len, d)
    tgt = loss_target.reshape(s_len, d)

    g1 = norm_gain.reshape(1, d)
    g2 = final_gain.reshape(1, d)
    scale_row = pool_scale.reshape(1, p)
    tn = sw
    per = w_in.shape[1] // tn
    hbm = pl.BlockSpec(memory_space=pl.ANY)

    w_bf = [w_in[:, r * tn:(r + 1) * tn].astype(BF) for r in range(per)]
    hb, landed = _rms_fwd(x2d, g1, comm=_gather_comm([w_bf[0]], cols=(0, tn)))
    win_rounds = []

    def store_bf16(acc, _, outs):
        outs[0][...] = acc.astype(BF)

    gate0, n_gate = (4 * a + 2 * p) // tn, d // tn

    def pos(t):
        g = t - gate0
        return jnp.where(t < gate0, t, jnp.where(g < n_gate, gate0 + 2 * g, gate0 + 2 * (g - n_gate) + 1))

    proj = None
    for r in range(per):
        win_rounds.append(landed)
        if r + 1 < per:
            comm = _gather_comm([w_bf[r + 1]], cols=(0, tn))
        else:
            comm = _gather_comm([w_out_attn.astype(BF), pool_w.astype(BF), gate_bias])
        ins = [(hb, pl.BlockSpec((tm, d), lambda i, j: (i, 0))),
               (landed, pl.BlockSpec((None, d, tn), lambda i, j: (j, 0, 0)))]
        if proj is not None:
            ins.append((proj, hbm))
        proj, landed, *rest = _mm(
            f"proj_{r}", (s_len // tm, N_DEV), ins,
            [(_sds((s_len, n_in), BF), pl.BlockSpec((tm, tn), lambda i, j, r=r: (i, pos(per * j + r))))],
            NN, store_bf16, aliases={2: 0} if r else None, comm=comm)
    woa_g = landed
    woa = woa_g.transpose(1, 0, 2).reshape(a, d)
    pw = rest[0].transpose(1, 0, 2, 3).reshape(ng, cg, cg)
    gb = rest[1].transpose(1, 0, 2).reshape(2, d)

    rb_pad = jnp.pad(rel_bias, ((0, 0), (0, N_REL_PAD - N_REL)))
    base = _bias_rows(rb_pad).reshape(heads, 1, SKEW)
    wo_bf = w_out.astype(BF)
    d_lo = d // 4
    (att, ya), (wop_g, wo_lo_g) = _attn_fwd(
        proj, base, a, comm=_both(_gather_comm([w_out_pool.astype(BF)], pass_at=0.9),
                                  _gather_comm([wo_bf], cols=(0, d_lo), pass_at=0.9)))
    wop = wop_g.transpose(1, 0, 2).reshape(p, d)
    u_blk, z_blk = 4 * a // p, 4 * a // p + 1
    dmean, ypre, yp = _pool_fwd(proj, pw, scale_row, p, u_blk, z_blk)

    def gate_kernel(in_refs, out_refs, _):
        ya_ref, woa_ref, yp_ref, wop_ref, ga_ref, gp_ref, gb_ref = in_refs
        m_ref, a_ref, p_ref = out_refs
        am = jnp.dot(ya_ref[...], woa_ref[...], preferred_element_type=F32)
        pm = jnp.dot(yp_ref[...], wop_ref[...], preferred_element_type=F32)
        sa = _sig(ga_ref[...].astype(F32) + gb_ref[0:1, :])
        sp = _sig(gp_ref[...].astype(F32) + gb_ref[1:2, :])
        m_ref[...] = (sa * am + sp * pm).astype(BF)
        a_ref[...] = am.astype(BF)
        p_ref[...] = pm.astype(BF)

    tile_ij = pl.BlockSpec((tm, tn), lambda i, j: (i, j))
    act_d = _sds((s_len, d), BF)
    merged, am, pm, wo_hi_g = _call(
        "gate_merge", gate_kernel, (s_len // tm, d // tn),
        [(ya, pl.BlockSpec((tm, a), lambda i, j: (i, 0))), (woa_g, pl.BlockSpec((None, a, tn), lambda i, j: (j, 0, 0))),
         (yp, pl.BlockSpec((tm, p), lambda i, j: (i, 0))), (wop_g, pl.BlockSpec((None, p, tn), lambda i, j: (j, 0, 0))),
         (proj, pl.BlockSpec((tm, tn), lambda i, j: (i, gate0 + 2 * j))),
         (proj, pl.BlockSpec((tm, tn), lambda i, j: (i, gate0 + 2 * j + 1))),
         (gb, pl.BlockSpec((2, tn), lambda i, j: (0, j)))],
        [(act_d, tile_ij)] * 3, comm=_gather_comm([wo_bf], cols=(d_lo, d - d_lo), pass_at=0.9))
    wo_halves = [wo_lo_g.reshape(d, d_lo), wo_hi_g.reshape(d, d - d_lo)]

    def out_proj(in_refs, out_refs, _):
        m_ref, lo_ref, hi_ref, x_ref = in_refs
        j = pl.program_id(1)

        @pl.when(j < n_half)
        def _():
            out_refs[0][...] = x_ref[...] + jnp.dot(m_ref[...], lo_ref[...], preferred_element_type=F32)

        @pl.when(j >= n_half)
        def _():
            out_refs[0][...] = x_ref[...] + jnp.dot(m_ref[...], hi_ref[...], preferred_element_type=F32)

    n_half = d_lo // tn
    x2 = _call("out_proj", out_proj, (s_len // tm, d // tn),
               [(merged, pl.BlockSpec((tm, d), lambda i, j: (i, 0))),
                (wo_halves[0], pl.BlockSpec((d, tn), lambda i, j: (0, jnp.minimum(j, n_half - 1)))),
                (wo_halves[1], pl.BlockSpec((d, tn), lambda i, j: (0, jnp.maximum(j - n_half, 0)))),
                (x2d, tile_ij)],
               [(_sds((s_len, d), F32), tile_ij)])[0]

    dx2, dx2b, dg2, loss_part = _final_norm(x2, tgt, g2)

    tmb = _tile(s_len, 512)
    tile_ji = pl.BlockSpec((tmb, tn), lambda j, i: (i, j))

    def gate_bwd(dm, ex, outs):
        a_ref, p_ref, ga_ref, gp_ref, gb_ref = ex
        da_ref, dp_ref, dgate_ref, dgb_ref = outs
        i = pl.program_id(1)
        sa = _sig(ga_ref[...].astype(F32) + gb_ref[0:1, :])
        sp = _sig(gp_ref[...].astype(F32) + gb_ref[1:2, :])
        dga = dm * a_ref[...].astype(F32) * sa * (1.0 - sa)
        dgp = dm * p_ref[...].astype(F32) * sp * (1.0 - sp)
        da_ref[...] = (dm * sa).astype(BF)
        dp_ref[...] = (dm * sp).astype(BF)
        dgate_ref[:, :tn] = dga.astype(BF)
        dgate_ref[:, tn:] = dgp.astype(BF)
        r = lax.broadcasted_iota(jnp.int32, (8, tn), 0)
        sums = jnp.where(r == 0, jnp.sum(dga, axis=0, keepdims=True),
                         jnp.where(r == 1, jnp.sum(dgp, axis=0, keepdims=True), 0.0))

        @pl.when(i == 0)
        def _():
            dgb_ref[...] = sums

        @pl.when(i > 0)
        def _():
            dgb_ref[...] += sums

    dproj_shape = _sds((s_len, n_in), BF)
    d_am, d_pm, dproj, dgb8 = _mm(
        "gate_bwd", (d // tn, s_len // tmb),
        [(dx2b, pl.BlockSpec((tmb, d), lambda j, i: (i, 0)))]
        + [(w, pl.BlockSpec((tn, w.shape[1]), lambda j, i: (j, 0))) for w in wo_halves]
        + [(am, tile_ji), (pm, tile_ji),
           (proj, pl.BlockSpec((tmb, tn), lambda j, i: (i, gate0 + 2 * j))),
           (proj, pl.BlockSpec((tmb, tn), lambda j, i: (i, gate0 + 2 * j + 1))),
           (gb, pl.BlockSpec((2, tn), lambda j, i: (0, j)))],
        [(_sds((s_len, d), BF), tile_ji)] * 2
        + [(dproj_shape, pl.BlockSpec((tmb, 2 * tn), lambda j, i: (i, gate0 // 2 + j))),
           (_sds((8, d), F32), pl.BlockSpec((8, tn), lambda j, i: (0, j)))],
        NT, gate_bwd, nb=2)

    za_t = 3 * a // tn

    def attn_gate_bwd(dya, ex, outs):
        silu, dsilu = _silu_and_grad(ex[0][...].astype(F32))
        outs[0][...] = (dya * silu).astype(BF)
        outs[1][...] = (dya * ex[1][...].astype(F32) * dsilu).astype(BF)

    datt, dproj = _mm(
        "attn_gate_bwd", (s_len // tm, a // tn),
        [(d_am, pl.BlockSpec((tm, d), lambda i, j: (i, 0))), (woa, pl.BlockSpec((tn, d), lambda i, j: (j, 0))),
         (proj, pl.BlockSpec((tm, tn), lambda i, j: (i, za_t + j))), (att, tile_ij), (dproj, hbm)],
        [(_sds((s_len, a), BF), tile_ij), (dproj_shape, pl.BlockSpec((tm, tn), lambda i, j: (i, za_t + j)))],
        NT, attn_gate_bwd, aliases={4: 1})

    zp_t = (4 * a + p) // tn

    def pool_gate_bwd(dyp, ex, outs):
        z_ref, y_ref, sc_ref, _ = ex
        dzp_ref, dy_ref, dps_ref = outs
        i = pl.program_id(1)
        silu, dsilu = _silu_and_grad(z_ref[...].astype(F32))
        y = y_ref[...].astype(F32)
        sc = sc_ref[...]
        dyp0 = dyp * silu
        dzp_ref[...] = (dyp * (y * sc) * dsilu).astype(BF)
        dy_ref[...] = (dyp0 * sc).astype(BF)
        dps = jnp.sum(dyp0 * y, axis=0, keepdims=True)

        @pl.when(i == 0)
        def _():
            dps_ref[...] = dps

        @pl.when(i > 0)
        def _():
            dps_ref[...] += dps

    dproj, dy_pool, dps = _mm(
        "pool_gate_bwd", (p // tn, s_len // tmb),
        [(d_pm, pl.BlockSpec((tmb, d), lambda j, i: (i, 0))), (wop, pl.BlockSpec((tn, d), lambda j, i: (j, 0))),
         (proj, pl.BlockSpec((tmb, tn), lambda j, i: (i, zp_t + j))), (ypre, tile_ji),
         (scale_row, pl.BlockSpec((1, tn), lambda j, i: (0, j))), (dproj, hbm)],
        [(dproj_shape, pl.BlockSpec((tmb, tn), lambda j, i: (i, zp_t + j))), (_sds((s_len, p), BF), tile_ji),
         (_sds((1, p), F32), pl.BlockSpec((1, tn), lambda j, i: (0, j)))],
        NT, pool_gate_bwd, aliases={5: 0})

    dproj, dpw = _pool_bwd(dy_pool, dmean, pw, dproj, u_blk)

    o_wop, o_wo, o_pool = a, d, 2 * d
    slab_a = _sds((2, 4, 2 * d + cg, sw), BF)
    slab_b = _sds((2, 4, 5 * d // 2, sw), BF)
    hbm = pl.BlockSpec(memory_space=pl.ANY)
    tmw = _tile(a, 1024)
    core = lax.axis_index("c").astype(jnp.int32).reshape(1)

    def pack_small(dpw_ref, dgb_ref, o_ref):
        rows = cg // N_DEV
        for j in range(N_DEV):
            for g in range(ng):
                o_ref[j % 2, j // 2, g * rows:(g + 1) * rows, :] = dpw_ref[g, j * rows:(j + 1) * rows, :].astype(BF)
            o_ref[j % 2, j // 2, ng * rows:, :] = jnp.concatenate(
                [dgb_ref[:, j * sw:(j + 1) * sw], jnp.zeros((cg - ng * rows - 8, sw), F32)], axis=0).astype(BF)

    slab = pl.pallas_call(
        pack_small, grid=(1,),
        in_specs=[pl.BlockSpec((ng, cg, cg), lambda i: (0, 0, 0)), pl.BlockSpec((8, d), lambda i: (0, 0))],
        out_specs=pl.BlockSpec((2, 4, cg, sw), lambda i: (0, 0, o_pool // cg, 0)), out_shape=slab_a,
        name="dw_small", compiler_params=_params(1))(dpw, dgb8)

    def into_slab(acc, _, outs):
        outs[0][...] = acc.astype(BF)

    def weight_grad(name, slab, lhs, rhs, grid, lhs_spec, rhs_spec, out_spec, epi=into_slab):
        return _mm(name, grid, [(lhs, lhs_spec), (rhs, rhs_spec), (slab, hbm)], [(slab_a, out_spec)],
                   TN, epi, aliases={2: 0})

    def into_both_cores(acc, _, outs):
        outs[0][0] = acc[:sw].astype(BF)
        outs[0][1] = acc[sw:].astype(BF)

    slab = weight_grad("dw_out", slab, merged, dx2b, (N_DEV // 2, d // sw),
                       pl.BlockSpec((s_len, 2 * sw), lambda k, t: (0, k)),
                       pl.BlockSpec((s_len, sw), lambda k, t: (0, t)),
                       pl.BlockSpec((2, None, sw, sw), lambda k, t: (0, k, o_wo // sw + t, 0)), into_both_cores)[0]
    slab = weight_grad("dw_out_attn", slab, ya, d_am, (a // tmw, N_DEV),
                       pl.BlockSpec((s_len, tmw), lambda i, j: (0, i)), pl.BlockSpec((s_len, sw), lambda i, j: (0, j)),
                       pl.BlockSpec((None, None, tmw, sw), lambda i, j: (j % 2, j // 2, i, 0)))[0]
    slab = weight_grad("dw_out_pool", slab, yp, d_pm, (p // tmw, N_DEV),
                       pl.BlockSpec((s_len, tmw), lambda i, j: (0, i)), pl.BlockSpec((s_len, sw), lambda i, j: (0, j)),
                       pl.BlockSpec((None, None, tmw, sw), lambda i, j: (j % 2, j // 2, o_wop // tmw + i, 0)))[0]

    (dproj, dk, dv, ddiag), (from_sibling_a,) = _attn_bwd(proj, datt, base, a, dproj, comm=_cores_comm(slab))
    chip_part_a = _add_core_partials("add_core_partials_a", slab, from_sibling_a, core, (2 * d + cg) // 2)
    drb = _bias_grad(ddiag.reshape(heads, SKEW))
    dproj = lax.dynamic_update_slice(dproj, dk, (0, a))
    dproj = lax.dynamic_update_slice(dproj, dv, (0, 2 * a))

    tmd = _tile(d // 2, 1024)
    nrb = d // 2 // tmd

    def dw_in_rows(name, half, comm):
        return _mm(
            name, (nrb, n_in // sw),
            [(hb, pl.BlockSpec((s_len, tmd), lambda i, t: (0, half * nrb + i))),
             (dproj, pl.BlockSpec((s_len, sw), lambda i, t: (0, pos(t))))],
            [(slab_b, pl.BlockSpec((None, None, tmd, sw),
                                   lambda i, t: ((t // per) % 2, (t // per) // 2, (t % per) * nrb + i, 0)))],
            TN, into_slab, comm=comm)

    slab_lo, parts_a = dw_in_rows("dw_in_lo", 0, _chips_comm(chip_part_a))
    slab_hi, from_sibling_lo = dw_in_rows("dw_in_hi", 1, _cores_comm(slab_lo))
    chip_part_lo = _add_core_partials("add_core_partials_lo", slab_lo, from_sibling_lo, core, 4 * sw)

    tnh = _tile(d, 1024)
    tmh = tm if s_len > tm else s_len // 2
    n_row = s_len // tmh

    def dh_rows(name, lo, hi, prev, comm):
        def body(in_refs, out_refs, scratch_refs):
            acc_ref = scratch_refs[0]
            k = pl.program_id(2)
            part = lax.dot_general(in_refs[0][...], in_refs[per][...], NT, preferred_element_type=F32)
            for r in range(1, per):
                part += lax.dot_general(in_refs[r][...], in_refs[per + r][...], NT, preferred_element_type=F32)

            @pl.when(k == 0)
            def _():
                acc_ref[...] = part

            @pl.when(k > 0)
            def _():
                acc_ref[...] += part

            @pl.when(k == N_DEV - 1)
            def _():
                out_refs[0][...] = acc_ref[...]

        ins = [(dproj, pl.BlockSpec((tmh, sw), lambda i, j, k, r=r: (lo + i, pos(per * k + r)))) for r in range(per)]
        ins += [(w, pl.BlockSpec((None, tnh, sw), lambda i, j, k: (k, j, 0))) for w in win_rounds]
        if prev is not None:
            ins.append((prev, hbm))
        return _call(name, body, (hi - lo, d // tnh, N_DEV), ins,
                     [(_sds((s_len, d), F32), pl.BlockSpec((tmh, tnh), lambda i, j, k: (lo + i, j)))],
                     scratch=[pltpu.VMEM((tmh, tnh), F32)], aliases={2 * per: 0} if prev is not None else None,
                     comm=comm)

    dh, parts_lo, from_sibling_hi = dh_rows("dh_head", 0, n_row // 2, None,
                                            _both(_chips_comm(chip_part_lo), _cores_comm(slab_hi)))
    chip_part_hi = _add_core_partials("add_core_partials_hi", slab_hi, from_sibling_hi, core, 4 * sw)
    dh, parts_hi = dh_rows("dh_rest", n_row // 2, n_row, dh, _chips_comm(chip_part_hi))

    dx, dg1 = _rms_bwd(x2d, dh, dx2, g1)

    g_win, d_win, m_win, v_win = _adamw_row_halves("adamw_w_in", parts_lo, parts_hi, w_in, m_w_in, v_w_in)
    g_woa, d_woa, m_woa, v_woa = _adamw_shard("adamw_w_out_attn", parts_a, w_out_attn, m_w_out_attn, v_w_out_attn, 0)
    g_wop, d_wop, m_wop, v_wop = _adamw_shard("adamw_w_out_pool", parts_a, w_out_pool, m_w_out_pool, v_w_out_pool, o_wop)
    g_wo, d_wo, m_wo, v_wo = _adamw_shard("adamw_w_out", parts_a, w_out, m_w_out, v_w_out, o_wo)
    flat = lambda t: t.reshape(cg // 2, sw)
    pool_out = _adamw_shard("adamw_pool_w", parts_a, flat(pool_w), flat(m_pool_w), flat(v_pool_w), o_pool)
    g_pw, d_pw, m_pw, v_pw = [t.reshape(pool_w.shape) for t in pool_out]
    pad16 = lambda t: jnp.pad(t, ((0, 14), (0, 0)))
    gb_out = _adamw_shard("adamw_gate_bias", parts_a, pad16(gate_bias), pad16(m_gate_bias), pad16(v_gate_bias),
                          o_pool + cg // 2)
    g_gb, d_gb, m_gb, v_gb = [t[:2] for t in gb_out]

    def pack(n_gain, f_gain, scale, rb, last):
        rows = [n_gain.reshape(-1, LANES), f_gain.reshape(-1, LANES), scale.reshape(-1, LANES),
                rb.reshape(-1, LANES), last]
        return jnp.concatenate(rows, axis=0)

    pad_rb = lambda t: jnp.pad(t, ((0, 0), (0, N_REL_PAD - N_REL)))
    zeros8 = jnp.zeros((8, LANES), F32)
    loss_rows = jnp.pad(loss_part, ((0, 7), (0, 0)))
    small = _small_allreduce_adamw(
        pack(dg1, dg2, dps, drb, loss_rows),
        pack(norm_gain, final_gain, pool_scale, pad_rb(rel_bias), zeros8),
        pack(m_norm_gain, m_final_gain, m_pool_scale, pad_rb(m_rel_bias), zeros8),
        pack(v_norm_gain, v_final_gain, v_pool_scale, pad_rb(v_rel_bias), zeros8))

    n1, n2, n3 = d // LANES, 2 * d // LANES, (2 * d + p) // LANES
    n4 = n3 + heads * N_REL_PAD // LANES

    def unpack(t):
        return (t[:n1].reshape(d), t[n1:n2].reshape(d), t[n2:n3].reshape(p),
                t[n3:n4].reshape(heads, N_REL_PAD)[:, :N_REL])

    (g_ng, g_fg, g_ps, g_rb), (d_ng, d_fg, d_ps, d_rb), (m_ng, m_fg, m_ps, m_rb), (v_ng, v_fg, v_ps, v_rb) = [
        unpack(t) for t in small]
    loss = small[0][n4, 0]

    return (loss, dx.reshape(x.shape),
            g_ng, g_win, g_rb, g_pw, g_ps, g_woa, g_wop, g_gb, g_wo, g_fg,
            d_ng, d_win, d_rb, d_pw, d_ps, d_woa, d_wop, d_gb, d_wo, d_fg,
            m_ng, m_win, m_rb, m_pw, m_ps, m_woa, m_wop, m_gb, m_wo, m_fg,
            v_ng, v_win, v_rb, v_pw, v_ps, v_woa, v_wop, v_gb, v_wo, v_fg)
```

```python
import jax
import jax.numpy as jnp
from jax import lax
from jax.experimental import pallas as pl
from jax.experimental.pallas import tpu as pltpu

F32 = jnp.float32
BF = jnp.bfloat16
MESH = pl.DeviceIdType.MESH

N_DEV = 8
CHUNK = 64
N_LEFT_CHUNKS = 8
HEAD_DIM = 128
MAX_REL = 128
N_REL = 2 * MAX_REL + 1
N_REL_PAD = 384
POOL_WINDOWS = (2, 4, 8, 16)
HALO = 16
EPS = 1e-6
ADAM_LR = 0.001
ADAM_B1 = 0.9
ADAM_B2 = 0.999
ADAM_EPS = 1e-08
ADAM_WD = 0.01
ADAM_STEP = 10
NEG = -1e30
LANES = 128
TQ = N_LEFT_CHUNKS * CHUNK
TK = 2 * TQ
SKEW = 2 * TK
VMEM_LIMIT = 52 * 1024 * 1024

NN = (((1,), (0,)), ((), ()))
NT = (((1,), (1,)), ((), ()))
TN = (((0,), (0,)), ((), ()))


def _params(n_grid):
    return pltpu.CompilerParams(dimension_semantics=("arbitrary",) * n_grid, vmem_limit_bytes=VMEM_LIMIT)


def _sig(z):
    return 1.0 / (1.0 + jnp.exp(-z))


def _silu_and_grad(z):
    s = _sig(z)
    return z * s, s * (1.0 + z * (1.0 - s))


def _tile(n, pref):
    t = min(n, pref)
    assert n % t == 0, (n, pref)
    return t


def _sds(shape, dtype):
    return jax.ShapeDtypeStruct(shape, dtype)


class _Comm:
    def __init__(self, ins, outs, scratch, start, wait, hooks=()):
        self.ins, self.outs, self.scratch = list(ins), list(outs), list(scratch)
        self.start, self.wait, self.hooks = start, wait, tuple(hooks)


def _call(name, body, grid, ins, outs, scratch=(), aliases=None, comm=None):
    n_in, n_out, n_scr = len(ins), len(outs), len(scratch)
    c_in = len(comm.ins) if comm else 0
    c_out = len(comm.outs) if comm else 0
    n_steps = 1
    for g in grid:
        n_steps *= g

    def kern(*refs):
        o0 = n_in + c_in
        s0 = o0 + n_out + c_out
        if comm:
            c_refs = (refs[n_in:o0], refs[o0 + n_out:s0], refs[s0 + n_scr:])
            step = pl.program_id(0)
            for ax in range(1, len(grid)):
                step = step * grid[ax] + pl.program_id(ax)

            @pl.when(step == 0)
            def _():
                comm.start(*c_refs)

            for frac, hook in comm.hooks:
                @pl.when(step == int(frac * n_steps))
                def _(hook=hook):
                    hook(*c_refs)

        body(refs[:n_in], refs[o0:o0 + n_out], refs[s0:s0 + n_scr])

        if comm:
            @pl.when(step == n_steps - 1)
            def _():
                comm.wait(*c_refs)

    hbm = pl.BlockSpec(memory_space=pl.ANY)
    return pl.pallas_call(
        kern, grid=grid,
        in_specs=[s for _, s in ins] + [hbm] * c_in, out_specs=[s for _, s in outs] + [hbm] * c_out,
        out_shape=[o for o, _ in outs] + (comm.outs if comm else []),
        scratch_shapes=list(scratch) + (comm.scratch if comm else []),
        name=name, compiler_params=_params(len(grid)), input_output_aliases=aliases or {},
    )(*([a for a, _ in ins] + (comm.ins if comm else [])))


def _mm(name, grid, ins, outs, dims, epi, aliases=None, comm=None, nb=1):
    def body(in_refs, out_refs, _):
        if nb == 1:
            acc = lax.dot_general(in_refs[0][...], in_refs[1][...], dims, preferred_element_type=F32)
        else:
            acc, k0 = 0.0, 0
            for b_ref in in_refs[1:1 + nb]:
                k1 = k0 + b_ref.shape[1]
                acc = acc + lax.dot_general(in_refs[0][:, k0:k1], b_ref[...], dims, preferred_element_type=F32)
                k0 = k1
        epi(acc, in_refs[1 + nb:], out_refs)

    return _call(name, body, grid, ins, outs, aliases=aliases, comm=comm)


def _rms_fwd(x, g, comm=None):
    s, d = x.shape
    tr = _tile(s, 256)

    def body(in_refs, out_refs, _):
        xv = in_refs[0][...]
        r = lax.rsqrt(jnp.mean(xv * xv, axis=-1, keepdims=True) + EPS)
        out_refs[0][...] = (xv * r * in_refs[1][...]).astype(BF)

    row = pl.BlockSpec((tr, d), lambda i: (i, 0))
    return _call("rms_fwd", body, (s // tr,), [(x, row), (g, pl.BlockSpec((1, d), lambda i: (0, 0)))],
                 [(_sds((s, d), BF), row)], comm=comm)


def _final_norm(x2, target, g):
    s, d = x2.shape
    tr = _tile(s, 128)

    def kern(x_ref, t_ref, g_ref, dx_ref, dxb_ref, dg_ref, loss_ref):
        i = pl.program_id(0)
        xv = x_ref[...]
        gv = g_ref[...]
        r = lax.rsqrt(jnp.mean(xv * xv, axis=-1, keepdims=True) + EPS)
        xhat = xv * r
        err = xhat * gv - t_ref[...]
        dy = err * (1.0 / d)
        gy = dy * gv
        dx = r * (gy - xhat * jnp.mean(gy * xhat, axis=-1, keepdims=True))
        dx_ref[...] = dx
        dxb_ref[...] = dx.astype(BF)
        dg = jnp.sum(dy * xhat, axis=0, keepdims=True)
        ls = jnp.broadcast_to(0.5 * jnp.sum(jnp.mean(err * err, axis=-1, keepdims=True)), (1, LANES))

        @pl.when(i == 0)
        def _():
            dg_ref[...] = dg
            loss_ref[...] = ls

        @pl.when(i > 0)
        def _():
            dg_ref[...] += dg
            loss_ref[...] += ls

    row = pl.BlockSpec((tr, d), lambda i: (i, 0))
    vec = pl.BlockSpec((1, d), lambda i: (0, 0))
    return pl.pallas_call(
        kern, grid=(s // tr,), in_specs=[row, row, vec],
        out_specs=[row, row, vec, pl.BlockSpec((1, LANES), lambda i: (0, 0))],
        out_shape=[_sds((s, d), F32), _sds((s, d), BF), _sds((1, d), F32), _sds((1, LANES), F32)],
        name="final_norm", compiler_params=_params(1))(x2, target, g)


def _rms_bwd(x, dh, dx2, g):
    s, d = x.shape
    tr = _tile(s, 128)

    def kern(x_ref, dh_ref, dx2_ref, g_ref, dx_ref, dg_ref):
        i = pl.program_id(0)
        xv = x_ref[...]
        r = lax.rsqrt(jnp.mean(xv * xv, axis=-1, keepdims=True) + EPS)
        xhat = xv * r
        dhv = dh_ref[...]
        gh = dhv * g_ref[...]
        dx_ref[...] = dx2_ref[...] + r * (gh - xhat * jnp.mean(gh * xhat, axis=-1, keepdims=True))
        dg = jnp.sum(dhv * xhat, axis=0, keepdims=True)

        @pl.when(i == 0)
        def _():
            dg_ref[...] = dg

        @pl.when(i > 0)
        def _():
            dg_ref[...] += dg

    row = pl.BlockSpec((tr, d), lambda i: (i, 0))
    vec = pl.BlockSpec((1, d), lambda i: (0, 0))
    return pl.pallas_call(
        kern, grid=(s // tr,), in_specs=[row, row, row, vec], out_specs=[row, vec],
        out_shape=[_sds((s, d), F32), _sds((1, d), F32)],
        name="rms_bwd", compiler_params=_params(1))(x, dh, dx2, g)


def _rel_index(j, backward):
    if backward:
        rel = 2 * TQ - 1 - j
    else:
        rel = TQ - jnp.where(j < TK, j, j - SKEW)
    return jnp.clip(rel, -MAX_REL, MAX_REL) + MAX_REL


def _bias_rows(rel_bias_pad):
    h = rel_bias_pad.shape[0]

    def kern(rb_ref, o_ref):
        j = lax.broadcasted_iota(jnp.int32, (N_REL_PAD, SKEW), 1)
        k = lax.broadcasted_iota(jnp.int32, (N_REL_PAD, SKEW), 0)
        onehot = (_rel_index(j, False) == k).astype(F32)
        o_ref[...] = jnp.dot(rb_ref[...], onehot, preferred_element_type=F32, precision=lax.Precision.HIGHEST)

    return pl.pallas_call(kern, out_shape=_sds((h, SKEW), F32), name="bias_rows")(rel_bias_pad)


def _bias_grad(ddiag):
    h = ddiag.shape[0]

    def kern(d_ref, o_ref):
        j = lax.broadcasted_iota(jnp.int32, (N_REL_PAD, SKEW), 1)
        k = lax.broadcasted_iota(jnp.int32, (N_REL_PAD, SKEW), 0)
        onehot = ((_rel_index(j, True) == k) & (j < TQ + TK - 1)).astype(F32)
        o_ref[...] = lax.dot_general(d_ref[...], onehot, NT, preferred_element_type=F32,
                                     precision=lax.Precision.HIGHEST)

    return pl.pallas_call(kern, out_shape=_sds((h, N_REL_PAD), F32), name="bias_grad")(ddiag)


def _bias_tile(row):
    t = pltpu.roll(jnp.broadcast_to(row, (TQ, SKEW)), 0, 1, stride=1, stride_axis=0)[:, :TK]
    r = lax.broadcasted_iota(jnp.int32, (TQ, TK), 0) // CHUNK
    c = lax.broadcasted_iota(jnp.int32, (TQ, TK), 1) // CHUNK
    dist = N_LEFT_CHUNKS + r - c
    return jnp.where((dist >= 0) & (dist <= N_LEFT_CHUNKS), t, NEG)


def _scores(q, keys, tile):
    s = lax.dot_general(q, keys, NT, preferred_element_type=F32) * (HEAD_DIM ** -0.5) + tile
    m = jnp.max(s, axis=1, keepdims=True)
    p = jnp.exp(s - m)
    return p, jnp.sum(p, axis=1, keepdims=True)


HQ = TQ // 2
BANDS = ((0, 0, TQ + HQ), (1, HQ, TK))
BANDS_FIRST = ((0, TQ, TQ + HQ), (1, TQ, TK))


def _key_rows(prev_ref, cur_ref, first, end):
    if first >= TQ:
        return cur_ref[first - TQ:end - TQ, :]
    return jnp.concatenate([prev_ref[first:, :], cur_ref[:end - TQ, :]], axis=0)


def _attn_fwd(proj, base, a_width, comm=None):
    s_len = proj.shape[0]
    heads = a_width // HEAD_DIM
    nq = s_len // TQ
    kb, vb, zb = heads, 2 * heads, 3 * heads

    def kern(q_ref, kp_ref, kc_ref, vp_ref, vc_ref, z_ref, base_ref, att_ref, ya_ref, tile_ref):
        i = pl.program_id(1)

        def attend(bands):
            for half, first, end in bands:
                rows = slice(half * HQ, (half + 1) * HQ)
                p, l = _scores(q_ref[rows, :], _key_rows(kp_ref, kc_ref, first, end), tile_ref[rows, first:end])
                o = jnp.dot(p.astype(BF), _key_rows(vp_ref, vc_ref, first, end), preferred_element_type=F32) / l
                att_ref[rows, :] = o.astype(BF)
                z = z_ref[rows, :].astype(F32)
                ya_ref[rows, :] = (o * (z * _sig(z))).astype(BF)

        @pl.when(i == 0)
        def _():
            tile_ref[...] = _bias_tile(base_ref[...])
            attend(BANDS_FIRST)

        @pl.when(i > 0)
        def _():
            attend(BANDS)

    blk = lambda off: pl.BlockSpec((TQ, HEAD_DIM), lambda h, i: (i, off + h))
    prev = lambda off: pl.BlockSpec((TQ, HEAD_DIM), lambda h, i: (jnp.maximum(i - 1, 0), off + h))
    out = pl.BlockSpec((TQ, HEAD_DIM), lambda h, i: (i, h))
    def body(in_refs, out_refs, scratch_refs):
        kern(*in_refs, *out_refs, *scratch_refs)

    act = _sds((s_len, a_width), BF)
    res = _call(
        "attn_fwd", body, (heads, nq),
        [(proj, blk(0)), (proj, prev(kb)), (proj, blk(kb)), (proj, prev(vb)), (proj, blk(vb)), (proj, blk(zb)),
         (base, pl.BlockSpec((None, 1, SKEW), lambda h, i: (h, 0, 0)))],
        [(act, out), (act, out)], scratch=[pltpu.VMEM((TQ, TK), F32)], comm=comm)
    return res[:2], res[2:]


def _attn_bwd(proj, datt, base, a_width, dproj, comm=None):
    s_len = proj.shape[0]
    heads = a_width // HEAD_DIM
    nq = s_len // TQ
    kb, vb = heads, 2 * heads
    scale = HEAD_DIM ** -0.5

    def body(in_refs, out_refs, scratch_refs):
        q_ref, kp_ref, kc_ref, vp_ref, vc_ref, do_ref, base_ref, _ = in_refs
        dq_ref, dk_ref, dv_ref, dd_ref = out_refs
        tile_ref, dsacc_ref, ak_ref, av_ref, dsb_ref, pb_ref = scratch_refs
        i = pl.program_id(1)

        def backward(bands, first, end):
            for half, c0, c1 in bands:
                rows = slice(half * HQ, (half + 1) * HQ)
                keys = _key_rows(kp_ref, kc_ref, c0, c1)
                p, l = _scores(q_ref[rows, :], keys, tile_ref[rows, c0:c1])
                p = p / l
                dp = lax.dot_general(do_ref[rows, :], _key_rows(vp_ref, vc_ref, c0, c1), NT,
                                     preferred_element_type=F32)
                ds = p * (dp - jnp.sum(p * dp, axis=1, keepdims=True))
                dsacc_ref[rows, c0:c1] += ds
                dsb = ds.astype(BF)
                dq_ref[rows, :] = (jnp.dot(dsb, keys, preferred_element_type=F32) * scale).astype(BF)
                dsb_ref[rows, c0:c1] = dsb
                pb_ref[rows, c0:c1] = p.astype(BF)
            return (lax.dot_general(dsb_ref[:, first:end], q_ref[...], TN, preferred_element_type=F32) * scale,
                    lax.dot_general(pb_ref[:, first:end], do_ref[...], TN, preferred_element_type=F32))

        @pl.when(i == 0)
        def _():
            tile_ref[...] = _bias_tile(base_ref[...])
            dsacc_ref[...] = jnp.zeros_like(dsacc_ref)
            dsb_ref[...] = jnp.zeros_like(dsb_ref)
            pb_ref[...] = jnp.zeros_like(pb_ref)
            ak_ref[...], av_ref[...] = backward(BANDS_FIRST, TQ, TK)
            dk_ref[...] = jnp.zeros_like(dk_ref)
            dv_ref[...] = jnp.zeros_like(dv_ref)

        @pl.when((i > 0) & (i < nq))
        def _():
            dkc, dvc = backward(BANDS, 0, TK)
            dk_ref[...] = (ak_ref[...] + dkc[:TQ]).astype(BF)
            dv_ref[...] = (av_ref[...] + dvc[:TQ]).astype(BF)
            ak_ref[...] = dkc[TQ:]
            av_ref[...] = dvc[TQ:]

        @pl.when(i == nq)
        def _():
            dk_ref[...] = ak_ref[...].astype(BF)
            dv_ref[...] = av_ref[...].astype(BF)
            acc = dsacc_ref[...]
            rr = lax.broadcasted_iota(jnp.int32, (TQ, TQ), 0)
            cc = lax.broadcasted_iota(jnp.int32, (TQ, TQ), 1)
            flip = (rr + cc == TQ - 1).astype(BF)
            hi = acc.astype(BF)
            lo = (acc - hi.astype(F32)).astype(BF)
            rev = jnp.dot(flip, hi, preferred_element_type=F32) + jnp.dot(flip, lo, preferred_element_type=F32)
            wide = jnp.concatenate([rev, jnp.zeros((TQ, SKEW - TK), F32)], axis=1)
            dd_ref[...] = jnp.sum(pltpu.roll(wide, 0, 1, stride=1, stride_axis=0), axis=0, keepdims=True)

    last = nq - 1
    cur = lambda off: pl.BlockSpec((TQ, HEAD_DIM), lambda h, i: (jnp.minimum(i, last), off + h))
    prev = lambda off: pl.BlockSpec((TQ, HEAD_DIM), lambda h, i: (jnp.maximum(jnp.minimum(i, last) - 1, 0), off + h))
    done = pl.BlockSpec((TQ, HEAD_DIM), lambda h, i: (jnp.maximum(i - 1, 0), h))
    row = pl.BlockSpec((None, 1, SKEW), lambda h, i: (h, 0, 0))
    act = _sds((s_len, a_width), BF)
    res = _call(
        "attn_bwd", body, (heads, nq + 1),
        [(proj, cur(0)), (proj, prev(kb)), (proj, cur(kb)), (proj, prev(vb)), (proj, cur(vb)), (datt, cur(0)),
         (base, row), (dproj, pl.BlockSpec(memory_space=pl.ANY))],
        [(_sds(dproj.shape, dproj.dtype), cur(0)), (act, done), (act, done), (_sds((heads, 1, SKEW), F32), row)],
        aliases={7: 0},
        scratch=[pltpu.VMEM((TQ, TK), F32), pltpu.VMEM((TQ, TK), F32),
                 pltpu.VMEM((TQ, HEAD_DIM), F32), pltpu.VMEM((TQ, HEAD_DIM), F32),
                 pltpu.VMEM((TQ, TK), BF), pltpu.VMEM((TQ, TK), BF)],
        comm=comm)
    return res[:4], res[4:]


def _pool_fwd(proj, pool_w, pool_scale, p_width, u_blk, z_blk):
    s_len = proj.shape[0]
    cg = p_width // len(POOL_WINDOWS)
    tt = _tile(s_len, 512)

    def kern(up_ref, uc_ref, z_ref, pw_ref, sc_ref, d_ref, y_ref, yp_ref):
        t = pl.program_id(0)
        row = lax.broadcasted_iota(jnp.int32, (tt, 1), 0) + t * tt
        for g, w in enumerate(POOL_WINDOWS):
            cs = slice(g * cg, (g + 1) * cg)
            prev = jnp.where(t == 0, 0.0, up_ref[:, cs].astype(F32))
            cur = uc_ref[:, cs].astype(F32)
            ws = jnp.concatenate([prev, cur], axis=0)
            sh = 1
            while sh < w:
                ws = ws + pltpu.roll(ws, sh, 0)
                sh *= 2
            cnt = jnp.minimum(row + 1, w).astype(F32)
            db = (ws[HALO:, :] / cnt - cur).astype(BF)
            y = jnp.dot(db, pw_ref[g], preferred_element_type=F32)
            d_ref[:, cs] = db
            y_ref[:, cs] = y.astype(BF)
            z = z_ref[:, cs].astype(F32)
            yp_ref[:, cs] = (y * sc_ref[:, cs] * (z * _sig(z))).astype(BF)

    full = pl.BlockSpec((tt, p_width), lambda t: (t, 0))
    return pl.pallas_call(
        kern, grid=(s_len // tt,),
        in_specs=[pl.BlockSpec((HALO, p_width), lambda t: (jnp.maximum(t * (tt // HALO) - 1, 0), u_blk)),
                  pl.BlockSpec((tt, p_width), lambda t: (t, u_blk)),
                  pl.BlockSpec((tt, p_width), lambda t: (t, z_blk)),
                  pl.BlockSpec((len(POOL_WINDOWS), cg, cg), lambda t: (0, 0, 0)),
                  pl.BlockSpec((1, p_width), lambda t: (0, 0))],
        out_specs=[full, full, full], out_shape=[_sds((s_len, p_width), BF)] * 3,
        name="pool_fwd", compiler_params=_params(1))(proj, proj, proj, pool_w, pool_scale)


def _pool_bwd(dy, dmean, pool_w, dproj, u_blk):
    s_len, p_width = dy.shape
    ng = len(POOL_WINDOWS)
    cg = p_width // ng
    tt = _tile(s_len, 512)
    nt = s_len // tt

    def kern(dyc_ref, dyn_ref, d_ref, pw_ref, _, du_ref, dpw_ref):
        t = pl.program_id(0)

        @pl.when(t == 0)
        def _():
            dpw_ref[...] = jnp.zeros_like(dpw_ref)

        row = lax.broadcasted_iota(jnp.int32, (tt + HALO, 1), 0) + t * tt
        for g, w in enumerate(POOL_WINDOWS):
            cs = slice(g * cg, (g + 1) * cg)
            dyc = dyc_ref[:, cs]
            ddc = lax.dot_general(dyc, pw_ref[g], NT, preferred_element_type=F32)
            ddn = lax.dot_general(dyn_ref[:, cs], pw_ref[g], NT, preferred_element_type=F32)
            ddn = jnp.where(t == nt - 1, 0.0, ddn)
            cnt = jnp.minimum(row + 1, w).astype(F32)
            ws = jnp.concatenate([ddc, ddn], axis=0) / cnt
            sh = 1
            while sh < w:
                ws = ws + pltpu.roll(ws, tt + HALO - sh, 0)
                sh *= 2
            du_ref[:, cs] = (ws[:tt, :] - ddc).astype(BF)
            dpw_ref[g] += lax.dot_general(d_ref[:, cs], dyc, TN, preferred_element_type=F32)

    full = pl.BlockSpec((tt, p_width), lambda t: (t, 0))
    pw_spec = pl.BlockSpec((ng, cg, cg), lambda t: (0, 0, 0))
    return pl.pallas_call(
        kern, grid=(nt,),
        in_specs=[full,
                  pl.BlockSpec((HALO, p_width), lambda t: (jnp.minimum((t + 1) * (tt // HALO), s_len // HALO - 1), 0)),
                  full, pw_spec, pl.BlockSpec(memory_space=pl.ANY)],
        out_specs=[pl.BlockSpec((tt, p_width), lambda t: (t, u_blk)), pw_spec],
        out_shape=[_sds(dproj.shape, dproj.dtype), _sds((ng, cg, cg), F32)],
        input_output_aliases={4: 0},
        name="pool_bwd", compiler_params=_params(1))(dy, dy, dmean, pool_w, dproj)


def _adam(g, w_ref, m_ref, v_ref, g_out, d_out, m_out, v_out):
    m = ADAM_B1 * m_ref[...] + (1.0 - ADAM_B1) * g
    v = ADAM_B2 * v_ref[...] + (1.0 - ADAM_B2) * (g * g)
    m_hat = m / (1.0 - ADAM_B1 ** ADAM_STEP)
    v_hat = v / (1.0 - ADAM_B2 ** ADAM_STEP)
    g_out[...] = g
    d_out[...] = -ADAM_LR * (m_hat / (jnp.sqrt(v_hat) + ADAM_EPS) + ADAM_WD * w_ref[...])
    m_out[...] = m
    v_out[...] = v


def _adamw_shard(name, parts, w, m, v, row_off):
    rw, cw = w.shape
    sw = parts.shape[2]
    tr = _tile(rw, 512)
    assert row_off % tr == 0 and cw % sw == 0

    def kern(b_ref, w_ref, m_ref, v_ref, g_out, d_out, m_out, v_out):
        b = b_ref[...].astype(F32)
        _adam(((b[0] + b[1]) + b[2]) + b[3], w_ref, m_ref, v_ref, g_out, d_out, m_out, v_out)

    blk = pl.BlockSpec((tr, sw), lambda ct, i: (i, ct))
    return pl.pallas_call(
        kern, grid=(cw // sw, rw // tr),
        in_specs=[pl.BlockSpec((4, tr, sw), lambda ct, i: (0, (row_off + ct * rw) // tr + i, 0)), blk, blk, blk],
        out_specs=[blk] * 4, out_shape=[_sds((rw, cw), F32)] * 4,
        name=name, compiler_params=_params(2))(parts, w, m, v)


def _adamw_row_halves(name, parts_lo, parts_hi, w, m, v):
    rw, cw = w.shape
    sw = parts_lo.shape[2]
    half = rw // 2
    tr = _tile(half, 512)
    nh = half // tr

    def kern(lo_ref, hi_ref, w_ref, m_ref, v_ref, g_out, d_out, m_out, v_out):
        i = pl.program_id(1)

        def update(b_ref):
            b = b_ref[...].astype(F32)
            _adam(((b[0] + b[1]) + b[2]) + b[3], w_ref, m_ref, v_ref, g_out, d_out, m_out, v_out)

        @pl.when(i < nh)
        def _():
            update(lo_ref)

        @pl.when(i >= nh)
        def _():
            update(hi_ref)

    blk = pl.BlockSpec((tr, sw), lambda ct, i: (i, ct))
    return pl.pallas_call(
        kern, grid=(cw // sw, rw // tr),
        in_specs=[pl.BlockSpec((4, tr, sw), lambda ct, i: (0, ct * nh + jnp.minimum(i, nh - 1), 0)),
                  pl.BlockSpec((4, tr, sw), lambda ct, i: (0, ct * nh + jnp.maximum(i - nh, 0), 0)), blk, blk, blk],
        out_specs=[blk] * 4, out_shape=[_sds((rw, cw), F32)] * 4,
        name=name, compiler_params=_params(2))(parts_lo, parts_hi, w, m, v)


def _both(c1, c2):
    n_in, n_out, n_sem = len(c1.ins), len(c1.outs), len(c1.scratch)

    def split(ins, outs, sems):
        return (ins[:n_in], outs[:n_out], sems[:n_sem]), (ins[n_in:], outs[n_out:], sems[n_sem:])

    def start(*refs):
        r1, r2 = split(*refs)
        c1.start(*r1)
        c2.start(*r2)

    def wait(*refs):
        r1, r2 = split(*refs)
        c1.wait(*r1)
        c2.wait(*r2)

    def of(which, hook):
        return lambda *refs: hook(*split(*refs)[which])

    hooks = [(f, of(0, h)) for f, h in c1.hooks] + [(f, of(1, h)) for f, h in c2.hooks]
    return _Comm(c1.ins + c2.ins, c1.outs + c2.outs, c1.scratch + c2.scratch, start, wait, hooks)


def _position():
    return lax.axis_index("x"), lax.axis_index("y"), lax.axis_index("c")


def _gather_comm(shards, cols=None, pass_at=None):
    if cols is None:
        pieces = [(a, None) for a in range(len(shards))]
        shapes = [_sds(s.shape, s.dtype) for s in shards]
    else:
        half = shards[0].shape[0] // 2
        pieces = [(0, pl.ds(0, half)), (0, pl.ds(half, half))]
        shapes = [_sds((shards[0].shape[0], cols[1]), shards[0].dtype)]
    n = len(pieces)

    def plan(xs, outs, sems, only=None):
        send_sems, recv_sems, local_sems = sems
        x, y, c = _position()
        me, sibling = (x, y, c), (x, y, 1 - c)
        chips = [(1 - x, y), (x, 1 - y), (1 - x, 1 - y)]
        which = range(n) if only is None else only

        def source(v):
            a, rows = pieces[v]
            return xs[a] if rows is None else xs[a].at[rows, pl.ds(cols[0], cols[1])]

        def landing(v, block):
            a, rows = pieces[v]
            dst = outs[a].at[4 * block[0] + 2 * block[1] + block[2]]
            return dst if rows is None else dst.at[rows, :]

        def copy(v, k, block, to, own=False):
            dst = landing(v, block)
            return pltpu.make_async_remote_copy(
                src_ref=source(v) if own else dst, dst_ref=dst,
                send_sem=send_sems.at[7 * v + k], recv_sem=recv_sems.at[7 * v + k],
                device_id=to, device_id_type=MESH)

        by_chip = [(j, chip, v) for v in which for j, chip in enumerate(chips)]
        return dict(
            mine=lambda: [pltpu.make_async_copy(source(v), landing(v, me), local_sems.at[v]) for v in which],
            first=lambda: ([copy(v, 0, me, sibling, own=True) for v in which]
                           + [copy(v, 1 + j, me, (*chip, c), own=True) for j, chip, v in by_chip]),
            landed=lambda: [copy(v, 1 + j, (*chip, c), me) for j, chip, v in by_chip],
            passed=lambda: [copy(v, 4 + j, (*chip, c), sibling) for j, chip, v in by_chip],
            rest=lambda: ([copy(v, 0, sibling, me) for v in which]
                          + [copy(v, 4 + j, (*chip, 1 - c), me) for j, chip, v in by_chip]))

    def start(*refs):
        p = plan(*refs)
        for cp in p["mine"]() + p["first"]():
            cp.start()

    def pass_on(only):
        def hook(*refs):
            p = plan(*refs, only=only)
            for arrived, onward in zip(p["landed"](), p["passed"]()):
                arrived.wait_recv()
                onward.start()
        return hook

    in_halves = cols is not None and pass_at is None

    def wait(*refs):
        if in_halves:
            pass_on([1])(*refs)
        p = plan(*refs)
        for cp in p["rest"]():
            cp.wait_recv()
        for cp in p["first"]() + p["passed"]():
            cp.wait_send()
        for cp in p["mine"]():
            cp.wait()

    hooks = [(0.5, pass_on([0]))] if in_halves else [(0.75 if pass_at is None else pass_at, pass_on(None))]
    return _Comm(shards, [_sds((N_DEV,) + s.shape, s.dtype) for s in shapes],
                 [pltpu.SemaphoreType.DMA((7 * n,)), pltpu.SemaphoreType.DMA((7 * n,)),
                  pltpu.SemaphoreType.DMA((n,))], start, wait, hooks)


def _cores_comm(slab):
    _, _, r, sw = slab.shape

    def copies(ins, outs, sems):
        x, y, c = _position()
        return [pltpu.make_async_remote_copy(
            src_ref=ins[0].at[1 - c], dst_ref=outs[0], send_sem=sems[0], recv_sem=sems[1],
            device_id=(x, y, 1 - c), device_id_type=MESH)]

    def start(*refs):
        for cp in copies(*refs):
            cp.start()

    def wait(*refs):
        for cp in copies(*refs):
            cp.wait()

    return _Comm([slab], [_sds((4, r, sw), slab.dtype)],
                 [pltpu.SemaphoreType.DMA, pltpu.SemaphoreType.DMA], start, wait)


def _add_core_partials(name, slab, recv, core, tr):
    _, _, r, sw = slab.shape

    def kern(c_ref, a_ref, b_ref, o_ref):
        o_ref[...] = (a_ref[...].astype(F32) + b_ref[...].astype(F32)).astype(BF)

    return pl.pallas_call(
        kern,
        grid_spec=pltpu.PrefetchScalarGridSpec(
            num_scalar_prefetch=1, grid=(4, r // tr),
            in_specs=[pl.BlockSpec((None, None, tr, sw), lambda k, i, c_ref: (c_ref[0], k, i, 0)),
                      pl.BlockSpec((None, tr, sw), lambda k, i, c_ref: (k, i, 0))],
            out_specs=pl.BlockSpec((None, tr, sw), lambda k, i, c_ref: (k, i, 0))),
        out_shape=_sds((4, r, sw), BF), name=name, compiler_params=_params(2))(core, slab, recv)


def _chips_comm(part):
    _, r, sw = part.shape

    def copies(ins, outs, sems):
        send_sems, recv_sems, local_sem = sems
        x, y, c = _position()
        mine = 2 * x + y
        local = pltpu.make_async_copy(ins[0].at[mine], outs[0].at[mine], local_sem)
        chips = [(1 - x, y), (x, 1 - y), (1 - x, 1 - y)]
        remote = [pltpu.make_async_remote_copy(
            src_ref=ins[0].at[2 * px + py], dst_ref=outs[0].at[mine],
            send_sem=send_sems.at[j], recv_sem=recv_sems.at[j],
            device_id=(px, py, c), device_id_type=MESH) for j, (px, py) in enumerate(chips)]
        return [local] + remote

    def start(*refs):
        for cp in copies(*refs):
            cp.start()

    def wait(*refs):
        for cp in copies(*refs):
            cp.wait()

    return _Comm([part], [_sds((4, r, sw), part.dtype)],
                 [pltpu.SemaphoreType.DMA((3,)), pltpu.SemaphoreType.DMA((3,)), pltpu.SemaphoreType.DMA],
                 start, wait)


def _small_allreduce_adamw(partial, w, m, v):
    nr = partial.shape[0]

    def kern(p_ref, w_ref, m_ref, v_ref, g_out, d_out, m_out, v_out, gath_ref, send_sems, recv_sems):
        x, y, c = _position()
        me = 4 * x + 2 * y + c
        gath_ref[me] = p_ref[...]
        copies = []
        for mask in range(1, N_DEV):
            peer = (x ^ (mask >> 2), y ^ ((mask >> 1) & 1), c ^ (mask & 1))
            copies.append(pltpu.make_async_remote_copy(
                src_ref=p_ref, dst_ref=gath_ref.at[me],
                send_sem=send_sems.at[mask - 1], recv_sem=recv_sems.at[mask - 1],
                device_id=peer, device_id_type=MESH))
        for cp in copies:
            cp.start()
        for cp in copies:
            cp.wait()
        tot = gath_ref[0]
        for k in range(1, N_DEV):
            tot = tot + gath_ref[k]
        _adam(tot, w_ref, m_ref, v_ref, g_out, d_out, m_out, v_out)

    vmem = pl.BlockSpec(memory_space=pltpu.VMEM)
    return pl.pallas_call(
        kern, in_specs=[vmem] * 4, out_specs=[vmem] * 4, out_shape=[_sds((nr, LANES), F32)] * 4,
        scratch_shapes=[pltpu.VMEM((N_DEV, nr, LANES), F32),
                        pltpu.SemaphoreType.DMA((N_DEV - 1,)), pltpu.SemaphoreType.DMA((N_DEV - 1,))],
        name="small_allreduce_adamw")(partial, w, m, v)


def kernel(x, norm_gain, w_in, rel_bias, pool_w, pool_scale, w_out_attn, w_out_pool, gate_bias, w_out, final_gain, loss_target, m_norm_gain, m_w_in, m_rel_bias, m_pool_w, m_pool_scale, m_w_out_attn, m_w_out_pool, m_gate_bias, m_w_out, m_final_gain, v_norm_gain, v_w_in, v_rel_bias, v_pool_w, v_pool_scale, v_w_out_attn, v_w_out_pool, v_gate_bias, v_w_out, v_final_gain):
    _, s_len, d = x.shape
    a = w_out_attn.shape[0]
    p = w_out_pool.shape[0]
    heads = a // HEAD_DIM
    ng = len(POOL_WINDOWS)
    cg = p // ng
    sw = d // N_DEV
    n_in = w_in.shape[1] * N_DEV
    assert a == p and a + p == d and cg == sw and n_in == 5 * d and w_in.shape[1] == 5 * sw
    assert s_len % TQ == 0 and rel_bias.shape == (heads, N_REL)
    tm = _tile(s_len, 1024)
    x2d = x.reshape(s_len, d)
    tgt = loss_target.reshape(s_len, d)

    g1 = norm_gain.reshape(1, d)
    g2 = final_gain.reshape(1, d)
    scale_row = pool_scale.reshape(1, p)
    tn = sw
    per = w_in.shape[1] // tn
    hbm = pl.BlockSpec(memory_space=pl.ANY)

    w_bf = [w_in[:, r * tn:(r + 1) * tn].astype(BF) for r in range(per)]
    hb, landed = _rms_fwd(x2d, g1, comm=_gather_comm([w_bf[0]], cols=(0, tn)))
    win_rounds = []

    def store_bf16(acc, _, outs):
        outs[0][...] = acc.astype(BF)

    gate0, n_gate = (4 * a + 2 * p) // tn, d // tn

    def pos(t):
        g = t - gate0
        return jnp.where(t < gate0, t, jnp.where(g < n_gate, gate0 + 2 * g, gate0 + 2 * (g - n_gate) + 1))

    proj = None
    for r in range(per):
        win_rounds.append(landed)
        if r + 1 < per:
            comm = _gather_comm([w_bf[r + 1]], cols=(0, tn))
        else:
            comm = _gather_comm([w_out_attn.astype(BF), pool_w.astype(BF), gate_bias])
        ins = [(hb, pl.BlockSpec((tm, d), lambda i, j: (i, 0))),
               (landed, pl.BlockSpec((None, d, tn), lambda i, j: (j, 0, 0)))]
        if proj is not None:
            ins.append((proj, hbm))
        proj, landed, *rest = _mm(
            f"proj_{r}", (s_len // tm, N_DEV), ins,
            [(_sds((s_len, n_in), BF), pl.BlockSpec((tm, tn), lambda i, j, r=r: (i, pos(per * j + r))))],
            NN, store_bf16, aliases={2: 0} if r else None, comm=comm)
    woa_g = landed
    woa = woa_g.transpose(1, 0, 2).reshape(a, d)
    pw = rest[0].transpose(1, 0, 2, 3).reshape(ng, cg, cg)
    gb = rest[1].transpose(1, 0, 2).reshape(2, d)

    rb_pad = jnp.pad(rel_bias, ((0, 0), (0, N_REL_PAD - N_REL)))
    base = _bias_rows(rb_pad).reshape(heads, 1, SKEW)
    wo_bf = w_out.astype(BF)
    d_lo = d // 4
    (att, ya), (wop_g, wo_lo_g) = _attn_fwd(
        proj, base, a, comm=_both(_gather_comm([w_out_pool.astype(BF)], pass_at=0.9),
                                  _gather_comm([wo_bf], cols=(0, d_lo), pass_at=0.9)))
    wop = wop_g.transpose(1, 0, 2).reshape(p, d)
    u_blk, z_blk = 4 * a // p, 4 * a // p + 1
    dmean, ypre, yp = _pool_fwd(proj, pw, scale_row, p, u_blk, z_blk)

    def gate_kernel(in_refs, out_refs, _):
        ya_ref, woa_ref, yp_ref, wop_ref, ga_ref, gp_ref, gb_ref = in_refs
        m_ref, a_ref, p_ref = out_refs
        am = jnp.dot(ya_ref[...], woa_ref[...], preferred_element_type=F32)
        pm = jnp.dot(yp_ref[...], wop_ref[...], preferred_element_type=F32)
        sa = _sig(ga_ref[...].astype(F32) + gb_ref[0:1, :])
        sp = _sig(gp_ref[...].astype(F32) + gb_ref[1:2, :])
        m_ref[...] = (sa * am + sp * pm).astype(BF)
        a_ref[...] = am.astype(BF)
        p_ref[...] = pm.astype(BF)

    tile_ij = pl.BlockSpec((tm, tn), lambda i, j: (i, j))
    act_d = _sds((s_len, d), BF)
    merged, am, pm, wo_hi_g = _call(
        "gate_merge", gate_kernel, (s_len // tm, d // tn),
        [(ya, pl.BlockSpec((tm, a), lambda i, j: (i, 0))), (woa_g, pl.BlockSpec((None, a, tn), lambda i, j: (j, 0, 0))),
         (yp, pl.BlockSpec((tm, p), lambda i, j: (i, 0))), (wop_g, pl.BlockSpec((None, p, tn), lambda i, j: (j, 0, 0))),
         (proj, pl.BlockSpec((tm, tn), lambda i, j: (i, gate0 + 2 * j))),
         (proj, pl.BlockSpec((tm, tn), lambda i, j: (i, gate0 + 2 * j + 1))),
         (gb, pl.BlockSpec((2, tn), lambda i, j: (0, j)))],
        [(act_d, tile_ij)] * 3, comm=_gather_comm([wo_bf], cols=(d_lo, d - d_lo), pass_at=0.9))
    wo_halves = [wo_lo_g.reshape(d, d_lo), wo_hi_g.reshape(d, d - d_lo)]

    def out_proj(in_refs, out_refs, _):
        m_ref, lo_ref, hi_ref, x_ref = in_refs
        j = pl.program_id(1)

        @pl.when(j < n_half)
        def _():
            out_refs[0][...] = x_ref[...] + jnp.dot(m_ref[...], lo_ref[...], preferred_element_type=F32)

        @pl.when(j >= n_half)
        def _():
            out_refs[0][...] = x_ref[...] + jnp.dot(m_ref[...], hi_ref[...], preferred_element_type=F32)

    n_half = d_lo // tn
    x2 = _call("out_proj", out_proj, (s_len // tm, d // tn),
               [(merged, pl.BlockSpec((tm, d), lambda i, j: (i, 0))),
                (wo_halves[0], pl.BlockSpec((d, tn), lambda i, j: (0, jnp.minimum(j, n_half - 1)))),
                (wo_halves[1], pl.BlockSpec((d, tn), lambda i, j: (0, jnp.maximum(j - n_half, 0)))),
                (x2d, tile_ij)],
               [(_sds((s_len, d), F32), tile_ij)])[0]

    dx2, dx2b, dg2, loss_part = _final_norm(x2, tgt, g2)

    tmb = _tile(s_len, 512)
    tile_ji = pl.BlockSpec((tmb, tn), lambda j, i: (i, j))

    def gate_bwd(dm, ex, outs):
        a_ref, p_ref, ga_ref, gp_ref, gb_ref = ex
        da_ref, dp_ref, dgate_ref, dgb_ref = outs
        i = pl.program_id(1)
        sa = _sig(ga_ref[...].astype(F32) + gb_ref[0:1, :])
        sp = _sig(gp_ref[...].astype(F32) + gb_ref[1:2, :])
        dga = dm * a_ref[...].astype(F32) * sa * (1.0 - sa)
        dgp = dm * p_ref[...].astype(F32) * sp * (1.0 - sp)
        da_ref[...] = (dm * sa).astype(BF)
        dp_ref[...] = (dm * sp).astype(BF)
        dgate_ref[:, :tn] = dga.astype(BF)
        dgate_ref[:, tn:] = dgp.astype(BF)
        r = lax.broadcasted_iota(jnp.int32, (8, tn), 0)
        sums = jnp.where(r == 0, jnp.sum(dga, axis=0, keepdims=True),
                         jnp.where(r == 1, jnp.sum(dgp, axis=0, keepdims=True), 0.0))

        @pl.when(i == 0)
        def _():
            dgb_ref[...] = sums

        @pl.when(i > 0)
        def _():
            dgb_ref[...] += sums

    dproj_shape = _sds((s_len, n_in), BF)
    d_am, d_pm, dproj, dgb8 = _mm(
        "gate_bwd", (d // tn, s_len // tmb),
        [(dx2b, pl.BlockSpec((tmb, d), lambda j, i: (i, 0)))]
        + [(w, pl.BlockSpec((tn, w.shape[1]), lambda j, i: (j, 0))) for w in wo_halves]
        + [(am, tile_ji), (pm, tile_ji),
           (proj, pl.BlockSpec((tmb, tn), lambda j, i: (i, gate0 + 2 * j))),
           (proj, pl.BlockSpec((tmb, tn), lambda j, i: (i, gate0 + 2 * j + 1))),
           (gb, pl.BlockSpec((2, tn), lambda j, i: (0, j)))],
        [(_sds((s_len, d), BF), tile_ji)] * 2
        + [(dproj_shape, pl.BlockSpec((tmb, 2 * tn), lambda j, i: (i, gate0 // 2 + j))),
           (_sds((8, d), F32), pl.BlockSpec((8, tn), lambda j, i: (0, j)))],
        NT, gate_bwd, nb=2)

    za_t = 3 * a // tn

    def attn_gate_bwd(dya, ex, outs):
        silu, dsilu = _silu_and_grad(ex[0][...].astype(F32))
        outs[0][...] = (dya * silu).astype(BF)
        outs[1][...] = (dya * ex[1][...].astype(F32) * dsilu).astype(BF)

    datt, dproj = _mm(
        "attn_gate_bwd", (s_len // tm, a // tn),
        [(d_am, pl.BlockSpec((tm, d), lambda i, j: (i, 0))), (woa, pl.BlockSpec((tn, d), lambda i, j: (j, 0))),
         (proj, pl.BlockSpec((tm, tn), lambda i, j: (i, za_t + j))), (att, tile_ij), (dproj, hbm)],
        [(_sds((s_len, a), BF), tile_ij), (dproj_shape, pl.BlockSpec((tm, tn), lambda i, j: (i, za_t + j)))],
        NT, attn_gate_bwd, aliases={4: 1})

    zp_t = (4 * a + p) // tn

    def pool_gate_bwd(dyp, ex, outs):
        z_ref, y_ref, sc_ref, _ = ex
        dzp_ref, dy_ref, dps_ref = outs
        i = pl.program_id(1)
        silu, dsilu = _silu_and_grad(z_ref[...].astype(F32))
        y = y_ref[...].astype(F32)
        sc = sc_ref[...]
        dyp0 = dyp * silu
        dzp_ref[...] = (dyp * (y * sc) * dsilu).astype(BF)
        dy_ref[...] = (dyp0 * sc).astype(BF)
        dps = jnp.sum(dyp0 * y, axis=0, keepdims=True)

        @pl.when(i == 0)
        def _():
            dps_ref[...] = dps

        @pl.when(i > 0)
        def _():
            dps_ref[...] += dps

    dproj, dy_pool, dps = _mm(
        "pool_gate_bwd", (p // tn, s_len // tmb),
        [(d_pm, pl.BlockSpec((tmb, d), lambda j, i: (i, 0))), (wop, pl.BlockSpec((tn, d), lambda j, i: (j, 0))),
         (proj, pl.BlockSpec((tmb, tn), lambda j, i: (i, zp_t + j))), (ypre, tile_ji),
         (scale_row, pl.BlockSpec((1, tn), lambda j, i: (0, j))), (dproj, hbm)],
        [(dproj_shape, pl.BlockSpec((tmb, tn), lambda j, i: (i, zp_t + j))), (_sds((s_len, p), BF), tile_ji),
         (_sds((1, p), F32), pl.BlockSpec((1, tn), lambda j, i: (0, j)))],
        NT, pool_gate_bwd, aliases={5: 0})

    dproj, dpw = _pool_bwd(dy_pool, dmean, pw, dproj, u_blk)

    o_wop, o_wo, o_pool = a, d, 2 * d
    slab_a = _sds((2, 4, 2 * d + cg, sw), BF)
    slab_b = _sds((2, 4, 5 * d // 2, sw), BF)
    hbm = pl.BlockSpec(memory_space=pl.ANY)
    tmw = _tile(a, 1024)
    core = lax.axis_index("c").astype(jnp.int32).reshape(1)

    def pack_small(dpw_ref, dgb_ref, o_ref):
        rows = cg // N_DEV
        for j in range(N_DEV):
            for g in range(ng):
                o_ref[j % 2, j // 2, g * rows:(g + 1) * rows, :] = dpw_ref[g, j * rows:(j + 1) * rows, :].astype(BF)
            o_ref[j % 2, j // 2, ng * rows:, :] = jnp.concatenate(
                [dgb_ref[:, j * sw:(j + 1) * sw], jnp.zeros((cg - ng * rows - 8, sw), F32)], axis=0).astype(BF)

    slab = pl.pallas_call(
        pack_small, grid=(1,),
        in_specs=[pl.BlockSpec((ng, cg, cg), lambda i: (0, 0, 0)), pl.BlockSpec((8, d), lambda i: (0, 0))],
        out_specs=pl.BlockSpec((2, 4, cg, sw), lambda i: (0, 0, o_pool // cg, 0)), out_shape=slab_a,
        name="dw_small", compiler_params=_params(1))(dpw, dgb8)

    def into_slab(acc, _, outs):
        outs[0][...] = acc.astype(BF)

    def weight_grad(name, slab, lhs, rhs, grid, lhs_spec, rhs_spec, out_spec, epi=into_slab):
        return _mm(name, grid, [(lhs, lhs_spec), (rhs, rhs_spec), (slab, hbm)], [(slab_a, out_spec)],
                   TN, epi, aliases={2: 0})

    def into_both_cores(acc, _, outs):
        outs[0][0] = acc[:sw].astype(BF)
        outs[0][1] = acc[sw:].astype(BF)

    slab = weight_grad("dw_out", slab, merged, dx2b, (N_DEV // 2, d // sw),
                       pl.BlockSpec((s_len, 2 * sw), lambda k, t: (0, k)),
                       pl.BlockSpec((s_len, sw), lambda k, t: (0, t)),
                       pl.BlockSpec((2, None, sw, sw), lambda k, t: (0, k, o_wo // sw + t, 0)), into_both_cores)[0]
    slab = weight_grad("dw_out_attn", slab, ya, d_am, (a // tmw, N_DEV),
                       pl.BlockSpec((s_len, tmw), lambda i, j: (0, i)), pl.BlockSpec((s_len, sw), lambda i, j: (0, j)),
                       pl.BlockSpec((None, None, tmw, sw), lambda i, j: (j % 2, j // 2, i, 0)))[0]
    slab = weight_grad("dw_out_pool", slab, yp, d_pm, (p // tmw, N_DEV),
                       pl.BlockSpec((s_len, tmw), lambda i, j: (0, i)), pl.BlockSpec((s_len, sw), lambda i, j: (0, j)),
                       pl.BlockSpec((None, None, tmw, sw), lambda i, j: (j % 2, j // 2, o_wop // tmw + i, 0)))[0]

    (dproj, dk, dv, ddiag), (from_sibling_a,) = _attn_bwd(proj, datt, base, a, dproj, comm=_cores_comm(slab))
    chip_part_a = _add_core_partials("add_core_partials_a", slab, from_sibling_a, core, (2 * d + cg) // 2)
    drb = _bias_grad(ddiag.reshape(heads, SKEW))
    dproj = lax.dynamic_update_slice(dproj, dk, (0, a))
    dproj = lax.dynamic_update_slice(dproj, dv, (0, 2 * a))

    tmd = _tile(d // 2, 1024)
    nrb = d // 2 // tmd

    def dw_in_rows(name, half, comm):
        return _mm(
            name, (nrb, n_in // sw),
            [(hb, pl.BlockSpec((s_len, tmd), lambda i, t: (0, half * nrb + i))),
             (dproj, pl.BlockSpec((s_len, sw), lambda i, t: (0, pos(t))))],
            [(slab_b, pl.BlockSpec((None, None, tmd, sw),
                                   lambda i, t: ((t // per) % 2, (t // per) // 2, (t % per) * nrb + i, 0)))],
            TN, into_slab, comm=comm)

    slab_lo, parts_a = dw_in_rows("dw_in_lo", 0, _chips_comm(chip_part_a))
    slab_hi, from_sibling_lo = dw_in_rows("dw_in_hi", 1, _cores_comm(slab_lo))
    chip_part_lo = _add_core_partials("add_core_partials_lo", slab_lo, from_sibling_lo, core, 4 * sw)

    tnh = _tile(d, 1024)
    tmh = tm if s_len > tm else s_len // 2
    n_row = s_len // tmh

    def dh_rows(name, lo, hi, prev, comm):
        def body(in_refs, out_refs, scratch_refs):
            acc_ref = scratch_refs[0]
            k = pl.program_id(2)
            part = lax.dot_general(in_refs[0][...], in_refs[per][...], NT, preferred_element_type=F32)
            for r in range(1, per):
                part += lax.dot_general(in_refs[r][...], in_refs[per + r][...], NT, preferred_element_type=F32)

            @pl.when(k == 0)
            def _():
                acc_ref[...] = part

            @pl.when(k > 0)
            def _():
                acc_ref[...] += part

            @pl.when(k == N_DEV - 1)
            def _():
                out_refs[0][...] = acc_ref[...]

        ins = [(dproj, pl.BlockSpec((tmh, sw), lambda i, j, k, r=r: (lo + i, pos(per * k + r)))) for r in range(per)]
        ins += [(w, pl.BlockSpec((None, tnh, sw), lambda i, j, k: (k, j, 0))) for w in win_rounds]
        if prev is not None:
            ins.append((prev, hbm))
        return _call(name, body, (hi - lo, d // tnh, N_DEV), ins,
                     [(_sds((s_len, d), F32), pl.BlockSpec((tmh, tnh), lambda i, j, k: (lo + i, j)))],
                     scratch=[pltpu.VMEM((tmh, tnh), F32)], aliases={2 * per: 0} if prev is not None else None,
                     comm=comm)

    dh, parts_lo, from_sibling_hi = dh_rows("dh_head", 0, n_row // 2, None,
                                            _both(_chips_comm(chip_part_lo), _cores_comm(slab_hi)))
    chip_part_hi = _add_core_partials("add_core_partials_hi", slab_hi, from_sibling_hi, core, 4 * sw)
    dh, parts_hi = dh_rows("dh_rest", n_row // 2, n_row, dh, _chips_comm(chip_part_hi))

    dx, dg1 = _rms_bwd(x2d, dh, dx2, g1)

    g_win, d_win, m_win, v_win = _adamw_row_halves("adamw_w_in", parts_lo, parts_hi, w_in, m_w_in, v_w_in)
    g_woa, d_woa, m_woa, v_woa = _adamw_shard("adamw_w_out_attn", parts_a, w_out_attn, m_w_out_attn, v_w_out_attn, 0)
    g_wop, d_wop, m_wop, v_wop = _adamw_shard("adamw_w_out_pool", parts_a, w_out_pool, m_w_out_pool, v_w_out_pool, o_wop)
    g_wo, d_wo, m_wo, v_wo = _adamw_shard("adamw_w_out", parts_a, w_out, m_w_out, v_w_out, o_wo)
    flat = lambda t: t.reshape(cg // 2, sw)
    pool_out = _adamw_shard("adamw_pool_w", parts_a, flat(pool_w), flat(m_pool_w), flat(v_pool_w), o_pool)
    g_pw, d_pw, m_pw, v_pw = [t.reshape(pool_w.shape) for t in pool_out]
    pad16 = lambda t: jnp.pad(t, ((0, 14), (0, 0)))
    gb_out = _adamw_shard("adamw_gate_bias", parts_a, pad16(gate_bias), pad16(m_gate_bias), pad16(v_gate_bias),
                          o_pool + cg // 2)
    g_gb, d_gb, m_gb, v_gb = [t[:2] for t in gb_out]

    def pack(n_gain, f_gain, scale, rb, last):
        rows = [n_gain.reshape(-1, LANES), f_gain.reshape(-1, LANES), scale.reshape(-1, LANES),
                rb.reshape(-1, LANES), last]
        return jnp.concatenate(rows, axis=0)

    pad_rb = lambda t: jnp.pad(t, ((0, 0), (0, N_REL_PAD - N_REL)))
    zeros8 = jnp.zeros((8, LANES), F32)
    loss_rows = jnp.pad(loss_part, ((0, 7), (0, 0)))
    small = _small_allreduce_adamw(
        pack(dg1, dg2, dps, drb, loss_rows),
        pack(norm_gain, final_gain, pool_scale, pad_rb(rel_bias), zeros8),
        pack(m_norm_gain, m_final_gain, m_pool_scale, pad_rb(m_rel_bias), zeros8),
        pack(v_norm_gain, v_final_gain, v_pool_scale, pad_rb(v_rel_bias), zeros8))

    n1, n2, n3 = d // LANES, 2 * d // LANES, (2 * d + p) // LANES
    n4 = n3 + heads * N_REL_PAD // LANES

    def unpack(t):
        return (t[:n1].reshape(d), t[n1:n2].reshape(d), t[n2:n3].reshape(p),
                t[n3:n4].reshape(heads, N_REL_PAD)[:, :N_REL])

    (g_ng, g_fg, g_ps, g_rb), (d_ng, d_fg, d_ps, d_rb), (m_ng, m_fg, m_ps, m_rb), (v_ng, v_fg, v_ps, v_rb) = [
        unpack(t) for t in small]
    loss = small[0][n4, 0]

    return (loss, dx.reshape(x.shape),
            g_ng, g_win, g_rb, g_pw, g_ps, g_woa, g_wop, g_gb, g_wo, g_fg,
            d_ng, d_win, d_rb, d_pw, d_ps, d_woa, d_wop, d_gb, d_wo, d_fg,
            m_ng, m_win, m_rb, m_pw, m_ps, m_woa, m_wop, m_gb, m_wo, m_fg,
            v_ng, v_win, v_rb, v_pw, v_ps, v_woa, v_wop, v_gb, v_wo, v_fg)
```

```python
import jax
import jax.numpy as jnp
from jax import lax
from jax.experimental import pallas as pl
from jax.experimental.pallas import tpu as pltpu

F32 = jnp.float32
BF = jnp.bfloat16
MESH = pl.DeviceIdType.MESH

N_DEV = 8
CHUNK = 64
N_LEFT_CHUNKS = 8
HEAD_DIM = 128
MAX_REL = 128
N_REL = 2 * MAX_REL + 1
N_REL_PAD = 384
POOL_WINDOWS = (2, 4, 8, 16)
HALO = 16
EPS = 1e-6
ADAM_LR = 0.001
ADAM_B1 = 0.9
ADAM_B2 = 0.999
ADAM_EPS = 1e-08
ADAM_WD = 0.01
ADAM_STEP = 10
NEG = -1e30
LANES = 128
TQ = N_LEFT_CHUNKS * CHUNK
TK = 2 * TQ
SKEW = 2 * TK
VMEM_LIMIT = 52 * 1024 * 1024

NN = (((1,), (0,)), ((), ()))
NT = (((1,), (1,)), ((), ()))
TN = (((0,), (0,)), ((), ()))


def _params(n_grid):
    return pltpu.CompilerParams(dimension_semantics=("arbitrary",) * n_grid, vmem_limit_bytes=VMEM_LIMIT)


def _sig(z):
    return 1.0 / (1.0 + jnp.exp(-z))


def _silu_and_grad(z):
    s = _sig(z)
    return z * s, s * (1.0 + z * (1.0 - s))


def _tile(n, pref):
    t = min(n, pref)
    assert n % t == 0, (n, pref)
    return t


def _sds(shape, dtype):
    return jax.ShapeDtypeStruct(shape, dtype)


class _Comm:
    def __init__(self, ins, outs, scratch, start, wait, hooks=()):
        self.ins, self.outs, self.scratch = list(ins), list(outs), list(scratch)
        self.start, self.wait, self.hooks = start, wait, tuple(hooks)


def _call(name, body, grid, ins, outs, scratch=(), aliases=None, comm=None):
    n_in, n_out, n_scr = len(ins), len(outs), len(scratch)
    c_in = len(comm.ins) if comm else 0
    c_out = len(comm.outs) if comm else 0
    n_steps = 1
    for g in grid:
        n_steps *= g

    def kern(*refs):
        o0 = n_in + c_in
        s0 = o0 + n_out + c_out
        if comm:
            c_refs = (refs[n_in:o0], refs[o0 + n_out:s0], refs[s0 + n_scr:])
            step = pl.program_id(0)
            for ax in range(1, len(grid)):
                step = step * grid[ax] + pl.program_id(ax)

            @pl.when(step == 0)
            def _():
                comm.start(*c_refs)

            for frac, hook in comm.hooks:
                @pl.when(step == int(frac * n_steps))
                def _(hook=hook):
                    hook(*c_refs)

        body(refs[:n_in], refs[o0:o0 + n_out], refs[s0:s0 + n_scr])

        if comm:
            @pl.when(step == n_steps - 1)
            def _():
                comm.wait(*c_refs)

    hbm = pl.BlockSpec(memory_space=pl.ANY)
    return pl.pallas_call(
        kern, grid=grid,
        in_specs=[s for _, s in ins] + [hbm] * c_in, out_specs=[s for _, s in outs] + [hbm] * c_out,
        out_shape=[o for o, _ in outs] + (comm.outs if comm else []),
        scratch_shapes=list(scratch) + (comm.scratch if comm else []),
        name=name, compiler_params=_params(len(grid)), input_output_aliases=aliases or {},
    )(*([a for a, _ in ins] + (comm.ins if comm else [])))


def _mm(name, grid, ins, outs, dims, epi, aliases=None, comm=None, nb=1):
    def body(in_refs, out_refs, _):
        if nb == 1:
            acc = lax.dot_general(in_refs[0][...], in_refs[1][...], dims, preferred_element_type=F32)
        else:
            acc, k0 = 0.0, 0
            for b_ref in in_refs[1:1 + nb]:
                k1 = k0 + b_ref.shape[1]
                acc = acc + lax.dot_general(in_refs[0][:, k0:k1], b_ref[...], dims, preferred_element_type=F32)
                k0 = k1
        epi(acc, in_refs[1 + nb:], out_refs)

    return _call(name, body, grid, ins, outs, aliases=aliases, comm=comm)


def _rms_fwd(x, g, w, comm=None):
    s, d = x.shape
    tr = _tile(s, 256)
    wr = w.shape[0] // (s // tr)

    def body(in_refs, out_refs, _):
        xv = in_refs[0][...]
        r = lax.rsqrt(jnp.mean(xv * xv, axis=-1, keepdims=True) + EPS)
        out_refs[0][...] = (xv * r * in_refs[1][...]).astype(BF)
        out_refs[1][...] = in_refs[2][...].astype(BF)

    row = pl.BlockSpec((tr, d), lambda i: (i, 0))
    w_rows = pl.BlockSpec((wr, w.shape[1]), lambda i: (i, 0))
    return _call("rms_fwd", body, (s // tr,), [(x, row), (g, pl.BlockSpec((1, d), lambda i: (0, 0))), (w, w_rows)],
                 [(_sds((s, d), BF), row), (_sds(w.shape, BF), w_rows)], comm=comm)


def _final_norm(x2, target, g):
    s, d = x2.shape
    tr = _tile(s, 128)

    def kern(x_ref, t_ref, g_ref, dx_ref, dxb_ref, dg_ref, loss_ref):
        i = pl.program_id(0)
        xv = x_ref[...]
        gv = g_ref[...]
        r = lax.rsqrt(jnp.mean(xv * xv, axis=-1, keepdims=True) + EPS)
        xhat = xv * r
        err = xhat * gv - t_ref[...]
        dy = err * (1.0 / d)
        gy = dy * gv
        dx = r * (gy - xhat * jnp.mean(gy * xhat, axis=-1, keepdims=True))
        dx_ref[...] = dx
        dxb_ref[...] = dx.astype(BF)
        dg = jnp.sum(dy * xhat, axis=0, keepdims=True)
        ls = jnp.broadcast_to(0.5 * jnp.sum(jnp.mean(err * err, axis=-1, keepdims=True)), (1, LANES))

        @pl.when(i == 0)
        def _():
            dg_ref[...] = dg
            loss_ref[...] = ls

        @pl.when(i > 0)
        def _():
            dg_ref[...] += dg
            loss_ref[...] += ls

    row = pl.BlockSpec((tr, d), lambda i: (i, 0))
    vec = pl.BlockSpec((1, d), lambda i: (0, 0))
    return pl.pallas_call(
        kern, grid=(s // tr,), in_specs=[row, row, vec],
        out_specs=[row, row, vec, pl.BlockSpec((1, LANES), lambda i: (0, 0))],
        out_shape=[_sds((s, d), F32), _sds((s, d), BF), _sds((1, d), F32), _sds((1, LANES), F32)],
        name="final_norm", compiler_params=_params(1))(x2, target, g)


def _rms_bwd(x, dh, dx2, g):
    s, d = x.shape
    tr = _tile(s, 128)

    def kern(x_ref, dh_ref, dx2_ref, g_ref, dx_ref, dg_ref):
        i = pl.program_id(0)
        xv = x_ref[...]
        r = lax.rsqrt(jnp.mean(xv * xv, axis=-1, keepdims=True) + EPS)
        xhat = xv * r
        dhv = dh_ref[...]
        gh = dhv * g_ref[...]
        dx_ref[...] = dx2_ref[...] + r * (gh - xhat * jnp.mean(gh * xhat, axis=-1, keepdims=True))
        dg = jnp.sum(dhv * xhat, axis=0, keepdims=True)

        @pl.when(i == 0)
        def _():
            dg_ref[...] = dg

        @pl.when(i > 0)
        def _():
            dg_ref[...] += dg

    row = pl.BlockSpec((tr, d), lambda i: (i, 0))
    vec = pl.BlockSpec((1, d), lambda i: (0, 0))
    return pl.pallas_call(
        kern, grid=(s // tr,), in_specs=[row, row, row, vec], out_specs=[row, vec],
        out_shape=[_sds((s, d), F32), _sds((1, d), F32)],
        name="rms_bwd", compiler_params=_params(1))(x, dh, dx2, g)


def _rel_index(j, backward):
    if backward:
        rel = 2 * TQ - 1 - j
    else:
        rel = TQ - jnp.where(j < TK, j, j - SKEW)
    return jnp.clip(rel, -MAX_REL, MAX_REL) + MAX_REL


def _bias_rows(rel_bias_pad):
    h = rel_bias_pad.shape[0]

    def kern(rb_ref, o_ref):
        j = lax.broadcasted_iota(jnp.int32, (N_REL_PAD, SKEW), 1)
        k = lax.broadcasted_iota(jnp.int32, (N_REL_PAD, SKEW), 0)
        onehot = (_rel_index(j, False) == k).astype(F32)
        o_ref[...] = jnp.dot(rb_ref[...], onehot, preferred_element_type=F32, precision=lax.Precision.HIGHEST)

    return pl.pallas_call(kern, out_shape=_sds((h, SKEW), F32), name="bias_rows")(rel_bias_pad)


def _bias_grad(ddiag):
    h = ddiag.shape[0]

    def kern(d_ref, o_ref):
        j = lax.broadcasted_iota(jnp.int32, (N_REL_PAD, SKEW), 1)
        k = lax.broadcasted_iota(jnp.int32, (N_REL_PAD, SKEW), 0)
        onehot = ((_rel_index(j, True) == k) & (j < TQ + TK - 1)).astype(F32)
        o_ref[...] = lax.dot_general(d_ref[...], onehot, NT, preferred_element_type=F32,
                                     precision=lax.Precision.HIGHEST)

    return pl.pallas_call(kern, out_shape=_sds((h, N_REL_PAD), F32), name="bias_grad")(ddiag)


def _bias_tile(row):
    t = pltpu.roll(jnp.broadcast_to(row, (TQ, SKEW)), 0, 1, stride=1, stride_axis=0)[:, :TK]
    r = lax.broadcasted_iota(jnp.int32, (TQ, TK), 0) // CHUNK
    c = lax.broadcasted_iota(jnp.int32, (TQ, TK), 1) // CHUNK
    dist = N_LEFT_CHUNKS + r - c
    return jnp.where((dist >= 0) & (dist <= N_LEFT_CHUNKS), t, NEG)


def _scores(q, keys, tile):
    s = lax.dot_general(q, keys, NT, preferred_element_type=F32) * (HEAD_DIM ** -0.5) + tile
    m = jnp.max(s, axis=1, keepdims=True)
    p = jnp.exp(s - m)
    return p, jnp.sum(p, axis=1, keepdims=True)


HQ = TQ // 2
BANDS = ((0, 0, TQ + HQ), (1, HQ, TK))
BANDS_FIRST = ((0, TQ, TQ + HQ), (1, TQ, TK))


def _key_rows(prev_ref, cur_ref, first, end):
    if first >= TQ:
        return cur_ref[first - TQ:end - TQ, :]
    return jnp.concatenate([prev_ref[first:, :], cur_ref[:end - TQ, :]], axis=0)


def _attn_fwd(proj, base, a_width, comm=None):
    s_len = proj.shape[0]
    heads = a_width // HEAD_DIM
    nq = s_len // TQ
    kb, vb, zb = heads, 2 * heads, 3 * heads

    def kern(q_ref, kp_ref, kc_ref, vp_ref, vc_ref, z_ref, base_ref, att_ref, ya_ref, tile_ref):
        i = pl.program_id(1)

        def attend(bands):
            for half, first, end in bands:
                rows = slice(half * HQ, (half + 1) * HQ)
                p, l = _scores(q_ref[rows, :], _key_rows(kp_ref, kc_ref, first, end), tile_ref[rows, first:end])
                o = jnp.dot(p.astype(BF), _key_rows(vp_ref, vc_ref, first, end), preferred_element_type=F32) / l
                att_ref[rows, :] = o.astype(BF)
                z = z_ref[rows, :].astype(F32)
                ya_ref[rows, :] = (o * (z * _sig(z))).astype(BF)

        @pl.when(i == 0)
        def _():
            tile_ref[...] = _bias_tile(base_ref[...])
            attend(BANDS_FIRST)

        @pl.when(i > 0)
        def _():
            attend(BANDS)

    blk = lambda off: pl.BlockSpec((TQ, HEAD_DIM), lambda h, i: (i, off + h))
    prev = lambda off: pl.BlockSpec((TQ, HEAD_DIM), lambda h, i: (jnp.maximum(i - 1, 0), off + h))
    out = pl.BlockSpec((TQ, HEAD_DIM), lambda h, i: (i, h))
    def body(in_refs, out_refs, scratch_refs):
        kern(*in_refs, *out_refs, *scratch_refs)

    act = _sds((s_len, a_width), BF)
    res = _call(
        "attn_fwd", body, (heads, nq),
        [(proj, blk(0)), (proj, prev(kb)), (proj, blk(kb)), (proj, prev(vb)), (proj, blk(vb)), (proj, blk(zb)),
         (base, pl.BlockSpec((None, 1, SKEW), lambda h, i: (h, 0, 0)))],
        [(act, out), (act, out)], scratch=[pltpu.VMEM((TQ, TK), F32)], comm=comm)
    return res[:2], res[2:]


def _attn_bwd(proj, datt, base, a_width, dproj, comm=None):
    s_len = proj.shape[0]
    heads = a_width // HEAD_DIM
    nq = s_len // TQ
    kb, vb = heads, 2 * heads
    scale = HEAD_DIM ** -0.5

    def body(in_refs, out_refs, scratch_refs):
        q_ref, kp_ref, kc_ref, vp_ref, vc_ref, do_ref, base_ref, _ = in_refs
        dq_ref, dk_ref, dv_ref, dd_ref = out_refs
        tile_ref, dsacc_ref, ak_ref, av_ref = scratch_refs
        i = pl.program_id(1)

        def backward(keys, vals, tile, cols):
            q = q_ref[...]
            do = do_ref[...]
            p, l = _scores(q, keys, tile)
            p = p / l
            dp = lax.dot_general(do, vals, NT, preferred_element_type=F32)
            ds = p * (dp - jnp.sum(p * dp, axis=1, keepdims=True))
            dsacc_ref[:, cols] += ds
            dsb = ds.astype(BF)
            dq_ref[...] = (jnp.dot(dsb, keys, preferred_element_type=F32) * scale).astype(BF)
            return (lax.dot_general(dsb, q, TN, preferred_element_type=F32) * scale,
                    lax.dot_general(p.astype(BF), do, TN, preferred_element_type=F32))

        @pl.when(i == 0)
        def _():
            tile_ref[...] = _bias_tile(base_ref[...])
            dsacc_ref[...] = jnp.zeros_like(dsacc_ref)
            ak_ref[...], av_ref[...] = backward(kc_ref[...], vc_ref[...], tile_ref[:, TQ:], slice(TQ, TK))
            dk_ref[...] = jnp.zeros_like(dk_ref)
            dv_ref[...] = jnp.zeros_like(dv_ref)

        @pl.when((i > 0) & (i < nq))
        def _():
            dkc, dvc = backward(jnp.concatenate([kp_ref[...], kc_ref[...]], axis=0),
                                jnp.concatenate([vp_ref[...], vc_ref[...]], axis=0), tile_ref[...], slice(0, TK))
            dk_ref[...] = (ak_ref[...] + dkc[:TQ]).astype(BF)
            dv_ref[...] = (av_ref[...] + dvc[:TQ]).astype(BF)
            ak_ref[...] = dkc[TQ:]
            av_ref[...] = dvc[TQ:]

        @pl.when(i == nq)
        def _():
            dk_ref[...] = ak_ref[...].astype(BF)
            dv_ref[...] = av_ref[...].astype(BF)
            acc = dsacc_ref[...]
            rr = lax.broadcasted_iota(jnp.int32, (TQ, TQ), 0)
            cc = lax.broadcasted_iota(jnp.int32, (TQ, TQ), 1)
            flip = (rr + cc == TQ - 1).astype(BF)
            hi = acc.astype(BF)
            lo = (acc - hi.astype(F32)).astype(BF)
            rev = jnp.dot(flip, hi, preferred_element_type=F32) + jnp.dot(flip, lo, preferred_element_type=F32)
            wide = jnp.concatenate([rev, jnp.zeros((TQ, SKEW - TK), F32)], axis=1)
            dd_ref[...] = jnp.sum(pltpu.roll(wide, 0, 1, stride=1, stride_axis=0), axis=0, keepdims=True)

    last = nq - 1
    cur = lambda off: pl.BlockSpec((TQ, HEAD_DIM), lambda h, i: (jnp.minimum(i, last), off + h))
    prev = lambda off: pl.BlockSpec((TQ, HEAD_DIM), lambda h, i: (jnp.maximum(jnp.minimum(i, last) - 1, 0), off + h))
    done = pl.BlockSpec((TQ, HEAD_DIM), lambda h, i: (jnp.maximum(i - 1, 0), h))
    row = pl.BlockSpec((None, 1, SKEW), lambda h, i: (h, 0, 0))
    act = _sds((s_len, a_width), BF)
    res = _call(
        "attn_bwd", body, (heads, nq + 1),
        [(proj, cur(0)), (proj, prev(kb)), (proj, cur(kb)), (proj, prev(vb)), (proj, cur(vb)), (datt, cur(0)),
         (base, row), (dproj, pl.BlockSpec(memory_space=pl.ANY))],
        [(_sds(dproj.shape, dproj.dtype), cur(0)), (act, done), (act, done), (_sds((heads, 1, SKEW), F32), row)],
        aliases={7: 0},
        scratch=[pltpu.VMEM((TQ, TK), F32), pltpu.VMEM((TQ, TK), F32),
                 pltpu.VMEM((TQ, HEAD_DIM), F32), pltpu.VMEM((TQ, HEAD_DIM), F32)],
        comm=comm)
    return res[:4], res[4:]


def _pool_fwd(proj, pool_w, pool_scale, p_width, u_blk, z_blk):
    s_len = proj.shape[0]
    cg = p_width // len(POOL_WINDOWS)
    tt = _tile(s_len, 512)

    def kern(up_ref, uc_ref, z_ref, pw_ref, sc_ref, d_ref, y_ref, yp_ref):
        t = pl.program_id(0)
        row = lax.broadcasted_iota(jnp.int32, (tt, 1), 0) + t * tt
        for g, w in enumerate(POOL_WINDOWS):
            cs = slice(g * cg, (g + 1) * cg)
            prev = jnp.where(t == 0, 0.0, up_ref[:, cs].astype(F32))
            cur = uc_ref[:, cs].astype(F32)
            ws = jnp.concatenate([prev, cur], axis=0)
            sh = 1
            while sh < w:
                ws = ws + pltpu.roll(ws, sh, 0)
                sh *= 2
            cnt = jnp.minimum(row + 1, w).astype(F32)
            db = (ws[HALO:, :] / cnt - cur).astype(BF)
            y = jnp.dot(db, pw_ref[g], preferred_element_type=F32)
            d_ref[:, cs] = db
            y_ref[:, cs] = y.astype(BF)
            z = z_ref[:, cs].astype(F32)
            yp_ref[:, cs] = (y * sc_ref[:, cs] * (z * _sig(z))).astype(BF)

    full = pl.BlockSpec((tt, p_width), lambda t: (t, 0))
    return pl.pallas_call(
        kern, grid=(s_len // tt,),
        in_specs=[pl.BlockSpec((HALO, p_width), lambda t: (jnp.maximum(t * (tt // HALO) - 1, 0), u_blk)),
                  pl.BlockSpec((tt, p_width), lambda t: (t, u_blk)),
                  pl.BlockSpec((tt, p_width), lambda t: (t, z_blk)),
                  pl.BlockSpec((len(POOL_WINDOWS), cg, cg), lambda t: (0, 0, 0)),
                  pl.BlockSpec((1, p_width), lambda t: (0, 0))],
        out_specs=[full, full, full], out_shape=[_sds((s_len, p_width), BF)] * 3,
        name="pool_fwd", compiler_params=_params(1))(proj, proj, proj, pool_w, pool_scale)


def _pool_bwd(dy, dmean, pool_w, dproj, u_blk):
    s_len, p_width = dy.shape
    ng = len(POOL_WINDOWS)
    cg = p_width // ng
    tt = _tile(s_len, 512)
    nt = s_len // tt

    def kern(dyc_ref, dyn_ref, d_ref, pw_ref, _, du_ref, dpw_ref):
        t = pl.program_id(0)

        @pl.when(t == 0)
        def _():
            dpw_ref[...] = jnp.zeros_like(dpw_ref)

        row = lax.broadcasted_iota(jnp.int32, (tt + HALO, 1), 0) + t * tt
        for g, w in enumerate(POOL_WINDOWS):
            cs = slice(g * cg, (g + 1) * cg)
            dyc = dyc_ref[:, cs]
            ddc = lax.dot_general(dyc, pw_ref[g], NT, preferred_element_type=F32)
            ddn = lax.dot_general(dyn_ref[:, cs], pw_ref[g], NT, preferred_element_type=F32)
            ddn = jnp.where(t == nt - 1, 0.0, ddn)
            cnt = jnp.minimum(row + 1, w).astype(F32)
            ws = jnp.concatenate([ddc, ddn], axis=0) / cnt
            sh = 1
            while sh < w:
                ws = ws + pltpu.roll(ws, tt + HALO - sh, 0)
                sh *= 2
            du_ref[:, cs] = (ws[:tt, :] - ddc).astype(BF)
            dpw_ref[g] += lax.dot_general(d_ref[:, cs], dyc, TN, preferred_element_type=F32)

    full = pl.BlockSpec((tt, p_width), lambda t: (t, 0))
    pw_spec = pl.BlockSpec((ng, cg, cg), lambda t: (0, 0, 0))
    return pl.pallas_call(
        kern, grid=(nt,),
        in_specs=[full,
                  pl.BlockSpec((HALO, p_width), lambda t: (jnp.minimum((t + 1) * (tt // HALO), s_len // HALO - 1), 0)),
                  full, pw_spec, pl.BlockSpec(memory_space=pl.ANY)],
        out_specs=[pl.BlockSpec((tt, p_width), lambda t: (t, u_blk)), pw_spec],
        out_shape=[_sds(dproj.shape, dproj.dtype), _sds((ng, cg, cg), F32)],
        input_output_aliases={4: 0},
        name="pool_bwd", compiler_params=_params(1))(dy, dy, dmean, pool_w, dproj)


def _adam(g, w_ref, m_ref, v_ref, g_out, d_out, m_out, v_out):
    m = ADAM_B1 * m_ref[...] + (1.0 - ADAM_B1) * g
    v = ADAM_B2 * v_ref[...] + (1.0 - ADAM_B2) * (g * g)
    m_hat = m / (1.0 - ADAM_B1 ** ADAM_STEP)
    v_hat = v / (1.0 - ADAM_B2 ** ADAM_STEP)
    g_out[...] = g
    d_out[...] = -ADAM_LR * (m_hat / (jnp.sqrt(v_hat) + ADAM_EPS) + ADAM_WD * w_ref[...])
    m_out[...] = m
    v_out[...] = v


def _adamw_shard(name, parts, w, m, v, row_off):
    rw, cw = w.shape
    sw = parts.shape[2]
    tr = _tile(rw, 512)
    assert row_off % tr == 0 and cw % sw == 0

    def kern(b_ref, w_ref, m_ref, v_ref, g_out, d_out, m_out, v_out):
        b = b_ref[...].astype(F32)
        _adam(((b[0] + b[1]) + b[2]) + b[3], w_ref, m_ref, v_ref, g_out, d_out, m_out, v_out)

    blk = pl.BlockSpec((tr, sw), lambda ct, i: (i, ct))
    return pl.pallas_call(
        kern, grid=(cw // sw, rw // tr),
        in_specs=[pl.BlockSpec((4, tr, sw), lambda ct, i: (0, (row_off + ct * rw) // tr + i, 0)), blk, blk, blk],
        out_specs=[blk] * 4, out_shape=[_sds((rw, cw), F32)] * 4,
        name=name, compiler_params=_params(2))(parts, w, m, v)


def _adamw_row_halves(name, parts_lo, parts_hi, w, m, v):
    rw, cw = w.shape
    sw = parts_lo.shape[2]
    half = rw // 2
    tr = _tile(half, 512)
    nh = half // tr

    def kern(lo_ref, hi_ref, w_ref, m_ref, v_ref, g_out, d_out, m_out, v_out):
        i = pl.program_id(1)

        def update(b_ref):
            b = b_ref[...].astype(F32)
            _adam(((b[0] + b[1]) + b[2]) + b[3], w_ref, m_ref, v_ref, g_out, d_out, m_out, v_out)

        @pl.when(i < nh)
        def _():
            update(lo_ref)

        @pl.when(i >= nh)
        def _():
            update(hi_ref)

    blk = pl.BlockSpec((tr, sw), lambda ct, i: (i, ct))
    return pl.pallas_call(
        kern, grid=(cw // sw, rw // tr),
        in_specs=[pl.BlockSpec((4, tr, sw), lambda ct, i: (0, ct * nh + jnp.minimum(i, nh - 1), 0)),
                  pl.BlockSpec((4, tr, sw), lambda ct, i: (0, ct * nh + jnp.maximum(i - nh, 0), 0)), blk, blk, blk],
        out_specs=[blk] * 4, out_shape=[_sds((rw, cw), F32)] * 4,
        name=name, compiler_params=_params(2))(parts_lo, parts_hi, w, m, v)


def _both(c1, c2):
    n_in, n_out, n_sem = len(c1.ins), len(c1.outs), len(c1.scratch)

    def split(ins, outs, sems):
        return (ins[:n_in], outs[:n_out], sems[:n_sem]), (ins[n_in:], outs[n_out:], sems[n_sem:])

    def start(*refs):
        r1, r2 = split(*refs)
        c1.start(*r1)
        c2.start(*r2)

    def wait(*refs):
        r1, r2 = split(*refs)
        c1.wait(*r1)
        c2.wait(*r2)

    def of(which, hook):
        return lambda *refs: hook(*split(*refs)[which])

    hooks = [(f, of(0, h)) for f, h in c1.hooks] + [(f, of(1, h)) for f, h in c2.hooks]
    return _Comm(c1.ins + c2.ins, c1.outs + c2.outs, c1.scratch + c2.scratch, start, wait, hooks)


def _position():
    return lax.axis_index("x"), lax.axis_index("y"), lax.axis_index("c")


def _gather_comm(shards, cols=None, pass_at=None):
    if cols is None:
        pieces = [(a, None) for a in range(len(shards))]
        shapes = [_sds(s.shape, s.dtype) for s in shards]
    else:
        half = shards[0].shape[0] // 2
        pieces = [(0, pl.ds(0, half)), (0, pl.ds(half, half))]
        shapes = [_sds((shards[0].shape[0], cols[1]), shards[0].dtype)]
    n = len(pieces)

    def plan(xs, outs, sems, only=None):
        send_sems, recv_sems, local_sems = sems
        x, y, c = _position()
        me, sibling = (x, y, c), (x, y, 1 - c)
        chips = [(1 - x, y), (x, 1 - y), (1 - x, 1 - y)]
        which = range(n) if only is None else only

        def source(v):
            a, rows = pieces[v]
            return xs[a] if rows is None else xs[a].at[rows, pl.ds(cols[0], cols[1])]

        def landing(v, block):
            a, rows = pieces[v]
            dst = outs[a].at[4 * block[0] + 2 * block[1] + block[2]]
            return dst if rows is None else dst.at[rows, :]

        def copy(v, k, block, to, own=False):
            dst = landing(v, block)
            return pltpu.make_async_remote_copy(
                src_ref=source(v) if own else dst, dst_ref=dst,
                send_sem=send_sems.at[7 * v + k], recv_sem=recv_sems.at[7 * v + k],
                device_id=to, device_id_type=MESH)

        by_chip = [(j, chip, v) for v in which for j, chip in enumerate(chips)]
        return dict(
            mine=lambda: [pltpu.make_async_copy(source(v), landing(v, me), local_sems.at[v]) for v in which],
            first=lambda: ([copy(v, 0, me, sibling, own=True) for v in which]
                           + [copy(v, 1 + j, me, (*chip, c), own=True) for j, chip, v in by_chip]),
            landed=lambda: [copy(v, 1 + j, (*chip, c), me) for j, chip, v in by_chip],
            passed=lambda: [copy(v, 4 + j, (*chip, c), sibling) for j, chip, v in by_chip],
            rest=lambda: ([copy(v, 0, sibling, me) for v in which]
                          + [copy(v, 4 + j, (*chip, 1 - c), me) for j, chip, v in by_chip]))

    def start(*refs):
        p = plan(*refs)
        for cp in p["mine"]() + p["first"]():
            cp.start()

    def pass_on(only):
        def hook(*refs):
            p = plan(*refs, only=only)
            for arrived, onward in zip(p["landed"](), p["passed"]()):
                arrived.wait_recv()
                onward.start()
        return hook

    in_halves = cols is not None and pass_at is None

    def wait(*refs):
        if in_halves:
            pass_on([1])(*refs)
        p = plan(*refs)
        for cp in p["rest"]():
            cp.wait_recv()
        for cp in p["first"]() + p["passed"]():
            cp.wait_send()
        for cp in p["mine"]():
            cp.wait()

    hooks = [(0.5, pass_on([0]))] if in_halves else [(0.75 if pass_at is None else pass_at, pass_on(None))]
    return _Comm(shards, [_sds((N_DEV,) + s.shape, s.dtype) for s in shapes],
                 [pltpu.SemaphoreType.DMA((7 * n,)), pltpu.SemaphoreType.DMA((7 * n,)),
                  pltpu.SemaphoreType.DMA((n,))], start, wait, hooks)


def _cores_comm(slab):
    _, _, r, sw = slab.shape

    def copies(ins, outs, sems):
        x, y, c = _position()
        return [pltpu.make_async_remote_copy(
            src_ref=ins[0].at[1 - c], dst_ref=outs[0], send_sem=sems[0], recv_sem=sems[1],
            device_id=(x, y, 1 - c), device_id_type=MESH)]

    def start(*refs):
        for cp in copies(*refs):
            cp.start()

    def wait(*refs):
        for cp in copies(*refs):
            cp.wait()

    return _Comm([slab], [_sds((4, r, sw), slab.dtype)],
                 [pltpu.SemaphoreType.DMA, pltpu.SemaphoreType.DMA], start, wait)


def _add_core_partials(name, slab, recv, core, tr):
    _, _, r, sw = slab.shape

    def kern(c_ref, a_ref, b_ref, o_ref):
        o_ref[...] = (a_ref[...].astype(F32) + b_ref[...].astype(F32)).astype(BF)

    return pl.pallas_call(
        kern,
        grid_spec=pltpu.PrefetchScalarGridSpec(
            num_scalar_prefetch=1, grid=(4, r // tr),
            in_specs=[pl.BlockSpec((None, None, tr, sw), lambda k, i, c_ref: (c_ref[0], k, i, 0)),
                      pl.BlockSpec((None, tr, sw), lambda k, i, c_ref: (k, i, 0))],
            out_specs=pl.BlockSpec((None, tr, sw), lambda k, i, c_ref: (k, i, 0))),
        out_shape=_sds((4, r, sw), BF), name=name, compiler_params=_params(2))(core, slab, recv)


def _chips_comm(part):
    _, r, sw = part.shape

    def copies(ins, outs, sems):
        send_sems, recv_sems, local_sem = sems
        x, y, c = _position()
        mine = 2 * x + y
        local = pltpu.make_async_copy(ins[0].at[mine], outs[0].at[mine], local_sem)
        chips = [(1 - x, y), (x, 1 - y), (1 - x, 1 - y)]
        remote = [pltpu.make_async_remote_copy(
            src_ref=ins[0].at[2 * px + py], dst_ref=outs[0].at[mine],
            send_sem=send_sems.at[j], recv_sem=recv_sems.at[j],
            device_id=(px, py, c), device_id_type=MESH) for j, (px, py) in enumerate(chips)]
        return [local] + remote

    def start(*refs):
        for cp in copies(*refs):
            cp.start()

    def wait(*refs):
        for cp in copies(*refs):
            cp.wait()

    return _Comm([part], [_sds((4, r, sw), part.dtype)],
                 [pltpu.SemaphoreType.DMA((3,)), pltpu.SemaphoreType.DMA((3,)), pltpu.SemaphoreType.DMA],
                 start, wait)


def _small_allreduce_adamw(partial, w, m, v):
    nr = partial.shape[0]

    def kern(p_ref, w_ref, m_ref, v_ref, g_out, d_out, m_out, v_out, gath_ref, send_sems, recv_sems):
        x, y, c = _position()
        me = 4 * x + 2 * y + c
        gath_ref[me] = p_ref[...]
        copies = []
        for mask in range(1, N_DEV):
            peer = (x ^ (mask >> 2), y ^ ((mask >> 1) & 1), c ^ (mask & 1))
            copies.append(pltpu.make_async_remote_copy(
                src_ref=p_ref, dst_ref=gath_ref.at[me],
                send_sem=send_sems.at[mask - 1], recv_sem=recv_sems.at[mask - 1],
                device_id=peer, device_id_type=MESH))
        for cp in copies:
            cp.start()
        for cp in copies:
            cp.wait()
        tot = gath_ref[0]
        for k in range(1, N_DEV):
            tot = tot + gath_ref[k]
        _adam(tot, w_ref, m_ref, v_ref, g_out, d_out, m_out, v_out)

    vmem = pl.BlockSpec(memory_space=pltpu.VMEM)
    return pl.pallas_call(
        kern, in_specs=[vmem] * 4, out_specs=[vmem] * 4, out_shape=[_sds((nr, LANES), F32)] * 4,
        scratch_shapes=[pltpu.VMEM((N_DEV, nr, LANES), F32),
                        pltpu.SemaphoreType.DMA((N_DEV - 1,)), pltpu.SemaphoreType.DMA((N_DEV - 1,))],
        name="small_allreduce_adamw")(partial, w, m, v)


def kernel(x, norm_gain, w_in, rel_bias, pool_w, pool_scale, w_out_attn, w_out_pool, gate_bias, w_out, final_gain, loss_target, m_norm_gain, m_w_in, m_rel_bias, m_pool_w, m_pool_scale, m_w_out_attn, m_w_out_pool, m_gate_bias, m_w_out, m_final_gain, v_norm_gain, v_w_in, v_rel_bias, v_pool_w, v_pool_scale, v_w_out_attn, v_w_out_pool, v_gate_bias, v_w_out, v_final_gain):
    _, s_len, d = x.shape
    a = w_out_attn.shape[0]
    p = w_out_pool.shape[0]
    heads = a // HEAD_DIM
    ng = len(POOL_WINDOWS)
    cg = p // ng
    sw = d // N_DEV
    n_in = w_in.shape[1] * N_DEV
    assert a == p and a + p == d and cg == sw and n_in == 5 * d and w_in.shape[1] == 5 * sw
    assert s_len % TQ == 0 and rel_bias.shape == (heads, N_REL)
    tm = _tile(s_len, 1024)
    x2d = x.reshape(s_len, d)
    tgt = loss_target.reshape(s_len, d)

    g1 = norm_gain.reshape(1, d)
    g2 = final_gain.reshape(1, d)
    scale_row = pool_scale.reshape(1, p)
    tn = sw
    per = w_in.shape[1] // tn
    hbm = pl.BlockSpec(memory_space=pl.ANY)

    hb, w_bf, landed = _rms_fwd(x2d, g1, w_in, comm=_gather_comm([w_in[:, :tn].astype(BF)], cols=(0, tn)))
    win_rounds = []

    def store_bf16(acc, _, outs):
        outs[0][...] = acc.astype(BF)

    gate0, n_gate = (4 * a + 2 * p) // tn, d // tn

    def pos(t):
        g = t - gate0
        return jnp.where(t < gate0, t, jnp.where(g < n_gate, gate0 + 2 * g, gate0 + 2 * (g - n_gate) + 1))

    proj = None
    for r in range(per):
        win_rounds.append(landed)
        if r + 1 < per:
            comm = _gather_comm([w_bf], cols=((r + 1) * tn, tn))
        else:
            comm = _gather_comm([w_out_attn.astype(BF), pool_w.astype(BF), gate_bias])
        ins = [(hb, pl.BlockSpec((tm, d), lambda i, j: (i, 0))),
               (landed, pl.BlockSpec((None, d, tn), lambda i, j: (j, 0, 0)))]
        if proj is not None:
            ins.append((proj, hbm))
        proj, landed, *rest = _mm(
            f"proj_{r}", (s_len // tm, N_DEV), ins,
            [(_sds((s_len, n_in), BF), pl.BlockSpec((tm, tn), lambda i, j, r=r: (i, pos(per * j + r))))],
            NN, store_bf16, aliases={2: 0} if r else None, comm=comm)
    woa_g = landed
    woa = woa_g.transpose(1, 0, 2).reshape(a, d)
    pw = rest[0].transpose(1, 0, 2, 3).reshape(ng, cg, cg)
    gb = rest[1].transpose(1, 0, 2).reshape(2, d)

    rb_pad = jnp.pad(rel_bias, ((0, 0), (0, N_REL_PAD - N_REL)))
    base = _bias_rows(rb_pad).reshape(heads, 1, SKEW)
    wo_bf = w_out.astype(BF)
    d_lo = d // 4
    (att, ya), (wop_g, wo_lo_g) = _attn_fwd(
        proj, base, a, comm=_both(_gather_comm([w_out_pool.astype(BF)], pass_at=0.9),
                                  _gather_comm([wo_bf], cols=(0, d_lo), pass_at=0.9)))
    wop = wop_g.transpose(1, 0, 2).reshape(p, d)
    u_blk, z_blk = 4 * a // p, 4 * a // p + 1
    dmean, ypre, yp = _pool_fwd(proj, pw, scale_row, p, u_blk, z_blk)

    def gate_kernel(in_refs, out_refs, _):
        ya_ref, woa_ref, yp_ref, wop_ref, ga_ref, gp_ref, gb_ref = in_refs
        m_ref, a_ref, p_ref = out_refs
        am = jnp.dot(ya_ref[...], woa_ref[...], preferred_element_type=F32)
        pm = jnp.dot(yp_ref[...], wop_ref[...], preferred_element_type=F32)
        sa = _sig(ga_ref[...].astype(F32) + gb_ref[0:1, :])
        sp = _sig(gp_ref[...].astype(F32) + gb_ref[1:2, :])
        m_ref[...] = (sa * am + sp * pm).astype(BF)
        a_ref[...] = am.astype(BF)
        p_ref[...] = pm.astype(BF)

    tile_ij = pl.BlockSpec((tm, tn), lambda i, j: (i, j))
    act_d = _sds((s_len, d), BF)
    merged, am, pm, wo_hi_g = _call(
        "gate_merge", gate_kernel, (s_len // tm, d // tn),
        [(ya, pl.BlockSpec((tm, a), lambda i, j: (i, 0))), (woa_g, pl.BlockSpec((None, a, tn), lambda i, j: (j, 0, 0))),
         (yp, pl.BlockSpec((tm, p), lambda i, j: (i, 0))), (wop_g, pl.BlockSpec((None, p, tn), lambda i, j: (j, 0, 0))),
         (proj, pl.BlockSpec((tm, tn), lambda i, j: (i, gate0 + 2 * j))),
         (proj, pl.BlockSpec((tm, tn), lambda i, j: (i, gate0 + 2 * j + 1))),
         (gb, pl.BlockSpec((2, tn), lambda i, j: (0, j)))],
        [(act_d, tile_ij)] * 3, comm=_gather_comm([wo_bf], cols=(d_lo, d - d_lo), pass_at=0.9))
    wo_halves = [wo_lo_g.reshape(d, d_lo), wo_hi_g.reshape(d, d - d_lo)]

    def out_proj(in_refs, out_refs, _):
        m_ref, lo_ref, hi_ref, x_ref = in_refs
        j = pl.program_id(1)

        @pl.when(j < n_half)
        def _():
            out_refs[0][...] = x_ref[...] + jnp.dot(m_ref[...], lo_ref[...], preferred_element_type=F32)

        @pl.when(j >= n_half)
        def _():
            out_refs[0][...] = x_ref[...] + jnp.dot(m_ref[...], hi_ref[...], preferred_element_type=F32)

    n_half = d_lo // tn
    x2 = _call("out_proj", out_proj, (s_len // tm, d // tn),
               [(merged, pl.BlockSpec((tm, d), lambda i, j: (i, 0))),
                (wo_halves[0], pl.BlockSpec((d, tn), lambda i, j: (0, jnp.minimum(j, n_half - 1)))),
                (wo_halves[1], pl.BlockSpec((d, tn), lambda i, j: (0, jnp.maximum(j - n_half, 0)))),
                (x2d, tile_ij)],
               [(_sds((s_len, d), F32), tile_ij)])[0]

    dx2, dx2b, dg2, loss_part = _final_norm(x2, tgt, g2)

    tmb = _tile(s_len, 512)
    tile_ji = pl.BlockSpec((tmb, tn), lambda j, i: (i, j))

    def gate_bwd(dm, ex, outs):
        a_ref, p_ref, ga_ref, gp_ref, gb_ref = ex
        da_ref, dp_ref, dgate_ref, dgb_ref = outs
        i = pl.program_id(1)
        sa = _sig(ga_ref[...].astype(F32) + gb_ref[0:1, :])
        sp = _sig(gp_ref[...].astype(F32) + gb_ref[1:2, :])
        dga = dm * a_ref[...].astype(F32) * sa * (1.0 - sa)
        dgp = dm * p_ref[...].astype(F32) * sp * (1.0 - sp)
        da_ref[...] = (dm * sa).astype(BF)
        dp_ref[...] = (dm * sp).astype(BF)
        dgate_ref[:, :tn] = dga.astype(BF)
        dgate_ref[:, tn:] = dgp.astype(BF)
        r = lax.broadcasted_iota(jnp.int32, (8, tn), 0)
        sums = jnp.where(r == 0, jnp.sum(dga, axis=0, keepdims=True),
                         jnp.where(r == 1, jnp.sum(dgp, axis=0, keepdims=True), 0.0))

        @pl.when(i == 0)
        def _():
            dgb_ref[...] = sums

        @pl.when(i > 0)
        def _():
            dgb_ref[...] += sums

    dproj_shape = _sds((s_len, n_in), BF)
    d_am, d_pm, dproj, dgb8 = _mm(
        "gate_bwd", (d // tn, s_len // tmb),
        [(dx2b, pl.BlockSpec((tmb, d), lambda j, i: (i, 0)))]
        + [(w, pl.BlockSpec((tn, w.shape[1]), lambda j, i: (j, 0))) for w in wo_halves]
        + [(am, tile_ji), (pm, tile_ji),
           (proj, pl.BlockSpec((tmb, tn), lambda j, i: (i, gate0 + 2 * j))),
           (proj, pl.BlockSpec((tmb, tn), lambda j, i: (i, gate0 + 2 * j + 1))),
           (gb, pl.BlockSpec((2, tn), lambda j, i: (0, j)))],
        [(_sds((s_len, d), BF), tile_ji)] * 2
        + [(dproj_shape, pl.BlockSpec((tmb, 2 * tn), lambda j, i: (i, gate0 // 2 + j))),
           (_sds((8, d), F32), pl.BlockSpec((8, tn), lambda j, i: (0, j)))],
        NT, gate_bwd, nb=2)

    za_t = 3 * a // tn

    def attn_gate_bwd(dya, ex, outs):
        silu, dsilu = _silu_and_grad(ex[0][...].astype(F32))
        outs[0][...] = (dya * silu).astype(BF)
        outs[1][...] = (dya * ex[1][...].astype(F32) * dsilu).astype(BF)

    datt, dproj = _mm(
        "attn_gate_bwd", (s_len // tm, a // tn),
        [(d_am, pl.BlockSpec((tm, d), lambda i, j: (i, 0))), (woa, pl.BlockSpec((tn, d), lambda i, j: (j, 0))),
         (proj, pl.BlockSpec((tm, tn), lambda i, j: (i, za_t + j))), (att, tile_ij), (dproj, hbm)],
        [(_sds((s_len, a), BF), tile_ij), (dproj_shape, pl.BlockSpec((tm, tn), lambda i, j: (i, za_t + j)))],
        NT, attn_gate_bwd, aliases={4: 1})

    zp_t = (4 * a + p) // tn

    def pool_gate_bwd(dyp, ex, outs):
        z_ref, y_ref, sc_ref, _ = ex
        dzp_ref, dy_ref, dps_ref = outs
        i = pl.program_id(1)
        silu, dsilu = _silu_and_grad(z_ref[...].astype(F32))
        y = y_ref[...].astype(F32)
        sc = sc_ref[...]
        dyp0 = dyp * silu
        dzp_ref[...] = (dyp * (y * sc) * dsilu).astype(BF)
        dy_ref[...] = (dyp0 * sc).astype(BF)
        dps = jnp.sum(dyp0 * y, axis=0, keepdims=True)

        @pl.when(i == 0)
        def _():
            dps_ref[...] = dps

        @pl.when(i > 0)
        def _():
            dps_ref[...] += dps

    dproj, dy_pool, dps = _mm(
        "pool_gate_bwd", (p // tn, s_len // tmb),
        [(d_pm, pl.BlockSpec((tmb, d), lambda j, i: (i, 0))), (wop, pl.BlockSpec((tn, d), lambda j, i: (j, 0))),
         (proj, pl.BlockSpec((tmb, tn), lambda j, i: (i, zp_t + j))), (ypre, tile_ji),
         (scale_row, pl.BlockSpec((1, tn), lambda j, i: (0, j))), (dproj, hbm)],
        [(dproj_shape, pl.BlockSpec((tmb, tn), lambda j, i: (i, zp_t + j))), (_sds((s_len, p), BF), tile_ji),
         (_sds((1, p), F32), pl.BlockSpec((1, tn), lambda j, i: (0, j)))],
        NT, pool_gate_bwd, aliases={5: 0})

    dproj, dpw = _pool_bwd(dy_pool, dmean, pw, dproj, u_blk)

    o_wop, o_wo, o_pool = a, d, 2 * d
    slab_a = _sds((2, 4, 2 * d + cg, sw), BF)
    slab_b = _sds((2, 4, 5 * d // 2, sw), BF)
    hbm = pl.BlockSpec(memory_space=pl.ANY)
    tmw = _tile(a, 1024)
    core = lax.axis_index("c").astype(jnp.int32).reshape(1)

    def pack_small(dpw_ref, dgb_ref, o_ref):
        rows = cg // N_DEV
        for j in range(N_DEV):
            for g in range(ng):
                o_ref[j % 2, j // 2, g * rows:(g + 1) * rows, :] = dpw_ref[g, j * rows:(j + 1) * rows, :].astype(BF)
            o_ref[j % 2, j // 2, ng * rows:, :] = jnp.concatenate(
                [dgb_ref[:, j * sw:(j + 1) * sw], jnp.zeros((cg - ng * rows - 8, sw), F32)], axis=0).astype(BF)

    slab = pl.pallas_call(
        pack_small, grid=(1,),
        in_specs=[pl.BlockSpec((ng, cg, cg), lambda i: (0, 0, 0)), pl.BlockSpec((8, d), lambda i: (0, 0))],
        out_specs=pl.BlockSpec((2, 4, cg, sw), lambda i: (0, 0, o_pool // cg, 0)), out_shape=slab_a,
        name="dw_small", compiler_params=_params(1))(dpw, dgb8)

    def into_slab(acc, _, outs):
        outs[0][...] = acc.astype(BF)

    def weight_grad(name, slab, lhs, rhs, grid, lhs_spec, rhs_spec, out_spec, epi=into_slab):
        return _mm(name, grid, [(lhs, lhs_spec), (rhs, rhs_spec), (slab, hbm)], [(slab_a, out_spec)],
                   TN, epi, aliases={2: 0})

    def into_both_cores(acc, _, outs):
        outs[0][0] = acc[:sw].astype(BF)
        outs[0][1] = acc[sw:].astype(BF)

    slab = weight_grad("dw_out", slab, merged, dx2b, (N_DEV // 2, d // sw),
                       pl.BlockSpec((s_len, 2 * sw), lambda k, t: (0, k)),
                       pl.BlockSpec((s_len, sw), lambda k, t: (0, t)),
                       pl.BlockSpec((2, None, sw, sw), lambda k, t: (0, k, o_wo // sw + t, 0)), into_both_cores)[0]
    slab = weight_grad("dw_out_attn", slab, ya, d_am, (a // tmw, N_DEV),
                       pl.BlockSpec((s_len, tmw), lambda i, j: (0, i)), pl.BlockSpec((s_len, sw), lambda i, j: (0, j)),
                       pl.BlockSpec((None, None, tmw, sw), lambda i, j: (j % 2, j // 2, i, 0)))[0]
    slab = weight_grad("dw_out_pool", slab, yp, d_pm, (p // tmw, N_DEV),
                       pl.BlockSpec((s_len, tmw), lambda i, j: (0, i)), pl.BlockSpec((s_len, sw), lambda i, j: (0, j)),
                       pl.BlockSpec((None, None, tmw, sw), lambda i, j: (j % 2, j // 2, o_wop // tmw + i, 0)))[0]

    (dproj, dk, dv, ddiag), (from_sibling_a,) = _attn_bwd(proj, datt, base, a, dproj, comm=_cores_comm(slab))
    chip_part_a = _add_core_partials("add_core_partials_a", slab, from_sibling_a, core, (2 * d + cg) // 2)
    drb = _bias_grad(ddiag.reshape(heads, SKEW))
    dproj = lax.dynamic_update_slice(dproj, dk, (0, a))
    dproj = lax.dynamic_update_slice(dproj, dv, (0, 2 * a))

    tmd = _tile(d // 2, 1024)
    nrb = d // 2 // tmd

    def dw_in_rows(name, half, comm):
        return _mm(
            name, (nrb, n_in // sw),
            [(hb, pl.BlockSpec((s_len, tmd), lambda i, t: (0, half * nrb + i))),
             (dproj, pl.BlockSpec((s_len, sw), lambda i, t: (0, pos(t))))],
            [(slab_b, pl.BlockSpec((None, None, tmd, sw),
                                   lambda i, t: ((t // per) % 2, (t // per) // 2, (t % per) * nrb + i, 0)))],
            TN, into_slab, comm=comm)

    slab_lo, parts_a = dw_in_rows("dw_in_lo", 0, _chips_comm(chip_part_a))
    slab_hi, from_sibling_lo = dw_in_rows("dw_in_hi", 1, _cores_comm(slab_lo))
    chip_part_lo = _add_core_partials("add_core_partials_lo", slab_lo, from_sibling_lo, core, 4 * sw)

    tnh = _tile(d, 1024)
    tmh = tm if s_len > tm else s_len // 2
    n_row = s_len // tmh

    def dh_rows(name, lo, hi, prev, comm):
        def body(in_refs, out_refs, scratch_refs):
            acc_ref = scratch_refs[0]
            k = pl.program_id(2)
            part = lax.dot_general(in_refs[0][...], in_refs[per][...], NT, preferred_element_type=F32)
            for r in range(1, per):
                part += lax.dot_general(in_refs[r][...], in_refs[per + r][...], NT, preferred_element_type=F32)

            @pl.when(k == 0)
            def _():
                acc_ref[...] = part

            @pl.when(k > 0)
            def _():
                acc_ref[...] += part

            @pl.when(k == N_DEV - 1)
            def _():
                out_refs[0][...] = acc_ref[...]

        ins = [(dproj, pl.BlockSpec((tmh, sw), lambda i, j, k, r=r: (lo + i, pos(per * k + r)))) for r in range(per)]
        ins += [(w, pl.BlockSpec((None, tnh, sw), lambda i, j, k: (k, j, 0))) for w in win_rounds]
        if prev is not None:
            ins.append((prev, hbm))
        return _call(name, body, (hi - lo, d // tnh, N_DEV), ins,
                     [(_sds((s_len, d), F32), pl.BlockSpec((tmh, tnh), lambda i, j, k: (lo + i, j)))],
                     scratch=[pltpu.VMEM((tmh, tnh), F32)], aliases={2 * per: 0} if prev is not None else None,
                     comm=comm)

    dh, parts_lo, from_sibling_hi = dh_rows("dh_head", 0, n_row // 2, None,
                                            _both(_chips_comm(chip_part_lo), _cores_comm(slab_hi)))
    chip_part_hi = _add_core_partials("add_core_partials_hi", slab_hi, from_sibling_hi, core, 4 * sw)
    dh, parts_hi = dh_rows("dh_rest", n_row // 2, n_row, dh, _chips_comm(chip_part_hi))

    dx, dg1 = _rms_bwd(x2d, dh, dx2, g1)

    g_win, d_win, m_win, v_win = _adamw_row_halves("adamw_w_in", parts_lo, parts_hi, w_in, m_w_in, v_w_in)
    g_woa, d_woa, m_woa, v_woa = _adamw_shard("adamw_w_out_attn", parts_a, w_out_attn, m_w_out_attn, v_w_out_attn, 0)
    g_wop, d_wop, m_wop, v_wop = _adamw_shard("adamw_w_out_pool", parts_a, w_out_pool, m_w_out_pool, v_w_out_pool, o_wop)
    g_wo, d_wo, m_wo, v_wo = _adamw_shard("adamw_w_out", parts_a, w_out, m_w_out, v_w_out, o_wo)
    flat = lambda t: t.reshape(cg // 2, sw)
    pool_out = _adamw_shard("adamw_pool_w", parts_a, flat(pool_w), flat(m_pool_w), flat(v_pool_w), o_pool)
    g_pw, d_pw, m_pw, v_pw = [t.reshape(pool_w.shape) for t in pool_out]
    pad16 = lambda t: jnp.pad(t, ((0, 14), (0, 0)))
    gb_out = _adamw_shard("adamw_gate_bias", parts_a, pad16(gate_bias), pad16(m_gate_bias), pad16(v_gate_bias),
                          o_pool + cg // 2)
    g_gb, d_gb, m_gb, v_gb = [t[:2] for t in gb_out]

    def pack(n_gain, f_gain, scale, rb, last):
        rows = [n_gain.reshape(-1, LANES), f_gain.reshape(-1, LANES), scale.reshape(-1, LANES),
                rb.reshape(-1, LANES), last]
        return jnp.concatenate(rows, axis=0)

    pad_rb = lambda t: jnp.pad(t, ((0, 0), (0, N_REL_PAD - N_REL)))
    zeros8 = jnp.zeros((8, LANES), F32)
    loss_rows = jnp.pad(loss_part, ((0, 7), (0, 0)))
    small = _small_allreduce_adamw(
        pack(dg1, dg2, dps, drb, loss_rows),
        pack(norm_gain, final_gain, pool_scale, pad_rb(rel_bias), zeros8),
        pack(m_norm_gain, m_final_gain, m_pool_scale, pad_rb(m_rel_bias), zeros8),
        pack(v_norm_gain, v_final_gain, v_pool_scale, pad_rb(v_rel_bias), zeros8))

    n1, n2, n3 = d // LANES, 2 * d // LANES, (2 * d + p) // LANES
    n4 = n3 + heads * N_REL_PAD // LANES

    def unpack(t):
        return (t[:n1].reshape(d), t[n1:n2].reshape(d), t[n2:n3].reshape(p),
                t[n3:n4].reshape(heads, N_REL_PAD)[:, :N_REL])

    (g_ng, g_fg, g_ps, g_rb), (d_ng, d_fg, d_ps, d_rb), (m_ng, m_fg, m_ps, m_rb), (v_ng, v_fg, v_ps, v_rb) = [
        unpack(t) for t in small]
    loss = small[0][n4, 0]

    return (loss, dx.reshape(x.shape),
            g_ng, g_win, g_rb, g_pw, g_ps, g_woa, g_wop, g_gb, g_wo, g_fg,
            d_ng, d_win, d_rb, d_pw, d_ps, d_woa, d_wop, d_gb, d_wo, d_fg,
            m_ng, m_win, m_rb, m_pw, m_ps, m_woa, m_wop, m_gb, m_wo, m_fg,
            v_ng, v_win, v_rb, v_pw, v_ps, v_woa, v_wop, v_gb, v_wo, v_fg)
```
